```python
import math
import jax, jax.numpy as jnp
from jax import lax
import numpy as np

D_MODEL = 2048
BATCH = 4
SEQ = 2048
DEPTH = 2
DEC_BATCH = 8
DEC_SEQ = 4
PAST_LEN = 16384
PAGE_SIZE = 128

N_AB = (DEPTH + 1) // 2
N_CD = DEPTH // 2

A_WIDTH = D_MODEL // 2
CHUNK = 128
A_GROUPS = 8
A_GROUP_DIM = A_WIDTH // A_GROUPS
B_WIDTH = D_MODEL // 2
POOL_WINDOWS = (2, 4, 8, 16)
B_GROUP_DIM = B_WIDTH // len(POOL_WINDOWS)
B_BUF = max(POOL_WINDOWS) - 1
C_HEADS = 8
HEAD_DIM = 128
C_WIDTH = C_HEADS * HEAD_DIM
ROT_DIM = HEAD_DIM // 4
ROPE_THETA = 500000.0
MOBA_BLOCK = 256
MOBA_TOPK = 3
MOBA_Q_CHUNK = 8
D_WIDTH = D_MODEL // 2
CONV_WIDTH = 31
D_BUF = CONV_WIDTH - 1
N_MEM = 256
MEM_HEADS = 4
MEM_HEAD_DIM = 128
MEM_WIDTH = MEM_HEADS * MEM_HEAD_DIM
N_EXPERTS = 16
N_EXPERT_GROUPS = 4
EXPERTS_PER_GROUP = N_EXPERTS // N_EXPERT_GROUPS
MOE_TOPK = 2
D_EXPERT = 1024
ALPHA = (2 * DEPTH) ** 0.25
BETA = (8 * DEPTH) ** -0.25
LN_EPS = 1e-5

AB_IN = 2 * A_WIDTH + B_WIDTH
CD_IN = 3 * C_WIDTH + 2 * D_WIDTH

kernel_name = "hybrid_gmlp_pool_moba_conformer_moe_step"


def layer_norm(x, g, b):
    xf = x.astype(jnp.float32)
    mu = xf.mean(-1, keepdims=True)
    var = jnp.square(xf - mu).mean(-1, keepdims=True)
    return ((xf - mu) * lax.rsqrt(var + LN_EPS) * g + b).astype(x.dtype)


def partial_rope(x, pos):
    half = ROT_DIM // 2
    inv = jnp.power(ROPE_THETA, -jnp.arange(half, dtype=jnp.float32) / half)
    ang = pos.astype(jnp.float32)[:, None] * inv[None, :]
    cos = jnp.cos(ang)[None, :, None, :]
    sin = jnp.sin(ang)[None, :, None, :]
    xr = x[..., :ROT_DIM].astype(jnp.float32)
    x1, x2 = xr[..., :half], xr[..., half:]
    rot = jnp.concatenate([x1 * cos - x2 * sin, x2 * cos + x1 * sin], axis=-1).astype(x.dtype)
    return jnp.concatenate([rot, x[..., ROT_DIM:]], axis=-1)


def chunk_mlp_mixer(u, v, ln_g, ln_b, w_s, b_s):
    Bq, S, _ = u.shape
    vn = layer_norm(v, ln_g, ln_b)
    n_ch = -(-S // CHUNK)
    vp = jnp.pad(vn, ((0, 0), (0, n_ch * CHUNK - S), (0, 0)))
    vp = vp.reshape(Bq, n_ch, CHUNK, A_GROUPS, A_GROUP_DIM)
    w = jnp.tril(w_s)
    mixed = jnp.einsum('gts,bcsgd->bctgd', w, vp) + b_s.T[None, None, :, :, None]
    mixed = mixed.reshape(Bq, n_ch * CHUNK, A_WIDTH)[:, :S]
    return u * mixed, vn


def pool_mixer(z_ext, n_prev, pos0, w_g, scale):
    S = z_ext.shape[1] - n_prev
    zf = z_ext.astype(jnp.float32)
    cs = jnp.concatenate([jnp.zeros_like(zf[:, :1]), jnp.cumsum(zf, axis=1)], axis=1)
    t = jnp.arange(S, dtype=jnp.int32)
    pos = pos0 + t
    end = n_prev + t + 1
    z_new = zf[:, n_prev:]
    outs = []
    for gi, w in enumerate(POOL_WINDOWS):
        sl = slice(gi * B_GROUP_DIM, (gi + 1) * B_GROUP_DIM)
        cnt = jnp.minimum(w, pos + 1)
        start = end - cnt
        mean = (cs[:, end, sl] - cs[:, start, sl]) / cnt[None, :, None].astype(jnp.float32)
        d = (mean - z_new[:, :, sl]).astype(z_ext.dtype)
        outs.append(jnp.einsum('bsc,ce->bse', d, w_g[gi]))
    return jnp.concatenate(outs, axis=-1) * scale


def moba_attention(q, k_all, v_all, q_pos0):
    Bq, Sq, H, HD = q.shape
    L = k_all.shape[1]
    nb = -(-L // MOBA_BLOCK)
    pad = nb * MOBA_BLOCK - L
    kb = jnp.pad(k_all, ((0, 0), (0, pad), (0, 0), (0, 0))).reshape(Bq, nb, MOBA_BLOCK, H, HD).transpose(0, 3, 1, 2, 4)
    vb = jnp.pad(v_all, ((0, 0), (0, pad), (0, 0), (0, 0))).reshape(Bq, nb, MOBA_BLOCK, H, HD).transpose(0, 3, 1, 2, 4)
    kmean = kb.astype(jnp.float32).mean(axis=3)
    n_sel = min(MOBA_TOPK, nb)
    qc = math.gcd(Sq, MOBA_Q_CHUNK)
    n_qc = Sq // qc
    qh = q.transpose(0, 2, 1, 3).reshape(Bq, H, n_qc, qc, HD).transpose(2, 0, 1, 3, 4)
    scale = HD ** -0.5
    bi = jnp.arange(Bq)[:, None, None, None]
    hi = jnp.arange(H)[None, :, None, None]
    blk_ids = jnp.arange(nb)

    def one_chunk(args):
        qb, c = args
        pos = q_pos0 + c * qc + jnp.arange(qc, dtype=jnp.int32)
        own = pos // MOBA_BLOCK
        bscore = jnp.einsum('bhqd,bhnd->bhqn', qb.astype(jnp.float32), kmean)
        past = blk_ids[None, :] < own[:, None]
        bscore = jnp.where(past[None, None], bscore, -jnp.inf)
        _, sel = lax.top_k(bscore, n_sel)
        slot_ok = jnp.arange(n_sel)[None, :] < jnp.minimum(MOBA_TOPK, own)[:, None]
        ids = jnp.concatenate([sel, jnp.broadcast_to(own[None, None, :, None], (Bq, H, qc, 1))], axis=-1)
        kg = kb[bi, hi, ids]
        vg = vb[bi, hi, ids]
        s = jnp.einsum('bhqd,bhqnld->bhqnl', qb, kg).astype(jnp.float32) * scale
        key_pos = own[:, None] * MOBA_BLOCK + jnp.arange(MOBA_BLOCK)[None, :]
        own_ok = key_pos <= pos[:, None]
        mask = jnp.concatenate([jnp.broadcast_to(slot_ok[:, :, None], (qc, n_sel, MOBA_BLOCK)),
                                own_ok[:, None, :]], axis=1)
        s = jnp.where(mask[None, None], s, -jnp.inf)
        p = jax.nn.softmax(s.reshape(Bq, H, qc, -1), axis=-1).reshape(s.shape).astype(vg.dtype)
        return jnp.einsum('bhqnl,bhqnld->bhqd', p, vg)

    out = lax.map(one_chunk, (qh, jnp.arange(n_qc, dtype=jnp.int32)))
    return out.transpose(1, 0, 3, 2, 4).reshape(Bq, Sq, H, HD)


def conv_module(a, g, ctx, w_dw, b_dw, ln_g, ln_b):
    h = a * jax.nn.sigmoid(g)
    ext = jnp.concatenate([ctx, h], axis=1)
    y = lax.conv_general_dilated(ext, w_dw[:, None, :], window_strides=(1,), padding='VALID',
                                 dimension_numbers=('NWC', 'WIO', 'NWC'),
                                 feature_group_count=D_WIDTH) + b_dw
    y = layer_norm(y, ln_g, ln_b)
    return jax.nn.silu(y), ext[:, -D_BUF:]


def memory_attention(x, mk, mv, w_q, w_o):
    Bq, S, _ = x.shape
    q = (x @ w_q).reshape(Bq, S, MEM_HEADS, MEM_HEAD_DIM)
    s = jnp.einsum('bshd,bmhd->bhsm', q, mk).astype(jnp.float32) * (MEM_HEAD_DIM ** -0.5)
    p = jax.nn.softmax(s, axis=-1).astype(mv.dtype)
    o = jnp.einsum('bhsm,bmhd->bshd', p, mv).reshape(Bq, S, MEM_WIDTH)
    return o @ w_o


def moe_ffn(x, w_router, b_router, w_gate, w_up, w_down):
    logits = (x @ w_router).astype(jnp.float32) + b_router
    probs = jax.nn.softmax(logits, axis=-1)
    pg = probs.reshape(probs.shape[:-1] + (N_EXPERT_GROUPS, EXPERTS_PER_GROUP))
    gscore = lax.top_k(pg, MOE_TOPK)[0].sum(-1)
    gsel = jnp.argmax(gscore, axis=-1)
    in_sel = (jnp.arange(N_EXPERTS) // EXPERTS_PER_GROUP) == gsel[..., None]
    topv, topi = lax.top_k(jnp.where(in_sel, probs, -1.0), MOE_TOPK)
    wts = topv / topv.sum(-1, keepdims=True)
    gate = (jax.nn.one_hot(topi, N_EXPERTS, dtype=jnp.float32) * wts[..., None]).sum(-2)
    hg = jnp.einsum('bsd,edf->bsef', x, w_gate)
    hu = jnp.einsum('bsd,edf->bsef', x, w_up)
    hh = jax.nn.silu(hg) * hu * gate[..., None].astype(x.dtype)
    return jnp.einsum('bsef,efd->bsd', hh, w_down)


def decoder_layers(x, pos0, mem_k, mem_v, b_ctx, d_ctx, cache_k, cache_v, page_table, prm):
    Bq, S, _ = x.shape
    pos = pos0 + jnp.arange(S, dtype=jnp.int32)
    new_kc, new_vc, new_b, new_av, new_d = [], [], [], [], []
    for layer in range(DEPTH):
        i = layer // 2
        if layer % 2 == 0:
            h = x @ prm['ab_w_in'][i]
            u = jax.nn.gelu(h[..., :A_WIDTH])
            v = jax.nn.gelu(h[..., A_WIDTH:2 * A_WIDTH])
            z = h[..., 2 * A_WIDTH:]
            a_out, vn = chunk_mlp_mixer(u, v, prm['ab_a_ln_g'][i], prm['ab_a_ln_b'][i],
                                        prm['ab_a_ws'][i], prm['ab_a_bs'][i])
            z_ext = z if b_ctx is None else jnp.concatenate([b_ctx[i], z], axis=1)
            b_out = pool_mixer(z_ext, z_ext.shape[1] - S, pos0, prm['ab_b_wg'][i], prm['ab_b_scale'][i])
            new_b.append(z_ext[:, -B_BUF:])
            new_av.append(vn)
            mix = jnp.concatenate([a_out, b_out], axis=-1) @ prm['ab_w_out'][i]
        else:
            h = x @ prm['cd_w_in'][i]
            q, k, v, ga, gg = jnp.split(h, [C_WIDTH, 2 * C_WIDTH, 3 * C_WIDTH, 3 * C_WIDTH + D_WIDTH], axis=-1)
            q = partial_rope(q.reshape(Bq, S, C_HEADS, HEAD_DIM), pos)
            k = partial_rope(k.reshape(Bq, S, C_HEADS, HEAD_DIM), pos)
            v = v.reshape(Bq, S, C_HEADS, HEAD_DIM)
            if cache_k is None:
                k_all, v_all = k, v
            else:
                pk = cache_k[i][page_table].reshape(Bq, -1, C_HEADS, HEAD_DIM)
                pv = cache_v[i][page_table].reshape(Bq, -1, C_HEADS, HEAD_DIM)
                k_all = jnp.concatenate([pk, k], axis=1)
                v_all = jnp.concatenate([pv, v], axis=1)
            c_out = moba_attention(q, k_all, v_all, pos0).reshape(Bq, S, C_WIDTH)
            ctx = jnp.zeros((Bq, D_BUF, D_WIDTH), x.dtype) if d_ctx is None else d_ctx[i]
            d_out, d_state = conv_module(ga, gg, ctx, prm['cd_d_conv_w'][i], prm['cd_d_conv_b'][i],
                                         prm['cd_d_ln_g'][i], prm['cd_d_ln_b'][i])
            new_kc.append(k)
            new_vc.append(v)
            new_d.append(d_state)
            mix = jnp.concatenate([c_out, d_out], axis=-1) @ prm['cd_w_out'][i]
        x = layer_norm(ALPHA * x + mix, prm['mix_ln_g'][layer], prm['mix_ln_b'][layer])
        xa = memory_attention(x, mem_k[layer], mem_v[layer], prm['mem_w_q'][layer], prm['mem_w_o'][layer])
        x = layer_norm(ALPHA * x + xa, prm['mem_ln_g'][layer], prm['mem_ln_b'][layer])
        xm = moe_ffn(x, prm['router_w'], prm['router_b'], prm['moe_w_gate'][layer],
                     prm['moe_w_up'][layer], prm['moe_w_down'][layer])
        x = layer_norm(ALPHA * x + xm, prm['moe_ln_g'][layer], prm['moe_ln_b'][layer])
    return (x, jnp.stack(new_kc), jnp.stack(new_vc), jnp.stack(new_b), jnp.stack(new_av), jnp.stack(new_d))


def setup_inputs(seed: int = 0) -> dict:
    key = jax.random.key(seed)
    ks = iter(jax.random.split(key, 64))

    def nrm(shape, s=1.0):
        return jax.random.normal(next(ks), shape, jnp.float32) * s

    def gain(shape):
        return 1.0 + nrm(shape, 0.02)

    n_pages = PAST_LEN // PAGE_SIZE
    n_pool = (DEC_BATCH * n_pages * 5) // 4
    perm = jax.random.permutation(next(ks), n_pool)
    page_table = perm[:DEC_BATCH * n_pages].reshape(DEC_BATCH, n_pages).astype(jnp.int32)
    return {
        'x_prompt': nrm((BATCH, SEQ, D_MODEL)),
        'x_sample': nrm((DEC_BATCH, DEC_SEQ, D_MODEL)),
        'mem_prompt': nrm((BATCH, N_MEM, D_MODEL)),
        'cache_c_k': nrm((N_CD, n_pool, PAGE_SIZE, C_HEADS, HEAD_DIM)),
        'cache_c_v': nrm((N_CD, n_pool, PAGE_SIZE, C_HEADS, HEAD_DIM)),
        'page_table': page_table,
        'state_b_buf': nrm((N_AB, DEC_BATCH, B_BUF, B_WIDTH)),
        'state_d_buf': nrm((N_CD, DEC_BATCH, D_BUF, D_WIDTH), 0.5),
        'cache_mem_k': nrm((DEPTH, DEC_BATCH, N_MEM, MEM_HEADS, MEM_HEAD_DIM)),
        'cache_mem_v': nrm((DEPTH, DEC_BATCH, N_MEM, MEM_HEADS, MEM_HEAD_DIM)),
        'ab_w_in': nrm((N_AB, D_MODEL, AB_IN), D_MODEL ** -0.5),
        'ab_a_ln_g': gain((N_AB, A_WIDTH)),
        'ab_a_ln_b': nrm((N_AB, A_WIDTH), 0.02),
        'ab_a_ws': nrm((N_AB, A_GROUPS, CHUNK, CHUNK), CHUNK ** -0.5),
        'ab_a_bs': gain((N_AB, A_GROUPS, CHUNK)),
        'ab_b_wg': nrm((N_AB, len(POOL_WINDOWS), B_GROUP_DIM, B_GROUP_DIM), B_GROUP_DIM ** -0.5),
        'ab_b_scale': gain((N_AB, B_WIDTH)),
        'ab_w_out': nrm((N_AB, A_WIDTH + B_WIDTH, D_MODEL), BETA * (A_WIDTH + B_WIDTH) ** -0.5),
        'cd_w_in': nrm((N_CD, D_MODEL, CD_IN), D_MODEL ** -0.5),
        'cd_d_conv_w': nrm((N_CD, CONV_WIDTH, D_WIDTH), CONV_WIDTH ** -0.5),
        'cd_d_conv_b': nrm((N_CD, D_WIDTH), 0.02),
        'cd_d_ln_g': gain((N_CD, D_WIDTH)),
        'cd_d_ln_b': nrm((N_CD, D_WIDTH), 0.02),
        'cd_w_out': nrm((N_CD, C_WIDTH + D_WIDTH, D_MODEL), BETA * (C_WIDTH + D_WIDTH) ** -0.5),
        'mix_ln_g': gain((DEPTH, D_MODEL)),
        'mix_ln_b': nrm((DEPTH, D_MODEL), 0.02),
        'mem_w_q': nrm((DEPTH, D_MODEL, MEM_WIDTH), D_MODEL ** -0.5),
        'mem_w_k': nrm((DEPTH, D_MODEL, MEM_WIDTH), D_MODEL ** -0.5),
        'mem_w_v': nrm((DEPTH, D_MODEL, MEM_WIDTH), D_MODEL ** -0.5),
        'mem_w_o': nrm((DEPTH, MEM_WIDTH, D_MODEL), BETA * MEM_WIDTH ** -0.5),
        'mem_ln_g': gain((DEPTH, D_MODEL)),
        'mem_ln_b': nrm((DEPTH, D_MODEL), 0.02),
        'router_w': nrm((D_MODEL, N_EXPERTS), D_MODEL ** -0.5),
        'router_b': nrm((N_EXPERTS,), 0.01),
        'moe_w_gate': nrm((DEPTH, N_EXPERTS, D_MODEL, D_EXPERT), D_MODEL ** -0.5),
        'moe_w_up': nrm((DEPTH, N_EXPERTS, D_MODEL, D_EXPERT), D_MODEL ** -0.5),
        'moe_w_down': nrm((DEPTH, N_EXPERTS, D_EXPERT, D_MODEL), BETA * D_EXPERT ** -0.5),
        'moe_ln_g': gain((DEPTH, D_MODEL)),
        'moe_ln_b': nrm((DEPTH, D_MODEL), 0.02),
    }


def reference(x_prompt, x_sample, mem_prompt, cache_c_k, cache_c_v, page_table, state_b_buf, state_d_buf,
              cache_mem_k, cache_mem_v, ab_w_in, ab_a_ln_g, ab_a_ln_b, ab_a_ws, ab_a_bs, ab_b_wg, ab_b_scale,
              ab_w_out, cd_w_in, cd_d_conv_w, cd_d_conv_b, cd_d_ln_g, cd_d_ln_b, cd_w_out, mix_ln_g, mix_ln_b,
              mem_w_q, mem_w_k, mem_w_v, mem_w_o, mem_ln_g, mem_ln_b, router_w, router_b,
              moe_w_gate, moe_w_up, moe_w_down, moe_ln_g, moe_ln_b):
    prm = {
        'ab_w_in': ab_w_in, 'ab_a_ln_g': ab_a_ln_g, 'ab_a_ln_b': ab_a_ln_b, 'ab_a_ws': ab_a_ws,
        'ab_a_bs': ab_a_bs, 'ab_b_wg': ab_b_wg, 'ab_b_scale': ab_b_scale, 'ab_w_out': ab_w_out,
        'cd_w_in': cd_w_in, 'cd_d_conv_w': cd_d_conv_w, 'cd_d_conv_b': cd_d_conv_b,
        'cd_d_ln_g': cd_d_ln_g, 'cd_d_ln_b': cd_d_ln_b, 'cd_w_out': cd_w_out,
        'mix_ln_g': mix_ln_g, 'mix_ln_b': mix_ln_b, 'mem_w_q': mem_w_q, 'mem_w_o': mem_w_o,
        'mem_ln_g': mem_ln_g, 'mem_ln_b': mem_ln_b, 'router_w': router_w, 'router_b': router_b,
        'moe_w_gate': moe_w_gate, 'moe_w_up': moe_w_up, 'moe_w_down': moe_w_down,
        'moe_ln_g': moe_ln_g, 'moe_ln_b': moe_ln_b,
    }
    bp = x_prompt.shape[0]
    prompt_mem_k = jnp.einsum('bmd,ldk->lbmk', mem_prompt, mem_w_k).reshape(DEPTH, bp, N_MEM, MEM_HEADS, MEM_HEAD_DIM)
    prompt_mem_v = jnp.einsum('bmd,ldk->lbmk', mem_prompt, mem_w_v).reshape(DEPTH, bp, N_MEM, MEM_HEADS, MEM_HEAD_DIM)
    y_prompt, prompt_c_k, prompt_c_v, prompt_b_buf, _, prompt_d_buf = decoder_layers(
        x_prompt, 0, prompt_mem_k, prompt_mem_v, None, None, None, None, None, prm)
    y_sample, sample_c_k, sample_c_v, sample_b_buf, sample_a_v, sample_d_buf = decoder_layers(
        x_sample, PAST_LEN, cache_mem_k, cache_mem_v, state_b_buf, state_d_buf,
        cache_c_k, cache_c_v, page_table, prm)
    return (y_prompt, y_sample, prompt_c_k, prompt_c_v, sample_c_k, sample_c_v,
            prompt_b_buf, sample_b_buf, sample_a_v, prompt_d_buf, sample_d_buf,
            prompt_mem_k, prompt_mem_v)
```

```python
import functools
import math

import jax
import jax.numpy as jnp
from jax import lax
from jax.experimental import pallas as pl
from jax.experimental.pallas import tpu as pltpu

F32 = jnp.float32
BF16 = jnp.bfloat16
HIGHEST = lax.Precision.HIGHEST

D_MODEL = 2048
BATCH = 4
SEQ = 2048
DEPTH = 2
DEC_BATCH = 8
DEC_SEQ = 4
PAST_LEN = 16384
PAGE_SIZE = 128
A_WIDTH = 1024
CHUNK = 128
A_GROUPS = 8
B_WIDTH = 1024
POOL_WINDOWS = (2, 4, 8, 16)
B_GROUP_DIM = B_WIDTH // len(POOL_WINDOWS)
B_BUF = 15
C_HEADS = 8
HEAD_DIM = 128
C_WIDTH = 1024
ROT_DIM = 32
ROPE_THETA = 500000.0
MOBA_BLOCK = 256
MOBA_TOPK = 3
D_WIDTH = 1024
CONV_WIDTH = 31
D_BUF = 30
N_MEM = 256
MEM_HEADS = 4
MEM_HEAD_DIM = 128
MEM_WIDTH = 512
N_EXPERTS = 16
N_EXPERT_GROUPS = 4
EXPERTS_PER_GROUP = 4
MOE_TOPK = 2
D_EXPERT = 1024
ALPHA = (2 * DEPTH) ** 0.25
LN_EPS = 1e-5

N_PROMPT = BATCH * SEQ
N_SAMPLE = DEC_BATCH * DEC_SEQ
N_TOK = N_PROMPT + N_SAMPLE
N_PAST_BLOCKS = PAST_LEN // MOBA_BLOCK
PAGES_PER_BLOCK = MOBA_BLOCK // PAGE_SIZE
N_COMBO = DEC_SEQ * C_HEADS

MOE_TM = 256
MOE_ROWS = ((N_TOK * MOE_TOPK + N_EXPERTS * (MOE_TM - 1)) // MOE_TM + 1) * MOE_TM
MOE_TILES = MOE_ROWS // MOE_TM

VMEM_LIMIT = 56 * 1024 * 1024


def _cparams(n_axes, vmem=VMEM_LIMIT):
    return pltpu.CompilerParams(dimension_semantics=("arbitrary",) * n_axes, vmem_limit_bytes=vmem)


def _ln(x, g, b):
    mu = jnp.mean(x, axis=-1, keepdims=True)
    xc = x - mu
    var = jnp.mean(xc * xc, axis=-1, keepdims=True)
    return xc * lax.rsqrt(var + LN_EPS) * g + b


def _dot(a, b):
    return jnp.dot(a, b, preferred_element_type=F32)


def _dot_nt(a, b, precision=None):
    return lax.dot_general(a, b, (((1,), (1,)), ((), ())), precision=precision, preferred_element_type=F32)


def _mm_kernel(x_ref, w_ref, o_ref, *, n_gelu):
    acc = _dot(x_ref[...].astype(BF16), w_ref[...])
    if n_gelu:
        j = pl.program_id(0)

        @pl.when(j < n_gelu)
        def _():
            o_ref[...] = jax.nn.gelu(acc).astype(o_ref.dtype)

        @pl.when(j >= n_gelu)
        def _():
            o_ref[...] = acc.astype(o_ref.dtype)
    else:
        o_ref[...] = acc.astype(o_ref.dtype)


def matmul(x, w, *, m, tm, tn, xoff=0, n_gelu=0, name="mm"):
    k = x.shape[1]
    n = w.shape[1]
    return pl.pallas_call(
        functools.partial(_mm_kernel, n_gelu=n_gelu),
        grid=(n // tn, m // tm),
        in_specs=[pl.BlockSpec((tm, k), lambda j, i: (i + xoff, 0)),
                  pl.BlockSpec((k, tn), lambda j, i: (0, j))],
        out_specs=pl.BlockSpec((tm, tn), lambda j, i: (i, j)),
        out_shape=jax.ShapeDtypeStruct((m, n), F32),
        compiler_params=_cparams(2),
        name=name,
    )(x, w)


def _mm_res_ln_kernel(*refs, n_in):
    a_refs = refs[:n_in]
    w_refs = refs[n_in:2 * n_in]
    r_ref, g_ref, b_ref, o_ref = refs[2 * n_in:]
    acc = None
    for a_ref, w_ref in zip(a_refs, w_refs):
        d = _dot(a_ref[...].astype(BF16), w_ref[...])
        acc = d if acc is None else acc + d
    o_ref[...] = _ln(ALPHA * r_ref[...] + acc, g_ref[...], b_ref[...])


def matmul_res_ln(a_list, w_list, resid, g, b, *, m, tm, roff=0, name="mm_res_ln"):
    n_in = len(a_list)
    in_specs = [pl.BlockSpec((tm, a.shape[1]), lambda i: (i, 0)) for a in a_list]
    in_specs += [pl.BlockSpec(w.shape, lambda i: (0, 0)) for w in w_list]
    in_specs += [pl.BlockSpec((tm, D_MODEL), lambda i: (i + roff, 0)),
                 pl.BlockSpec((1, D_MODEL), lambda i: (0, 0)),
                 pl.BlockSpec((1, D_MODEL), lambda i: (0, 0))]
    return pl.pallas_call(
        functools.partial(_mm_res_ln_kernel, n_in=n_in),
        grid=(m // tm,),
        in_specs=in_specs,
        out_specs=pl.BlockSpec((tm, D_MODEL), lambda i: (i, 0)),
        out_shape=jax.ShapeDtypeStruct((m, D_MODEL), F32),
        compiler_params=_cparams(1),
        name=name,
    )(*a_list, *w_list, resid, g, b)


def _mixer_ab_kernel(u_ref, v_ref, z_ref, zp_ref, lng_ref, lnb_ref, ws_ref, bs_ref, wg_ref, sc_ref,
                     mix_ref, vn_ref, *, pos0, has_ctx):
    c = pl.program_id(1)
    vn = _ln(v_ref[...], lng_ref[...], lnb_ref[...])
    vn_ref[...] = vn
    u = u_ref[...]
    row = lax.broadcasted_iota(jnp.int32, (CHUNK, CHUNK), 0)
    col = lax.broadcasted_iota(jnp.int32, (CHUNK, CHUNK), 1)
    causal = col <= row
    gd = A_WIDTH // A_GROUPS
    for g in range(A_GROUPS):
        sl = slice(g * gd, (g + 1) * gd)
        w = jnp.where(causal, ws_ref[g], 0.0).astype(BF16)
        mixed = _dot(w, vn[:, sl].astype(BF16)) + bs_ref[:, g:g + 1]
        mix_ref[:, sl] = (u[:, sl] * mixed).astype(mix_ref.dtype)

    z = z_ref[...]
    zp = zp_ref[...]
    if not has_ctx:
        zp = jnp.where(c == 0, 0.0, zp)
    zext = jnp.concatenate([zp, z], axis=0)
    pos = pos0 + c * CHUNK + lax.broadcasted_iota(jnp.int32, (CHUNK, 1), 0)
    for gi, wdw in enumerate(POOL_WINDOWS):
        sl = slice(gi * B_GROUP_DIM, (gi + 1) * B_GROUP_DIM)
        s = zext[:, sl]
        sh = 1
        while sh < wdw:
            s = s + pltpu.roll(s, sh, axis=0)
            sh *= 2
        cnt = jnp.minimum(wdw, pos + 1).astype(F32)
        d = (s[16:, :] / cnt - z[:, sl]).astype(BF16)
        bo = _dot(d, wg_ref[gi]) * sc_ref[:, sl]
        mix_ref[:, A_WIDTH + gi * B_GROUP_DIM:A_WIDTH + (gi + 1) * B_GROUP_DIM] = bo.astype(mix_ref.dtype)


def mixer_ab(h, zprev, zprev_map, *, nb, n_chunks, pos0, has_ctx, prm, name):
    m = nb * n_chunks * CHUNK
    row = lambda b, c: b * n_chunks + c
    const2 = lambda b, c: (0, 0)
    return pl.pallas_call(
        functools.partial(_mixer_ab_kernel, pos0=pos0, has_ctx=has_ctx),
        grid=(nb, n_chunks),
        in_specs=[pl.BlockSpec((CHUNK, A_WIDTH), lambda b, c: (row(b, c), 0)),
                  pl.BlockSpec((CHUNK, A_WIDTH), lambda b, c: (row(b, c), 1)),
                  pl.BlockSpec((CHUNK, B_WIDTH), lambda b, c: (row(b, c), 2)),
                  pl.BlockSpec((16, B_WIDTH), zprev_map),
                  pl.BlockSpec((1, A_WIDTH), const2),
                  pl.BlockSpec((1, A_WIDTH), const2),
                  pl.BlockSpec((A_GROUPS, CHUNK, CHUNK), lambda b, c: (0, 0, 0)),
                  pl.BlockSpec((CHUNK, A_GROUPS), const2),
                  pl.BlockSpec((len(POOL_WINDOWS), B_GROUP_DIM, B_GROUP_DIM), lambda b, c: (0, 0, 0)),
                  pl.BlockSpec((1, B_WIDTH), const2)],
        out_specs=[pl.BlockSpec((CHUNK, A_WIDTH + B_WIDTH), lambda b, c: (row(b, c), 0)),
                   pl.BlockSpec((CHUNK, A_WIDTH), lambda b, c: (row(b, c), 0))],
        out_shape=[jax.ShapeDtypeStruct((m, A_WIDTH + B_WIDTH), BF16),
                   jax.ShapeDtypeStruct((m, A_WIDTH), F32)],
        compiler_params=_cparams(2),
        name=name,
    )(h, h, h, zprev, prm["ln_g"], prm["ln_b"], prm["ws"], prm["bs_t"], prm["wg"], prm["scale"])


def _rope_kernel(q_ref, k_ref, c_ref, sa_ref, sb_ref, qo_ref, ko_ref):
    c = c_ref[...]
    sa = sa_ref[...]
    sb = sb_ref[...]
    half = ROT_DIM // 2
    for h in range(C_HEADS):
        sl = slice(h * HEAD_DIM, (h + 1) * HEAD_DIM)
        for src, dst in ((q_ref, qo_ref), (k_ref, ko_ref)):
            x = src[:, sl]
            dst[:, sl] = (x * c + pltpu.roll(x, half, axis=1) * sa
                          + pltpu.roll(x, HEAD_DIM - half, axis=1) * sb)


def rope_qk(h, tabs, *, m, tm, xoff, tab_blocks, name):
    tmap = lambda i: (i % tab_blocks, 0)
    return pl.pallas_call(
        _rope_kernel,
        grid=(m // tm,),
        in_specs=[pl.BlockSpec((tm, C_WIDTH), lambda i: (i + xoff, 0)),
                  pl.BlockSpec((tm, C_WIDTH), lambda i: (i + xoff, 1)),
                  pl.BlockSpec((tm, HEAD_DIM), tmap),
                  pl.BlockSpec((tm, HEAD_DIM), tmap),
                  pl.BlockSpec((tm, HEAD_DIM), tmap)],
        out_specs=[pl.BlockSpec((tm, C_WIDTH), lambda i: (i, 0)),
                   pl.BlockSpec((tm, C_WIDTH), lambda i: (i, 0))],
        out_shape=[jax.ShapeDtypeStruct((m, C_WIDTH), F32),
                   jax.ShapeDtypeStruct((m, C_WIDTH), F32)],
        compiler_params=_cparams(1),
        name=name,
    )(h, h, *tabs)


def _moba_kernel(q_ref, k_ref, v_ref, o_ref, *, n_blocks):
    i = pl.program_id(2)
    blk_rows = MOBA_BLOCK
    scale = HEAD_DIM ** -0.5
    q = q_ref[...]
    qb = q.astype(BF16)
    kmean = jnp.concatenate(
        [jnp.mean(k_ref[j * blk_rows:(j + 1) * blk_rows, :], axis=0, keepdims=True) for j in range(n_blocks)],
        axis=0)
    bscore = _dot_nt(q, kmean, precision=HIGHEST)
    blk = lax.broadcasted_iota(jnp.int32, (blk_rows, n_blocks), 1)
    past = blk < i
    bs = jnp.where(past, bscore, -jnp.inf)
    rank = jnp.zeros((blk_rows, n_blocks), jnp.int32)
    for m_ in range(n_blocks):
        cm = bs[:, m_:m_ + 1]
        rank = rank + ((cm > bs) | ((cm == bs) & (m_ < blk))).astype(jnp.int32)
    sel = (past & (rank < MOBA_TOPK)).astype(F32)

    start = pl.multiple_of(i * blk_rows, blk_rows)
    s = _dot_nt(qb, k_ref[pl.ds(start, blk_rows), :].astype(BF16)) * scale
    r_i = lax.broadcasted_iota(jnp.int32, (blk_rows, blk_rows), 0)
    c_i = lax.broadcasted_iota(jnp.int32, (blk_rows, blk_rows), 1)
    s = jnp.where(c_i <= r_i, s, -jnp.inf)
    m0 = jnp.max(s, axis=-1, keepdims=True)
    p = jnp.exp(s - m0)
    l0 = jnp.sum(p, axis=-1, keepdims=True)
    acc0 = _dot(p.astype(BF16), v_ref[pl.ds(start, blk_rows), :].astype(BF16))

    def body(j, carry):
        m, l, acc = carry
        st = pl.multiple_of(j * blk_rows, blk_rows)
        sj = _dot_nt(qb, k_ref[pl.ds(st, blk_rows), :].astype(BF16)) * scale
        chosen = jnp.sum(jnp.where(blk == j, sel, 0.0), axis=-1, keepdims=True) > 0.0
        sj = jnp.where(chosen, sj, -jnp.inf)
        m_new = jnp.maximum(m, jnp.max(sj, axis=-1, keepdims=True))
        a = jnp.exp(m - m_new)
        pj = jnp.exp(sj - m_new)
        l = a * l + jnp.sum(pj, axis=-1, keepdims=True)
        acc = a * acc + _dot(pj.astype(BF16), v_ref[pl.ds(st, blk_rows), :].astype(BF16))
        return m_new, l, acc

    _, l, acc = lax.fori_loop(0, i, body, (m0, l0, acc0))
    o_ref[...] = (acc / l).astype(o_ref.dtype)


def moba_prompt(q_rot, k_rot, h, *, nb, seq, name="moba_prompt"):
    n_blocks = seq // MOBA_BLOCK
    v_col0 = 2 * C_WIDTH // HEAD_DIM
    return pl.pallas_call(
        functools.partial(_moba_kernel, n_blocks=n_blocks),
        grid=(nb, C_HEADS, n_blocks),
        in_specs=[pl.BlockSpec((MOBA_BLOCK, HEAD_DIM), lambda b, hh, i: (b * n_blocks + i, hh)),
                  pl.BlockSpec((seq, HEAD_DIM), lambda b, hh, i: (b, hh)),
                  pl.BlockSpec((seq, HEAD_DIM), lambda b, hh, i: (b, v_col0 + hh))],
        out_specs=pl.BlockSpec((MOBA_BLOCK, HEAD_DIM), lambda b, hh, i: (b * n_blocks + i, hh)),
        out_shape=jax.ShapeDtypeStruct((nb * seq, C_WIDTH), BF16),
        compiler_params=_cparams(3),
        name=name,
    )(q_rot, k_rot, h)


CONV_HALO = 32


def _conv_kernel(ga_ref, gg_ref, pa_ref, pg_ref, w_ref, bdw_ref, lng_ref, lnb_ref,
                 o_ref, tail_ref, ext_ref, y_ref, *, tm, rt, tail, prev_is_state):
    t = pl.program_id(1)
    hcur = ga_ref[...] * jax.nn.sigmoid(gg_ref[...])
    if prev_is_state:
        hprev = pa_ref[...]
    else:
        hprev = pa_ref[...] * jax.nn.sigmoid(pg_ref[...])
        hprev = jnp.where(t == 0, 0.0, hprev)
    ext_ref[0:CONV_HALO, :] = hprev
    ext_ref[CONV_HALO:CONV_HALO + tm, :] = hcur
    tail_ref[...] = hcur[tm - tail:, :]
    off = CONV_HALO - D_BUF
    for cc in range(D_WIDTH // 128):
        cs = slice(cc * 128, (cc + 1) * 128)
        for rc in range(tm // rt):
            r0 = rc * rt
            acc = jnp.zeros((rt, 128), F32)
            for j in range(CONV_WIDTH):
                acc = acc + w_ref[j:j + 1, cs] * ext_ref[r0 + off + j:r0 + off + j + rt, cs]
            y_ref[r0:r0 + rt, cs] = acc + bdw_ref[:, cs]
    y = _ln(y_ref[...], lng_ref[...], lnb_ref[...])
    o_ref[...] = (y * jax.nn.sigmoid(y)).astype(o_ref.dtype)


def conv_module(h, col_a, prev_a, prev_g, prev_map_a, prev_map_g, *, nb, n_tiles, tm, rt, tail, xoff,
                prev_is_state, prm, name):
    m = nb * n_tiles * tm
    row = lambda b, t: b * n_tiles + t + xoff
    const2 = lambda b, t: (0, 0)
    return pl.pallas_call(
        functools.partial(_conv_kernel, tm=tm, rt=rt, tail=tail, prev_is_state=prev_is_state),
        grid=(nb, n_tiles),
        in_specs=[pl.BlockSpec((tm, D_WIDTH), lambda b, t: (row(b, t), col_a)),
                  pl.BlockSpec((tm, D_WIDTH), lambda b, t: (row(b, t), col_a + 1)),
                  pl.BlockSpec((CONV_HALO, D_WIDTH), prev_map_a),
                  pl.BlockSpec((CONV_HALO, D_WIDTH), prev_map_g),
                  pl.BlockSpec((CONV_WIDTH, D_WIDTH), const2),
                  pl.BlockSpec((1, D_WIDTH), const2),
                  pl.BlockSpec((1, D_WIDTH), const2),
                  pl.BlockSpec((1, D_WIDTH), const2)],
        out_specs=[pl.BlockSpec((tm, D_WIDTH), lambda b, t: (b * n_tiles + t, 0)),
                   pl.BlockSpec((tail, D_WIDTH), lambda b, t: (b, 0))],
        out_shape=[jax.ShapeDtypeStruct((m, D_WIDTH), BF16),
                   jax.ShapeDtypeStruct((nb * tail, D_WIDTH), F32)],
        scratch_shapes=[pltpu.VMEM((CONV_HALO + tm, D_WIDTH), F32), pltpu.VMEM((tm, D_WIDTH), F32)],
        compiler_params=_cparams(2),
        name=name,
    )(h, h, prev_a, prev_g, prm["conv_w"], prm["conv_b"], prm["ln_g"], prm["ln_b"])


def _paged_partial_kernel(pt_ref, q_ref, k0_ref, k1_ref, v0_ref, v1_ref, m_ref, l_ref, o_ref, ks_ref):
    del pt_ref
    rows = PAGE_SIZE * C_HEADS
    scale = HEAD_DIM ** -0.5
    k0 = k0_ref[...]
    k1 = k1_ref[...]
    ks_ref[...] = jnp.sum(k0, axis=0) + jnp.sum(k1, axis=0)
    kp = jnp.concatenate([k0.reshape(rows, HEAD_DIM), k1.reshape(rows, HEAD_DIM)], axis=0).astype(BF16)
    vp = jnp.concatenate([v0_ref[...].reshape(rows, HEAD_DIM), v1_ref[...].reshape(rows, HEAD_DIM)],
                         axis=0).astype(BF16)
    s = _dot_nt(q_ref[...].astype(BF16), kp) * scale
    c_i = lax.broadcasted_iota(jnp.int32, s.shape, 0)
    l_i = lax.broadcasted_iota(jnp.int32, s.shape, 1)
    same_head = (l_i & (C_HEADS - 1)) == (c_i & (C_HEADS - 1))
    s = jnp.where(same_head, s, -jnp.inf)
    m = jnp.max(s, axis=-1, keepdims=True)
    p = jnp.exp(s - m)
    m_ref[...] = jnp.broadcast_to(m, m_ref.shape)
    l_ref[...] = jnp.broadcast_to(jnp.sum(p, axis=-1, keepdims=True), l_ref.shape)
    o_ref[...] = _dot(p.astype(BF16), vp)


def paged_partials(page_table, q_combo, cache_k, cache_v, layer):
    page_block = (None, None, PAGE_SIZE, C_HEADS, HEAD_DIM)

    def page_map(which):
        return lambda b, n, pt: (layer, pt[b, PAGES_PER_BLOCK * n + which], 0, 0, 0)

    part_shape = jax.ShapeDtypeStruct((DEC_BATCH, N_PAST_BLOCKS, N_COMBO, HEAD_DIM), F32)
    part_spec = pl.BlockSpec((None, None, N_COMBO, HEAD_DIM), lambda b, n, pt: (b, n, 0, 0))
    return pl.pallas_call(
        _paged_partial_kernel,
        grid_spec=pltpu.PrefetchScalarGridSpec(
            num_scalar_prefetch=1,
            grid=(DEC_BATCH, N_PAST_BLOCKS),
            in_specs=[pl.BlockSpec((None, N_COMBO, HEAD_DIM), lambda b, n, pt: (b, 0, 0)),
                      pl.BlockSpec(page_block, page_map(0)),
                      pl.BlockSpec(page_block, page_map(1)),
                      pl.BlockSpec(page_block, page_map(0)),
                      pl.BlockSpec(page_block, page_map(1))],
            out_specs=[part_spec, part_spec, part_spec,
                       pl.BlockSpec((None, None, C_HEADS, HEAD_DIM), lambda b, n, pt: (b, n, 0, 0))]),
        out_shape=[part_shape, part_shape, part_shape,
                   jax.ShapeDtypeStruct((DEC_BATCH, N_PAST_BLOCKS, C_HEADS, HEAD_DIM), F32)],
        compiler_params=_cparams(2),
        name="paged_partials",
    )(page_table, q_combo, cache_k, cache_k, cache_v, cache_v)


def _paged_merge_kernel(q_ref, kn_ref, vn_ref, m_ref, l_ref, o_ref, ks_ref, out_ref):
    nb = N_PAST_BLOCKS
    scale = HEAD_DIM ** -0.5
    q = q_ref[...]
    kmean = ks_ref[...] * (1.0 / MOBA_BLOCK)
    kmean = jnp.concatenate([kmean] * DEC_SEQ, axis=1)
    bs = jnp.sum(kmean * q[None], axis=-1, keepdims=True)
    work = jnp.broadcast_to(bs, (nb, N_COMBO, HEAD_DIM))
    n_i = lax.broadcasted_iota(jnp.int32, (nb, N_COMBO, HEAD_DIM), 0)
    sel = n_i < 0
    for _ in range(MOBA_TOPK):
        mx = jnp.max(work, axis=0, keepdims=True)
        first = jnp.min(jnp.where(work == mx, n_i, nb), axis=0, keepdims=True)
        hit = n_i == first
        sel = sel | hit
        work = jnp.where(hit, -jnp.inf, work)

    s = _dot_nt(q.astype(BF16), kn_ref[...].astype(BF16)) * scale
    c_i = lax.broadcasted_iota(jnp.int32, s.shape, 0)
    l_i = lax.broadcasted_iota(jnp.int32, s.shape, 1)
    ok = ((l_i & (C_HEADS - 1)) == (c_i & (C_HEADS - 1))) & ((l_i >> 3) <= (c_i >> 3))
    s = jnp.where(ok, s, -jnp.inf)
    m_own = jnp.max(s, axis=-1, keepdims=True)
    p = jnp.exp(s - m_own)
    l_own = jnp.sum(p, axis=-1, keepdims=True)
    o_own = _dot(p.astype(BF16), vn_ref[...].astype(BF16))

    mp = m_ref[...]
    m_all = jnp.maximum(jnp.max(jnp.where(sel, mp, -jnp.inf), axis=0), m_own)
    w = jnp.where(sel, jnp.exp(mp - m_all[None]), 0.0)
    w_own = jnp.exp(m_own - m_all)
    den = jnp.sum(w * l_ref[...], axis=0) + w_own * l_own
    num = jnp.sum(w * o_ref[...], axis=0) + w_own * o_own
    out_ref[...] = num / den


def paged_merge(q_combo, k_new, v_new, m_p, l_p, o_p, ksum):
    combo = pl.BlockSpec((None, N_COMBO, HEAD_DIM), lambda b: (b, 0, 0))
    part = pl.BlockSpec((None, N_PAST_BLOCKS, N_COMBO, HEAD_DIM), lambda b: (b, 0, 0, 0))
    return pl.pallas_call(
        _paged_merge_kernel,
        grid=(DEC_BATCH,),
        in_specs=[combo, combo, combo, part, part, part,
                  pl.BlockSpec((None, N_PAST_BLOCKS, C_HEADS, HEAD_DIM), lambda b: (b, 0, 0, 0))],
        out_specs=combo,
        out_shape=jax.ShapeDtypeStruct((DEC_BATCH, N_COMBO, HEAD_DIM), F32),
        compiler_params=_cparams(1),
        name="paged_merge",
    )(q_combo, k_new, v_new, m_p, l_p, o_p, ksum)


def _memattn_kernel(x_ref, wq_ref, mk_ref, mv_ref, wo_ref, g_ref, b_ref, o_ref, *, nb_tile, rows_per_b):
    scale = MEM_HEAD_DIM ** -0.5
    x = x_ref[...]
    tm = x.shape[0]
    q = _dot(x.astype(BF16), wq_ref[...])
    row_b = lax.broadcasted_iota(jnp.int32, (tm, 1), 0) >> int(math.log2(rows_per_b))
    heads = []
    for hh in range(MEM_HEADS):
        sl = slice(hh * MEM_HEAD_DIM, (hh + 1) * MEM_HEAD_DIM)
        qh = q[:, sl].astype(BF16)
        oh = None
        for bb in range(nb_tile):
            s = _dot_nt(qh, mk_ref[bb, :, sl].astype(BF16)) * scale
            s = s - jnp.max(s, axis=-1, keepdims=True)
            p = jnp.exp(s)
            p = p / jnp.sum(p, axis=-1, keepdims=True)
            ob = _dot(p.astype(BF16), mv_ref[bb, :, sl].astype(BF16))
            if nb_tile > 1:
                ob = jnp.where(row_b == bb, ob, 0.0)
            oh = ob if oh is None else oh + ob
        heads.append(oh.astype(BF16))
    o = jnp.concatenate(heads, axis=-1)
    y = ALPHA * x + _dot(o, wo_ref[...])
    o_ref[...] = _ln(y, g_ref[...], b_ref[...])


def memory_attention_ln(x, wq, mk, mv, wo, g, b, *, m, tm, rows_per_b, name):
    if rows_per_b >= tm:
        nb_tile = 1
        tiles_per_b = rows_per_b // tm
        kv_map = lambda i: (i // tiles_per_b, 0, 0)
    else:
        nb_tile = tm // rows_per_b
        kv_map = lambda i: (i, 0, 0)
    const2 = lambda i: (0, 0)
    return pl.pallas_call(
        functools.partial(_memattn_kernel, nb_tile=nb_tile, rows_per_b=rows_per_b),
        grid=(m // tm,),
        in_specs=[pl.BlockSpec((tm, D_MODEL), lambda i: (i, 0)),
                  pl.BlockSpec((D_MODEL, MEM_WIDTH), const2),
                  pl.BlockSpec((nb_tile, N_MEM, MEM_WIDTH), kv_map),
                  pl.BlockSpec((nb_tile, N_MEM, MEM_WIDTH), kv_map),
                  pl.BlockSpec((MEM_WIDTH, D_MODEL), const2),
                  pl.BlockSpec((1, D_MODEL), const2),
                  pl.BlockSpec((1, D_MODEL), const2)],
        out_specs=pl.BlockSpec((tm, D_MODEL), lambda i: (i, 0)),
        out_shape=jax.ShapeDtypeStruct((m, D_MODEL), F32),
        compiler_params=_cparams(1),
        name=name,
    )(x, wq, mk, mv, wo, g, b)


def _router_kernel(x_ref, w_ref, b_ref, o_ref):
    logits = jnp.dot(x_ref[...], w_ref[...], precision=HIGHEST, preferred_element_type=F32) + b_ref[...]
    logits = logits - jnp.max(logits, axis=-1, keepdims=True)
    e = jnp.exp(logits)
    probs = e / jnp.sum(e, axis=-1, keepdims=True)
    p = [probs[:, j:j + 1] for j in range(N_EXPERTS)]
    gbest = None
    gsel = None
    for g in range(N_EXPERT_GROUPS):
        a, b_, c, d = p[4 * g:4 * g + 4]
        hi1, lo1 = jnp.maximum(a, b_), jnp.minimum(a, b_)
        hi2, lo2 = jnp.maximum(c, d), jnp.minimum(c, d)
        gs = jnp.maximum(hi1, hi2) + jnp.maximum(jnp.minimum(hi1, hi2), jnp.maximum(lo1, lo2))
        if g == 0:
            gbest, gsel = gs, jnp.zeros(gs.shape, jnp.int32)
        else:
            better = gs > gbest
            gbest = jnp.where(better, gs, gbest)
            gsel = jnp.where(better, g, gsel)
    cand = [jnp.where(gsel == j // EXPERTS_PER_GROUP, p[j], -1.0) for j in range(N_EXPERTS)]
    v1 = cand[0]
    i1 = jnp.zeros(v1.shape, jnp.int32)
    for j in range(1, N_EXPERTS):
        better = cand[j] > v1
        v1 = jnp.where(better, cand[j], v1)
        i1 = jnp.where(better, j, i1)
    v2 = jnp.full(v1.shape, -2.0, F32)
    i2 = jnp.zeros(v1.shape, jnp.int32)
    for j in range(N_EXPERTS):
        better = (cand[j] > v2) & (i1 != j)
        v2 = jnp.where(better, cand[j], v2)
        i2 = jnp.where(better, j, i2)
    tot = v1 + v2
    lane = lax.broadcasted_iota(jnp.int32, o_ref.shape, 1)
    out = jnp.where(lane == 0, i1.astype(F32),
                    jnp.where(lane == 1, i2.astype(F32),
                              jnp.where(lane == 2, v1 / tot, jnp.where(lane == 3, v2 / tot, 0.0))))
    o_ref[...] = out


def router(x, w, b, *, tm=256):
    m = x.shape[0]
    return pl.pallas_call(
        _router_kernel,
        grid=(m // tm,),
        in_specs=[pl.BlockSpec((tm, D_MODEL), lambda i: (i, 0)),
                  pl.BlockSpec((D_MODEL, N_EXPERTS), lambda i: (0, 0)),
                  pl.BlockSpec((1, N_EXPERTS), lambda i: (0, 0))],
        out_specs=pl.BlockSpec((tm, 128), lambda i: (i, 0)),
        out_shape=jax.ShapeDtypeStruct((m, 128), F32),
        compiler_params=_cparams(1),
        name="router",
    )(x, w, b)


GATHER_CHUNK = 256


def _gather_rows_kernel(src_ref, x_hbm, o_hbm, sem, *, n_rows):
    n_chunks = n_rows // GATHER_CHUNK

    def issue(c):
        def body(r, carry):
            row = c * GATHER_CHUNK + r
            pltpu.make_async_copy(x_hbm.at[pl.ds(src_ref[row], 1)], o_hbm.at[pl.ds(row, 1)], sem).start()
            return carry
        lax.fori_loop(0, GATHER_CHUNK, body, 0)

    def wait_chunk():
        pltpu.make_async_copy(x_hbm.at[pl.ds(0, GATHER_CHUNK)], o_hbm.at[pl.ds(0, GATHER_CHUNK)], sem).wait()

    issue(0)

    def outer(c, carry):
        issue(c)
        wait_chunk()
        return carry

    lax.fori_loop(1, n_chunks, outer, 0)
    wait_chunk()


def gather_rows(src, x, n_rows):
    return pl.pallas_call(
        functools.partial(_gather_rows_kernel, n_rows=n_rows),
        grid_spec=pltpu.PrefetchScalarGridSpec(
            num_scalar_prefetch=1,
            grid=(1,),
            in_specs=[pl.BlockSpec(memory_space=pl.ANY)],
            out_specs=pl.BlockSpec(memory_space=pl.ANY),
            scratch_shapes=[pltpu.SemaphoreType.DMA]),
        out_shape=jax.ShapeDtypeStruct((n_rows, x.shape[1]), x.dtype),
        compiler_params=_cparams(1),
        name="moe_dispatch_gather",
    )(src, x)


def _moe_up_kernel(te_ref, first_ref, nv_ref, x_ref, wrow_ref, wg_ref, wu_ref, hh_ref, wgb_ref, wub_ref):
    i = pl.program_id(0)

    @pl.when(i < nv_ref[0])
    def _():
        @pl.when(first_ref[i] == 1)
        def _():
            wgb_ref[...] = wg_ref[...].astype(BF16)
            wub_ref[...] = wu_ref[...].astype(BF16)

        x = x_ref[...].astype(BF16)
        hg = _dot(x, wgb_ref[...])
        hu = _dot(x, wub_ref[...])
        hh_ref[...] = (hg * jax.nn.sigmoid(hg) * hu * wrow_ref[...]).astype(hh_ref.dtype)

    @pl.when(i >= nv_ref[0])
    def _():
        hh_ref[...] = jnp.zeros(hh_ref.shape, hh_ref.dtype)


def _moe_down_kernel(te_ref, first_ref, nv_ref, hh_ref, wd_ref, y_ref, wdb_ref):
    i = pl.program_id(0)

    @pl.when(i < nv_ref[0])
    def _():
        @pl.when(first_ref[i] == 1)
        def _():
            wdb_ref[...] = wd_ref[...].astype(BF16)

        y_ref[...] = _dot(hh_ref[...], wdb_ref[...])

    @pl.when(i >= nv_ref[0])
    def _():
        y_ref[...] = jnp.zeros(y_ref.shape, y_ref.dtype)


def moe_experts(tile_expert, tile_first, n_valid, x_sorted, w_sorted, w_gate, w_up, w_down, layer):
    tm = MOE_TM
    row_map = lambda i, te, fi, nv: (jnp.minimum(i, nv[0] - 1), 0)
    wmap = lambda i, te, fi, nv: (layer, te[i], 0, 0)
    hh = pl.pallas_call(
        _moe_up_kernel,
        grid_spec=pltpu.PrefetchScalarGridSpec(
            num_scalar_prefetch=3,
            grid=(MOE_TILES,),
            in_specs=[pl.BlockSpec((tm, D_MODEL), row_map),
                      pl.BlockSpec((tm, 1), row_map),
                      pl.BlockSpec((None, None, D_MODEL, D_EXPERT), wmap),
                      pl.BlockSpec((None, None, D_MODEL, D_EXPERT), wmap)],
            out_specs=pl.BlockSpec((tm, D_EXPERT), lambda i, te, fi, nv: (i, 0)),
            scratch_shapes=[pltpu.VMEM((D_MODEL, D_EXPERT), BF16), pltpu.VMEM((D_MODEL, D_EXPERT), BF16)]),
        out_shape=jax.ShapeDtypeStruct((MOE_ROWS, D_EXPERT), BF16),
        compiler_params=_cparams(1),
        name="moe_up",
    )(tile_expert, tile_first, n_valid, x_sorted, w_sorted, w_gate, w_up)
    return pl.pallas_call(
        _moe_down_kernel,
        grid_spec=pltpu.PrefetchScalarGridSpec(
            num_scalar_prefetch=3,
            grid=(MOE_TILES,),
            in_specs=[pl.BlockSpec((tm, D_EXPERT), row_map),
                      pl.BlockSpec((None, None, D_EXPERT, D_MODEL), wmap)],
            out_specs=pl.BlockSpec((tm, D_MODEL), lambda i, te, fi, nv: (i, 0)),
            scratch_shapes=[pltpu.VMEM((D_EXPERT, D_MODEL), BF16)]),
        out_shape=jax.ShapeDtypeStruct((MOE_ROWS, D_MODEL), F32),
        compiler_params=_cparams(1),
        name="moe_down",
    )(tile_expert, tile_first, n_valid, hh, w_down)


def _combine_kernel(d0_ref, d1_ref, x_ref, g_ref, b_ref, y_hbm, o_ref, buf_ref, sem, *, tm, tok0):
    base = tok0 + pl.program_id(0) * tm

    def body(r, carry):
        pltpu.make_async_copy(y_hbm.at[pl.ds(d0_ref[base + r], 1)], buf_ref.at[0, pl.ds(r, 1)], sem).start()
        pltpu.make_async_copy(y_hbm.at[pl.ds(d1_ref[base + r], 1)], buf_ref.at[1, pl.ds(r, 1)], sem).start()
        return carry

    lax.fori_loop(0, tm, body, 0)
    for slot in range(2):
        pltpu.make_async_copy(y_hbm.at[pl.ds(0, tm)], buf_ref.at[slot], sem).wait()
    y = ALPHA * x_ref[...] + buf_ref[0] + buf_ref[1]
    o_ref[...] = _ln(y, g_ref[...], b_ref[...])


def moe_combine_ln(d0, d1, x, g, b, y_sorted, *, m, tm, tok0, name):
    xoff = tok0 // tm
    return pl.pallas_call(
        functools.partial(_combine_kernel, tm=tm, tok0=tok0),
        grid_spec=pltpu.PrefetchScalarGridSpec(
            num_scalar_prefetch=2,
            grid=(m // tm,),
            in_specs=[pl.BlockSpec((tm, D_MODEL), lambda i, a, c: (i + xoff, 0)),
                      pl.BlockSpec((1, D_MODEL), lambda i, a, c: (0, 0)),
                      pl.BlockSpec((1, D_MODEL), lambda i, a, c: (0, 0)),
                      pl.BlockSpec(memory_space=pl.ANY)],
            out_specs=pl.BlockSpec((tm, D_MODEL), lambda i, a, c: (i, 0)),
            scratch_shapes=[pltpu.VMEM((2, tm, D_MODEL), F32), pltpu.SemaphoreType.DMA]),
        out_shape=jax.ShapeDtypeStruct((m, D_MODEL), F32),
        compiler_params=_cparams(1),
        name=name,
    )(d0, d1, x, g, b, y_sorted)


def _routing_plan(route):
    ids = route[:, 0:2].astype(jnp.int32)
    wts = route[:, 2:4]
    onehot = (ids[:, :, None] == jnp.arange(N_EXPERTS, dtype=jnp.int32)).astype(jnp.int32).sum(axis=1)
    before = jnp.cumsum(onehot, axis=0) - onehot
    counts = onehot.sum(axis=0)
    padded = ((counts + MOE_TM - 1) // MOE_TM) * MOE_TM
    ends = jnp.cumsum(padded)
    starts = ends - padded
    dest = starts[ids] + jnp.take_along_axis(before, ids, axis=1)
    tok = jnp.broadcast_to(jnp.arange(N_TOK, dtype=jnp.int32)[:, None], dest.shape)
    src = jnp.zeros((MOE_ROWS,), jnp.int32).at[dest.reshape(-1)].set(tok.reshape(-1))
    w_sorted = jnp.zeros((MOE_ROWS,), F32).at[dest.reshape(-1)].set(wts.reshape(-1))
    tile_start = jnp.arange(MOE_TILES, dtype=jnp.int32) * MOE_TM
    tile_expert = jnp.minimum(jnp.searchsorted(ends, tile_start, side="right"), N_EXPERTS - 1).astype(jnp.int32)
    tile_first = (tile_start == starts[tile_expert]).astype(jnp.int32)
    n_valid = (ends[-1] // MOE_TM).astype(jnp.int32).reshape(1)
    return src, w_sorted.reshape(MOE_ROWS, 1), tile_expert, tile_first, n_valid, dest[:, 0], dest[:, 1]


def moe_ln(x_p, x_s, layer, p):
    route = jnp.concatenate([router(x_p, p["router_w"], p["router_b"], tm=256),
                             router(x_s, p["router_w"], p["router_b"], tm=N_SAMPLE)], axis=0)
    x_all = jnp.concatenate([x_p, x_s], axis=0)
    src, w_sorted, tile_expert, tile_first, n_valid, d0, d1 = _routing_plan(route)
    x_sorted = gather_rows(src, x_all, MOE_ROWS)
    y_sorted = moe_experts(tile_expert, tile_first, n_valid, x_sorted, w_sorted,
                           p["moe_w_gate"], p["moe_w_up"], p["moe_w_down"], layer)
    g, b = p["moe_ln_g"][layer], p["moe_ln_b"][layer]
    out_p = moe_combine_ln(d0, d1, x_all, g, b, y_sorted, m=N_PROMPT, tm=256, tok0=0, name="moe_combine_prompt")
    out_s = moe_combine_ln(d0, d1, x_all, g, b, y_sorted, m=N_SAMPLE, tm=N_SAMPLE, tok0=N_PROMPT,
                           name="moe_combine_sample")
    return out_p, out_s


def _rope_tables(pos):
    half = ROT_DIM // 2
    inv = jnp.power(ROPE_THETA, -jnp.arange(half, dtype=F32) / half)
    ang = pos.astype(F32)[:, None] * inv[None, :]
    cos, sin = jnp.cos(ang), jnp.sin(ang)
    n = pos.shape[0]
    ones = jnp.ones((n, HEAD_DIM - ROT_DIM), F32)
    zeros_h = jnp.zeros((n, half), F32)
    zeros_r = jnp.zeros((n, HEAD_DIM - ROT_DIM), F32)
    c = jnp.concatenate([cos, cos, ones], axis=1)
    sa = jnp.concatenate([zeros_h, sin, zeros_r], axis=1)
    sb = jnp.concatenate([-sin, zeros_h, zeros_r], axis=1)
    return c, sa, sb


def kernel(x_prompt, x_sample, mem_prompt, cache_c_k, cache_c_v, page_table, state_b_buf, state_d_buf, cache_mem_k, cache_mem_v, ab_w_in, ab_a_ln_g, ab_a_ln_b, ab_a_ws, ab_a_bs, ab_b_wg, ab_b_scale, ab_w_out, cd_w_in, cd_d_conv_w, cd_d_conv_b, cd_d_ln_g, cd_d_ln_b, cd_w_out, mix_ln_g, mix_ln_b, mem_w_q, mem_w_k, mem_w_v, mem_w_o, mem_ln_g, mem_ln_b, router_w, router_b, moe_w_gate, moe_w_up, moe_w_down, moe_ln_g, moe_ln_b):
    row = lambda v: v.reshape(1, -1)
    xp = x_prompt.reshape(N_PROMPT, D_MODEL)
    xs = x_sample.reshape(N_SAMPLE, D_MODEL)
    moe_p = {"router_w": router_w, "router_b": row(router_b), "moe_w_gate": moe_w_gate, "moe_w_up": moe_w_up,
             "moe_w_down": moe_w_down, "moe_ln_g": [row(moe_ln_g[l]) for l in range(DEPTH)],
             "moe_ln_b": [row(moe_ln_b[l]) for l in range(DEPTH)]}

    w_kv = jnp.concatenate([mem_w_k[0], mem_w_k[1], mem_w_v[0], mem_w_v[1]], axis=1).astype(BF16)
    n_memrows = BATCH * N_MEM
    kv = matmul(mem_prompt.reshape(n_memrows, D_MODEL), w_kv, m=n_memrows, tm=256, tn=1024, name="mem_kv")
    pmk = [kv[:, l * MEM_WIDTH:(l + 1) * MEM_WIDTH].reshape(BATCH, N_MEM, MEM_WIDTH) for l in range(DEPTH)]
    pmv = [kv[:, (DEPTH + l) * MEM_WIDTH:(DEPTH + l + 1) * MEM_WIDTH].reshape(BATCH, N_MEM, MEM_WIDTH)
           for l in range(DEPTH)]
    smk = cache_mem_k.reshape(DEPTH, DEC_BATCH, N_MEM, MEM_WIDTH)
    smv = cache_mem_v.reshape(DEPTH, DEC_BATCH, N_MEM, MEM_WIDTH)

    def mem_and_moe(x_p, x_s, layer):
        wq = mem_w_q[layer].astype(BF16)
        wo = mem_w_o[layer].astype(BF16)
        g, b = row(mem_ln_g[layer]), row(mem_ln_b[layer])
        x_p = memory_attention_ln(x_p, wq, pmk[layer], pmv[layer], wo, g, b, m=N_PROMPT, tm=512,
                                  rows_per_b=SEQ, name="memattn_prompt")
        x_s = memory_attention_ln(x_s, wq, smk[layer], smv[layer], wo, g, b, m=N_SAMPLE, tm=N_SAMPLE,
                                  rows_per_b=DEC_SEQ, name="memattn_sample")
        return moe_ln(x_p, x_s, layer, moe_p)

    w_in0 = ab_w_in[0].astype(BF16)
    h0p = matmul(xp, w_in0, m=N_PROMPT, tm=512, tn=1024, n_gelu=2, name="ab_in_prompt")
    h0s = matmul(xs, w_in0, m=N_SAMPLE, tm=N_SAMPLE, tn=1024, n_gelu=2, name="ab_in_sample")
    ab_prm = {"ln_g": row(ab_a_ln_g[0]), "ln_b": row(ab_a_ln_b[0]), "ws": ab_a_ws[0], "bs_t": ab_a_bs[0].T,
              "wg": ab_b_wg[0].astype(BF16), "scale": row(ab_b_scale[0])}
    n_chunks = SEQ // CHUNK
    mix_p, _ = mixer_ab(h0p, h0p, lambda b, c: (jnp.maximum(b * (SEQ // 16) + c * (CHUNK // 16) - 1, 0), 2),
                        nb=BATCH, n_chunks=n_chunks, pos0=0, has_ctx=False, prm=ab_prm, name="mixer_ab_prompt")
    h0s_pad = jnp.pad(h0s.reshape(DEC_BATCH, DEC_SEQ, -1), ((0, 0), (0, CHUNK - DEC_SEQ), (0, 0)))
    h0s_pad = h0s_pad.reshape(DEC_BATCH * CHUNK, -1)
    zctx = jnp.pad(state_b_buf[0], ((0, 0), (16 - B_BUF, 0), (0, 0))).reshape(DEC_BATCH * 16, B_WIDTH)
    mix_s, vn_s = mixer_ab(h0s_pad, zctx, lambda b, c: (b, 0), nb=DEC_BATCH, n_chunks=1, pos0=PAST_LEN,
                           has_ctx=True, prm=ab_prm, name="mixer_ab_sample")
    mix_s = mix_s.reshape(DEC_BATCH, CHUNK, -1)[:, :DEC_SEQ].reshape(N_SAMPLE, -1)
    w_out0 = ab_w_out[0].astype(BF16)
    g, b = row(mix_ln_g[0]), row(mix_ln_b[0])
    x1p = matmul_res_ln([mix_p], [w_out0], xp, g, b, m=N_PROMPT, tm=512, name="ab_out_prompt")
    x1s = matmul_res_ln([mix_s], [w_out0], xs, g, b, m=N_SAMPLE, tm=N_SAMPLE, name="ab_out_sample")
    x2p, x2s = mem_and_moe(x1p, x1s, 0)

    w_in1 = cd_w_in[0].astype(BF16)
    h1p = matmul(x2p, w_in1, m=N_PROMPT, tm=512, tn=1024, name="cd_in_prompt")
    h1s = matmul(x2s, w_in1, m=N_SAMPLE, tm=N_SAMPLE, tn=1024, name="cd_in_sample")
    tabs_p = _rope_tables(jnp.arange(SEQ, dtype=jnp.int32))
    tabs_s = _rope_tables(PAST_LEN + (jnp.arange(N_SAMPLE, dtype=jnp.int32) % DEC_SEQ))
    qr_p, kr_p = rope_qk(h1p, tabs_p, m=N_PROMPT, tm=256, xoff=0, tab_blocks=SEQ // 256, name="rope_prompt")
    qr_s, kr_s = rope_qk(h1s, tabs_s, m=N_SAMPLE, tm=N_SAMPLE, xoff=0, tab_blocks=1, name="rope_sample")
    c_p = moba_prompt(qr_p, kr_p, h1p, nb=BATCH, seq=SEQ)
    v_s = h1s[:, 2 * C_WIDTH:3 * C_WIDTH]
    q_combo = qr_s.reshape(DEC_BATCH, N_COMBO, HEAD_DIM)
    m_p, l_p, o_p, ksum = paged_partials(page_table, q_combo, cache_c_k, cache_c_v, 0)
    c_s = paged_merge(q_combo, kr_s.reshape(DEC_BATCH, N_COMBO, HEAD_DIM),
                      v_s.reshape(DEC_BATCH, N_COMBO, HEAD_DIM), m_p, l_p, o_p, ksum)
    c_s = c_s.reshape(N_SAMPLE, C_WIDTH)

    cd_prm = {"conv_w": cd_d_conv_w[0], "conv_b": row(cd_d_conv_b[0]), "ln_g": row(cd_d_ln_g[0]),
              "ln_b": row(cd_d_ln_b[0])}
    col_a = 3 * C_WIDTH // D_WIDTH
    tiles_b = SEQ // 256
    halo_per_tile = 256 // CONV_HALO

    def prev_map(col):
        return lambda b, t: (jnp.maximum((b * tiles_b + t) * halo_per_tile - 1, 0), col)

    d_p, tail_p = conv_module(h1p, col_a, h1p, h1p, prev_map(col_a), prev_map(col_a + 1), nb=BATCH,
                              n_tiles=tiles_b, tm=256, rt=128, tail=CONV_HALO, xoff=0, prev_is_state=False,
                              prm=cd_prm, name="conv_prompt")
    gl_s = jnp.pad(h1s[:, 3 * C_WIDTH:].reshape(DEC_BATCH, DEC_SEQ, 2 * D_WIDTH), ((0, 0), (0, 8 - DEC_SEQ), (0, 0)))
    gl_s = gl_s.reshape(DEC_BATCH * 8, 2 * D_WIDTH)
    dctx = jnp.pad(state_d_buf[0], ((0, 0), (CONV_HALO - D_BUF, 0), (0, 0))).reshape(DEC_BATCH * CONV_HALO, D_WIDTH)
    d_s, tail_s = conv_module(gl_s, 0, dctx, dctx, lambda b, t: (b, 0), lambda b, t: (b, 0), nb=DEC_BATCH,
                              n_tiles=1, tm=8, rt=8, tail=8, xoff=0, prev_is_state=True, prm=cd_prm,
                              name="conv_sample")
    d_s = d_s.reshape(DEC_BATCH, 8, D_WIDTH)[:, :DEC_SEQ].reshape(N_SAMPLE, D_WIDTH)
    w_out1 = cd_w_out[0].astype(BF16)
    g, b = row(mix_ln_g[1]), row(mix_ln_b[1])
    x3p = matmul_res_ln([c_p, d_p], [w_out1[:C_WIDTH], w_out1[C_WIDTH:]], x2p, g, b, m=N_PROMPT, tm=512,
                        name="cd_out_prompt")
    x3s = matmul_res_ln([c_s, d_s], [w_out1[:C_WIDTH], w_out1[C_WIDTH:]], x2s, g, b, m=N_SAMPLE, tm=N_SAMPLE,
                        name="cd_out_sample")
    y_p, y_s = mem_and_moe(x3p, x3s, 1)

    kv_shape_p = (1, BATCH, SEQ, C_HEADS, HEAD_DIM)
    kv_shape_s = (1, DEC_BATCH, DEC_SEQ, C_HEADS, HEAD_DIM)
    z_p = h0p[:, 2 * A_WIDTH:].reshape(BATCH, SEQ, B_WIDTH)
    z_s = h0s[:, 2 * A_WIDTH:].reshape(DEC_BATCH, DEC_SEQ, B_WIDTH)
    h_s = tail_s.reshape(DEC_BATCH, 8, D_WIDTH)[:, :DEC_SEQ]
    mem_shape = (BATCH, N_MEM, MEM_HEADS, MEM_HEAD_DIM)
    return (y_p.reshape(BATCH, SEQ, D_MODEL),
            y_s.reshape(DEC_BATCH, DEC_SEQ, D_MODEL),
            kr_p.reshape(kv_shape_p),
            h1p[:, 2 * C_WIDTH:3 * C_WIDTH].reshape(kv_shape_p),
            kr_s.reshape(kv_shape_s),
            v_s.reshape(kv_shape_s),
            z_p[:, SEQ - B_BUF:][None],
            jnp.concatenate([state_b_buf[0], z_s], axis=1)[:, DEC_SEQ:][None],
            vn_s.reshape(DEC_BATCH, CHUNK, A_WIDTH)[:, :DEC_SEQ][None],
            tail_p.reshape(BATCH, CONV_HALO, D_WIDTH)[:, CONV_HALO - D_BUF:][None],
            jnp.concatenate([state_d_buf[0], h_s], axis=1)[:, DEC_SEQ:][None],
            jnp.stack([m_.reshape(mem_shape) for m_ in pmk]),
            jnp.stack([m_.reshape(mem_shape) for m_ in pmv]))
```

```python
import functools
import math
from typing import NamedTuple

import jax
import jax.numpy as jnp
from jax import lax
from jax.experimental import pallas as pl
from jax.experimental.pallas import tpu as pltpu

F32 = jnp.float32
BF16 = jnp.bfloat16
HIGHEST = lax.Precision.HIGHEST

D_MODEL = 2048
BATCH = 4
SEQ = 2048
DEPTH = 2
DEC_BATCH = 8
DEC_SEQ = 4
PAST_LEN = 16384
PAGE_SIZE = 128
A_WIDTH = 1024
CHUNK = 128
A_GROUPS = 8
B_WIDTH = 1024
POOL_WINDOWS = (2, 4, 8, 16)
B_GROUP_DIM = B_WIDTH // len(POOL_WINDOWS)
B_BUF = 15
C_HEADS = 8
HEAD_DIM = 128
C_WIDTH = 1024
ROT_DIM = 32
ROPE_THETA = 500000.0
MOBA_BLOCK = 256
MOBA_TOPK = 3
D_WIDTH = 1024
CONV_WIDTH = 31
D_BUF = 30
N_MEM = 256
MEM_HEADS = 4
MEM_HEAD_DIM = 128
MEM_WIDTH = 512
N_EXPERTS = 16
N_EXPERT_GROUPS = 4
EXPERTS_PER_GROUP = 4
MOE_TOPK = 2
D_EXPERT = 1024
ALPHA = (2 * DEPTH) ** 0.25
LN_EPS = 1e-5

N_PROMPT = BATCH * SEQ
N_SAMPLE = DEC_BATCH * DEC_SEQ
N_TOK = N_PROMPT + N_SAMPLE
N_PAST_BLOCKS = PAST_LEN // MOBA_BLOCK
PAGES_PER_BLOCK = MOBA_BLOCK // PAGE_SIZE
N_COMBO = DEC_SEQ * C_HEADS


class MoeCfg(NamedTuple):
    n_tok: int
    tok_tile: int
    tile: int
    rows: int
    n_tiles: int
    precise: bool


def _moe_cfg(n_tok, tile, precise):
    rows = ((n_tok * MOE_TOPK + N_EXPERTS * (tile - 1)) // tile + 1) * tile
    return MoeCfg(n_tok, tile, tile, rows, rows // tile, precise)


MOE_PROMPT = _moe_cfg(N_PROMPT, 256, False)
MOE_SAMPLE = _moe_cfg(N_SAMPLE, N_SAMPLE, True)

VMEM_LIMIT = 56 * 1024 * 1024
VMEM_LIMIT_SMALL = 40 * 1024 * 1024
SMALL_CALL_ROWS = 1024


def _cparams(n_axes, rows=None):
    vmem = VMEM_LIMIT if rows is None or rows > SMALL_CALL_ROWS else VMEM_LIMIT_SMALL
    return pltpu.CompilerParams(dimension_semantics=("arbitrary",) * n_axes, vmem_limit_bytes=vmem)


def _ln(x, g, b):
    mu = jnp.mean(x, axis=-1, keepdims=True)
    xc = x - mu
    var = jnp.mean(xc * xc, axis=-1, keepdims=True)
    return xc * lax.rsqrt(var + LN_EPS) * g + b


def _dot(a, b):
    return jnp.dot(a, b, preferred_element_type=F32)


def _dot_nt(a, b, precision=None):
    return lax.dot_general(a, b, (((1,), (1,)), ((), ())), precision=precision, preferred_element_type=F32)


def _split_bf16(v):
    hi = v.astype(BF16)
    return hi, (v.astype(F32) - hi.astype(F32)).astype(BF16)


def _mm(a, b, precise, nt=False):
    dot = _dot_nt if nt else _dot
    if not precise:
        b0 = b[0] if isinstance(b, tuple) else b
        return dot(a.astype(BF16), b0.astype(BF16))
    ah, al = _split_bf16(a)
    bh, bl = b if isinstance(b, tuple) else _split_bf16(b)
    return dot(ah, bh) + (dot(al, bh) + dot(ah, bl))


def _load_all(refs):
    return tuple(r[...] for r in refs)


def split_weight(w, precise):
    if not precise:
        return (w.astype(BF16),)
    bits = lax.bitcast_convert_type(w, jnp.uint32) & jnp.uint32(0xFFFF0000)
    hi = lax.bitcast_convert_type(bits, F32)
    return hi.astype(BF16), (w - hi).astype(BF16)


def _mm_kernel(x_ref, *refs, n_gelu, precise):
    o_ref = refs[-1]
    acc = _mm(x_ref[...], _load_all(refs[:-1]), precise)
    if n_gelu:
        j = pl.program_id(0)

        @pl.when(j < n_gelu)
        def _():
            o_ref[...] = jax.nn.gelu(acc).astype(o_ref.dtype)

        @pl.when(j >= n_gelu)
        def _():
            o_ref[...] = acc.astype(o_ref.dtype)
    else:
        o_ref[...] = acc.astype(o_ref.dtype)


def matmul(x, w, *, m, tm, tn, xoff=0, n_gelu=0, precise=False, name="mm"):
    k = x.shape[1]
    n = w[0].shape[1]
    return pl.pallas_call(
        functools.partial(_mm_kernel, n_gelu=n_gelu, precise=precise),
        grid=(n // tn, m // tm),
        in_specs=[pl.BlockSpec((tm, k), lambda j, i: (i + xoff, 0))]
        + [pl.BlockSpec((k, tn), lambda j, i: (0, j))] * len(w),
        out_specs=pl.BlockSpec((tm, tn), lambda j, i: (i, j)),
        out_shape=jax.ShapeDtypeStruct((m, n), F32),
        compiler_params=_cparams(2, m),
        name=name,
    )(x, *w)


def _mm_res_ln_kernel(*refs, n_in, precise):
    n_w = 2 if precise else 1
    a_refs = refs[:n_in]
    w_refs = refs[n_in:n_in + n_in * n_w]
    r_ref, g_ref, b_ref, o_ref = refs[n_in + n_in * n_w:]
    acc = None
    for k, a_ref in enumerate(a_refs):
        d = _mm(a_ref[...], _load_all(w_refs[k * n_w:(k + 1) * n_w]), precise)
        acc = d if acc is None else acc + d
    o_ref[...] = _ln(ALPHA * r_ref[...] + acc, g_ref[...], b_ref[...])


def matmul_res_ln(a_list, w_list, resid, g, b, *, m, tm, roff=0, precise=False, name="mm_res_ln"):
    n_in = len(a_list)
    w_flat = [part for w in w_list for part in w]
    in_specs = [pl.BlockSpec((tm, a.shape[1]), lambda i: (i, 0)) for a in a_list]
    in_specs += [pl.BlockSpec(w.shape, lambda i: (0, 0)) for w in w_flat]
    in_specs += [pl.BlockSpec((tm, D_MODEL), lambda i: (i + roff, 0)),
                 pl.BlockSpec((1, D_MODEL), lambda i: (0, 0)),
                 pl.BlockSpec((1, D_MODEL), lambda i: (0, 0))]
    return pl.pallas_call(
        functools.partial(_mm_res_ln_kernel, n_in=n_in, precise=precise),
        grid=(m // tm,),
        in_specs=in_specs,
        out_specs=pl.BlockSpec((tm, D_MODEL), lambda i: (i, 0)),
        out_shape=jax.ShapeDtypeStruct((m, D_MODEL), F32),
        compiler_params=_cparams(1, m),
        name=name,
    )(*a_list, *w_flat, resid, g, b)


def _mixer_ab_kernel(u_ref, v_ref, z_ref, zp_ref, lng_ref, lnb_ref, ws_ref, bs_ref, wg_ref, sc_ref,
                     mix_ref, vn_ref, *, pos0, has_ctx, precise):
    c = pl.program_id(1)
    vn = _ln(v_ref[...], lng_ref[...], lnb_ref[...])
    vn_ref[...] = vn
    u = u_ref[...]
    row = lax.broadcasted_iota(jnp.int32, (CHUNK, CHUNK), 0)
    col = lax.broadcasted_iota(jnp.int32, (CHUNK, CHUNK), 1)
    causal = col <= row
    gd = A_WIDTH // A_GROUPS
    for g in range(A_GROUPS):
        sl = slice(g * gd, (g + 1) * gd)
        w = jnp.where(causal, ws_ref[g], 0.0)
        mixed = _mm(w, vn[:, sl], precise) + bs_ref[:, g:g + 1]
        mix_ref[:, sl] = (u[:, sl] * mixed).astype(mix_ref.dtype)

    z = z_ref[...]
    zp = zp_ref[...]
    if not has_ctx:
        zp = jnp.where(c == 0, 0.0, zp)
    zext = jnp.concatenate([zp, z], axis=0)
    pos = pos0 + c * CHUNK + lax.broadcasted_iota(jnp.int32, (CHUNK, 1), 0)
    for gi, wdw in enumerate(POOL_WINDOWS):
        sl = slice(gi * B_GROUP_DIM, (gi + 1) * B_GROUP_DIM)
        s = zext[:, sl]
        sh = 1
        while sh < wdw:
            s = s + pltpu.roll(s, sh, axis=0)
            sh *= 2
        cnt = jnp.minimum(wdw, pos + 1).astype(F32)
        d = s[16:, :] / cnt - z[:, sl]
        bo = _mm(d, wg_ref[gi], precise) * sc_ref[:, sl]
        mix_ref[:, A_WIDTH + gi * B_GROUP_DIM:A_WIDTH + (gi + 1) * B_GROUP_DIM] = bo.astype(mix_ref.dtype)


def mixer_ab(h, zprev, zprev_map, *, nb, n_chunks, pos0, has_ctx, prm, name, precise=False):
    m = nb * n_chunks * CHUNK
    row = lambda b, c: b * n_chunks + c
    const2 = lambda b, c: (0, 0)
    return pl.pallas_call(
        functools.partial(_mixer_ab_kernel, pos0=pos0, has_ctx=has_ctx, precise=precise),
        grid=(nb, n_chunks),
        in_specs=[pl.BlockSpec((CHUNK, A_WIDTH), lambda b, c: (row(b, c), 0)),
                  pl.BlockSpec((CHUNK, A_WIDTH), lambda b, c: (row(b, c), 1)),
                  pl.BlockSpec((CHUNK, B_WIDTH), lambda b, c: (row(b, c), 2)),
                  pl.BlockSpec((16, B_WIDTH), zprev_map),
                  pl.BlockSpec((1, A_WIDTH), const2),
                  pl.BlockSpec((1, A_WIDTH), const2),
                  pl.BlockSpec((A_GROUPS, CHUNK, CHUNK), lambda b, c: (0, 0, 0)),
                  pl.BlockSpec((CHUNK, A_GROUPS), const2),
                  pl.BlockSpec((len(POOL_WINDOWS), B_GROUP_DIM, B_GROUP_DIM), lambda b, c: (0, 0, 0)),
                  pl.BlockSpec((1, B_WIDTH), const2)],
        out_specs=[pl.BlockSpec((CHUNK, A_WIDTH + B_WIDTH), lambda b, c: (row(b, c), 0)),
                   pl.BlockSpec((CHUNK, A_WIDTH), lambda b, c: (row(b, c), 0))],
        out_shape=[jax.ShapeDtypeStruct((m, A_WIDTH + B_WIDTH), F32 if precise else BF16),
                   jax.ShapeDtypeStruct((m, A_WIDTH), F32)],
        compiler_params=_cparams(2, m),
        name=name,
    )(h, h, h, zprev, prm["ln_g"], prm["ln_b"], prm["ws"], prm["bs_t"], prm["wg"], prm["scale"])


def _rope_kernel(q_ref, k_ref, c_ref, sa_ref, sb_ref, qo_ref, ko_ref):
    c = c_ref[...]
    sa = sa_ref[...]
    sb = sb_ref[...]
    half = ROT_DIM // 2
    for h in range(C_HEADS):
        sl = slice(h * HEAD_DIM, (h + 1) * HEAD_DIM)
        for src, dst in ((q_ref, qo_ref), (k_ref, ko_ref)):
            x = src[:, sl]
            dst[:, sl] = (x * c + pltpu.roll(x, half, axis=1) * sa
                          + pltpu.roll(x, HEAD_DIM - half, axis=1) * sb)


def rope_qk(h, tabs, *, m, tm, xoff, tab_blocks, name):
    tmap = lambda i: (i % tab_blocks, 0)
    return pl.pallas_call(
        _rope_kernel,
        grid=(m // tm,),
        in_specs=[pl.BlockSpec((tm, C_WIDTH), lambda i: (i + xoff, 0)),
                  pl.BlockSpec((tm, C_WIDTH), lambda i: (i + xoff, 1)),
                  pl.BlockSpec((tm, HEAD_DIM), tmap),
                  pl.BlockSpec((tm, HEAD_DIM), tmap),
                  pl.BlockSpec((tm, HEAD_DIM), tmap)],
        out_specs=[pl.BlockSpec((tm, C_WIDTH), lambda i: (i, 0)),
                   pl.BlockSpec((tm, C_WIDTH), lambda i: (i, 0))],
        out_shape=[jax.ShapeDtypeStruct((m, C_WIDTH), F32),
                   jax.ShapeDtypeStruct((m, C_WIDTH), F32)],
        compiler_params=_cparams(1, m),
        name=name,
    )(h, h, *tabs)


def _moba_kernel(q_ref, k_ref, v_ref, o_ref, s_ref, mp_ref, lp_ref, acc_ref, *, n_blocks):
    i = pl.program_id(2)
    blk_rows = MOBA_BLOCK
    scale = HEAD_DIM ** -0.5
    q = q_ref[...]
    qb = q.astype(BF16)
    kmean = jnp.concatenate(
        [jnp.mean(k_ref[j * blk_rows:(j + 1) * blk_rows, :], axis=0, keepdims=True) for j in range(n_blocks)],
        axis=0)
    bscore = _dot_nt(q, kmean, precision=HIGHEST)
    blk = lax.broadcasted_iota(jnp.int32, (blk_rows, n_blocks), 1)
    past = blk < i
    bs = jnp.where(past, bscore, -jnp.inf)
    rank = jnp.zeros((blk_rows, n_blocks), jnp.int32)
    for m_ in range(n_blocks):
        cm = bs[:, m_:m_ + 1]
        rank = rank + ((cm > bs) | ((cm == bs) & (m_ < blk))).astype(jnp.int32)
    sel = (past & (rank < MOBA_TOPK)).astype(F32)

    r_i = lax.broadcasted_iota(jnp.int32, (blk_rows, blk_rows), 0)
    c_i = lax.broadcasted_iota(jnp.int32, (blk_rows, blk_rows), 1)
    causal = (c_i <= r_i).astype(F32)
    half = blk_rows // 2
    mp_ref[...] = jnp.full(mp_ref.shape, -jnp.inf, F32)
    lp_ref[...] = jnp.zeros(lp_ref.shape, F32)
    acc_ref[...] = jnp.zeros(acc_ref.shape, F32)

    for j in range(n_blocks):
        @pl.when(j <= i)
        def _():
            cols = slice(j * blk_rows, (j + 1) * blk_rows)
            sj = _dot_nt(qb, k_ref[cols, :].astype(BF16)) * scale
            allowed = jnp.where(i == j, causal, sel[:, j:j + 1])
            sj = jnp.where(allowed > 0.0, sj, -jnp.inf)
            s_ref[:, cols] = sj
            mp_ref[...] = jnp.maximum(mp_ref[...], jnp.maximum(sj[:, :half], sj[:, half:]))

    m = jnp.max(mp_ref[...], axis=-1, keepdims=True)

    for j in range(n_blocks):
        @pl.when(j <= i)
        def _():
            cols = slice(j * blk_rows, (j + 1) * blk_rows)
            pj = jnp.exp(s_ref[:, cols] - m)
            lp_ref[...] += pj[:, :half] + pj[:, half:]
            acc_ref[...] += _dot(pj.astype(BF16), v_ref[cols, :].astype(BF16))

    l = jnp.sum(lp_ref[...], axis=-1, keepdims=True)
    o_ref[...] = (acc_ref[...] / l).astype(o_ref.dtype)


def moba_prompt(q_rot, k_rot, h, *, nb, seq, name="moba_prompt"):
    n_blocks = seq // MOBA_BLOCK
    v_col0 = 2 * C_WIDTH // HEAD_DIM
    return pl.pallas_call(
        functools.partial(_moba_kernel, n_blocks=n_blocks),
        grid=(nb, C_HEADS, n_blocks),
        in_specs=[pl.BlockSpec((MOBA_BLOCK, HEAD_DIM), lambda b, hh, i: (b * n_blocks + i, hh)),
                  pl.BlockSpec((seq, HEAD_DIM), lambda b, hh, i: (b, hh)),
                  pl.BlockSpec((seq, HEAD_DIM), lambda b, hh, i: (b, v_col0 + hh))],
        out_specs=pl.BlockSpec((MOBA_BLOCK, HEAD_DIM), lambda b, hh, i: (b * n_blocks + i, hh)),
        out_shape=jax.ShapeDtypeStruct((nb * seq, C_WIDTH), BF16),
        scratch_shapes=[pltpu.VMEM((MOBA_BLOCK, seq), F32), pltpu.VMEM((MOBA_BLOCK, MOBA_BLOCK // 2), F32),
                        pltpu.VMEM((MOBA_BLOCK, MOBA_BLOCK // 2), F32), pltpu.VMEM((MOBA_BLOCK, HEAD_DIM), F32)],
        compiler_params=_cparams(3),
        name=name,
    )(q_rot, k_rot, h)


CONV_HALO = 32


def _conv_kernel(ga_ref, gg_ref, pa_ref, pg_ref, w_ref, bdw_ref, lng_ref, lnb_ref,
                 o_ref, tail_ref, ext_ref, y_ref, *, tm, rt, tail, prev_is_state):
    t = pl.program_id(1)
    hcur = ga_ref[...] * jax.nn.sigmoid(gg_ref[...])
    if prev_is_state:
        hprev = pa_ref[...]
    else:
        hprev = pa_ref[...] * jax.nn.sigmoid(pg_ref[...])
        hprev = jnp.where(t == 0, 0.0, hprev)
    ext_ref[0:CONV_HALO, :] = hprev
    ext_ref[CONV_HALO:CONV_HALO + tm, :] = hcur
    tail_ref[...] = hcur[tm - tail:, :]
    off = CONV_HALO - D_BUF
    for cc in range(D_WIDTH // 128):
        cs = slice(cc * 128, (cc + 1) * 128)
        for rc in range(tm // rt):
            r0 = rc * rt
            acc = jnp.zeros((rt, 128), F32)
            for j in range(CONV_WIDTH):
                acc = acc + w_ref[j:j + 1, cs] * ext_ref[r0 + off + j:r0 + off + j + rt, cs]
            y_ref[r0:r0 + rt, cs] = acc + bdw_ref[:, cs]
    y = _ln(y_ref[...], lng_ref[...], lnb_ref[...])
    o_ref[...] = (y * jax.nn.sigmoid(y)).astype(o_ref.dtype)


def conv_module(h, col_a, prev_a, prev_g, prev_map_a, prev_map_g, *, nb, n_tiles, tm, rt, tail, xoff,
                prev_is_state, prm, name, out_dtype=BF16):
    m = nb * n_tiles * tm
    row = lambda b, t: b * n_tiles + t + xoff
    const2 = lambda b, t: (0, 0)
    return pl.pallas_call(
        functools.partial(_conv_kernel, tm=tm, rt=rt, tail=tail, prev_is_state=prev_is_state),
        grid=(nb, n_tiles),
        in_specs=[pl.BlockSpec((tm, D_WIDTH), lambda b, t: (row(b, t), col_a)),
                  pl.BlockSpec((tm, D_WIDTH), lambda b, t: (row(b, t), col_a + 1)),
                  pl.BlockSpec((CONV_HALO, D_WIDTH), prev_map_a),
                  pl.BlockSpec((CONV_HALO, D_WIDTH), prev_map_g),
                  pl.BlockSpec((CONV_WIDTH, D_WIDTH), const2),
                  pl.BlockSpec((1, D_WIDTH), const2),
                  pl.BlockSpec((1, D_WIDTH), const2),
                  pl.BlockSpec((1, D_WIDTH), const2)],
        out_specs=[pl.BlockSpec((tm, D_WIDTH), lambda b, t: (b * n_tiles + t, 0)),
                   pl.BlockSpec((tail, D_WIDTH), lambda b, t: (b, 0))],
        out_shape=[jax.ShapeDtypeStruct((m, D_WIDTH), out_dtype),
                   jax.ShapeDtypeStruct((nb * tail, D_WIDTH), F32)],
        scratch_shapes=[pltpu.VMEM((CONV_HALO + tm, D_WIDTH), F32), pltpu.VMEM((tm, D_WIDTH), F32)],
        compiler_params=_cparams(2, m),
        name=name,
    )(h, h, prev_a, prev_g, prm["conv_w"], prm["conv_b"], prm["ln_g"], prm["ln_b"])


PAGED_BLOCKS_PER_STEP = 4
PAGED_PAGES_PER_STEP = PAGED_BLOCKS_PER_STEP * PAGES_PER_BLOCK


def _paged_partial_kernel(pt_ref, q_ref, *refs):
    del pt_ref
    pages = refs[:2 * PAGED_PAGES_PER_STEP]
    m_ref, l_ref, o_ref, ks_ref = refs[2 * PAGED_PAGES_PER_STEP:]
    rows = PAGE_SIZE * C_HEADS
    scale = HEAD_DIM ** -0.5
    k_refs = pages[:PAGED_PAGES_PER_STEP]
    v_refs = pages[PAGED_PAGES_PER_STEP:]
    qb = q_ref[...].astype(BF16)
    c_i = lax.broadcasted_iota(jnp.int32, (N_COMBO, rows), 0)
    l_i = lax.broadcasted_iota(jnp.int32, (N_COMBO, rows), 1)
    same_head = (l_i & (C_HEADS - 1)) == (c_i & (C_HEADS - 1))
    for blk in range(PAGED_BLOCKS_PER_STEP):
        ks = None
        s = []
        for pg in range(PAGES_PER_BLOCK):
            k = k_refs[blk * PAGES_PER_BLOCK + pg][...]
            ksum = jnp.sum(k, axis=0)
            ks = ksum if ks is None else ks + ksum
            sp = _dot_nt(qb, k.reshape(rows, HEAD_DIM).astype(BF16)) * scale
            s.append(jnp.where(same_head, sp, -jnp.inf))
        ks_ref[blk] = ks
        m = jnp.max(jnp.maximum(s[0], s[1]), axis=-1, keepdims=True)
        l = None
        o = None
        for pg in range(PAGES_PER_BLOCK):
            p = jnp.exp(s[pg] - m)
            v = v_refs[blk * PAGES_PER_BLOCK + pg][...].reshape(rows, HEAD_DIM).astype(BF16)
            lp = jnp.sum(p, axis=-1, keepdims=True)
            op = _dot(p.astype(BF16), v)
            l = lp if l is None else l + lp
            o = op if o is None else o + op
        m_ref[blk] = jnp.broadcast_to(m, (N_COMBO, HEAD_DIM))
        l_ref[blk] = jnp.broadcast_to(l, (N_COMBO, HEAD_DIM))
        o_ref[blk] = o


def paged_partials(page_table, q_combo, cache_k, cache_v, layer):
    page_block = (None, None, PAGE_SIZE, C_HEADS, HEAD_DIM)
    nbs = PAGED_BLOCKS_PER_STEP

    def page_map(which):
        return lambda b, n, pt: (layer, pt[b, PAGED_PAGES_PER_STEP * n + which], 0, 0, 0)

    page_specs = [pl.BlockSpec(page_block, page_map(w)) for w in range(PAGED_PAGES_PER_STEP)]
    part_shape = jax.ShapeDtypeStruct((DEC_BATCH, N_PAST_BLOCKS, N_COMBO, HEAD_DIM), F32)
    part_spec = pl.BlockSpec((None, nbs, N_COMBO, HEAD_DIM), lambda b, n, pt: (b, n, 0, 0))
    return pl.pallas_call(
        _paged_partial_kernel,
        grid_spec=pltpu.PrefetchScalarGridSpec(
            num_scalar_prefetch=1,
            grid=(DEC_BATCH, N_PAST_BLOCKS // nbs),
            in_specs=[pl.BlockSpec((None, N_COMBO, HEAD_DIM), lambda b, n, pt: (b, 0, 0))] + page_specs + page_specs,
            out_specs=[part_spec, part_spec, part_spec,
                       pl.BlockSpec((None, nbs, C_HEADS, HEAD_DIM), lambda b, n, pt: (b, n, 0, 0))]),
        out_shape=[part_shape, part_shape, part_shape,
                   jax.ShapeDtypeStruct((DEC_BATCH, N_PAST_BLOCKS, C_HEADS, HEAD_DIM), F32)],
        compiler_params=_cparams(2, N_SAMPLE),
        name="paged_partials",
    )(page_table, q_combo, *([cache_k] * PAGED_PAGES_PER_STEP), *([cache_v] * PAGED_PAGES_PER_STEP))


def _paged_merge_kernel(q_ref, kn_ref, vn_ref, m_ref, l_ref, o_ref, ks_ref, out_ref):
    nb = N_PAST_BLOCKS
    scale = HEAD_DIM ** -0.5
    q = q_ref[...]
    kmean = ks_ref[...] * (1.0 / MOBA_BLOCK)
    kmean = jnp.concatenate([kmean] * DEC_SEQ, axis=1)
    bs = jnp.sum(kmean * q[None], axis=-1, keepdims=True)
    work = jnp.broadcast_to(bs, (nb, N_COMBO, HEAD_DIM))
    n_i = lax.broadcasted_iota(jnp.int32, (nb, N_COMBO, HEAD_DIM), 0)
    sel = n_i < 0
    for _ in range(MOBA_TOPK):
        mx = jnp.max(work, axis=0, keepdims=True)
        first = jnp.min(jnp.where(work == mx, n_i, nb), axis=0, keepdims=True)
        hit = n_i == first
        sel = sel | hit
        work = jnp.where(hit, -jnp.inf, work)

    s = _dot_nt(q.astype(BF16), kn_ref[...].astype(BF16)) * scale
    c_i = lax.broadcasted_iota(jnp.int32, s.shape, 0)
    l_i = lax.broadcasted_iota(jnp.int32, s.shape, 1)
    ok = ((l_i & (C_HEADS - 1)) == (c_i & (C_HEADS - 1))) & ((l_i >> 3) <= (c_i >> 3))
    s = jnp.where(ok, s, -jnp.inf)
    m_own = jnp.max(s, axis=-1, keepdims=True)
    p = jnp.exp(s - m_own)
    l_own = jnp.sum(p, axis=-1, keepdims=True)
    o_own = _dot(p.astype(BF16), vn_ref[...].astype(BF16))

    mp = m_ref[...]
    m_all = jnp.maximum(jnp.max(jnp.where(sel, mp, -jnp.inf), axis=0), m_own)
    w = jnp.where(sel, jnp.exp(mp - m_all[None]), 0.0)
    w_own = jnp.exp(m_own - m_all)
    den = jnp.sum(w * l_ref[...], axis=0) + w_own * l_own
    num = jnp.sum(w * o_ref[...], axis=0) + w_own * o_own
    out_ref[...] = num / den


def paged_merge(q_combo, k_new, v_new, m_p, l_p, o_p, ksum):
    combo = pl.BlockSpec((None, N_COMBO, HEAD_DIM), lambda b: (b, 0, 0))
    part = pl.BlockSpec((None, N_PAST_BLOCKS, N_COMBO, HEAD_DIM), lambda b: (b, 0, 0, 0))
    return pl.pallas_call(
        _paged_merge_kernel,
        grid=(DEC_BATCH,),
        in_specs=[combo, combo, combo, part, part, part,
                  pl.BlockSpec((None, N_PAST_BLOCKS, C_HEADS, HEAD_DIM), lambda b: (b, 0, 0, 0))],
        out_specs=combo,
        out_shape=jax.ShapeDtypeStruct((DEC_BATCH, N_COMBO, HEAD_DIM), F32),
        compiler_params=_cparams(1, N_SAMPLE),
        name="paged_merge",
    )(q_combo, k_new, v_new, m_p, l_p, o_p, ksum)


def _memattn_kernel(x_ref, mk_ref, mv_ref, g_ref, b_ref, *refs, nb_tile, rows_per_b, precise):
    n_w = 2 if precise else 1
    wq = _load_all(refs[:n_w])
    wo = _load_all(refs[n_w:2 * n_w])
    o_ref = refs[2 * n_w]
    scale = MEM_HEAD_DIM ** -0.5
    x = x_ref[...]
    tm = x.shape[0]
    q = _mm(x, wq, precise)
    row_b = lax.broadcasted_iota(jnp.int32, (tm, 1), 0) >> int(math.log2(rows_per_b))
    heads = []
    for hh in range(MEM_HEADS):
        sl = slice(hh * MEM_HEAD_DIM, (hh + 1) * MEM_HEAD_DIM)
        qh = q[:, sl]
        oh = None
        for bb in range(nb_tile):
            s = _mm(qh, mk_ref[bb, :, sl], precise, nt=True) * scale
            s = s - jnp.max(s, axis=-1, keepdims=True)
            p = jnp.exp(s)
            p = p / jnp.sum(p, axis=-1, keepdims=True)
            ob = _mm(p, mv_ref[bb, :, sl], precise)
            if nb_tile > 1:
                ob = jnp.where(row_b == bb, ob, 0.0)
            oh = ob if oh is None else oh + ob
        heads.append(oh)
    o = jnp.concatenate(heads, axis=-1)
    y = ALPHA * x + _mm(o, wo, precise)
    o_ref[...] = _ln(y, g_ref[...], b_ref[...])


def memory_attention_ln(x, wq, mk, mv, wo, g, b, *, m, tm, rows_per_b, name, precise=False, b0=0):
    if rows_per_b >= tm:
        nb_tile = 1
        tiles_per_b = rows_per_b // tm
        kv_map = lambda i: (b0 + i // tiles_per_b, 0, 0)
    else:
        nb_tile = tm // rows_per_b
        kv_map = lambda i: (i, 0, 0)
    const2 = lambda i: (0, 0)
    return pl.pallas_call(
        functools.partial(_memattn_kernel, nb_tile=nb_tile, rows_per_b=rows_per_b, precise=precise),
        grid=(m // tm,),
        in_specs=[pl.BlockSpec((tm, D_MODEL), lambda i: (i, 0)),
                  pl.BlockSpec((nb_tile, N_MEM, MEM_WIDTH), kv_map),
                  pl.BlockSpec((nb_tile, N_MEM, MEM_WIDTH), kv_map),
                  pl.BlockSpec((1, D_MODEL), const2),
                  pl.BlockSpec((1, D_MODEL), const2)]
        + [pl.BlockSpec((D_MODEL, MEM_WIDTH), const2)] * len(wq)
        + [pl.BlockSpec((MEM_WIDTH, D_MODEL), const2)] * len(wo),
        out_specs=pl.BlockSpec((tm, D_MODEL), lambda i: (i, 0)),
        out_shape=jax.ShapeDtypeStruct((m, D_MODEL), F32),
        compiler_params=_cparams(1, m),
        name=name,
    )(x, mk, mv, g, b, *wq, *wo)


def _router_kernel(x_ref, xt_ref, w_ref, b_ref, o_ref, cnt_ref, run_ref, *, tiles_per_b):
    step = pl.program_id(0)
    tm = x_ref.shape[0]

    @pl.when(step == 0)
    def _():
        run_ref[...] = jnp.zeros(run_ref.shape, F32)

    x = x_ref[...]
    if tiles_per_b:
        is_last = lax.rem(step, tiles_per_b) == tiles_per_b - 1
        x = jnp.concatenate([x[:tm - CHUNK], jnp.where(is_last, xt_ref[...], x[tm - CHUNK:])], axis=0)
    xh, xl = _split_bf16(x)
    wh, wl = _split_bf16(w_ref[...])
    logits = _dot_nt(wh, xh) + (_dot_nt(wh, xl) + _dot_nt(wl, xh)) + b_ref[...]
    logits = logits - jnp.max(logits, axis=0, keepdims=True)
    e = jnp.exp(logits)
    probs = e / jnp.sum(e, axis=0, keepdims=True)
    p = [probs[j:j + 1, :] for j in range(N_EXPERTS)]
    gbest = None
    gsel = None
    for g in range(N_EXPERT_GROUPS):
        a, b_, c, d = p[4 * g:4 * g + 4]
        hi1, lo1 = jnp.maximum(a, b_), jnp.minimum(a, b_)
        hi2, lo2 = jnp.maximum(c, d), jnp.minimum(c, d)
        gs = jnp.maximum(hi1, hi2) + jnp.maximum(jnp.minimum(hi1, hi2), jnp.maximum(lo1, lo2))
        if g == 0:
            gbest, gsel = gs, jnp.zeros(gs.shape, jnp.int32)
        else:
            better = gs > gbest
            gbest = jnp.where(better, gs, gbest)
            gsel = jnp.where(better, g, gsel)
    cand = [jnp.where(gsel == j // EXPERTS_PER_GROUP, p[j], -1.0) for j in range(N_EXPERTS)]
    v1 = cand[0]
    i1 = jnp.zeros(v1.shape, jnp.int32)
    for j in range(1, N_EXPERTS):
        better = cand[j] > v1
        v1 = jnp.where(better, cand[j], v1)
        i1 = jnp.where(better, j, i1)
    v2 = jnp.full(v1.shape, -2.0, F32)
    i2 = jnp.zeros(v1.shape, jnp.int32)
    for j in range(N_EXPERTS):
        better = (cand[j] > v2) & (i1 != j)
        v2 = jnp.where(better, cand[j], v2)
        i2 = jnp.where(better, j, i2)
    tot = v1 + v2

    e_i = lax.broadcasted_iota(jnp.int32, (N_EXPERTS, tm), 0)
    pick1 = e_i == i1
    pick2 = e_i == i2
    onehot = (pick1 | pick2).astype(F32)
    t_r = lax.broadcasted_iota(jnp.int32, (tm, tm), 0)
    t_c = lax.broadcasted_iota(jnp.int32, (tm, tm), 1)
    earlier = (t_r < t_c).astype(BF16)
    before = _dot(onehot.astype(BF16), earlier) + run_ref[:, 0:1]
    rank1 = jnp.sum(jnp.where(pick1, before, 0.0), axis=0, keepdims=True)
    rank2 = jnp.sum(jnp.where(pick2, before, 0.0), axis=0, keepdims=True)
    run = run_ref[...] + jnp.sum(onehot, axis=1, keepdims=True)
    run_ref[...] = run
    cnt_ref[...] = run

    row = lax.broadcasted_iota(jnp.int32, o_ref.shape, 0)
    vals = (i1.astype(F32), i2.astype(F32), v1 / tot, v2 / tot, rank1, rank2)
    out = jnp.zeros(o_ref.shape, F32)
    for r, v in enumerate(vals):
        out = jnp.where(row == r, v, out)
    o_ref[...] = out


def router(x, x_tail, w_t, b, *, tm, name):
    m = x.shape[0]
    tiles_per_b = 0 if x_tail is None else SEQ // tm
    if x_tail is None:
        x_tail, tail_spec = x, pl.BlockSpec((min(CHUNK, tm), D_MODEL), lambda i: (0, 0))
    else:
        tail_spec = pl.BlockSpec((CHUNK, D_MODEL), lambda i: (i // tiles_per_b, 0))
    return pl.pallas_call(
        functools.partial(_router_kernel, tiles_per_b=tiles_per_b),
        grid=(m // tm,),
        in_specs=[pl.BlockSpec((tm, D_MODEL), lambda i: (i, 0)),
                  tail_spec,
                  pl.BlockSpec((N_EXPERTS, D_MODEL), lambda i: (0, 0)),
                  pl.BlockSpec((N_EXPERTS, 1), lambda i: (0, 0))],
        out_specs=[pl.BlockSpec((8, tm), lambda i: (0, i)),
                   pl.BlockSpec((N_EXPERTS, 128), lambda i: (0, 0))],
        out_shape=[jax.ShapeDtypeStruct((8, m), F32),
                   jax.ShapeDtypeStruct((N_EXPERTS, 128), F32)],
        scratch_shapes=[pltpu.VMEM((N_EXPERTS, 128), F32)],
        compiler_params=_cparams(1, m),
        name=name,
    )(x, x_tail, w_t, b)


def _dispatch_kernel(d0_ref, d1_ref, starts_ref, ends_ref, nv_ref, x_ref, o_hbm, zero_ref, sem, *, cfg):
    i = pl.program_id(0)
    tm = cfg.tok_tile

    def fill_copy(row0):
        return pltpu.make_async_copy(zero_ref, o_hbm.at[pl.ds(row0, cfg.tile)], sem)

    @pl.when(i == 0)
    def _():
        zero_ref[...] = jnp.zeros(zero_ref.shape, zero_ref.dtype)
        for wait in (False, True):
            for e in range(N_EXPERTS):
                @pl.when(ends_ref[e] > starts_ref[e])
                def _():
                    cp = fill_copy(pl.multiple_of(ends_ref[e] - cfg.tile, cfg.tile))
                    cp.wait() if wait else cp.start()

            def tail(t, carry):
                cp = fill_copy(pl.multiple_of(t * cfg.tile, cfg.tile))
                cp.wait() if wait else cp.start()
                return carry

            lax.fori_loop(nv_ref[0], cfg.n_tiles, tail, 0)

    base = i * tm

    def body(r, carry):
        pltpu.make_async_copy(x_ref.at[pl.ds(r, 1)], o_hbm.at[pl.ds(d0_ref[base + r], 1)], sem).start()
        pltpu.make_async_copy(x_ref.at[pl.ds(r, 1)], o_hbm.at[pl.ds(d1_ref[base + r], 1)], sem).start()
        return carry

    lax.fori_loop(0, tm, body, 0)
    for _ in range(MOE_TOPK):
        pltpu.make_async_copy(x_ref, o_hbm.at[pl.ds(0, tm)], sem).wait()


def moe_dispatch(d0, d1, starts, ends, n_valid, x, cfg, name):
    tm = cfg.tok_tile
    return pl.pallas_call(
        functools.partial(_dispatch_kernel, cfg=cfg),
        grid_spec=pltpu.PrefetchScalarGridSpec(
            num_scalar_prefetch=5,
            grid=(cfg.n_tok // tm,),
            in_specs=[pl.BlockSpec((tm, D_MODEL), lambda i, *_: (i, 0))],
            out_specs=pl.BlockSpec(memory_space=pl.ANY),
            scratch_shapes=[pltpu.VMEM((cfg.tile, D_MODEL), F32), pltpu.SemaphoreType.DMA]),
        out_shape=jax.ShapeDtypeStruct((cfg.rows, D_MODEL), F32),
        compiler_params=_cparams(1, cfg.n_tok),
        name=name,
    )(d0, d1, starts, ends, n_valid, x)


def _expert_weights(first, w_refs, scratch_refs, precise):
    @pl.when(first)
    def _():
        for k, w_ref in enumerate(w_refs):
            if precise:
                hi, lo = _split_bf16(w_ref[...])
                scratch_refs[2 * k][...] = hi
                scratch_refs[2 * k + 1][...] = lo
            else:
                scratch_refs[k][...] = w_ref[...].astype(BF16)


def _expert_dot(x, scratch_refs, k, precise):
    if not precise:
        return _dot(x.astype(BF16), scratch_refs[k][...])
    xh, xl = _split_bf16(x)
    wh, wl = scratch_refs[2 * k][...], scratch_refs[2 * k + 1][...]
    return _dot(xh, wh) + (_dot(xl, wh) + _dot(xh, wl))


def _moe_up_kernel(te_ref, first_ref, nv_ref, x_ref, wg_ref, wu_ref, hh_ref, *scratch, precise, f_tiles):
    i = pl.program_id(0)

    @pl.when(i < nv_ref[0])
    def _():
        _expert_weights((first_ref[i] == 1) | (f_tiles > 1), (wg_ref, wu_ref), scratch, precise)
        x = x_ref[...]
        hg = _expert_dot(x, scratch, 0, precise)
        hu = _expert_dot(x, scratch, 1, precise)
        hh_ref[...] = (hg * jax.nn.sigmoid(hg) * hu).astype(hh_ref.dtype)

    @pl.when(i >= nv_ref[0])
    def _():
        hh_ref[...] = jnp.zeros(hh_ref.shape, hh_ref.dtype)


def _moe_down_kernel(te_ref, first_ref, nv_ref, hh_ref, wd_ref, y_ref, *scratch, precise):
    i = pl.program_id(0)

    @pl.when(i < nv_ref[0])
    def _():
        _expert_weights(first_ref[i] == 1, (wd_ref,), scratch, precise)
        y_ref[...] = _expert_dot(hh_ref[...], scratch, 0, precise)

    @pl.when(i >= nv_ref[0])
    def _():
        y_ref[...] = jnp.zeros(y_ref.shape, y_ref.dtype)


def moe_experts(tile_expert, tile_first, n_valid, x_sorted, w_gate, w_up, w_down, layer, cfg, name):
    tm = cfg.tile
    n_copies = 2 if cfg.precise else 1
    f_tiles = 2 if cfg.precise else 1
    fw = D_EXPERT // f_tiles
    row_map = lambda i, te, fi, nv: (jnp.minimum(i, nv[0] - 1), 0)
    wmap = lambda i, te, fi, nv: (layer, te[i], 0, 0)
    hh = pl.pallas_call(
        functools.partial(_moe_up_kernel, precise=cfg.precise, f_tiles=f_tiles),
        grid_spec=pltpu.PrefetchScalarGridSpec(
            num_scalar_prefetch=3,
            grid=(cfg.n_tiles, f_tiles),
            in_specs=[pl.BlockSpec((tm, D_MODEL), lambda i, f, te, fi, nv: (jnp.minimum(i, nv[0] - 1), 0)),
                      pl.BlockSpec((None, None, D_MODEL, fw), lambda i, f, te, fi, nv: (layer, te[i], 0, f)),
                      pl.BlockSpec((None, None, D_MODEL, fw), lambda i, f, te, fi, nv: (layer, te[i], 0, f))],
            out_specs=pl.BlockSpec((tm, fw), lambda i, f, te, fi, nv: (i, f)),
            scratch_shapes=[pltpu.VMEM((D_MODEL, fw), BF16)] * (2 * n_copies)),
        out_shape=jax.ShapeDtypeStruct((cfg.rows, D_EXPERT), F32 if cfg.precise else BF16),
        compiler_params=_cparams(2, cfg.n_tok),
        name=name + "_up",
    )(tile_expert, tile_first, n_valid, x_sorted, w_gate, w_up)
    return pl.pallas_call(
        functools.partial(_moe_down_kernel, precise=cfg.precise),
        grid_spec=pltpu.PrefetchScalarGridSpec(
            num_scalar_prefetch=3,
            grid=(cfg.n_tiles,),
            in_specs=[pl.BlockSpec((tm, D_EXPERT), row_map),
                      pl.BlockSpec((None, None, D_EXPERT, D_MODEL), wmap)],
            out_specs=pl.BlockSpec((tm, D_MODEL), lambda i, te, fi, nv: (i, 0)),
            scratch_shapes=[pltpu.VMEM((D_EXPERT, D_MODEL), BF16)] * n_copies),
        out_shape=jax.ShapeDtypeStruct((cfg.rows, D_MODEL), F32),
        compiler_params=_cparams(1, cfg.n_tok),
        name=name + "_down",
    )(tile_expert, tile_first, n_valid, hh, w_down)


def _combine_kernel(d0_ref, d1_ref, x_ref, w_ref, g_ref, b_ref, y_hbm, o_ref, buf_ref, sem, *, tm):
    base = pl.program_id(0) * tm

    def body(r, carry):
        pltpu.make_async_copy(y_hbm.at[pl.ds(d0_ref[base + r], 1)], buf_ref.at[0, pl.ds(r, 1)], sem).start()
        pltpu.make_async_copy(y_hbm.at[pl.ds(d1_ref[base + r], 1)], buf_ref.at[1, pl.ds(r, 1)], sem).start()
        return carry

    lax.fori_loop(0, tm, body, 0)
    for slot in range(2):
        pltpu.make_async_copy(y_hbm.at[pl.ds(0, tm)], buf_ref.at[slot], sem).wait()
    w = w_ref[...]
    y = ALPHA * x_ref[...] + w[:, 0:1] * buf_ref[0] + w[:, 1:2] * buf_ref[1]
    o_ref[...] = _ln(y, g_ref[...], b_ref[...])


def moe_combine_ln(d0, d1, x, w_tok, g, b, y_sorted, *, tm, name):
    m = x.shape[0]
    return pl.pallas_call(
        functools.partial(_combine_kernel, tm=tm),
        grid_spec=pltpu.PrefetchScalarGridSpec(
            num_scalar_prefetch=2,
            grid=(m // tm,),
            in_specs=[pl.BlockSpec((tm, D_MODEL), lambda i, a, c: (i, 0)),
                      pl.BlockSpec((tm, MOE_TOPK), lambda i, a, c: (i, 0)),
                      pl.BlockSpec((1, D_MODEL), lambda i, a, c: (0, 0)),
                      pl.BlockSpec((1, D_MODEL), lambda i, a, c: (0, 0)),
                      pl.BlockSpec(memory_space=pl.ANY)],
            out_specs=pl.BlockSpec((tm, D_MODEL), lambda i, a, c: (i, 0)),
            scratch_shapes=[pltpu.VMEM((2, tm, D_MODEL), F32), pltpu.SemaphoreType.DMA]),
        out_shape=jax.ShapeDtypeStruct((m, D_MODEL), F32),
        compiler_params=_cparams(1, m),
        name=name,
    )(d0, d1, x, w_tok, g, b, y_sorted)


def _routing_plan(route, counts, cfg):
    counts = counts[:, 0].astype(jnp.int32)
    padded = ((counts + cfg.tile - 1) // cfg.tile) * cfg.tile
    ends = jnp.cumsum(padded)
    starts = ends - padded
    experts = jnp.arange(N_EXPERTS, dtype=jnp.int32)[:, None]

    def slot_rows(k):
        picked = route[k].astype(jnp.int32)[None, :] == experts
        return jnp.sum(jnp.where(picked, starts[:, None], 0), axis=0) + route[4 + k].astype(jnp.int32)

    tile_start = jnp.arange(cfg.n_tiles, dtype=jnp.int32) * cfg.tile
    tile_expert = jnp.minimum(jnp.sum((tile_start[:, None] >= ends[None, :]).astype(jnp.int32), axis=1),
                              N_EXPERTS - 1)
    start_of_tile_expert = jnp.sum(jnp.where(tile_expert[:, None] == experts.T, starts[None, :], 0), axis=1)
    tile_first = (tile_start == start_of_tile_expert).astype(jnp.int32)
    n_valid = (ends[-1] // cfg.tile).astype(jnp.int32).reshape(1)
    return slot_rows(0), slot_rows(1), route[2:4].T, starts, ends, tile_expert, tile_first, n_valid


def moe_ln(x, x_route, layer, p, cfg, name):
    route, cnt = router(x, x_route, p["router_w_t"], p["router_b"], tm=cfg.tok_tile, name=name + "_router")
    d0, d1, w_tok, starts, ends, tile_expert, tile_first, n_valid = _routing_plan(route, cnt, cfg)
    x_sorted = moe_dispatch(d0, d1, starts, ends, n_valid, x, cfg, name + "_dispatch")
    y_sorted = moe_experts(tile_expert, tile_first, n_valid, x_sorted,
                           p["moe_w_gate"], p["moe_w_up"], p["moe_w_down"], layer, cfg, name)
    return moe_combine_ln(d0, d1, x, w_tok, p["moe_ln_g"][layer], p["moe_ln_b"][layer], y_sorted,
                          tm=cfg.tok_tile, name=name + "_combine")


def _rope_tables(pos):
    half = ROT_DIM // 2
    inv = jnp.power(ROPE_THETA, -jnp.arange(half, dtype=F32) / half)
    ang = pos.astype(F32)[:, None] * inv[None, :]
    cos, sin = jnp.cos(ang), jnp.sin(ang)
    n = pos.shape[0]
    ones = jnp.ones((n, HEAD_DIM - ROT_DIM), F32)
    zeros_h = jnp.zeros((n, half), F32)
    zeros_r = jnp.zeros((n, HEAD_DIM - ROT_DIM), F32)
    c = jnp.concatenate([cos, cos, ones], axis=1)
    sa = jnp.concatenate([zeros_h, sin, zeros_r], axis=1)
    sb = jnp.concatenate([-sin, zeros_h, zeros_r], axis=1)
    return c, sa, sb


def kernel(x_prompt, x_sample, mem_prompt, cache_c_k, cache_c_v, page_table, state_b_buf, state_d_buf, cache_mem_k, cache_mem_v, ab_w_in, ab_a_ln_g, ab_a_ln_b, ab_a_ws, ab_a_bs, ab_b_wg, ab_b_scale, ab_w_out, cd_w_in, cd_d_conv_w, cd_d_conv_b, cd_d_ln_g, cd_d_ln_b, cd_w_out, mix_ln_g, mix_ln_b, mem_w_q, mem_w_k, mem_w_v, mem_w_o, mem_ln_g, mem_ln_b, router_w, router_b, moe_w_gate, moe_w_up, moe_w_down, moe_ln_g, moe_ln_b):
    row = lambda v: v.reshape(1, -1)
    xp = x_prompt.reshape(N_PROMPT, D_MODEL)
    xs = x_sample.reshape(N_SAMPLE, D_MODEL)
    moe_p = {"router_w_t": router_w.T, "router_b": router_b.reshape(N_EXPERTS, 1),
             "moe_w_gate": moe_w_gate, "moe_w_up": moe_w_up,
             "moe_w_down": moe_w_down, "moe_ln_g": [row(moe_ln_g[l]) for l in range(DEPTH)],
             "moe_ln_b": [row(moe_ln_b[l]) for l in range(DEPTH)]}

    w_kv = split_weight(jnp.concatenate([mem_w_k[0], mem_w_k[1], mem_w_v[0], mem_w_v[1]], axis=1), True)
    n_memrows = BATCH * N_MEM
    kv = matmul(mem_prompt.reshape(n_memrows, D_MODEL), w_kv, m=n_memrows, tm=256, tn=1024, precise=True,
                name="mem_kv")
    pmk = [kv[:, l * MEM_WIDTH:(l + 1) * MEM_WIDTH].reshape(BATCH, N_MEM, MEM_WIDTH) for l in range(DEPTH)]
    pmv = [kv[:, (DEPTH + l) * MEM_WIDTH:(DEPTH + l + 1) * MEM_WIDTH].reshape(BATCH, N_MEM, MEM_WIDTH)
           for l in range(DEPTH)]
    smk = cache_mem_k.reshape(DEPTH, DEC_BATCH, N_MEM, MEM_WIDTH)
    smv = cache_mem_v.reshape(DEPTH, DEC_BATCH, N_MEM, MEM_WIDTH)

    mem_w = [(split_weight(mem_w_q[l], True), split_weight(mem_w_o[l], True)) for l in range(DEPTH)]
    mem_ln = [(row(mem_ln_g[l]), row(mem_ln_b[l])) for l in range(DEPTH)]

    def memattn_prompt(x, layer):
        wq, wo = split_weight(mem_w_q[layer], False), split_weight(mem_w_o[layer], False)
        return memory_attention_ln(x, wq, pmk[layer], pmv[layer], wo, *mem_ln[layer], m=N_PROMPT, tm=512,
                                   rows_per_b=SEQ, name="memattn_prompt")

    def memattn_sample(x, layer):
        wq, wo = mem_w[layer]
        return memory_attention_ln(x, wq, smk[layer], smv[layer], wo, *mem_ln[layer], m=N_SAMPLE, tm=N_SAMPLE,
                                   rows_per_b=DEC_SEQ, precise=True, name="memattn_sample")

    w_in0 = split_weight(ab_w_in[0], True)
    w_out0 = split_weight(ab_w_out[0], True)
    ab_prm = {"ln_g": row(ab_a_ln_g[0]), "ln_b": row(ab_a_ln_b[0]), "ws": ab_a_ws[0], "bs_t": ab_a_bs[0].T,
              "wg": ab_b_wg[0].astype(BF16), "scale": row(ab_b_scale[0])}
    ab_prm_precise = dict(ab_prm, wg=ab_b_wg[0])
    g, b = row(mix_ln_g[0]), row(mix_ln_b[0])
    n_chunks = SEQ // CHUNK
    n_tail = BATCH * CHUNK

    h0p = matmul(xp, split_weight(ab_w_in[0], False), m=N_PROMPT, tm=512, tn=1024, n_gelu=2, name="ab_in_prompt")
    mix_p, _ = mixer_ab(h0p, h0p, lambda b, c: (jnp.maximum(b * (SEQ // 16) + c * (CHUNK // 16) - 1, 0), 2),
                        nb=BATCH, n_chunks=n_chunks, pos0=0, has_ctx=False, prm=ab_prm, name="mixer_ab_prompt")
    x1p = matmul_res_ln([mix_p], [split_weight(ab_w_out[0], False)], xp, g, b, m=N_PROMPT, tm=512,
                        name="ab_out_prompt")
    x1p = memattn_prompt(x1p, 0)

    xt = x_prompt[:, SEQ - CHUNK:].reshape(n_tail, D_MODEL)
    h0t = matmul(xt, w_in0, m=n_tail, tm=CHUNK, tn=1024, n_gelu=2, precise=True, name="ab_in_tail")
    mix_t, _ = mixer_ab(h0t, h0p, lambda b, c: ((b + 1) * (SEQ // 16) - CHUNK // 16 - 1, 2), nb=BATCH, n_chunks=1,
                        pos0=SEQ - CHUNK, has_ctx=True, prm=ab_prm_precise, precise=True, name="mixer_ab_tail")
    x1t = matmul_res_ln([mix_t], [w_out0], xt, g, b, m=n_tail, tm=CHUNK, precise=True, name="ab_out_tail")
    wq, wo = mem_w[0]
    x1t = memory_attention_ln(x1t, wq, pmk[0], pmv[0], wo, *mem_ln[0], m=n_tail, tm=CHUNK, rows_per_b=CHUNK,
                              precise=True, name="memattn_tail")

    h0s = matmul(xs, w_in0, m=N_SAMPLE, tm=N_SAMPLE, tn=1024, n_gelu=2, precise=True, name="ab_in_sample")
    h0s_pad = jnp.pad(h0s.reshape(DEC_BATCH, DEC_SEQ, -1), ((0, 0), (0, CHUNK - DEC_SEQ), (0, 0)))
    h0s_pad = h0s_pad.reshape(DEC_BATCH * CHUNK, -1)
    zctx = jnp.pad(state_b_buf[0], ((0, 0), (16 - B_BUF, 0), (0, 0))).reshape(DEC_BATCH * 16, B_WIDTH)
    mix_s, vn_s = mixer_ab(h0s_pad, zctx, lambda b, c: (b, 0), nb=DEC_BATCH, n_chunks=1, pos0=PAST_LEN,
                           has_ctx=True, prm=ab_prm_precise, precise=True, name="mixer_ab_sample")
    mix_s = mix_s.reshape(DEC_BATCH, CHUNK, -1)[:, :DEC_SEQ].reshape(N_SAMPLE, -1)
    x1s = matmul_res_ln([mix_s], [w_out0], xs, g, b, m=N_SAMPLE, tm=N_SAMPLE, precise=True, name="ab_out_sample")
    x1s = memattn_sample(x1s, 0)

    x2p = moe_ln(x1p, x1t, 0, moe_p, MOE_PROMPT, "moe_prompt")
    x2s = moe_ln(x1s, None, 0, moe_p, MOE_SAMPLE, "moe_sample")

    w_in1 = split_weight(cd_w_in[0], True)
    h1p = matmul(x2p, split_weight(cd_w_in[0], False), m=N_PROMPT, tm=512, tn=1024, name="cd_in_prompt")
    h1s = matmul(x2s, w_in1, m=N_SAMPLE, tm=N_SAMPLE, tn=1024, precise=True, name="cd_in_sample")
    tabs_p = _rope_tables(jnp.arange(SEQ, dtype=jnp.int32))
    tabs_s = _rope_tables(PAST_LEN + (jnp.arange(N_SAMPLE, dtype=jnp.int32) % DEC_SEQ))
    qr_p, kr_p = rope_qk(h1p, tabs_p, m=N_PROMPT, tm=256, xoff=0, tab_blocks=SEQ // 256, name="rope_prompt")
    qr_s, kr_s = rope_qk(h1s, tabs_s, m=N_SAMPLE, tm=N_SAMPLE, xoff=0, tab_blocks=1, name="rope_sample")
    c_p = moba_prompt(qr_p, kr_p, h1p, nb=BATCH, seq=SEQ)
    v_s = h1s[:, 2 * C_WIDTH:3 * C_WIDTH]
    q_combo = qr_s.reshape(DEC_BATCH, N_COMBO, HEAD_DIM)
    m_p, l_p, o_p, ksum = paged_partials(page_table, q_combo, cache_c_k, cache_c_v, 0)
    c_s = paged_merge(q_combo, kr_s.reshape(DEC_BATCH, N_COMBO, HEAD_DIM),
                      v_s.reshape(DEC_BATCH, N_COMBO, HEAD_DIM), m_p, l_p, o_p, ksum)
    c_s = c_s.reshape(N_SAMPLE, C_WIDTH)

    cd_prm = {"conv_w": cd_d_conv_w[0], "conv_b": row(cd_d_conv_b[0]), "ln_g": row(cd_d_ln_g[0]),
              "ln_b": row(cd_d_ln_b[0])}
    col_a = 3 * C_WIDTH // D_WIDTH
    tiles_b = SEQ // 256
    halo_per_tile = 256 // CONV_HALO

    def prev_map(col):
        return lambda b, t: (jnp.maximum((b * tiles_b + t) * halo_per_tile - 1, 0), col)

    d_p, tail_p = conv_module(h1p, col_a, h1p, h1p, prev_map(col_a), prev_map(col_a + 1), nb=BATCH,
                              n_tiles=tiles_b, tm=256, rt=128, tail=CONV_HALO, xoff=0, prev_is_state=False,
                              prm=cd_prm, name="conv_prompt")
    gl_s = jnp.pad(h1s[:, 3 * C_WIDTH:].reshape(DEC_BATCH, DEC_SEQ, 2 * D_WIDTH), ((0, 0), (0, 8 - DEC_SEQ), (0, 0)))
    gl_s = gl_s.reshape(DEC_BATCH * 8, 2 * D_WIDTH)
    dctx = jnp.pad(state_d_buf[0], ((0, 0), (CONV_HALO - D_BUF, 0), (0, 0))).reshape(DEC_BATCH * CONV_HALO, D_WIDTH)
    d_s, tail_s = conv_module(gl_s, 0, dctx, dctx, lambda b, t: (b, 0), lambda b, t: (b, 0), nb=DEC_BATCH,
                              n_tiles=1, tm=8, rt=8, tail=8, xoff=0, prev_is_state=True, prm=cd_prm,
                              name="conv_sample", out_dtype=F32)
    d_s = d_s.reshape(DEC_BATCH, 8, D_WIDTH)[:, :DEC_SEQ].reshape(N_SAMPLE, D_WIDTH)
    w_out1c = split_weight(cd_w_out[0][:C_WIDTH], True)
    w_out1d = split_weight(cd_w_out[0][C_WIDTH:], True)
    g, b = row(mix_ln_g[1]), row(mix_ln_b[1])
    w_out1 = cd_w_out[0].astype(BF16)
    x3p = matmul_res_ln([c_p, d_p], [(w_out1[:C_WIDTH],), (w_out1[C_WIDTH:],)], x2p, g, b, m=N_PROMPT, tm=512,
                        name="cd_out_prompt")
    x3s = matmul_res_ln([c_s, d_s], [w_out1c, w_out1d], x2s, g, b, m=N_SAMPLE, tm=N_SAMPLE, precise=True,
                        name="cd_out_sample")
    y_p = moe_ln(memattn_prompt(x3p, 1), None, 1, moe_p, MOE_PROMPT, "moe_prompt")
    y_s = moe_ln(memattn_sample(x3s, 1), None, 1, moe_p, MOE_SAMPLE, "moe_sample")

    kv_shape_p = (1, BATCH, SEQ, C_HEADS, HEAD_DIM)
    kv_shape_s = (1, DEC_BATCH, DEC_SEQ, C_HEADS, HEAD_DIM)
    z_p = h0p[:, 2 * A_WIDTH:].reshape(BATCH, SEQ, B_WIDTH)
    z_s = h0s[:, 2 * A_WIDTH:].reshape(DEC_BATCH, DEC_SEQ, B_WIDTH)
    h_s = tail_s.reshape(DEC_BATCH, 8, D_WIDTH)[:, :DEC_SEQ]
    mem_shape = (BATCH, N_MEM, MEM_HEADS, MEM_HEAD_DIM)
    return (y_p.reshape(BATCH, SEQ, D_MODEL),
            y_s.reshape(DEC_BATCH, DEC_SEQ, D_MODEL),
            kr_p.reshape(kv_shape_p),
            h1p[:, 2 * C_WIDTH:3 * C_WIDTH].reshape(kv_shape_p),
            kr_s.reshape(kv_shape_s),
            v_s.reshape(kv_shape_s),
            z_p[:, SEQ - B_BUF:][None],
            jnp.concatenate([state_b_buf[0], z_s], axis=1)[:, DEC_SEQ:][None],
            vn_s.reshape(DEC_BATCH, CHUNK, A_WIDTH)[:, :DEC_SEQ][None],
            tail_p.reshape(BATCH, CONV_HALO, D_WIDTH)[:, CONV_HALO - D_BUF:][None],
            jnp.concatenate([state_d_buf[0], h_s], axis=1)[:, DEC_SEQ:][None],
            jnp.stack([m_.reshape(mem_shape) for m_ in pmk]),
            jnp.stack([m_.reshape(mem_shape) for m_ in pmv]))
```

```python
import functools
import math
from typing import NamedTuple

import jax
import jax.numpy as jnp
from jax import lax
from jax.experimental import pallas as pl
from jax.experimental.pallas import tpu as pltpu

F32 = jnp.float32
BF16 = jnp.bfloat16
HIGHEST = lax.Precision.HIGHEST

D_MODEL = 2048
BATCH = 4
SEQ = 2048
DEPTH = 2
DEC_BATCH = 8
DEC_SEQ = 4
PAST_LEN = 16384
PAGE_SIZE = 128
A_WIDTH = 1024
CHUNK = 128
A_GROUPS = 8
B_WIDTH = 1024
POOL_WINDOWS = (2, 4, 8, 16)
B_GROUP_DIM = B_WIDTH // len(POOL_WINDOWS)
B_BUF = 15
C_HEADS = 8
HEAD_DIM = 128
C_WIDTH = 1024
ROT_DIM = 32
ROPE_THETA = 500000.0
MOBA_BLOCK = 256
MOBA_TOPK = 3
D_WIDTH = 1024
CONV_WIDTH = 31
D_BUF = 30
N_MEM = 256
MEM_HEADS = 4
MEM_HEAD_DIM = 128
MEM_WIDTH = 512
N_EXPERTS = 16
N_EXPERT_GROUPS = 4
EXPERTS_PER_GROUP = 4
MOE_TOPK = 2
D_EXPERT = 1024
ALPHA = (2 * DEPTH) ** 0.25
LN_EPS = 1e-5

N_PROMPT = BATCH * SEQ
N_SAMPLE = DEC_BATCH * DEC_SEQ
N_TOK = N_PROMPT + N_SAMPLE
N_PAST_BLOCKS = PAST_LEN // MOBA_BLOCK
PAGES_PER_BLOCK = MOBA_BLOCK // PAGE_SIZE
N_COMBO = DEC_SEQ * C_HEADS


class MoeCfg(NamedTuple):
    n_tok: int
    tok_tile: int
    tile: int
    rows: int
    n_tiles: int
    precise: bool


def _moe_cfg(n_tok, tile, precise):
    rows = ((n_tok * MOE_TOPK + N_EXPERTS * (tile - 1)) // tile + 1) * tile
    return MoeCfg(n_tok, tile, tile, rows, rows // tile, precise)


MOE_PROMPT = _moe_cfg(N_PROMPT, 256, False)
MOE_SAMPLE = _moe_cfg(N_SAMPLE, N_SAMPLE, True)

ROW_DMA_UNROLL = 8
VMEM_LIMIT = 56 * 1024 * 1024
VMEM_LIMIT_SMALL = 40 * 1024 * 1024
SMALL_CALL_ROWS = 1024


def _cparams(n_axes, rows=None):
    vmem = VMEM_LIMIT if rows is None or rows > SMALL_CALL_ROWS else VMEM_LIMIT_SMALL
    return pltpu.CompilerParams(dimension_semantics=("arbitrary",) * n_axes, vmem_limit_bytes=vmem)


def _ln(x, g, b):
    mu = jnp.mean(x, axis=-1, keepdims=True)
    xc = x - mu
    var = jnp.mean(xc * xc, axis=-1, keepdims=True)
    return xc * lax.rsqrt(var + LN_EPS) * g + b


def _dot(a, b):
    return jnp.dot(a, b, preferred_element_type=F32)


def _dot_nt(a, b, precision=None):
    return lax.dot_general(a, b, (((1,), (1,)), ((), ())), precision=precision, preferred_element_type=F32)


def _split_bf16(v):
    hi = v.astype(BF16)
    return hi, (v.astype(F32) - hi.astype(F32)).astype(BF16)


def _mm(a, b, precise, nt=False):
    dot = _dot_nt if nt else _dot
    if not precise:
        b0 = b[0] if isinstance(b, tuple) else b
        return dot(a.astype(BF16), b0.astype(BF16))
    ah, al = _split_bf16(a)
    bh, bl = b if isinstance(b, tuple) else _split_bf16(b)
    return dot(ah, bh) + (dot(al, bh) + dot(ah, bl))


def _load_all(refs):
    return tuple(r[...] for r in refs)


def split_weight(w, precise):
    if not precise:
        return (w.astype(BF16),)
    bits = lax.bitcast_convert_type(w, jnp.uint32) & jnp.uint32(0xFFFF0000)
    hi = lax.bitcast_convert_type(bits, F32)
    return hi.astype(BF16), (w - hi).astype(BF16)


def _mm_kernel(x_ref, *refs, n_gelu, precise):
    o_ref = refs[-1]
    acc = _mm(x_ref[...], _load_all(refs[:-1]), precise)
    if n_gelu:
        j = pl.program_id(0)

        @pl.when(j < n_gelu)
        def _():
            o_ref[...] = jax.nn.gelu(acc).astype(o_ref.dtype)

        @pl.when(j >= n_gelu)
        def _():
            o_ref[...] = acc.astype(o_ref.dtype)
    else:
        o_ref[...] = acc.astype(o_ref.dtype)


def matmul(x, w, *, m, tm, tn, xoff=0, n_gelu=0, precise=False, name="mm"):
    k = x.shape[1]
    n = w[0].shape[1]
    return pl.pallas_call(
        functools.partial(_mm_kernel, n_gelu=n_gelu, precise=precise),
        grid=(n // tn, m // tm),
        in_specs=[pl.BlockSpec((tm, k), lambda j, i: (i + xoff, 0))]
        + [pl.BlockSpec((k, tn), lambda j, i: (0, j))] * len(w),
        out_specs=pl.BlockSpec((tm, tn), lambda j, i: (i, j)),
        out_shape=jax.ShapeDtypeStruct((m, n), F32),
        compiler_params=_cparams(2, m),
        name=name,
    )(x, *w)


def _mm_res_ln_kernel(*refs, n_in, precise):
    n_w = 2 if precise else 1
    a_refs = refs[:n_in]
    w_refs = refs[n_in:n_in + n_in * n_w]
    r_ref, g_ref, b_ref, o_ref = refs[n_in + n_in * n_w:]
    acc = None
    for k, a_ref in enumerate(a_refs):
        d = _mm(a_ref[...], _load_all(w_refs[k * n_w:(k + 1) * n_w]), precise)
        acc = d if acc is None else acc + d
    o_ref[...] = _ln(ALPHA * r_ref[...] + acc, g_ref[...], b_ref[...])


def matmul_res_ln(a_list, w_list, resid, g, b, *, m, tm, roff=0, precise=False, name="mm_res_ln"):
    n_in = len(a_list)
    w_flat = [part for w in w_list for part in w]
    in_specs = [pl.BlockSpec((tm, a.shape[1]), lambda i: (i, 0)) for a in a_list]
    in_specs += [pl.BlockSpec(w.shape, lambda i: (0, 0)) for w in w_flat]
    in_specs += [pl.BlockSpec((tm, D_MODEL), lambda i: (i + roff, 0)),
                 pl.BlockSpec((1, D_MODEL), lambda i: (0, 0)),
                 pl.BlockSpec((1, D_MODEL), lambda i: (0, 0))]
    return pl.pallas_call(
        functools.partial(_mm_res_ln_kernel, n_in=n_in, precise=precise),
        grid=(m // tm,),
        in_specs=in_specs,
        out_specs=pl.BlockSpec((tm, D_MODEL), lambda i: (i, 0)),
        out_shape=jax.ShapeDtypeStruct((m, D_MODEL), F32),
        compiler_params=_cparams(1, m),
        name=name,
    )(*a_list, *w_flat, resid, g, b)


def _mixer_ab_kernel(u_ref, v_ref, z_ref, zp_ref, lng_ref, lnb_ref, ws_ref, bs_ref, wg_ref, sc_ref,
                     mix_ref, vn_ref, *, pos0, has_ctx, precise):
    c = pl.program_id(1)
    vn = _ln(v_ref[...], lng_ref[...], lnb_ref[...])
    vn_ref[...] = vn
    u = u_ref[...]
    row = lax.broadcasted_iota(jnp.int32, (CHUNK, CHUNK), 0)
    col = lax.broadcasted_iota(jnp.int32, (CHUNK, CHUNK), 1)
    causal = col <= row
    gd = A_WIDTH // A_GROUPS
    for g in range(A_GROUPS):
        sl = slice(g * gd, (g + 1) * gd)
        w = jnp.where(causal, ws_ref[g], 0.0)
        mixed = _mm(w, vn[:, sl], precise) + bs_ref[:, g:g + 1]
        mix_ref[:, sl] = (u[:, sl] * mixed).astype(mix_ref.dtype)

    z = z_ref[...]
    zp = zp_ref[...]
    if not has_ctx:
        zp = jnp.where(c == 0, 0.0, zp)
    zext = jnp.concatenate([zp, z], axis=0)
    pos = pos0 + c * CHUNK + lax.broadcasted_iota(jnp.int32, (CHUNK, 1), 0)
    for gi, wdw in enumerate(POOL_WINDOWS):
        sl = slice(gi * B_GROUP_DIM, (gi + 1) * B_GROUP_DIM)
        s = zext[:, sl]
        sh = 1
        while sh < wdw:
            s = s + pltpu.roll(s, sh, axis=0)
            sh *= 2
        cnt = jnp.minimum(wdw, pos + 1).astype(F32)
        d = s[16:, :] / cnt - z[:, sl]
        bo = _mm(d, wg_ref[gi], precise) * sc_ref[:, sl]
        mix_ref[:, A_WIDTH + gi * B_GROUP_DIM:A_WIDTH + (gi + 1) * B_GROUP_DIM] = bo.astype(mix_ref.dtype)


def mixer_ab(h, zprev, zprev_map, *, nb, n_chunks, pos0, has_ctx, prm, name, precise=False):
    m = nb * n_chunks * CHUNK
    row = lambda b, c: b * n_chunks + c
    const2 = lambda b, c: (0, 0)
    return pl.pallas_call(
        functools.partial(_mixer_ab_kernel, pos0=pos0, has_ctx=has_ctx, precise=precise),
        grid=(nb, n_chunks),
        in_specs=[pl.BlockSpec((CHUNK, A_WIDTH), lambda b, c: (row(b, c), 0)),
                  pl.BlockSpec((CHUNK, A_WIDTH), lambda b, c: (row(b, c), 1)),
                  pl.BlockSpec((CHUNK, B_WIDTH), lambda b, c: (row(b, c), 2)),
                  pl.BlockSpec((16, B_WIDTH), zprev_map),
                  pl.BlockSpec((1, A_WIDTH), const2),
                  pl.BlockSpec((1, A_WIDTH), const2),
                  pl.BlockSpec((A_GROUPS, CHUNK, CHUNK), lambda b, c: (0, 0, 0)),
                  pl.BlockSpec((CHUNK, A_GROUPS), const2),
                  pl.BlockSpec((len(POOL_WINDOWS), B_GROUP_DIM, B_GROUP_DIM), lambda b, c: (0, 0, 0)),
                  pl.BlockSpec((1, B_WIDTH), const2)],
        out_specs=[pl.BlockSpec((CHUNK, A_WIDTH + B_WIDTH), lambda b, c: (row(b, c), 0)),
                   pl.BlockSpec((CHUNK, A_WIDTH), lambda b, c: (row(b, c), 0))],
        out_shape=[jax.ShapeDtypeStruct((m, A_WIDTH + B_WIDTH), F32 if precise else BF16),
                   jax.ShapeDtypeStruct((m, A_WIDTH), F32)],
        compiler_params=_cparams(2, m),
        name=name,
    )(h, h, h, zprev, prm["ln_g"], prm["ln_b"], prm["ws"], prm["bs_t"], prm["wg"], prm["scale"])


def _rope_kernel(q_ref, k_ref, c_ref, sa_ref, sb_ref, qo_ref, ko_ref):
    c = c_ref[...]
    sa = sa_ref[...]
    sb = sb_ref[...]
    half = ROT_DIM // 2
    for h in range(C_HEADS):
        sl = slice(h * HEAD_DIM, (h + 1) * HEAD_DIM)
        for src, dst in ((q_ref, qo_ref), (k_ref, ko_ref)):
            x = src[:, sl]
            dst[:, sl] = (x * c + pltpu.roll(x, half, axis=1) * sa
                          + pltpu.roll(x, HEAD_DIM - half, axis=1) * sb)


def rope_qk(h, tabs, *, m, tm, xoff, tab_blocks, name):
    tmap = lambda i: (i % tab_blocks, 0)
    return pl.pallas_call(
        _rope_kernel,
        grid=(m // tm,),
        in_specs=[pl.BlockSpec((tm, C_WIDTH), lambda i: (i + xoff, 0)),
                  pl.BlockSpec((tm, C_WIDTH), lambda i: (i + xoff, 1)),
                  pl.BlockSpec((tm, HEAD_DIM), tmap),
                  pl.BlockSpec((tm, HEAD_DIM), tmap),
                  pl.BlockSpec((tm, HEAD_DIM), tmap)],
        out_specs=[pl.BlockSpec((tm, C_WIDTH), lambda i: (i, 0)),
                   pl.BlockSpec((tm, C_WIDTH), lambda i: (i, 0))],
        out_shape=[jax.ShapeDtypeStruct((m, C_WIDTH), F32),
                   jax.ShapeDtypeStruct((m, C_WIDTH), F32)],
        compiler_params=_cparams(1, m),
        name=name,
    )(h, h, *tabs)


def _moba_kernel(q_ref, k_ref, v_ref, o_ref, km_ref, *, n_blocks):
    i = pl.program_id(2)
    blk_rows = MOBA_BLOCK
    scale = HEAD_DIM ** -0.5

    @pl.when(i == 0)
    def _():
        for j in range(n_blocks):
            km_ref[j:j + 1, :] = jnp.mean(k_ref[j * blk_rows:(j + 1) * blk_rows, :], axis=0, keepdims=True)

    q = q_ref[...]
    qb = q.astype(BF16)
    blk = lax.broadcasted_iota(jnp.int32, (blk_rows, n_blocks), 1)
    bs = jnp.full((blk_rows, n_blocks), -jnp.inf, F32)
    for n in range(n_blocks - 1):
        bs = jnp.where(blk == n, jnp.sum(q * km_ref[n:n + 1, :], axis=-1, keepdims=True), bs)
    work = jnp.where(blk < i, bs, -jnp.inf)
    sel = jnp.zeros((blk_rows, n_blocks), F32)
    for _ in range(MOBA_TOPK):
        mx = jnp.max(work, axis=-1, keepdims=True)
        first = jnp.min(jnp.where(work == mx, blk, n_blocks), axis=-1, keepdims=True)
        hit = (blk == first) & (mx > -jnp.inf)
        sel = jnp.where(hit, 1.0, sel)
        work = jnp.where(hit, -jnp.inf, work)

    r_i = lax.broadcasted_iota(jnp.int32, (blk_rows, blk_rows), 0)
    c_i = lax.broadcasted_iota(jnp.int32, (blk_rows, blk_rows), 1)
    causal = (c_i <= r_i).astype(F32)

    for c in range(n_blocks):
        @pl.when(i == c)
        def _():
            nk = (c + 1) * blk_rows
            s = _dot_nt(qb, k_ref[0:nk, :].astype(BF16)) * scale
            allowed = jnp.concatenate(
                [jnp.broadcast_to(sel[:, j:j + 1], (blk_rows, blk_rows)) for j in range(c)] + [causal], axis=1)
            s = jnp.where(allowed > 0.0, s, -jnp.inf)
            m = jnp.max(s, axis=-1, keepdims=True)
            p = jnp.exp(s - m)
            l = jnp.sum(p, axis=-1, keepdims=True)
            o = _dot(p.astype(BF16), v_ref[0:nk, :].astype(BF16))
            o_ref[...] = (o / l).astype(o_ref.dtype)


def moba_prompt(q_rot, k_rot, h, *, nb, seq, name="moba_prompt"):
    n_blocks = seq // MOBA_BLOCK
    v_col0 = 2 * C_WIDTH // HEAD_DIM
    return pl.pallas_call(
        functools.partial(_moba_kernel, n_blocks=n_blocks),
        grid=(nb, C_HEADS, n_blocks),
        in_specs=[pl.BlockSpec((MOBA_BLOCK, HEAD_DIM), lambda b, hh, i: (b * n_blocks + i, hh)),
                  pl.BlockSpec((seq, HEAD_DIM), lambda b, hh, i: (b, hh)),
                  pl.BlockSpec((seq, HEAD_DIM), lambda b, hh, i: (b, v_col0 + hh))],
        out_specs=pl.BlockSpec((MOBA_BLOCK, HEAD_DIM), lambda b, hh, i: (b * n_blocks + i, hh)),
        out_shape=jax.ShapeDtypeStruct((nb * seq, C_WIDTH), BF16),
        scratch_shapes=[pltpu.VMEM((n_blocks, HEAD_DIM), F32)],
        compiler_params=_cparams(3),
        name=name,
    )(q_rot, k_rot, h)


CONV_HALO = 32


def _conv_kernel(ga_ref, gg_ref, pa_ref, pg_ref, w_ref, bdw_ref, lng_ref, lnb_ref,
                 o_ref, tail_ref, ext_ref, y_ref, *, tm, rt, tail, prev_is_state):
    t = pl.program_id(1)
    hcur = ga_ref[...] * jax.nn.sigmoid(gg_ref[...])
    if prev_is_state:
        hprev = pa_ref[...]
    else:
        hprev = pa_ref[...] * jax.nn.sigmoid(pg_ref[...])
        hprev = jnp.where(t == 0, 0.0, hprev)
    ext_ref[0:CONV_HALO, :] = hprev
    ext_ref[CONV_HALO:CONV_HALO + tm, :] = hcur
    tail_ref[...] = hcur[tm - tail:, :]
    off = CONV_HALO - D_BUF
    for cc in range(D_WIDTH // 128):
        cs = slice(cc * 128, (cc + 1) * 128)
        for rc in range(tm // rt):
            r0 = rc * rt
            acc = jnp.zeros((rt, 128), F32)
            for j in range(CONV_WIDTH):
                acc = acc + w_ref[j:j + 1, cs] * ext_ref[r0 + off + j:r0 + off + j + rt, cs]
            y_ref[r0:r0 + rt, cs] = acc + bdw_ref[:, cs]
    y = _ln(y_ref[...], lng_ref[...], lnb_ref[...])
    o_ref[...] = (y * jax.nn.sigmoid(y)).astype(o_ref.dtype)


def conv_module(h, col_a, prev_a, prev_g, prev_map_a, prev_map_g, *, nb, n_tiles, tm, rt, tail, xoff,
                prev_is_state, prm, name, out_dtype=BF16):
    m = nb * n_tiles * tm
    row = lambda b, t: b * n_tiles + t + xoff
    const2 = lambda b, t: (0, 0)
    return pl.pallas_call(
        functools.partial(_conv_kernel, tm=tm, rt=rt, tail=tail, prev_is_state=prev_is_state),
        grid=(nb, n_tiles),
        in_specs=[pl.BlockSpec((tm, D_WIDTH), lambda b, t: (row(b, t), col_a)),
                  pl.BlockSpec((tm, D_WIDTH), lambda b, t: (row(b, t), col_a + 1)),
                  pl.BlockSpec((CONV_HALO, D_WIDTH), prev_map_a),
                  pl.BlockSpec((CONV_HALO, D_WIDTH), prev_map_g),
                  pl.BlockSpec((CONV_WIDTH, D_WIDTH), const2),
                  pl.BlockSpec((1, D_WIDTH), const2),
                  pl.BlockSpec((1, D_WIDTH), const2),
                  pl.BlockSpec((1, D_WIDTH), const2)],
        out_specs=[pl.BlockSpec((tm, D_WIDTH), lambda b, t: (b * n_tiles + t, 0)),
                   pl.BlockSpec((tail, D_WIDTH), lambda b, t: (b, 0))],
        out_shape=[jax.ShapeDtypeStruct((m, D_WIDTH), out_dtype),
                   jax.ShapeDtypeStruct((nb * tail, D_WIDTH), F32)],
        scratch_shapes=[pltpu.VMEM((CONV_HALO + tm, D_WIDTH), F32), pltpu.VMEM((tm, D_WIDTH), F32)],
        compiler_params=_cparams(2, m),
        name=name,
    )(h, h, prev_a, prev_g, prm["conv_w"], prm["conv_b"], prm["ln_g"], prm["ln_b"])


PAGED_BLOCKS_PER_STEP = 4
PAGED_PAGES_PER_STEP = PAGED_BLOCKS_PER_STEP * PAGES_PER_BLOCK


def _paged_partial_kernel(pt_ref, q_ref, *refs):
    del pt_ref
    pages = refs[:2 * PAGED_PAGES_PER_STEP]
    m_ref, l_ref, o_ref, ks_ref = refs[2 * PAGED_PAGES_PER_STEP:]
    rows = PAGE_SIZE * C_HEADS
    scale = HEAD_DIM ** -0.5
    k_refs = pages[:PAGED_PAGES_PER_STEP]
    v_refs = pages[PAGED_PAGES_PER_STEP:]
    qb = q_ref[...].astype(BF16)
    c_i = lax.broadcasted_iota(jnp.int32, (N_COMBO, rows), 0)
    l_i = lax.broadcasted_iota(jnp.int32, (N_COMBO, rows), 1)
    same_head = (l_i & (C_HEADS - 1)) == (c_i & (C_HEADS - 1))
    for blk in range(PAGED_BLOCKS_PER_STEP):
        ks = None
        s = []
        for pg in range(PAGES_PER_BLOCK):
            k = k_refs[blk * PAGES_PER_BLOCK + pg][...]
            ksum = jnp.sum(k, axis=0)
            ks = ksum if ks is None else ks + ksum
            sp = _dot_nt(qb, k.reshape(rows, HEAD_DIM).astype(BF16)) * scale
            s.append(jnp.where(same_head, sp, -jnp.inf))
        ks_ref[blk] = ks
        m = jnp.max(jnp.maximum(s[0], s[1]), axis=-1, keepdims=True)
        l = None
        o = None
        for pg in range(PAGES_PER_BLOCK):
            p = jnp.exp(s[pg] - m)
            v = v_refs[blk * PAGES_PER_BLOCK + pg][...].reshape(rows, HEAD_DIM).astype(BF16)
            lp = jnp.sum(p, axis=-1, keepdims=True)
            op = _dot(p.astype(BF16), v)
            l = lp if l is None else l + lp
            o = op if o is None else o + op
        m_ref[blk] = jnp.broadcast_to(m, (N_COMBO, HEAD_DIM))
        l_ref[blk] = jnp.broadcast_to(l, (N_COMBO, HEAD_DIM))
        o_ref[blk] = o


def paged_partials(page_table, q_combo, cache_k, cache_v, layer):
    page_block = (None, None, PAGE_SIZE, C_HEADS, HEAD_DIM)
    nbs = PAGED_BLOCKS_PER_STEP

    def page_map(which):
        return lambda b, n, pt: (layer, pt[b, PAGED_PAGES_PER_STEP * n + which], 0, 0, 0)

    page_specs = [pl.BlockSpec(page_block, page_map(w)) for w in range(PAGED_PAGES_PER_STEP)]
    part_shape = jax.ShapeDtypeStruct((DEC_BATCH, N_PAST_BLOCKS, N_COMBO, HEAD_DIM), F32)
    part_spec = pl.BlockSpec((None, nbs, N_COMBO, HEAD_DIM), lambda b, n, pt: (b, n, 0, 0))
    return pl.pallas_call(
        _paged_partial_kernel,
        grid_spec=pltpu.PrefetchScalarGridSpec(
            num_scalar_prefetch=1,
            grid=(DEC_BATCH, N_PAST_BLOCKS // nbs),
            in_specs=[pl.BlockSpec((None, N_COMBO, HEAD_DIM), lambda b, n, pt: (b, 0, 0))] + page_specs + page_specs,
            out_specs=[part_spec, part_spec, part_spec,
                       pl.BlockSpec((None, nbs, C_HEADS, HEAD_DIM), lambda b, n, pt: (b, n, 0, 0))]),
        out_shape=[part_shape, part_shape, part_shape,
                   jax.ShapeDtypeStruct((DEC_BATCH, N_PAST_BLOCKS, C_HEADS, HEAD_DIM), F32)],
        compiler_params=_cparams(2, N_SAMPLE),
        name="paged_partials",
    )(page_table, q_combo, *([cache_k] * PAGED_PAGES_PER_STEP), *([cache_v] * PAGED_PAGES_PER_STEP))


def _paged_merge_kernel(q_ref, kn_ref, vn_ref, m_ref, l_ref, o_ref, ks_ref, out_ref):
    nb = N_PAST_BLOCKS
    scale = HEAD_DIM ** -0.5
    q = q_ref[...]
    kmean = ks_ref[...] * (1.0 / MOBA_BLOCK)
    kmean = jnp.concatenate([kmean] * DEC_SEQ, axis=1)
    bs = jnp.sum(kmean * q[None], axis=-1, keepdims=True)
    work = jnp.broadcast_to(bs, (nb, N_COMBO, HEAD_DIM))
    n_i = lax.broadcasted_iota(jnp.int32, (nb, N_COMBO, HEAD_DIM), 0)
    sel = n_i < 0
    for _ in range(MOBA_TOPK):
        mx = jnp.max(work, axis=0, keepdims=True)
        first = jnp.min(jnp.where(work == mx, n_i, nb), axis=0, keepdims=True)
        hit = n_i == first
        sel = sel | hit
        work = jnp.where(hit, -jnp.inf, work)

    s = _dot_nt(q.astype(BF16), kn_ref[...].astype(BF16)) * scale
    c_i = lax.broadcasted_iota(jnp.int32, s.shape, 0)
    l_i = lax.broadcasted_iota(jnp.int32, s.shape, 1)
    ok = ((l_i & (C_HEADS - 1)) == (c_i & (C_HEADS - 1))) & ((l_i >> 3) <= (c_i >> 3))
    s = jnp.where(ok, s, -jnp.inf)
    m_own = jnp.max(s, axis=-1, keepdims=True)
    p = jnp.exp(s - m_own)
    l_own = jnp.sum(p, axis=-1, keepdims=True)
    o_own = _dot(p.astype(BF16), vn_ref[...].astype(BF16))

    mp = m_ref[...]
    m_all = jnp.maximum(jnp.max(jnp.where(sel, mp, -jnp.inf), axis=0), m_own)
    w = jnp.where(sel, jnp.exp(mp - m_all[None]), 0.0)
    w_own = jnp.exp(m_own - m_all)
    den = jnp.sum(w * l_ref[...], axis=0) + w_own * l_own
    num = jnp.sum(w * o_ref[...], axis=0) + w_own * o_own
    out_ref[...] = num / den


def paged_merge(q_combo, k_new, v_new, m_p, l_p, o_p, ksum):
    combo = pl.BlockSpec((None, N_COMBO, HEAD_DIM), lambda b: (b, 0, 0))
    part = pl.BlockSpec((None, N_PAST_BLOCKS, N_COMBO, HEAD_DIM), lambda b: (b, 0, 0, 0))
    return pl.pallas_call(
        _paged_merge_kernel,
        grid=(DEC_BATCH,),
        in_specs=[combo, combo, combo, part, part, part,
                  pl.BlockSpec((None, N_PAST_BLOCKS, C_HEADS, HEAD_DIM), lambda b: (b, 0, 0, 0))],
        out_specs=combo,
        out_shape=jax.ShapeDtypeStruct((DEC_BATCH, N_COMBO, HEAD_DIM), F32),
        compiler_params=_cparams(1, N_SAMPLE),
        name="paged_merge",
    )(q_combo, k_new, v_new, m_p, l_p, o_p, ksum)


def _memattn_kernel(x_ref, mk_ref, mv_ref, g_ref, b_ref, *refs, nb_tile, rows_per_b, precise):
    n_w = 2 if precise else 1
    wq = _load_all(refs[:n_w])
    wo = _load_all(refs[n_w:2 * n_w])
    o_ref = refs[2 * n_w]
    scale = MEM_HEAD_DIM ** -0.5
    x = x_ref[...]
    tm = x.shape[0]
    q = _mm(x, wq, precise)
    row_b = lax.broadcasted_iota(jnp.int32, (tm, 1), 0) >> int(math.log2(rows_per_b))
    heads = []
    for hh in range(MEM_HEADS):
        sl = slice(hh * MEM_HEAD_DIM, (hh + 1) * MEM_HEAD_DIM)
        qh = q[:, sl]
        oh = None
        for bb in range(nb_tile):
            s = _mm(qh, mk_ref[bb, :, sl], precise, nt=True) * scale
            s = s - jnp.max(s, axis=-1, keepdims=True)
            p = jnp.exp(s)
            p = p / jnp.sum(p, axis=-1, keepdims=True)
            ob = _mm(p, mv_ref[bb, :, sl], precise)
            if nb_tile > 1:
                ob = jnp.where(row_b == bb, ob, 0.0)
            oh = ob if oh is None else oh + ob
        heads.append(oh)
    o = jnp.concatenate(heads, axis=-1)
    y = ALPHA * x + _mm(o, wo, precise)
    o_ref[...] = _ln(y, g_ref[...], b_ref[...])


def memory_attention_ln(x, wq, mk, mv, wo, g, b, *, m, tm, rows_per_b, name, precise=False, b0=0):
    if rows_per_b >= tm:
        nb_tile = 1
        tiles_per_b = rows_per_b // tm
        kv_map = lambda i: (b0 + i // tiles_per_b, 0, 0)
    else:
        nb_tile = tm // rows_per_b
        kv_map = lambda i: (i, 0, 0)
    const2 = lambda i: (0, 0)
    return pl.pallas_call(
        functools.partial(_memattn_kernel, nb_tile=nb_tile, rows_per_b=rows_per_b, precise=precise),
        grid=(m // tm,),
        in_specs=[pl.BlockSpec((tm, D_MODEL), lambda i: (i, 0)),
                  pl.BlockSpec((nb_tile, N_MEM, MEM_WIDTH), kv_map),
                  pl.BlockSpec((nb_tile, N_MEM, MEM_WIDTH), kv_map),
                  pl.BlockSpec((1, D_MODEL), const2),
                  pl.BlockSpec((1, D_MODEL), const2)]
        + [pl.BlockSpec((D_MODEL, MEM_WIDTH), const2)] * len(wq)
        + [pl.BlockSpec((MEM_WIDTH, D_MODEL), const2)] * len(wo),
        out_specs=pl.BlockSpec((tm, D_MODEL), lambda i: (i, 0)),
        out_shape=jax.ShapeDtypeStruct((m, D_MODEL), F32),
        compiler_params=_cparams(1, m),
        name=name,
    )(x, mk, mv, g, b, *wq, *wo)


def _router_kernel(x_ref, xt_ref, w_ref, b_ref, o_ref, cnt_ref, run_ref, *, tiles_per_b):
    step = pl.program_id(0)
    tm = x_ref.shape[0]

    @pl.when(step == 0)
    def _():
        run_ref[...] = jnp.zeros(run_ref.shape, F32)

    x = x_ref[...]
    if tiles_per_b:
        is_last = lax.rem(step, tiles_per_b) == tiles_per_b - 1
        x = jnp.concatenate([x[:tm - CHUNK], jnp.where(is_last, xt_ref[...], x[tm - CHUNK:])], axis=0)
    xh, xl = _split_bf16(x)
    wh, wl = _split_bf16(w_ref[...])
    logits = _dot_nt(wh, xh) + (_dot_nt(wh, xl) + _dot_nt(wl, xh)) + b_ref[...]
    logits = logits - jnp.max(logits, axis=0, keepdims=True)
    e = jnp.exp(logits)
    probs = e / jnp.sum(e, axis=0, keepdims=True)
    p = [probs[j:j + 1, :] for j in range(N_EXPERTS)]
    gbest = None
    gsel = None
    for g in range(N_EXPERT_GROUPS):
        a, b_, c, d = p[4 * g:4 * g + 4]
        hi1, lo1 = jnp.maximum(a, b_), jnp.minimum(a, b_)
        hi2, lo2 = jnp.maximum(c, d), jnp.minimum(c, d)
        gs = jnp.maximum(hi1, hi2) + jnp.maximum(jnp.minimum(hi1, hi2), jnp.maximum(lo1, lo2))
        if g == 0:
            gbest, gsel = gs, jnp.zeros(gs.shape, jnp.int32)
        else:
            better = gs > gbest
            gbest = jnp.where(better, gs, gbest)
            gsel = jnp.where(better, g, gsel)
    cand = [jnp.where(gsel == j // EXPERTS_PER_GROUP, p[j], -1.0) for j in range(N_EXPERTS)]
    v1 = cand[0]
    i1 = jnp.zeros(v1.shape, jnp.int32)
    for j in range(1, N_EXPERTS):
        better = cand[j] > v1
        v1 = jnp.where(better, cand[j], v1)
        i1 = jnp.where(better, j, i1)
    v2 = jnp.full(v1.shape, -2.0, F32)
    i2 = jnp.zeros(v1.shape, jnp.int32)
    for j in range(N_EXPERTS):
        better = (cand[j] > v2) & (i1 != j)
        v2 = jnp.where(better, cand[j], v2)
        i2 = jnp.where(better, j, i2)
    tot = v1 + v2

    e_i = lax.broadcasted_iota(jnp.int32, (N_EXPERTS, tm), 0)
    pick1 = e_i == i1
    pick2 = e_i == i2
    onehot = (pick1 | pick2).astype(F32)
    t_r = lax.broadcasted_iota(jnp.int32, (tm, tm), 0)
    t_c = lax.broadcasted_iota(jnp.int32, (tm, tm), 1)
    earlier = (t_r < t_c).astype(BF16)
    before = _dot(onehot.astype(BF16), earlier) + run_ref[:, 0:1]
    rank1 = jnp.sum(jnp.where(pick1, before, 0.0), axis=0, keepdims=True)
    rank2 = jnp.sum(jnp.where(pick2, before, 0.0), axis=0, keepdims=True)
    run = run_ref[...] + jnp.sum(onehot, axis=1, keepdims=True)
    run_ref[...] = run
    cnt_ref[...] = run

    row = lax.broadcasted_iota(jnp.int32, o_ref.shape, 0)
    vals = (i1.astype(F32), i2.astype(F32), v1 / tot, v2 / tot, rank1, rank2)
    out = jnp.zeros(o_ref.shape, F32)
    for r, v in enumerate(vals):
        out = jnp.where(row == r, v, out)
    o_ref[...] = out


def router(x, x_tail, w_t, b, *, tm, name):
    m = x.shape[0]
    tiles_per_b = 0 if x_tail is None else SEQ // tm
    if x_tail is None:
        x_tail, tail_spec = x, pl.BlockSpec((min(CHUNK, tm), D_MODEL), lambda i: (0, 0))
    else:
        tail_spec = pl.BlockSpec((CHUNK, D_MODEL), lambda i: (i // tiles_per_b, 0))
    return pl.pallas_call(
        functools.partial(_router_kernel, tiles_per_b=tiles_per_b),
        grid=(m // tm,),
        in_specs=[pl.BlockSpec((tm, D_MODEL), lambda i: (i, 0)),
                  tail_spec,
                  pl.BlockSpec((N_EXPERTS, D_MODEL), lambda i: (0, 0)),
                  pl.BlockSpec((N_EXPERTS, 1), lambda i: (0, 0))],
        out_specs=[pl.BlockSpec((8, tm), lambda i: (0, i)),
                   pl.BlockSpec((N_EXPERTS, 128), lambda i: (0, 0))],
        out_shape=[jax.ShapeDtypeStruct((8, m), F32),
                   jax.ShapeDtypeStruct((N_EXPERTS, 128), F32)],
        scratch_shapes=[pltpu.VMEM((N_EXPERTS, 128), F32)],
        compiler_params=_cparams(1, m),
        name=name,
    )(x, x_tail, w_t, b)


def _dispatch_kernel(d0_ref, d1_ref, starts_ref, ends_ref, nv_ref, x_ref, o_hbm, zero_ref, sem, *, cfg):
    i = pl.program_id(0)
    tm = cfg.tok_tile

    def fill_copy(row0):
        return pltpu.make_async_copy(zero_ref, o_hbm.at[pl.ds(row0, cfg.tile)], sem)

    @pl.when(i == 0)
    def _():
        zero_ref[...] = jnp.zeros(zero_ref.shape, zero_ref.dtype)
        for wait in (False, True):
            for e in range(N_EXPERTS):
                @pl.when(ends_ref[e] > starts_ref[e])
                def _():
                    cp = fill_copy(pl.multiple_of(ends_ref[e] - cfg.tile, cfg.tile))
                    cp.wait() if wait else cp.start()

            def tail(t, carry):
                cp = fill_copy(pl.multiple_of(t * cfg.tile, cfg.tile))
                cp.wait() if wait else cp.start()
                return carry

            lax.fori_loop(nv_ref[0], cfg.n_tiles, tail, 0)

    base = i * tm

    def body(it, carry):
        for j in range(ROW_DMA_UNROLL):
            r = it * ROW_DMA_UNROLL + j
            pltpu.make_async_copy(x_ref.at[pl.ds(r, 1)], o_hbm.at[pl.ds(d0_ref[base + r], 1)], sem).start(priority=0)
            pltpu.make_async_copy(x_ref.at[pl.ds(r, 1)], o_hbm.at[pl.ds(d1_ref[base + r], 1)], sem).start(priority=1)
        return carry

    lax.fori_loop(0, tm // ROW_DMA_UNROLL, body, 0)
    for _ in range(MOE_TOPK):
        pltpu.make_async_copy(x_ref, o_hbm.at[pl.ds(0, tm)], sem).wait()


def moe_dispatch(d0, d1, starts, ends, n_valid, x, cfg, name):
    tm = cfg.tok_tile
    return pl.pallas_call(
        functools.partial(_dispatch_kernel, cfg=cfg),
        grid_spec=pltpu.PrefetchScalarGridSpec(
            num_scalar_prefetch=5,
            grid=(cfg.n_tok // tm,),
            in_specs=[pl.BlockSpec((tm, D_MODEL), lambda i, *_: (i, 0))],
            out_specs=pl.BlockSpec(memory_space=pl.ANY),
            scratch_shapes=[pltpu.VMEM((cfg.tile, D_MODEL), F32), pltpu.SemaphoreType.DMA]),
        out_shape=jax.ShapeDtypeStruct((cfg.rows, D_MODEL), F32),
        compiler_params=_cparams(1, cfg.n_tok),
        name=name,
    )(d0, d1, starts, ends, n_valid, x)


def _expert_weights(first, w_refs, scratch_refs, precise):
    @pl.when(first)
    def _():
        for k, w_ref in enumerate(w_refs):
            if precise:
                hi, lo = _split_bf16(w_ref[...])
                scratch_refs[2 * k][...] = hi
                scratch_refs[2 * k + 1][...] = lo
            else:
                scratch_refs[k][...] = w_ref[...].astype(BF16)


def _expert_dot(x, scratch_refs, k, precise):
    if not precise:
        return _dot(x.astype(BF16), scratch_refs[k][...])
    xh, xl = _split_bf16(x)
    wh, wl = scratch_refs[2 * k][...], scratch_refs[2 * k + 1][...]
    return _dot(xh, wh) + (_dot(xl, wh) + _dot(xh, wl))


def _moe_up_kernel(te_ref, first_ref, nv_ref, x_ref, wg_ref, wu_ref, hh_ref, *scratch, precise, f_tiles):
    i = pl.program_id(0)

    @pl.when(i < nv_ref[0])
    def _():
        _expert_weights((first_ref[i] == 1) | (f_tiles > 1), (wg_ref, wu_ref), scratch, precise)
        x = x_ref[...]
        hg = _expert_dot(x, scratch, 0, precise)
        hu = _expert_dot(x, scratch, 1, precise)
        hh_ref[...] = (hg * jax.nn.sigmoid(hg) * hu).astype(hh_ref.dtype)

    @pl.when(i >= nv_ref[0])
    def _():
        hh_ref[...] = jnp.zeros(hh_ref.shape, hh_ref.dtype)


def _moe_down_kernel(te_ref, first_ref, nv_ref, hh_ref, wd_ref, y_ref, *scratch, precise):
    i = pl.program_id(0)

    @pl.when(i < nv_ref[0])
    def _():
        _expert_weights(first_ref[i] == 1, (wd_ref,), scratch, precise)
        y_ref[...] = _expert_dot(hh_ref[...], scratch, 0, precise)

    @pl.when(i >= nv_ref[0])
    def _():
        y_ref[...] = jnp.zeros(y_ref.shape, y_ref.dtype)


def moe_experts(tile_expert, tile_first, n_valid, x_sorted, w_gate, w_up, w_down, layer, cfg, name):
    tm = cfg.tile
    n_copies = 2 if cfg.precise else 1
    f_tiles = 2 if cfg.precise else 1
    fw = D_EXPERT // f_tiles
    row_map = lambda i, te, fi, nv: (jnp.minimum(i, nv[0] - 1), 0)
    wmap = lambda i, te, fi, nv: (layer, te[i], 0, 0)
    hh = pl.pallas_call(
        functools.partial(_moe_up_kernel, precise=cfg.precise, f_tiles=f_tiles),
        grid_spec=pltpu.PrefetchScalarGridSpec(
            num_scalar_prefetch=3,
            grid=(cfg.n_tiles, f_tiles),
            in_specs=[pl.BlockSpec((tm, D_MODEL), lambda i, f, te, fi, nv: (jnp.minimum(i, nv[0] - 1), 0)),
                      pl.BlockSpec((None, None, D_MODEL, fw), lambda i, f, te, fi, nv: (layer, te[i], 0, f)),
                      pl.BlockSpec((None, None, D_MODEL, fw), lambda i, f, te, fi, nv: (layer, te[i], 0, f))],
            out_specs=pl.BlockSpec((tm, fw), lambda i, f, te, fi, nv: (i, f)),
            scratch_shapes=[pltpu.VMEM((D_MODEL, fw), BF16)] * (2 * n_copies)),
        out_shape=jax.ShapeDtypeStruct((cfg.rows, D_EXPERT), F32 if cfg.precise else BF16),
        compiler_params=_cparams(2, cfg.n_tok),
        name=name + "_up",
    )(tile_expert, tile_first, n_valid, x_sorted, w_gate, w_up)
    return pl.pallas_call(
        functools.partial(_moe_down_kernel, precise=cfg.precise),
        grid_spec=pltpu.PrefetchScalarGridSpec(
            num_scalar_prefetch=3,
            grid=(cfg.n_tiles,),
            in_specs=[pl.BlockSpec((tm, D_EXPERT), row_map),
                      pl.BlockSpec((None, None, D_EXPERT, D_MODEL), wmap)],
            out_specs=pl.BlockSpec((tm, D_MODEL), lambda i, te, fi, nv: (i, 0)),
            scratch_shapes=[pltpu.VMEM((D_EXPERT, D_MODEL), BF16)] * n_copies),
        out_shape=jax.ShapeDtypeStruct((cfg.rows, D_MODEL), F32),
        compiler_params=_cparams(1, cfg.n_tok),
        name=name + "_down",
    )(tile_expert, tile_first, n_valid, hh, w_down)


def _combine_kernel(d0_ref, d1_ref, x_ref, w_ref, g_ref, b_ref, y_hbm, o_ref, buf_ref, sem, *, tm):
    base = pl.program_id(0) * tm

    def body(it, carry):
        for j in range(ROW_DMA_UNROLL):
            r = it * ROW_DMA_UNROLL + j
            pltpu.make_async_copy(y_hbm.at[pl.ds(d0_ref[base + r], 1)], buf_ref.at[0, pl.ds(r, 1)],
                                  sem).start(priority=0)
            pltpu.make_async_copy(y_hbm.at[pl.ds(d1_ref[base + r], 1)], buf_ref.at[1, pl.ds(r, 1)],
                                  sem).start(priority=1)
        return carry

    lax.fori_loop(0, tm // ROW_DMA_UNROLL, body, 0)
    for slot in range(2):
        pltpu.make_async_copy(y_hbm.at[pl.ds(0, tm)], buf_ref.at[slot], sem).wait()
    w = w_ref[...]
    y = ALPHA * x_ref[...] + w[:, 0:1] * buf_ref[0] + w[:, 1:2] * buf_ref[1]
    o_ref[...] = _ln(y, g_ref[...], b_ref[...])


def moe_combine_ln(d0, d1, x, w_tok, g, b, y_sorted, *, tm, name):
    m = x.shape[0]
    return pl.pallas_call(
        functools.partial(_combine_kernel, tm=tm),
        grid_spec=pltpu.PrefetchScalarGridSpec(
            num_scalar_prefetch=2,
            grid=(m // tm,),
            in_specs=[pl.BlockSpec((tm, D_MODEL), lambda i, a, c: (i, 0)),
                      pl.BlockSpec((tm, MOE_TOPK), lambda i, a, c: (i, 0)),
                      pl.BlockSpec((1, D_MODEL), lambda i, a, c: (0, 0)),
                      pl.BlockSpec((1, D_MODEL), lambda i, a, c: (0, 0)),
                      pl.BlockSpec(memory_space=pl.ANY)],
            out_specs=pl.BlockSpec((tm, D_MODEL), lambda i, a, c: (i, 0)),
            scratch_shapes=[pltpu.VMEM((2, tm, D_MODEL), F32), pltpu.SemaphoreType.DMA]),
        out_shape=jax.ShapeDtypeStruct((m, D_MODEL), F32),
        compiler_params=_cparams(1, m),
        name=name,
    )(d0, d1, x, w_tok, g, b, y_sorted)


def _routing_plan(route, counts, cfg):
    counts = counts[:, 0].astype(jnp.int32)
    padded = ((counts + cfg.tile - 1) // cfg.tile) * cfg.tile
    ends = jnp.cumsum(padded)
    starts = ends - padded
    experts = jnp.arange(N_EXPERTS, dtype=jnp.int32)[:, None]

    def slot_rows(k):
        picked = route[k].astype(jnp.int32)[None, :] == experts
        return jnp.sum(jnp.where(picked, starts[:, None], 0), axis=0) + route[4 + k].astype(jnp.int32)

    tile_start = jnp.arange(cfg.n_tiles, dtype=jnp.int32) * cfg.tile
    tile_expert = jnp.minimum(jnp.sum((tile_start[:, None] >= ends[None, :]).astype(jnp.int32), axis=1),
                              N_EXPERTS - 1)
    start_of_tile_expert = jnp.sum(jnp.where(tile_expert[:, None] == experts.T, starts[None, :], 0), axis=1)
    tile_first = (tile_start == start_of_tile_expert).astype(jnp.int32)
    n_valid = (ends[-1] // cfg.tile).astype(jnp.int32).reshape(1)
    return slot_rows(0), slot_rows(1), route[2:4].T, starts, ends, tile_expert, tile_first, n_valid


def moe_ln(x, x_route, layer, p, cfg, name):
    route, cnt = router(x, x_route, p["router_w_t"], p["router_b"], tm=cfg.tok_tile, name=name + "_router")
    d0, d1, w_tok, starts, ends, tile_expert, tile_first, n_valid = _routing_plan(route, cnt, cfg)
    x_sorted = moe_dispatch(d0, d1, starts, ends, n_valid, x, cfg, name + "_dispatch")
    y_sorted = moe_experts(tile_expert, tile_first, n_valid, x_sorted,
                           p["moe_w_gate"], p["moe_w_up"], p["moe_w_down"], layer, cfg, name)
    return moe_combine_ln(d0, d1, x, w_tok, p["moe_ln_g"][layer], p["moe_ln_b"][layer], y_sorted,
                          tm=cfg.tok_tile, name=name + "_combine")


def _rope_tables(pos):
    half = ROT_DIM // 2
    inv = jnp.power(ROPE_THETA, -jnp.arange(half, dtype=F32) / half)
    ang = pos.astype(F32)[:, None] * inv[None, :]
    cos, sin = jnp.cos(ang), jnp.sin(ang)
    n = pos.shape[0]
    ones = jnp.ones((n, HEAD_DIM - ROT_DIM), F32)
    zeros_h = jnp.zeros((n, half), F32)
    zeros_r = jnp.zeros((n, HEAD_DIM - ROT_DIM), F32)
    c = jnp.concatenate([cos, cos, ones], axis=1)
    sa = jnp.concatenate([zeros_h, sin, zeros_r], axis=1)
    sb = jnp.concatenate([-sin, zeros_h, zeros_r], axis=1)
    return c, sa, sb


def kernel(x_prompt, x_sample, mem_prompt, cache_c_k, cache_c_v, page_table, state_b_buf, state_d_buf, cache_mem_k, cache_mem_v, ab_w_in, ab_a_ln_g, ab_a_ln_b, ab_a_ws, ab_a_bs, ab_b_wg, ab_b_scale, ab_w_out, cd_w_in, cd_d_conv_w, cd_d_conv_b, cd_d_ln_g, cd_d_ln_b, cd_w_out, mix_ln_g, mix_ln_b, mem_w_q, mem_w_k, mem_w_v, mem_w_o, mem_ln_g, mem_ln_b, router_w, router_b, moe_w_gate, moe_w_up, moe_w_down, moe_ln_g, moe_ln_b):
    row = lambda v: v.reshape(1, -1)
    xp = x_prompt.reshape(N_PROMPT, D_MODEL)
    xs = x_sample.reshape(N_SAMPLE, D_MODEL)
    moe_p = {"router_w_t": router_w.T, "router_b": router_b.reshape(N_EXPERTS, 1),
             "moe_w_gate": moe_w_gate, "moe_w_up": moe_w_up,
             "moe_w_down": moe_w_down, "moe_ln_g": [row(moe_ln_g[l]) for l in range(DEPTH)],
             "moe_ln_b": [row(moe_ln_b[l]) for l in range(DEPTH)]}

    w_kv = split_weight(jnp.concatenate([mem_w_k[0], mem_w_k[1], mem_w_v[0], mem_w_v[1]], axis=1), True)
    n_memrows = BATCH * N_MEM
    kv = matmul(mem_prompt.reshape(n_memrows, D_MODEL), w_kv, m=n_memrows, tm=256, tn=1024, precise=True,
                name="mem_kv")
    pmk = [kv[:, l * MEM_WIDTH:(l + 1) * MEM_WIDTH].reshape(BATCH, N_MEM, MEM_WIDTH) for l in range(DEPTH)]
    pmv = [kv[:, (DEPTH + l) * MEM_WIDTH:(DEPTH + l + 1) * MEM_WIDTH].reshape(BATCH, N_MEM, MEM_WIDTH)
           for l in range(DEPTH)]
    smk = cache_mem_k.reshape(DEPTH, DEC_BATCH, N_MEM, MEM_WIDTH)
    smv = cache_mem_v.reshape(DEPTH, DEC_BATCH, N_MEM, MEM_WIDTH)

    mem_w = [(split_weight(mem_w_q[l], True), split_weight(mem_w_o[l], True)) for l in range(DEPTH)]
    mem_ln = [(row(mem_ln_g[l]), row(mem_ln_b[l])) for l in range(DEPTH)]

    def memattn_prompt(x, layer):
        wq, wo = split_weight(mem_w_q[layer], False), split_weight(mem_w_o[layer], False)
        return memory_attention_ln(x, wq, pmk[layer], pmv[layer], wo, *mem_ln[layer], m=N_PROMPT, tm=512,
                                   rows_per_b=SEQ, name="memattn_prompt")

    def memattn_sample(x, layer):
        wq, wo = mem_w[layer]
        return memory_attention_ln(x, wq, smk[layer], smv[layer], wo, *mem_ln[layer], m=N_SAMPLE, tm=N_SAMPLE,
                                   rows_per_b=DEC_SEQ, precise=True, name="memattn_sample")

    w_in0 = split_weight(ab_w_in[0], True)
    w_out0 = split_weight(ab_w_out[0], True)
    ab_prm = {"ln_g": row(ab_a_ln_g[0]), "ln_b": row(ab_a_ln_b[0]), "ws": ab_a_ws[0], "bs_t": ab_a_bs[0].T,
              "wg": ab_b_wg[0].astype(BF16), "scale": row(ab_b_scale[0])}
    ab_prm_precise = dict(ab_prm, wg=ab_b_wg[0])
    g, b = row(mix_ln_g[0]), row(mix_ln_b[0])
    n_chunks = SEQ // CHUNK
    n_tail = BATCH * CHUNK

    h0p = matmul(xp, split_weight(ab_w_in[0], False), m=N_PROMPT, tm=512, tn=1024, n_gelu=2, name="ab_in_prompt")
    mix_p, _ = mixer_ab(h0p, h0p, lambda b, c: (jnp.maximum(b * (SEQ // 16) + c * (CHUNK // 16) - 1, 0), 2),
                        nb=BATCH, n_chunks=n_chunks, pos0=0, has_ctx=False, prm=ab_prm, name="mixer_ab_prompt")
    x1p = matmul_res_ln([mix_p], [split_weight(ab_w_out[0], False)], xp, g, b, m=N_PROMPT, tm=512,
                        name="ab_out_prompt")
    x1p = memattn_prompt(x1p, 0)

    xt = x_prompt[:, SEQ - CHUNK:].reshape(n_tail, D_MODEL)
    h0t = matmul(xt, w_in0, m=n_tail, tm=CHUNK, tn=1024, n_gelu=2, precise=True, name="ab_in_tail")
    mix_t, _ = mixer_ab(h0t, h0p, lambda b, c: ((b + 1) * (SEQ // 16) - CHUNK // 16 - 1, 2), nb=BATCH, n_chunks=1,
                        pos0=SEQ - CHUNK, has_ctx=True, prm=ab_prm_precise, precise=True, name="mixer_ab_tail")
    x1t = matmul_res_ln([mix_t], [w_out0], xt, g, b, m=n_tail, tm=CHUNK, precise=True, name="ab_out_tail")
    wq, wo = mem_w[0]
    x1t = memory_attention_ln(x1t, wq, pmk[0], pmv[0], wo, *mem_ln[0], m=n_tail, tm=CHUNK, rows_per_b=CHUNK,
                              precise=True, name="memattn_tail")

    h0s = matmul(xs, w_in0, m=N_SAMPLE, tm=N_SAMPLE, tn=1024, n_gelu=2, precise=True, name="ab_in_sample")
    h0s_pad = jnp.pad(h0s.reshape(DEC_BATCH, DEC_SEQ, -1), ((0, 0), (0, CHUNK - DEC_SEQ), (0, 0)))
    h0s_pad = h0s_pad.reshape(DEC_BATCH * CHUNK, -1)
    zctx = jnp.pad(state_b_buf[0], ((0, 0), (16 - B_BUF, 0), (0, 0))).reshape(DEC_BATCH * 16, B_WIDTH)
    mix_s, vn_s = mixer_ab(h0s_pad, zctx, lambda b, c: (b, 0), nb=DEC_BATCH, n_chunks=1, pos0=PAST_LEN,
                           has_ctx=True, prm=ab_prm_precise, precise=True, name="mixer_ab_sample")
    mix_s = mix_s.reshape(DEC_BATCH, CHUNK, -1)[:, :DEC_SEQ].reshape(N_SAMPLE, -1)
    x1s = matmul_res_ln([mix_s], [w_out0], xs, g, b, m=N_SAMPLE, tm=N_SAMPLE, precise=True, name="ab_out_sample")
    x1s = memattn_sample(x1s, 0)

    x2p = moe_ln(x1p, x1t, 0, moe_p, MOE_PROMPT, "moe_prompt")
    x2s = moe_ln(x1s, None, 0, moe_p, MOE_SAMPLE, "moe_sample")

    w_in1 = split_weight(cd_w_in[0], True)
    h1p = matmul(x2p, split_weight(cd_w_in[0], False), m=N_PROMPT, tm=512, tn=1024, name="cd_in_prompt")
    h1s = matmul(x2s, w_in1, m=N_SAMPLE, tm=N_SAMPLE, tn=1024, precise=True, name="cd_in_sample")
    tabs_p = _rope_tables(jnp.arange(SEQ, dtype=jnp.int32))
    tabs_s = _rope_tables(PAST_LEN + (jnp.arange(N_SAMPLE, dtype=jnp.int32) % DEC_SEQ))
    qr_p, kr_p = rope_qk(h1p, tabs_p, m=N_PROMPT, tm=256, xoff=0, tab_blocks=SEQ // 256, name="rope_prompt")
    qr_s, kr_s = rope_qk(h1s, tabs_s, m=N_SAMPLE, tm=N_SAMPLE, xoff=0, tab_blocks=1, name="rope_sample")
    c_p = moba_prompt(qr_p, kr_p, h1p, nb=BATCH, seq=SEQ)
    v_s = h1s[:, 2 * C_WIDTH:3 * C_WIDTH]
    q_combo = qr_s.reshape(DEC_BATCH, N_COMBO, HEAD_DIM)
    m_p, l_p, o_p, ksum = paged_partials(page_table, q_combo, cache_c_k, cache_c_v, 0)
    c_s = paged_merge(q_combo, kr_s.reshape(DEC_BATCH, N_COMBO, HEAD_DIM),
                      v_s.reshape(DEC_BATCH, N_COMBO, HEAD_DIM), m_p, l_p, o_p, ksum)
    c_s = c_s.reshape(N_SAMPLE, C_WIDTH)

    cd_prm = {"conv_w": cd_d_conv_w[0], "conv_b": row(cd_d_conv_b[0]), "ln_g": row(cd_d_ln_g[0]),
              "ln_b": row(cd_d_ln_b[0])}
    col_a = 3 * C_WIDTH // D_WIDTH
    tiles_b = SEQ // 256
    halo_per_tile = 256 // CONV_HALO

    def prev_map(col):
        return lambda b, t: (jnp.maximum((b * tiles_b + t) * halo_per_tile - 1, 0), col)

    d_p, tail_p = conv_module(h1p, col_a, h1p, h1p, prev_map(col_a), prev_map(col_a + 1), nb=BATCH,
                              n_tiles=tiles_b, tm=256, rt=128, tail=CONV_HALO, xoff=0, prev_is_state=False,
                              prm=cd_prm, name="conv_prompt")
    gl_s = jnp.pad(h1s[:, 3 * C_WIDTH:].reshape(DEC_BATCH, DEC_SEQ, 2 * D_WIDTH), ((0, 0), (0, 8 - DEC_SEQ), (0, 0)))
    gl_s = gl_s.reshape(DEC_BATCH * 8, 2 * D_WIDTH)
    dctx = jnp.pad(state_d_buf[0], ((0, 0), (CONV_HALO - D_BUF, 0), (0, 0))).reshape(DEC_BATCH * CONV_HALO, D_WIDTH)
    d_s, tail_s = conv_module(gl_s, 0, dctx, dctx, lambda b, t: (b, 0), lambda b, t: (b, 0), nb=DEC_BATCH,
                              n_tiles=1, tm=8, rt=8, tail=8, xoff=0, prev_is_state=True, prm=cd_prm,
                              name="conv_sample", out_dtype=F32)
    d_s = d_s.reshape(DEC_BATCH, 8, D_WIDTH)[:, :DEC_SEQ].reshape(N_SAMPLE, D_WIDTH)
    w_out1c = split_weight(cd_w_out[0][:C_WIDTH], True)
    w_out1d = split_weight(cd_w_out[0][C_WIDTH:], True)
    g, b = row(mix_ln_g[1]), row(mix_ln_b[1])
    w_out1 = cd_w_out[0].astype(BF16)
    x3p = matmul_res_ln([c_p, d_p], [(w_out1[:C_WIDTH],), (w_out1[C_WIDTH:],)], x2p, g, b, m=N_PROMPT, tm=512,
                        name="cd_out_prompt")
    x3s = matmul_res_ln([c_s, d_s], [w_out1c, w_out1d], x2s, g, b, m=N_SAMPLE, tm=N_SAMPLE, precise=True,
                        name="cd_out_sample")
    y_p = moe_ln(memattn_prompt(x3p, 1), None, 1, moe_p, MOE_PROMPT, "moe_prompt")
    y_s = moe_ln(memattn_sample(x3s, 1), None, 1, moe_p, MOE_SAMPLE, "moe_sample")

    kv_shape_p = (1, BATCH, SEQ, C_HEADS, HEAD_DIM)
    kv_shape_s = (1, DEC_BATCH, DEC_SEQ, C_HEADS, HEAD_DIM)
    z_p = h0p[:, 2 * A_WIDTH:].reshape(BATCH, SEQ, B_WIDTH)
    z_s = h0s[:, 2 * A_WIDTH:].reshape(DEC_BATCH, DEC_SEQ, B_WIDTH)
    h_s = tail_s.reshape(DEC_BATCH, 8, D_WIDTH)[:, :DEC_SEQ]
    mem_shape = (BATCH, N_MEM, MEM_HEADS, MEM_HEAD_DIM)
    return (y_p.reshape(BATCH, SEQ, D_MODEL),
            y_s.reshape(DEC_BATCH, DEC_SEQ, D_MODEL),
            kr_p.reshape(kv_shape_p),
            h1p[:, 2 * C_WIDTH:3 * C_WIDTH].reshape(kv_shape_p),
            kr_s.reshape(kv_shape_s),
            v_s.reshape(kv_shape_s),
            z_p[:, SEQ - B_BUF:][None],
            jnp.concatenate([state_b_buf[0], z_s], axis=1)[:, DEC_SEQ:][None],
            vn_s.reshape(DEC_BATCH, CHUNK, A_WIDTH)[:, :DEC_SEQ][None],
            tail_p.reshape(BATCH, CONV_HALO, D_WIDTH)[:, CONV_HALO - D_BUF:][None],
            jnp.concatenate([state_d_buf[0], h_s], axis=1)[:, DEC_SEQ:][None],
            jnp.stack([m_.reshape(mem_shape) for m_ in pmk]),
            jnp.stack([m_.reshape(mem_shape) for m_ in pmv]))
```

```python
import functools
import math
from typing import NamedTuple

import jax
import jax.numpy as jnp
from jax import lax
from jax.experimental import pallas as pl
from jax.experimental.pallas import tpu as pltpu

F32 = jnp.float32
BF16 = jnp.bfloat16
HIGHEST = lax.Precision.HIGHEST

D_MODEL = 2048
BATCH = 4
SEQ = 2048
DEPTH = 2
DEC_BATCH = 8
DEC_SEQ = 4
PAST_LEN = 16384
PAGE_SIZE = 128
A_WIDTH = 1024
CHUNK = 128
A_GROUPS = 8
B_WIDTH = 1024
POOL_WINDOWS = (2, 4, 8, 16)
B_GROUP_DIM = B_WIDTH // len(POOL_WINDOWS)
B_BUF = 15
C_HEADS = 8
HEAD_DIM = 128
C_WIDTH = 1024
ROT_DIM = 32
ROPE_THETA = 500000.0
MOBA_BLOCK = 256
MOBA_TOPK = 3
D_WIDTH = 1024
CONV_WIDTH = 31
D_BUF = 30
N_MEM = 256
MEM_HEADS = 4
MEM_HEAD_DIM = 128
MEM_WIDTH = 512
N_EXPERTS = 16
N_EXPERT_GROUPS = 4
EXPERTS_PER_GROUP = 4
MOE_TOPK = 2
D_EXPERT = 1024
ALPHA = (2 * DEPTH) ** 0.25
LN_EPS = 1e-5

N_PROMPT = BATCH * SEQ
N_SAMPLE = DEC_BATCH * DEC_SEQ
N_TOK = N_PROMPT + N_SAMPLE
N_PAST_BLOCKS = PAST_LEN // MOBA_BLOCK
PAGES_PER_BLOCK = MOBA_BLOCK // PAGE_SIZE
N_COMBO = DEC_SEQ * C_HEADS


class MoeCfg(NamedTuple):
    n_tok: int
    tok_tile: int
    tile: int
    rows: int
    n_tiles: int
    precise: bool


def _moe_cfg(n_tok, tile, precise):
    rows = ((n_tok * MOE_TOPK + N_EXPERTS * (tile - 1)) // tile + 1) * tile
    return MoeCfg(n_tok, tile, tile, rows, rows // tile, precise)


MOE_PROMPT = _moe_cfg(N_PROMPT, 256, False)
MOE_SAMPLE = _moe_cfg(N_SAMPLE, N_SAMPLE, True)

ROW_DMA_UNROLL = 8
VMEM_LIMIT = 56 * 1024 * 1024
VMEM_LIMIT_SMALL = 40 * 1024 * 1024
SMALL_CALL_ROWS = 1024


def _cparams(n_axes, rows=None):
    vmem = VMEM_LIMIT if rows is None or rows > SMALL_CALL_ROWS else VMEM_LIMIT_SMALL
    return pltpu.CompilerParams(dimension_semantics=("arbitrary",) * n_axes, vmem_limit_bytes=vmem)


def _ln(x, g, b):
    mu = jnp.mean(x, axis=-1, keepdims=True)
    xc = x - mu
    var = jnp.mean(xc * xc, axis=-1, keepdims=True)
    return xc * lax.rsqrt(var + LN_EPS) * g + b


def _dot(a, b):
    return jnp.dot(a, b, preferred_element_type=F32)


def _dot_nt(a, b, precision=None):
    return lax.dot_general(a, b, (((1,), (1,)), ((), ())), precision=precision, preferred_element_type=F32)


def _split_bf16(v):
    hi = v.astype(BF16)
    return hi, (v.astype(F32) - hi.astype(F32)).astype(BF16)


def _mm(a, b, precise, nt=False):
    dot = _dot_nt if nt else _dot
    if not precise:
        b0 = b[0] if isinstance(b, tuple) else b
        return dot(a.astype(BF16), b0.astype(BF16))
    ah, al = _split_bf16(a)
    bh, bl = b if isinstance(b, tuple) else _split_bf16(b)
    return dot(ah, bh) + (dot(al, bh) + dot(ah, bl))


def _load_all(refs):
    return tuple(r[...] for r in refs)


def split_weight(w, precise):
    if not precise:
        return (w.astype(BF16),)
    bits = lax.bitcast_convert_type(w, jnp.uint32) & jnp.uint32(0xFFFF0000)
    hi = lax.bitcast_convert_type(bits, F32)
    return hi.astype(BF16), (w - hi).astype(BF16)


def _mm_kernel(x_ref, *refs, n_gelu, precise):
    o_ref = refs[-1]
    acc = _mm(x_ref[...], _load_all(refs[:-1]), precise)
    if n_gelu:
        j = pl.program_id(0)

        @pl.when(j < n_gelu)
        def _():
            o_ref[...] = jax.nn.gelu(acc).astype(o_ref.dtype)

        @pl.when(j >= n_gelu)
        def _():
            o_ref[...] = acc.astype(o_ref.dtype)
    else:
        o_ref[...] = acc.astype(o_ref.dtype)


def matmul(x, w, *, m, tm, tn, xoff=0, n_gelu=0, precise=False, name="mm"):
    k = x.shape[1]
    n = w[0].shape[1]
    return pl.pallas_call(
        functools.partial(_mm_kernel, n_gelu=n_gelu, precise=precise),
        grid=(n // tn, m // tm),
        in_specs=[pl.BlockSpec((tm, k), lambda j, i: (i + xoff, 0))]
        + [pl.BlockSpec((k, tn), lambda j, i: (0, j))] * len(w),
        out_specs=pl.BlockSpec((tm, tn), lambda j, i: (i, j)),
        out_shape=jax.ShapeDtypeStruct((m, n), F32),
        compiler_params=_cparams(2, m),
        name=name,
    )(x, *w)


def _mm_res_ln_kernel(*refs, n_in, precise):
    n_w = 2 if precise else 1
    a_refs = refs[:n_in]
    w_refs = refs[n_in:n_in + n_in * n_w]
    r_ref, g_ref, b_ref, o_ref = refs[n_in + n_in * n_w:]
    acc = None
    for k, a_ref in enumerate(a_refs):
        d = _mm(a_ref[...], _load_all(w_refs[k * n_w:(k + 1) * n_w]), precise)
        acc = d if acc is None else acc + d
    o_ref[...] = _ln(ALPHA * r_ref[...] + acc, g_ref[...], b_ref[...])


def matmul_res_ln(a_list, w_list, resid, g, b, *, m, tm, roff=0, precise=False, name="mm_res_ln"):
    n_in = len(a_list)
    w_flat = [part for w in w_list for part in w]
    in_specs = [pl.BlockSpec((tm, a.shape[1]), lambda i: (i, 0)) for a in a_list]
    in_specs += [pl.BlockSpec(w.shape, lambda i: (0, 0)) for w in w_flat]
    in_specs += [pl.BlockSpec((tm, D_MODEL), lambda i: (i + roff, 0)),
                 pl.BlockSpec((1, D_MODEL), lambda i: (0, 0)),
                 pl.BlockSpec((1, D_MODEL), lambda i: (0, 0))]
    return pl.pallas_call(
        functools.partial(_mm_res_ln_kernel, n_in=n_in, precise=precise),
        grid=(m // tm,),
        in_specs=in_specs,
        out_specs=pl.BlockSpec((tm, D_MODEL), lambda i: (i, 0)),
        out_shape=jax.ShapeDtypeStruct((m, D_MODEL), F32),
        compiler_params=_cparams(1, m),
        name=name,
    )(*a_list, *w_flat, resid, g, b)


def _mixer_ab_kernel(u_ref, v_ref, z_ref, zp_ref, lng_ref, lnb_ref, ws_ref, bs_ref, wg_ref, sc_ref,
                     mix_ref, vn_ref, *, pos0, has_ctx, precise):
    c = pl.program_id(1)
    vn = _ln(v_ref[...], lng_ref[...], lnb_ref[...])
    vn_ref[...] = vn
    u = u_ref[...]
    row = lax.broadcasted_iota(jnp.int32, (CHUNK, CHUNK), 0)
    col = lax.broadcasted_iota(jnp.int32, (CHUNK, CHUNK), 1)
    causal = col <= row
    gd = A_WIDTH // A_GROUPS
    for g in range(A_GROUPS):
        sl = slice(g * gd, (g + 1) * gd)
        w = jnp.where(causal, ws_ref[g], 0.0)
        mixed = _mm(w, vn[:, sl], precise) + bs_ref[:, g:g + 1]
        mix_ref[:, sl] = (u[:, sl] * mixed).astype(mix_ref.dtype)

    z = z_ref[...]
    zp = zp_ref[...]
    if not has_ctx:
        zp = jnp.where(c == 0, 0.0, zp)
    zext = jnp.concatenate([zp, z], axis=0)
    pos = pos0 + c * CHUNK + lax.broadcasted_iota(jnp.int32, (CHUNK, 1), 0)
    for gi, wdw in enumerate(POOL_WINDOWS):
        sl = slice(gi * B_GROUP_DIM, (gi + 1) * B_GROUP_DIM)
        s = zext[:, sl]
        sh = 1
        while sh < wdw:
            s = s + pltpu.roll(s, sh, axis=0)
            sh *= 2
        cnt = jnp.minimum(wdw, pos + 1).astype(F32)
        d = s[16:, :] / cnt - z[:, sl]
        bo = _mm(d, wg_ref[gi], precise) * sc_ref[:, sl]
        mix_ref[:, A_WIDTH + gi * B_GROUP_DIM:A_WIDTH + (gi + 1) * B_GROUP_DIM] = bo.astype(mix_ref.dtype)


def mixer_ab(h, zprev, zprev_map, *, nb, n_chunks, pos0, has_ctx, prm, name, precise=False):
    m = nb * n_chunks * CHUNK
    row = lambda b, c: b * n_chunks + c
    const2 = lambda b, c: (0, 0)
    return pl.pallas_call(
        functools.partial(_mixer_ab_kernel, pos0=pos0, has_ctx=has_ctx, precise=precise),
        grid=(nb, n_chunks),
        in_specs=[pl.BlockSpec((CHUNK, A_WIDTH), lambda b, c: (row(b, c), 0)),
                  pl.BlockSpec((CHUNK, A_WIDTH), lambda b, c: (row(b, c), 1)),
                  pl.BlockSpec((CHUNK, B_WIDTH), lambda b, c: (row(b, c), 2)),
                  pl.BlockSpec((16, B_WIDTH), zprev_map),
                  pl.BlockSpec((1, A_WIDTH), const2),
                  pl.BlockSpec((1, A_WIDTH), const2),
                  pl.BlockSpec((A_GROUPS, CHUNK, CHUNK), lambda b, c: (0, 0, 0)),
                  pl.BlockSpec((CHUNK, A_GROUPS), const2),
                  pl.BlockSpec((len(POOL_WINDOWS), B_GROUP_DIM, B_GROUP_DIM), lambda b, c: (0, 0, 0)),
                  pl.BlockSpec((1, B_WIDTH), const2)],
        out_specs=[pl.BlockSpec((CHUNK, A_WIDTH + B_WIDTH), lambda b, c: (row(b, c), 0)),
                   pl.BlockSpec((CHUNK, A_WIDTH), lambda b, c: (row(b, c), 0))],
        out_shape=[jax.ShapeDtypeStruct((m, A_WIDTH + B_WIDTH), F32 if precise else BF16),
                   jax.ShapeDtypeStruct((m, A_WIDTH), F32)],
        compiler_params=_cparams(2, m),
        name=name,
    )(h, h, h, zprev, prm["ln_g"], prm["ln_b"], prm["ws"], prm["bs_t"], prm["wg"], prm["scale"])


def _rope_kernel(q_ref, k_ref, c_ref, sa_ref, sb_ref, qo_ref, ko_ref):
    c = c_ref[...]
    sa = sa_ref[...]
    sb = sb_ref[...]
    half = ROT_DIM // 2
    for h in range(C_HEADS):
        sl = slice(h * HEAD_DIM, (h + 1) * HEAD_DIM)
        for src, dst in ((q_ref, qo_ref), (k_ref, ko_ref)):
            x = src[:, sl]
            dst[:, sl] = (x * c + pltpu.roll(x, half, axis=1) * sa
                          + pltpu.roll(x, HEAD_DIM - half, axis=1) * sb)


def rope_qk(h, tabs, *, m, tm, xoff, tab_blocks, name):
    tmap = lambda i: (i % tab_blocks, 0)
    return pl.pallas_call(
        _rope_kernel,
        grid=(m // tm,),
        in_specs=[pl.BlockSpec((tm, C_WIDTH), lambda i: (i + xoff, 0)),
                  pl.BlockSpec((tm, C_WIDTH), lambda i: (i + xoff, 1)),
                  pl.BlockSpec((tm, HEAD_DIM), tmap),
                  pl.BlockSpec((tm, HEAD_DIM), tmap),
                  pl.BlockSpec((tm, HEAD_DIM), tmap)],
        out_specs=[pl.BlockSpec((tm, C_WIDTH), lambda i: (i, 0)),
                   pl.BlockSpec((tm, C_WIDTH), lambda i: (i, 0))],
        out_shape=[jax.ShapeDtypeStruct((m, C_WIDTH), F32),
                   jax.ShapeDtypeStruct((m, C_WIDTH), F32)],
        compiler_params=_cparams(1, m),
        name=name,
    )(h, h, *tabs)


def _moba_kernel(q_ref, qall_ref, k_ref, v_ref, o_ref, selt_ref, *, n_blocks):
    i = pl.program_id(2)
    blk_rows = MOBA_BLOCK
    seq = n_blocks * blk_rows
    scale = HEAD_DIM ** -0.5

    @pl.when(i == 0)
    def _():
        kmean = jnp.concatenate(
            [jnp.mean(k_ref[j * blk_rows:(j + 1) * blk_rows, :], axis=0, keepdims=True) for j in range(n_blocks)],
            axis=0)
        bst = _mm(kmean, qall_ref[...], True, nt=True)
        blk = lax.broadcasted_iota(jnp.int32, (n_blocks, seq), 0)
        own = lax.broadcasted_iota(jnp.int32, (n_blocks, seq), 1) >> int(math.log2(blk_rows))
        work = jnp.where(blk < own, bst, -jnp.inf)
        sel = jnp.zeros((n_blocks, seq), F32)
        for _ in range(MOBA_TOPK):
            mx = jnp.max(work, axis=0, keepdims=True)
            first = jnp.min(jnp.where(work == mx, blk, n_blocks), axis=0, keepdims=True)
            hit = (blk == first) & (mx > -jnp.inf)
            sel = jnp.where(hit, 1.0, sel)
            work = jnp.where(hit, -jnp.inf, work)
        selt_ref[...] = jnp.zeros(selt_ref.shape, F32)
        for c in range(n_blocks):
            selt_ref[c, 0:n_blocks, :] = sel[:, c * blk_rows:(c + 1) * blk_rows]

    qb = q_ref[...].astype(BF16)
    sel = jnp.transpose(selt_ref[i])

    r_i = lax.broadcasted_iota(jnp.int32, (blk_rows, blk_rows), 0)
    c_i = lax.broadcasted_iota(jnp.int32, (blk_rows, blk_rows), 1)
    causal = (c_i <= r_i).astype(F32)

    for c in range(n_blocks):
        @pl.when(i == c)
        def _():
            nk = (c + 1) * blk_rows
            s = _dot_nt(qb, k_ref[0:nk, :].astype(BF16)) * scale
            allowed = jnp.concatenate(
                [jnp.broadcast_to(sel[:, j:j + 1], (blk_rows, blk_rows)) for j in range(c)] + [causal], axis=1)
            s = jnp.where(allowed > 0.0, s, -jnp.inf)
            m = jnp.max(s, axis=-1, keepdims=True)
            p = jnp.exp(s - m)
            l = jnp.sum(p, axis=-1, keepdims=True)
            o = _dot(p.astype(BF16), v_ref[0:nk, :].astype(BF16))
            o_ref[...] = (o / l).astype(o_ref.dtype)


def moba_prompt(q_rot, k_rot, h, *, nb, seq, name="moba_prompt"):
    n_blocks = seq // MOBA_BLOCK
    v_col0 = 2 * C_WIDTH // HEAD_DIM
    return pl.pallas_call(
        functools.partial(_moba_kernel, n_blocks=n_blocks),
        grid=(nb, C_HEADS, n_blocks),
        in_specs=[pl.BlockSpec((MOBA_BLOCK, HEAD_DIM), lambda b, hh, i: (b * n_blocks + i, hh)),
                  pl.BlockSpec((seq, HEAD_DIM), lambda b, hh, i: (b, hh)),
                  pl.BlockSpec((seq, HEAD_DIM), lambda b, hh, i: (b, hh)),
                  pl.BlockSpec((seq, HEAD_DIM), lambda b, hh, i: (b, v_col0 + hh))],
        out_specs=pl.BlockSpec((MOBA_BLOCK, HEAD_DIM), lambda b, hh, i: (b * n_blocks + i, hh)),
        out_shape=jax.ShapeDtypeStruct((nb * seq, C_WIDTH), BF16),
        scratch_shapes=[pltpu.VMEM((n_blocks, 128, MOBA_BLOCK), F32)],
        compiler_params=_cparams(3),
        name=name,
    )(q_rot, q_rot, k_rot, h)


CONV_HALO = 32
SUBLANES = 8


def _conv_kernel(ga_ref, gg_ref, pa_ref, pg_ref, w_ref, bdw_ref, lng_ref, lnb_ref,
                 o_ref, tail_ref, ext_ref, sh_ref, y_ref, *, tm, rt, tail, prev_is_state):
    t = pl.program_id(1)
    hcur = ga_ref[...] * jax.nn.sigmoid(gg_ref[...])
    if prev_is_state:
        hprev = pa_ref[...]
    else:
        hprev = pa_ref[...] * jax.nn.sigmoid(pg_ref[...])
        hprev = jnp.where(t == 0, 0.0, hprev)
    ext_ref[0:CONV_HALO, :] = hprev
    ext_ref[CONV_HALO:CONV_HALO + tm, :] = hcur
    tail_ref[...] = hcur[tm - tail:, :]
    off = CONV_HALO - D_BUF
    sh_rows = sh_ref.shape[1]
    for s in range(1, SUBLANES):
        sh_ref[s - 1] = ext_ref[s:s + sh_rows, :]
    for cc in range(D_WIDTH // 128):
        cs = slice(cc * 128, (cc + 1) * 128)
        for rc in range(tm // rt):
            r0 = rc * rt
            acc = jnp.zeros((rt, 128), F32)
            for j in range(CONV_WIDTH):
                s, a = (off + j) % SUBLANES, (off + j) // SUBLANES
                row0 = r0 + a * SUBLANES
                src = ext_ref[row0:row0 + rt, cs] if s == 0 else sh_ref[s - 1, row0:row0 + rt, cs]
                acc = acc + w_ref[j:j + 1, cs] * src
            y_ref[r0:r0 + rt, cs] = acc + bdw_ref[:, cs]
    y = _ln(y_ref[...], lng_ref[...], lnb_ref[...])
    o_ref[...] = (y * jax.nn.sigmoid(y)).astype(o_ref.dtype)


def conv_module(h, col_a, prev_a, prev_g, prev_map_a, prev_map_g, *, nb, n_tiles, tm, rt, tail, xoff,
                prev_is_state, prm, name, out_dtype=BF16):
    m = nb * n_tiles * tm
    row = lambda b, t: b * n_tiles + t + xoff
    const2 = lambda b, t: (0, 0)
    return pl.pallas_call(
        functools.partial(_conv_kernel, tm=tm, rt=rt, tail=tail, prev_is_state=prev_is_state),
        grid=(nb, n_tiles),
        in_specs=[pl.BlockSpec((tm, D_WIDTH), lambda b, t: (row(b, t), col_a)),
                  pl.BlockSpec((tm, D_WIDTH), lambda b, t: (row(b, t), col_a + 1)),
                  pl.BlockSpec((CONV_HALO, D_WIDTH), prev_map_a),
                  pl.BlockSpec((CONV_HALO, D_WIDTH), prev_map_g),
                  pl.BlockSpec((CONV_WIDTH, D_WIDTH), const2),
                  pl.BlockSpec((1, D_WIDTH), const2),
                  pl.BlockSpec((1, D_WIDTH), const2),
                  pl.BlockSpec((1, D_WIDTH), const2)],
        out_specs=[pl.BlockSpec((tm, D_WIDTH), lambda b, t: (b * n_tiles + t, 0)),
                   pl.BlockSpec((tail, D_WIDTH), lambda b, t: (b, 0))],
        out_shape=[jax.ShapeDtypeStruct((m, D_WIDTH), out_dtype),
                   jax.ShapeDtypeStruct((nb * tail, D_WIDTH), F32)],
        scratch_shapes=[pltpu.VMEM((CONV_HALO + tm, D_WIDTH), F32),
                        pltpu.VMEM((SUBLANES - 1, CONV_HALO + tm - SUBLANES, D_WIDTH), F32),
                        pltpu.VMEM((tm, D_WIDTH), F32)],
        compiler_params=_cparams(2, m),
        name=name,
    )(h, h, prev_a, prev_g, prm["conv_w"], prm["conv_b"], prm["ln_g"], prm["ln_b"])


PAGED_BLOCKS_PER_STEP = 4
PAGED_PAGES_PER_STEP = PAGED_BLOCKS_PER_STEP * PAGES_PER_BLOCK


def _paged_partial_kernel(pt_ref, q_ref, *refs):
    del pt_ref
    pages = refs[:2 * PAGED_PAGES_PER_STEP]
    m_ref, l_ref, o_ref, ks_ref = refs[2 * PAGED_PAGES_PER_STEP:]
    rows = PAGE_SIZE * C_HEADS
    scale = HEAD_DIM ** -0.5
    k_refs = pages[:PAGED_PAGES_PER_STEP]
    v_refs = pages[PAGED_PAGES_PER_STEP:]
    qb = q_ref[...].astype(BF16)
    c_i = lax.broadcasted_iota(jnp.int32, (N_COMBO, rows), 0)
    l_i = lax.broadcasted_iota(jnp.int32, (N_COMBO, rows), 1)
    same_head = (l_i & (C_HEADS - 1)) == (c_i & (C_HEADS - 1))
    for blk in range(PAGED_BLOCKS_PER_STEP):
        ks = None
        s = []
        for pg in range(PAGES_PER_BLOCK):
            k = k_refs[blk * PAGES_PER_BLOCK + pg][...]
            ksum = jnp.sum(k, axis=0)
            ks = ksum if ks is None else ks + ksum
            sp = _dot_nt(qb, k.reshape(rows, HEAD_DIM).astype(BF16)) * scale
            s.append(jnp.where(same_head, sp, -jnp.inf))
        ks_ref[blk] = ks
        m = jnp.max(jnp.maximum(s[0], s[1]), axis=-1, keepdims=True)
        l = None
        o = None
        for pg in range(PAGES_PER_BLOCK):
            p = jnp.exp(s[pg] - m)
            v = v_refs[blk * PAGES_PER_BLOCK + pg][...].reshape(rows, HEAD_DIM).astype(BF16)
            lp = jnp.sum(p, axis=-1, keepdims=True)
            op = _dot(p.astype(BF16), v)
            l = lp if l is None else l + lp
            o = op if o is None else o + op
        m_ref[blk] = jnp.broadcast_to(m, (N_COMBO, HEAD_DIM))
        l_ref[blk] = jnp.broadcast_to(l, (N_COMBO, HEAD_DIM))
        o_ref[blk] = o


def paged_partials(page_table, q_combo, cache_k, cache_v, layer):
    page_block = (None, None, PAGE_SIZE, C_HEADS, HEAD_DIM)
    nbs = PAGED_BLOCKS_PER_STEP

    def page_map(which):
        return lambda b, n, pt: (layer, pt[b, PAGED_PAGES_PER_STEP * n + which], 0, 0, 0)

    page_specs = [pl.BlockSpec(page_block, page_map(w)) for w in range(PAGED_PAGES_PER_STEP)]
    part_shape = jax.ShapeDtypeStruct((DEC_BATCH, N_PAST_BLOCKS, N_COMBO, HEAD_DIM), F32)
    part_spec = pl.BlockSpec((None, nbs, N_COMBO, HEAD_DIM), lambda b, n, pt: (b, n, 0, 0))
    return pl.pallas_call(
        _paged_partial_kernel,
        grid_spec=pltpu.PrefetchScalarGridSpec(
            num_scalar_prefetch=1,
            grid=(DEC_BATCH, N_PAST_BLOCKS // nbs),
            in_specs=[pl.BlockSpec((None, N_COMBO, HEAD_DIM), lambda b, n, pt: (b, 0, 0))] + page_specs + page_specs,
            out_specs=[part_spec, part_spec, part_spec,
                       pl.BlockSpec((None, nbs, C_HEADS, HEAD_DIM), lambda b, n, pt: (b, n, 0, 0))]),
        out_shape=[part_shape, part_shape, part_shape,
                   jax.ShapeDtypeStruct((DEC_BATCH, N_PAST_BLOCKS, C_HEADS, HEAD_DIM), F32)],
        compiler_params=_cparams(2, N_SAMPLE),
        name="paged_partials",
    )(page_table, q_combo, *([cache_k] * PAGED_PAGES_PER_STEP), *([cache_v] * PAGED_PAGES_PER_STEP))


def _paged_merge_kernel(q_ref, kn_ref, vn_ref, m_ref, l_ref, o_ref, ks_ref, out_ref):
    nb = N_PAST_BLOCKS
    scale = HEAD_DIM ** -0.5
    q = q_ref[...]
    kmean = ks_ref[...] * (1.0 / MOBA_BLOCK)
    kmean = jnp.concatenate([kmean] * DEC_SEQ, axis=1)
    bs = jnp.sum(kmean * q[None], axis=-1, keepdims=True)
    work = jnp.broadcast_to(bs, (nb, N_COMBO, HEAD_DIM))
    n_i = lax.broadcasted_iota(jnp.int32, (nb, N_COMBO, HEAD_DIM), 0)
    sel = n_i < 0
    for _ in range(MOBA_TOPK):
        mx = jnp.max(work, axis=0, keepdims=True)
        first = jnp.min(jnp.where(work == mx, n_i, nb), axis=0, keepdims=True)
        hit = n_i == first
        sel = sel | hit
        work = jnp.where(hit, -jnp.inf, work)

    s = _dot_nt(q.astype(BF16), kn_ref[...].astype(BF16)) * scale
    c_i = lax.broadcasted_iota(jnp.int32, s.shape, 0)
    l_i = lax.broadcasted_iota(jnp.int32, s.shape, 1)
    ok = ((l_i & (C_HEADS - 1)) == (c_i & (C_HEADS - 1))) & ((l_i >> 3) <= (c_i >> 3))
    s = jnp.where(ok, s, -jnp.inf)
    m_own = jnp.max(s, axis=-1, keepdims=True)
    p = jnp.exp(s - m_own)
    l_own = jnp.sum(p, axis=-1, keepdims=True)
    o_own = _dot(p.astype(BF16), vn_ref[...].astype(BF16))

    mp = m_ref[...]
    m_all = jnp.maximum(jnp.max(jnp.where(sel, mp, -jnp.inf), axis=0), m_own)
    w = jnp.where(sel, jnp.exp(mp - m_all[None]), 0.0)
    w_own = jnp.exp(m_own - m_all)
    den = jnp.sum(w * l_ref[...], axis=0) + w_own * l_own
    num = jnp.sum(w * o_ref[...], axis=0) + w_own * o_own
    out_ref[...] = num / den


def paged_merge(q_combo, k_new, v_new, m_p, l_p, o_p, ksum):
    combo = pl.BlockSpec((None, N_COMBO, HEAD_DIM), lambda b: (b, 0, 0))
    part = pl.BlockSpec((None, N_PAST_BLOCKS, N_COMBO, HEAD_DIM), lambda b: (b, 0, 0, 0))
    return pl.pallas_call(
        _paged_merge_kernel,
        grid=(DEC_BATCH,),
        in_specs=[combo, combo, combo, part, part, part,
                  pl.BlockSpec((None, N_PAST_BLOCKS, C_HEADS, HEAD_DIM), lambda b: (b, 0, 0, 0))],
        out_specs=combo,
        out_shape=jax.ShapeDtypeStruct((DEC_BATCH, N_COMBO, HEAD_DIM), F32),
        compiler_params=_cparams(1, N_SAMPLE),
        name="paged_merge",
    )(q_combo, k_new, v_new, m_p, l_p, o_p, ksum)


def _memattn_kernel(x_ref, mk_ref, mv_ref, g_ref, b_ref, *refs, nb_tile, rows_per_b, precise):
    n_w = 2 if precise else 1
    wq = _load_all(refs[:n_w])
    wo = _load_all(refs[n_w:2 * n_w])
    o_ref = refs[2 * n_w]
    scale = MEM_HEAD_DIM ** -0.5
    x = x_ref[...]
    tm = x.shape[0]
    q = _mm(x, wq, precise)
    row_b = lax.broadcasted_iota(jnp.int32, (tm, 1), 0) >> int(math.log2(rows_per_b))
    heads = []
    for hh in range(MEM_HEADS):
        sl = slice(hh * MEM_HEAD_DIM, (hh + 1) * MEM_HEAD_DIM)
        qh = q[:, sl]
        oh = None
        for bb in range(nb_tile):
            s = _mm(qh, mk_ref[bb, :, sl], precise, nt=True) * scale
            s = s - jnp.max(s, axis=-1, keepdims=True)
            p = jnp.exp(s)
            p = p / jnp.sum(p, axis=-1, keepdims=True)
            ob = _mm(p, mv_ref[bb, :, sl], precise)
            if nb_tile > 1:
                ob = jnp.where(row_b == bb, ob, 0.0)
            oh = ob if oh is None else oh + ob
        heads.append(oh)
    o = jnp.concatenate(heads, axis=-1)
    y = ALPHA * x + _mm(o, wo, precise)
    o_ref[...] = _ln(y, g_ref[...], b_ref[...])


def memory_attention_ln(x, wq, mk, mv, wo, g, b, *, m, tm, rows_per_b, name, precise=False, b0=0):
    if rows_per_b >= tm:
        nb_tile = 1
        tiles_per_b = rows_per_b // tm
        kv_map = lambda i: (b0 + i // tiles_per_b, 0, 0)
    else:
        nb_tile = tm // rows_per_b
        kv_map = lambda i: (i, 0, 0)
    const2 = lambda i: (0, 0)
    return pl.pallas_call(
        functools.partial(_memattn_kernel, nb_tile=nb_tile, rows_per_b=rows_per_b, precise=precise),
        grid=(m // tm,),
        in_specs=[pl.BlockSpec((tm, D_MODEL), lambda i: (i, 0)),
                  pl.BlockSpec((nb_tile, N_MEM, MEM_WIDTH), kv_map),
                  pl.BlockSpec((nb_tile, N_MEM, MEM_WIDTH), kv_map),
                  pl.BlockSpec((1, D_MODEL), const2),
                  pl.BlockSpec((1, D_MODEL), const2)]
        + [pl.BlockSpec((D_MODEL, MEM_WIDTH), const2)] * len(wq)
        + [pl.BlockSpec((MEM_WIDTH, D_MODEL), const2)] * len(wo),
        out_specs=pl.BlockSpec((tm, D_MODEL), lambda i: (i, 0)),
        out_shape=jax.ShapeDtypeStruct((m, D_MODEL), F32),
        compiler_params=_cparams(1, m),
        name=name,
    )(x, mk, mv, g, b, *wq, *wo)


def _router_kernel(x_ref, xt_ref, w_ref, b_ref, o_ref, cnt_ref, run_ref, *, tiles_per_b):
    step = pl.program_id(0)
    tm = x_ref.shape[0]

    @pl.when(step == 0)
    def _():
        run_ref[...] = jnp.zeros(run_ref.shape, F32)

    x = x_ref[...]
    if tiles_per_b:
        is_last = lax.rem(step, tiles_per_b) == tiles_per_b - 1
        x = jnp.concatenate([x[:tm - CHUNK], jnp.where(is_last, xt_ref[...], x[tm - CHUNK:])], axis=0)
    xh, xl = _split_bf16(x)
    wh, wl = _split_bf16(w_ref[...])
    logits = _dot_nt(wh, xh) + (_dot_nt(wh, xl) + _dot_nt(wl, xh)) + b_ref[...]
    logits = logits - jnp.max(logits, axis=0, keepdims=True)
    e = jnp.exp(logits)
    probs = e / jnp.sum(e, axis=0, keepdims=True)
    p = [probs[j:j + 1, :] for j in range(N_EXPERTS)]
    gbest = None
    gsel = None
    for g in range(N_EXPERT_GROUPS):
        a, b_, c, d = p[4 * g:4 * g + 4]
        hi1, lo1 = jnp.maximum(a, b_), jnp.minimum(a, b_)
        hi2, lo2 = jnp.maximum(c, d), jnp.minimum(c, d)
        gs = jnp.maximum(hi1, hi2) + jnp.maximum(jnp.minimum(hi1, hi2), jnp.maximum(lo1, lo2))
        if g == 0:
            gbest, gsel = gs, jnp.zeros(gs.shape, jnp.int32)
        else:
            better = gs > gbest
            gbest = jnp.where(better, gs, gbest)
            gsel = jnp.where(better, g, gsel)
    cand = [jnp.where(gsel == j // EXPERTS_PER_GROUP, p[j], -1.0) for j in range(N_EXPERTS)]
    v1 = cand[0]
    i1 = jnp.zeros(v1.shape, jnp.int32)
    for j in range(1, N_EXPERTS):
        better = cand[j] > v1
        v1 = jnp.where(better, cand[j], v1)
        i1 = jnp.where(better, j, i1)
    v2 = jnp.full(v1.shape, -2.0, F32)
    i2 = jnp.zeros(v1.shape, jnp.int32)
    for j in range(N_EXPERTS):
        better = (cand[j] > v2) & (i1 != j)
        v2 = jnp.where(better, cand[j], v2)
        i2 = jnp.where(better, j, i2)
    tot = v1 + v2

    e_i = lax.broadcasted_iota(jnp.int32, (N_EXPERTS, tm), 0)
    pick1 = e_i == i1
    pick2 = e_i == i2
    onehot = (pick1 | pick2).astype(F32)
    t_r = lax.broadcasted_iota(jnp.int32, (tm, tm), 0)
    t_c = lax.broadcasted_iota(jnp.int32, (tm, tm), 1)
    earlier = (t_r < t_c).astype(BF16)
    before = _dot(onehot.astype(BF16), earlier) + run_ref[:, 0:1]
    rank1 = jnp.sum(jnp.where(pick1, before, 0.0), axis=0, keepdims=True)
    rank2 = jnp.sum(jnp.where(pick2, before, 0.0), axis=0, keepdims=True)
    run = run_ref[...] + jnp.sum(onehot, axis=1, keepdims=True)
    run_ref[...] = run
    cnt_ref[...] = run

    row = lax.broadcasted_iota(jnp.int32, o_ref.shape, 0)
    vals = (i1.astype(F32), i2.astype(F32), v1 / tot, v2 / tot, rank1, rank2)
    out = jnp.zeros(o_ref.shape, F32)
    for r, v in enumerate(vals):
        out = jnp.where(row == r, v, out)
    o_ref[...] = out


def router(x, x_tail, w_t, b, *, tm, name):
    m = x.shape[0]
    tiles_per_b = 0 if x_tail is None else SEQ // tm
    if x_tail is None:
        x_tail, tail_spec = x, pl.BlockSpec((min(CHUNK, tm), D_MODEL), lambda i: (0, 0))
    else:
        tail_spec = pl.BlockSpec((CHUNK, D_MODEL), lambda i: (i // tiles_per_b, 0))
    return pl.pallas_call(
        functools.partial(_router_kernel, tiles_per_b=tiles_per_b),
        grid=(m // tm,),
        in_specs=[pl.BlockSpec((tm, D_MODEL), lambda i: (i, 0)),
                  tail_spec,
                  pl.BlockSpec((N_EXPERTS, D_MODEL), lambda i: (0, 0)),
                  pl.BlockSpec((N_EXPERTS, 1), lambda i: (0, 0))],
        out_specs=[pl.BlockSpec((8, tm), lambda i: (0, i)),
                   pl.BlockSpec((N_EXPERTS, 128), lambda i: (0, 0))],
        out_shape=[jax.ShapeDtypeStruct((8, m), F32),
                   jax.ShapeDtypeStruct((N_EXPERTS, 128), F32)],
        scratch_shapes=[pltpu.VMEM((N_EXPERTS, 128), F32)],
        compiler_params=_cparams(1, m),
        name=name,
    )(x, x_tail, w_t, b)


def _dispatch_kernel(d0_ref, d1_ref, starts_ref, ends_ref, nv_ref, x_ref, o_hbm, zero_ref, sem, *, cfg):
    i = pl.program_id(0)
    tm = cfg.tok_tile

    def fill_copy(row0):
        return pltpu.make_async_copy(zero_ref, o_hbm.at[pl.ds(row0, cfg.tile)], sem)

    @pl.when(i == 0)
    def _():
        zero_ref[...] = jnp.zeros(zero_ref.shape, zero_ref.dtype)
        for wait in (False, True):
            for e in range(N_EXPERTS):
                @pl.when(ends_ref[e] > starts_ref[e])
                def _():
                    cp = fill_copy(pl.multiple_of(ends_ref[e] - cfg.tile, cfg.tile))
                    cp.wait() if wait else cp.start()

            def tail(t, carry):
                cp = fill_copy(pl.multiple_of(t * cfg.tile, cfg.tile))
                cp.wait() if wait else cp.start()
                return carry

            lax.fori_loop(nv_ref[0], cfg.n_tiles, tail, 0)

    base = i * tm

    def body(it, carry):
        for j in range(ROW_DMA_UNROLL):
            r = it * ROW_DMA_UNROLL + j
            pltpu.make_async_copy(x_ref.at[pl.ds(r, 1)], o_hbm.at[pl.ds(d0_ref[base + r], 1)], sem).start(priority=0)
            pltpu.make_async_copy(x_ref.at[pl.ds(r, 1)], o_hbm.at[pl.ds(d1_ref[base + r], 1)], sem).start(priority=1)
        return carry

    lax.fori_loop(0, tm // ROW_DMA_UNROLL, body, 0)
    for _ in range(MOE_TOPK):
        pltpu.make_async_copy(x_ref, o_hbm.at[pl.ds(0, tm)], sem).wait()


def moe_dispatch(d0, d1, starts, ends, n_valid, x, cfg, name):
    tm = cfg.tok_tile
    return pl.pallas_call(
        functools.partial(_dispatch_kernel, cfg=cfg),
        grid_spec=pltpu.PrefetchScalarGridSpec(
            num_scalar_prefetch=5,
            grid=(cfg.n_tok // tm,),
            in_specs=[pl.BlockSpec((tm, D_MODEL), lambda i, *_: (i, 0))],
            out_specs=pl.BlockSpec(memory_space=pl.ANY),
            scratch_shapes=[pltpu.VMEM((cfg.tile, D_MODEL), F32), pltpu.SemaphoreType.DMA]),
        out_shape=jax.ShapeDtypeStruct((cfg.rows, D_MODEL), F32),
        compiler_params=_cparams(1, cfg.n_tok),
        name=name,
    )(d0, d1, starts, ends, n_valid, x)


def _expert_weights(first, w_refs, scratch_refs, precise):
    @pl.when(first)
    def _():
        for k, w_ref in enumerate(w_refs):
            if precise:
                hi, lo = _split_bf16(w_ref[...])
                scratch_refs[2 * k][...] = hi
                scratch_refs[2 * k + 1][...] = lo
            else:
                scratch_refs[k][...] = w_ref[...].astype(BF16)


def _expert_dot(x, scratch_refs, k, precise):
    if not precise:
        return _dot(x.astype(BF16), scratch_refs[k][...])
    xh, xl = _split_bf16(x)
    wh, wl = scratch_refs[2 * k][...], scratch_refs[2 * k + 1][...]
    return _dot(xh, wh) + (_dot(xl, wh) + _dot(xh, wl))


def _moe_up_kernel(te_ref, first_ref, nv_ref, x_ref, wg_ref, wu_ref, hh_ref, *scratch, precise, f_tiles):
    i = pl.program_id(0)

    @pl.when(i < nv_ref[0])
    def _():
        _expert_weights((first_ref[i] == 1) | (f_tiles > 1), (wg_ref, wu_ref), scratch, precise)
        x = x_ref[...]
        hg = _expert_dot(x, scratch, 0, precise)
        hu = _expert_dot(x, scratch, 1, precise)
        hh_ref[...] = (hg * jax.nn.sigmoid(hg) * hu).astype(hh_ref.dtype)

    @pl.when(i >= nv_ref[0])
    def _():
        hh_ref[...] = jnp.zeros(hh_ref.shape, hh_ref.dtype)


def _moe_down_kernel(te_ref, first_ref, nv_ref, hh_ref, wd_ref, y_ref, *scratch, precise):
    i = pl.program_id(0)

    @pl.when(i < nv_ref[0])
    def _():
        _expert_weights(first_ref[i] == 1, (wd_ref,), scratch, precise)
        y_ref[...] = _expert_dot(hh_ref[...], scratch, 0, precise)

    @pl.when(i >= nv_ref[0])
    def _():
        y_ref[...] = jnp.zeros(y_ref.shape, y_ref.dtype)


def moe_experts(tile_expert, tile_first, n_valid, x_sorted, w_gate, w_up, w_down, layer, cfg, name):
    tm = cfg.tile
    n_copies = 2 if cfg.precise else 1
    f_tiles = 2 if cfg.precise else 1
    fw = D_EXPERT // f_tiles
    row_map = lambda i, te, fi, nv: (jnp.minimum(i, nv[0] - 1), 0)
    wmap = lambda i, te, fi, nv: (layer, te[i], 0, 0)
    hh = pl.pallas_call(
        functools.partial(_moe_up_kernel, precise=cfg.precise, f_tiles=f_tiles),
        grid_spec=pltpu.PrefetchScalarGridSpec(
            num_scalar_prefetch=3,
            grid=(cfg.n_tiles, f_tiles),
            in_specs=[pl.BlockSpec((tm, D_MODEL), lambda i, f, te, fi, nv: (jnp.minimum(i, nv[0] - 1), 0)),
                      pl.BlockSpec((None, None, D_MODEL, fw), lambda i, f, te, fi, nv: (layer, te[i], 0, f)),
                      pl.BlockSpec((None, None, D_MODEL, fw), lambda i, f, te, fi, nv: (layer, te[i], 0, f))],
            out_specs=pl.BlockSpec((tm, fw), lambda i, f, te, fi, nv: (i, f)),
            scratch_shapes=[pltpu.VMEM((D_MODEL, fw), BF16)] * (2 * n_copies)),
        out_shape=jax.ShapeDtypeStruct((cfg.rows, D_EXPERT), F32 if cfg.precise else BF16),
        compiler_params=_cparams(2, cfg.n_tok),
        name=name + "_up",
    )(tile_expert, tile_first, n_valid, x_sorted, w_gate, w_up)
    return pl.pallas_call(
        functools.partial(_moe_down_kernel, precise=cfg.precise),
        grid_spec=pltpu.PrefetchScalarGridSpec(
            num_scalar_prefetch=3,
            grid=(cfg.n_tiles,),
            in_specs=[pl.BlockSpec((tm, D_EXPERT), row_map),
                      pl.BlockSpec((None, None, D_EXPERT, D_MODEL), wmap)],
            out_specs=pl.BlockSpec((tm, D_MODEL), lambda i, te, fi, nv: (i, 0)),
            scratch_shapes=[pltpu.VMEM((D_EXPERT, D_MODEL), BF16)] * n_copies),
        out_shape=jax.ShapeDtypeStruct((cfg.rows, D_MODEL), F32),
        compiler_params=_cparams(1, cfg.n_tok),
        name=name + "_down",
    )(tile_expert, tile_first, n_valid, hh, w_down)


def _combine_kernel(d0_ref, d1_ref, x_ref, w_ref, g_ref, b_ref, y_hbm, o_ref, buf_ref, sem, *, tm):
    base = pl.program_id(0) * tm

    def body(it, carry):
        for j in range(ROW_DMA_UNROLL):
            r = it * ROW_DMA_UNROLL + j
            pltpu.make_async_copy(y_hbm.at[pl.ds(d0_ref[base + r], 1)], buf_ref.at[0, pl.ds(r, 1)],
                                  sem).start(priority=0)
            pltpu.make_async_copy(y_hbm.at[pl.ds(d1_ref[base + r], 1)], buf_ref.at[1, pl.ds(r, 1)],
                                  sem).start(priority=1)
        return carry

    lax.fori_loop(0, tm // ROW_DMA_UNROLL, body, 0)
    for slot in range(2):
        pltpu.make_async_copy(y_hbm.at[pl.ds(0, tm)], buf_ref.at[slot], sem).wait()
    w = w_ref[...]
    y = ALPHA * x_ref[...] + w[:, 0:1] * buf_ref[0] + w[:, 1:2] * buf_ref[1]
    o_ref[...] = _ln(y, g_ref[...], b_ref[...])


def moe_combine_ln(d0, d1, x, w_tok, g, b, y_sorted, *, tm, name):
    m = x.shape[0]
    return pl.pallas_call(
        functools.partial(_combine_kernel, tm=tm),
        grid_spec=pltpu.PrefetchScalarGridSpec(
            num_scalar_prefetch=2,
            grid=(m // tm,),
            in_specs=[pl.BlockSpec((tm, D_MODEL), lambda i, a, c: (i, 0)),
                      pl.BlockSpec((tm, MOE_TOPK), lambda i, a, c: (i, 0)),
                      pl.BlockSpec((1, D_MODEL), lambda i, a, c: (0, 0)),
                      pl.BlockSpec((1, D_MODEL), lambda i, a, c: (0, 0)),
                      pl.BlockSpec(memory_space=pl.ANY)],
            out_specs=pl.BlockSpec((tm, D_MODEL), lambda i, a, c: (i, 0)),
            scratch_shapes=[pltpu.VMEM((2, tm, D_MODEL), F32), pltpu.SemaphoreType.DMA]),
        out_shape=jax.ShapeDtypeStruct((m, D_MODEL), F32),
        compiler_params=_cparams(1, m),
        name=name,
    )(d0, d1, x, w_tok, g, b, y_sorted)


def _routing_plan(route, counts, cfg):
    counts = counts[:, 0].astype(jnp.int32)
    padded = ((counts + cfg.tile - 1) // cfg.tile) * cfg.tile
    ends = jnp.cumsum(padded)
    starts = ends - padded
    experts = jnp.arange(N_EXPERTS, dtype=jnp.int32)[:, None]

    def slot_rows(k):
        picked = route[k].astype(jnp.int32)[None, :] == experts
        return jnp.sum(jnp.where(picked, starts[:, None], 0), axis=0) + route[4 + k].astype(jnp.int32)

    tile_start = jnp.arange(cfg.n_tiles, dtype=jnp.int32) * cfg.tile
    tile_expert = jnp.minimum(jnp.sum((tile_start[:, None] >= ends[None, :]).astype(jnp.int32), axis=1),
                              N_EXPERTS - 1)
    start_of_tile_expert = jnp.sum(jnp.where(tile_expert[:, None] == experts.T, starts[None, :], 0), axis=1)
    tile_first = (tile_start == start_of_tile_expert).astype(jnp.int32)
    n_valid = (ends[-1] // cfg.tile).astype(jnp.int32).reshape(1)
    return slot_rows(0), slot_rows(1), route[2:4].T, starts, ends, tile_expert, tile_first, n_valid


def moe_ln(x, x_route, layer, p, cfg, name):
    route, cnt = router(x, x_route, p["router_w_t"], p["router_b"], tm=cfg.tok_tile, name=name + "_router")
    d0, d1, w_tok, starts, ends, tile_expert, tile_first, n_valid = _routing_plan(route, cnt, cfg)
    x_sorted = moe_dispatch(d0, d1, starts, ends, n_valid, x, cfg, name + "_dispatch")
    y_sorted = moe_experts(tile_expert, tile_first, n_valid, x_sorted,
                           p["moe_w_gate"], p["moe_w_up"], p["moe_w_down"], layer, cfg, name)
    return moe_combine_ln(d0, d1, x, w_tok, p["moe_ln_g"][layer], p["moe_ln_b"][layer], y_sorted,
                          tm=cfg.tok_tile, name=name + "_combine")


def _rope_tables(pos):
    half = ROT_DIM // 2
    inv = jnp.power(ROPE_THETA, -jnp.arange(half, dtype=F32) / half)
    ang = pos.astype(F32)[:, None] * inv[None, :]
    cos, sin = jnp.cos(ang), jnp.sin(ang)
    n = pos.shape[0]
    ones = jnp.ones((n, HEAD_DIM - ROT_DIM), F32)
    zeros_h = jnp.zeros((n, half), F32)
    zeros_r = jnp.zeros((n, HEAD_DIM - ROT_DIM), F32)
    c = jnp.concatenate([cos, cos, ones], axis=1)
    sa = jnp.concatenate([zeros_h, sin, zeros_r], axis=1)
    sb = jnp.concatenate([-sin, zeros_h, zeros_r], axis=1)
    return c, sa, sb


def kernel(x_prompt, x_sample, mem_prompt, cache_c_k, cache_c_v, page_table, state_b_buf, state_d_buf, cache_mem_k, cache_mem_v, ab_w_in, ab_a_ln_g, ab_a_ln_b, ab_a_ws, ab_a_bs, ab_b_wg, ab_b_scale, ab_w_out, cd_w_in, cd_d_conv_w, cd_d_conv_b, cd_d_ln_g, cd_d_ln_b, cd_w_out, mix_ln_g, mix_ln_b, mem_w_q, mem_w_k, mem_w_v, mem_w_o, mem_ln_g, mem_ln_b, router_w, router_b, moe_w_gate, moe_w_up, moe_w_down, moe_ln_g, moe_ln_b):
    row = lambda v: v.reshape(1, -1)
    xp = x_prompt.reshape(N_PROMPT, D_MODEL)
    xs = x_sample.reshape(N_SAMPLE, D_MODEL)
    moe_p = {"router_w_t": router_w.T, "router_b": router_b.reshape(N_EXPERTS, 1),
             "moe_w_gate": moe_w_gate, "moe_w_up": moe_w_up,
             "moe_w_down": moe_w_down, "moe_ln_g": [row(moe_ln_g[l]) for l in range(DEPTH)],
             "moe_ln_b": [row(moe_ln_b[l]) for l in range(DEPTH)]}

    w_kv = split_weight(jnp.concatenate([mem_w_k[0], mem_w_k[1], mem_w_v[0], mem_w_v[1]], axis=1), True)
    n_memrows = BATCH * N_MEM
    kv = matmul(mem_prompt.reshape(n_memrows, D_MODEL), w_kv, m=n_memrows, tm=256, tn=1024, precise=True,
                name="mem_kv")
    pmk = [kv[:, l * MEM_WIDTH:(l + 1) * MEM_WIDTH].reshape(BATCH, N_MEM, MEM_WIDTH) for l in range(DEPTH)]
    pmv = [kv[:, (DEPTH + l) * MEM_WIDTH:(DEPTH + l + 1) * MEM_WIDTH].reshape(BATCH, N_MEM, MEM_WIDTH)
           for l in range(DEPTH)]
    smk = cache_mem_k.reshape(DEPTH, DEC_BATCH, N_MEM, MEM_WIDTH)
    smv = cache_mem_v.reshape(DEPTH, DEC_BATCH, N_MEM, MEM_WIDTH)

    mem_w = [(split_weight(mem_w_q[l], True), split_weight(mem_w_o[l], True)) for l in range(DEPTH)]
    mem_ln = [(row(mem_ln_g[l]), row(mem_ln_b[l])) for l in range(DEPTH)]

    def memattn_prompt(x, layer):
        wq, wo = split_weight(mem_w_q[layer], False), split_weight(mem_w_o[layer], False)
        return memory_attention_ln(x, wq, pmk[layer], pmv[layer], wo, *mem_ln[layer], m=N_PROMPT, tm=512,
                                   rows_per_b=SEQ, name="memattn_prompt")

    def memattn_sample(x, layer):
        wq, wo = mem_w[layer]
        return memory_attention_ln(x, wq, smk[layer], smv[layer], wo, *mem_ln[layer], m=N_SAMPLE, tm=N_SAMPLE,
                                   rows_per_b=DEC_SEQ, precise=True, name="memattn_sample")

    w_in0 = split_weight(ab_w_in[0], True)
    w_out0 = split_weight(ab_w_out[0], True)
    ab_prm = {"ln_g": row(ab_a_ln_g[0]), "ln_b": row(ab_a_ln_b[0]), "ws": ab_a_ws[0], "bs_t": ab_a_bs[0].T,
              "wg": ab_b_wg[0].astype(BF16), "scale": row(ab_b_scale[0])}
    ab_prm_precise = dict(ab_prm, wg=ab_b_wg[0])
    g, b = row(mix_ln_g[0]), row(mix_ln_b[0])
    n_chunks = SEQ // CHUNK
    n_tail = BATCH * CHUNK

    h0p = matmul(xp, split_weight(ab_w_in[0], False), m=N_PROMPT, tm=1024, tn=1024, n_gelu=2, name="ab_in_prompt")
    mix_p, _ = mixer_ab(h0p, h0p, lambda b, c: (jnp.maximum(b * (SEQ // 16) + c * (CHUNK // 16) - 1, 0), 2),
                        nb=BATCH, n_chunks=n_chunks, pos0=0, has_ctx=False, prm=ab_prm, name="mixer_ab_prompt")
    x1p = matmul_res_ln([mix_p], [split_weight(ab_w_out[0], False)], xp, g, b, m=N_PROMPT, tm=512,
                        name="ab_out_prompt")
    x1p = memattn_prompt(x1p, 0)

    xt = x_prompt[:, SEQ - CHUNK:].reshape(n_tail, D_MODEL)
    h0t = matmul(xt, w_in0, m=n_tail, tm=CHUNK, tn=1024, n_gelu=2, precise=True, name="ab_in_tail")
    mix_t, _ = mixer_ab(h0t, h0p, lambda b, c: ((b + 1) * (SEQ // 16) - CHUNK // 16 - 1, 2), nb=BATCH, n_chunks=1,
                        pos0=SEQ - CHUNK, has_ctx=True, prm=ab_prm_precise, precise=True, name="mixer_ab_tail")
    x1t = matmul_res_ln([mix_t], [w_out0], xt, g, b, m=n_tail, tm=CHUNK, precise=True, name="ab_out_tail")
    wq, wo = mem_w[0]
    x1t = memory_attention_ln(x1t, wq, pmk[0], pmv[0], wo, *mem_ln[0], m=n_tail, tm=CHUNK, rows_per_b=CHUNK,
                              precise=True, name="memattn_tail")

    h0s = matmul(xs, w_in0, m=N_SAMPLE, tm=N_SAMPLE, tn=1024, n_gelu=2, precise=True, name="ab_in_sample")
    h0s_pad = jnp.pad(h0s.reshape(DEC_BATCH, DEC_SEQ, -1), ((0, 0), (0, CHUNK - DEC_SEQ), (0, 0)))
    h0s_pad = h0s_pad.reshape(DEC_BATCH * CHUNK, -1)
    zctx = jnp.pad(state_b_buf[0], ((0, 0), (16 - B_BUF, 0), (0, 0))).reshape(DEC_BATCH * 16, B_WIDTH)
    mix_s, vn_s = mixer_ab(h0s_pad, zctx, lambda b, c: (b, 0), nb=DEC_BATCH, n_chunks=1, pos0=PAST_LEN,
                           has_ctx=True, prm=ab_prm_precise, precise=True, name="mixer_ab_sample")
    mix_s = mix_s.reshape(DEC_BATCH, CHUNK, -1)[:, :DEC_SEQ].reshape(N_SAMPLE, -1)
    x1s = matmul_res_ln([mix_s], [w_out0], xs, g, b, m=N_SAMPLE, tm=N_SAMPLE, precise=True, name="ab_out_sample")
    x1s = memattn_sample(x1s, 0)

    x2p = moe_ln(x1p, x1t, 0, moe_p, MOE_PROMPT, "moe_prompt")
    x2s = moe_ln(x1s, None, 0, moe_p, MOE_SAMPLE, "moe_sample")

    w_in1 = split_weight(cd_w_in[0], True)
    h1p = matmul(x2p, split_weight(cd_w_in[0], False), m=N_PROMPT, tm=1024, tn=1024, name="cd_in_prompt")
    h1s = matmul(x2s, w_in1, m=N_SAMPLE, tm=N_SAMPLE, tn=1024, precise=True, name="cd_in_sample")
    tabs_p = _rope_tables(jnp.arange(SEQ, dtype=jnp.int32))
    tabs_s = _rope_tables(PAST_LEN + (jnp.arange(N_SAMPLE, dtype=jnp.int32) % DEC_SEQ))
    qr_p, kr_p = rope_qk(h1p, tabs_p, m=N_PROMPT, tm=256, xoff=0, tab_blocks=SEQ // 256, name="rope_prompt")
    qr_s, kr_s = rope_qk(h1s, tabs_s, m=N_SAMPLE, tm=N_SAMPLE, xoff=0, tab_blocks=1, name="rope_sample")
    c_p = moba_prompt(qr_p, kr_p, h1p, nb=BATCH, seq=SEQ)
    v_s = h1s[:, 2 * C_WIDTH:3 * C_WIDTH]
    q_combo = qr_s.reshape(DEC_BATCH, N_COMBO, HEAD_DIM)
    m_p, l_p, o_p, ksum = paged_partials(page_table, q_combo, cache_c_k, cache_c_v, 0)
    c_s = paged_merge(q_combo, kr_s.reshape(DEC_BATCH, N_COMBO, HEAD_DIM),
                      v_s.reshape(DEC_BATCH, N_COMBO, HEAD_DIM), m_p, l_p, o_p, ksum)
    c_s = c_s.reshape(N_SAMPLE, C_WIDTH)

    cd_prm = {"conv_w": cd_d_conv_w[0], "conv_b": row(cd_d_conv_b[0]), "ln_g": row(cd_d_ln_g[0]),
              "ln_b": row(cd_d_ln_b[0])}
    col_a = 3 * C_WIDTH // D_WIDTH
    tiles_b = SEQ // 256
    halo_per_tile = 256 // CONV_HALO

    def prev_map(col):
        return lambda b, t: (jnp.maximum((b * tiles_b + t) * halo_per_tile - 1, 0), col)

    d_p, tail_p = conv_module(h1p, col_a, h1p, h1p, prev_map(col_a), prev_map(col_a + 1), nb=BATCH,
                              n_tiles=tiles_b, tm=256, rt=128, tail=CONV_HALO, xoff=0, prev_is_state=False,
                              prm=cd_prm, name="conv_prompt")
    gl_s = jnp.pad(h1s[:, 3 * C_WIDTH:].reshape(DEC_BATCH, DEC_SEQ, 2 * D_WIDTH), ((0, 0), (0, 8 - DEC_SEQ), (0, 0)))
    gl_s = gl_s.reshape(DEC_BATCH * 8, 2 * D_WIDTH)
    dctx = jnp.pad(state_d_buf[0], ((0, 0), (CONV_HALO - D_BUF, 0), (0, 0))).reshape(DEC_BATCH * CONV_HALO, D_WIDTH)
    d_s, tail_s = conv_module(gl_s, 0, dctx, dctx, lambda b, t: (b, 0), lambda b, t: (b, 0), nb=DEC_BATCH,
                              n_tiles=1, tm=8, rt=8, tail=8, xoff=0, prev_is_state=True, prm=cd_prm,
                              name="conv_sample", out_dtype=F32)
    d_s = d_s.reshape(DEC_BATCH, 8, D_WIDTH)[:, :DEC_SEQ].reshape(N_SAMPLE, D_WIDTH)
    w_out1c = split_weight(cd_w_out[0][:C_WIDTH], True)
    w_out1d = split_weight(cd_w_out[0][C_WIDTH:], True)
    g, b = row(mix_ln_g[1]), row(mix_ln_b[1])
    w_out1 = cd_w_out[0].astype(BF16)
    x3p = matmul_res_ln([c_p, d_p], [(w_out1[:C_WIDTH],), (w_out1[C_WIDTH:],)], x2p, g, b, m=N_PROMPT, tm=512,
                        name="cd_out_prompt")
    x3s = matmul_res_ln([c_s, d_s], [w_out1c, w_out1d], x2s, g, b, m=N_SAMPLE, tm=N_SAMPLE, precise=True,
                        name="cd_out_sample")
    y_p = moe_ln(memattn_prompt(x3p, 1), None, 1, moe_p, MOE_PROMPT, "moe_prompt")
    y_s = moe_ln(memattn_sample(x3s, 1), None, 1, moe_p, MOE_SAMPLE, "moe_sample")

    kv_shape_p = (1, BATCH, SEQ, C_HEADS, HEAD_DIM)
    kv_shape_s = (1, DEC_BATCH, DEC_SEQ, C_HEADS, HEAD_DIM)
    z_p = h0p[:, 2 * A_WIDTH:].reshape(BATCH, SEQ, B_WIDTH)
    z_s = h0s[:, 2 * A_WIDTH:].reshape(DEC_BATCH, DEC_SEQ, B_WIDTH)
    h_s = tail_s.reshape(DEC_BATCH, 8, D_WIDTH)[:, :DEC_SEQ]
    mem_shape = (BATCH, N_MEM, MEM_HEADS, MEM_HEAD_DIM)
    return (y_p.reshape(BATCH, SEQ, D_MODEL),
            y_s.reshape(DEC_BATCH, DEC_SEQ, D_MODEL),
            kr_p.reshape(kv_shape_p),
            h1p[:, 2 * C_WIDTH:3 * C_WIDTH].reshape(kv_shape_p),
            kr_s.reshape(kv_shape_s),
            v_s.reshape(kv_shape_s),
            z_p[:, SEQ - B_BUF:][None],
            jnp.concatenate([state_b_buf[0], z_s], axis=1)[:, DEC_SEQ:][None],
            vn_s.reshape(DEC_BATCH, CHUNK, A_WIDTH)[:, :DEC_SEQ][None],
            tail_p.reshape(BATCH, CONV_HALO, D_WIDTH)[:, CONV_HALO - D_BUF:][None],
            jnp.concatenate([state_d_buf[0], h_s], axis=1)[:, DEC_SEQ:][None],
            jnp.stack([m_.reshape(mem_shape) for m_ in pmk]),
            jnp.stack([m_.reshape(mem_shape) for m_ in pmv]))
```

```python
import functools
import math
from typing import NamedTuple

import jax
import jax.numpy as jnp
from jax import lax
from jax.experimental import pallas as pl
from jax.experimental.pallas import tpu as pltpu

F32 = jnp.float32
BF16 = jnp.bfloat16
HIGHEST = lax.Precision.HIGHEST

D_MODEL = 2048
BATCH = 4
SEQ = 2048
DEPTH = 2
DEC_BATCH = 8
DEC_SEQ = 4
PAST_LEN = 16384
PAGE_SIZE = 128
A_WIDTH = 1024
CHUNK = 128
A_GROUPS = 8
B_WIDTH = 1024
POOL_WINDOWS = (2, 4, 8, 16)
B_GROUP_DIM = B_WIDTH // len(POOL_WINDOWS)
B_BUF = 15
C_HEADS = 8
HEAD_DIM = 128
C_WIDTH = 1024
ROT_DIM = 32
ROPE_THETA = 500000.0
MOBA_BLOCK = 256
MOBA_TOPK = 3
D_WIDTH = 1024
CONV_WIDTH = 31
D_BUF = 30
N_MEM = 256
MEM_HEADS = 4
MEM_HEAD_DIM = 128
MEM_WIDTH = 512
N_EXPERTS = 16
N_EXPERT_GROUPS = 4
EXPERTS_PER_GROUP = 4
MOE_TOPK = 2
D_EXPERT = 1024
ALPHA = (2 * DEPTH) ** 0.25
LN_EPS = 1e-5

N_PROMPT = BATCH * SEQ
N_SAMPLE = DEC_BATCH * DEC_SEQ
N_TOK = N_PROMPT + N_SAMPLE
N_PAST_BLOCKS = PAST_LEN // MOBA_BLOCK
PAGES_PER_BLOCK = MOBA_BLOCK // PAGE_SIZE
N_COMBO = DEC_SEQ * C_HEADS


class MoeCfg(NamedTuple):
    n_tok: int
    tok_tile: int
    tile: int
    rows: int
    n_tiles: int
    precise: bool


def _moe_cfg(n_tok, tile, precise):
    rows = ((n_tok * MOE_TOPK + N_EXPERTS * (tile - 1)) // tile + 1) * tile
    return MoeCfg(n_tok, tile, tile, rows, rows // tile, precise)


MOE_PROMPT = _moe_cfg(N_PROMPT, 256, False)
MOE_SAMPLE = _moe_cfg(N_SAMPLE, N_SAMPLE, True)

ROW_DMA_UNROLL = 8
VMEM_LIMIT = 56 * 1024 * 1024
VMEM_LIMIT_SMALL = 40 * 1024 * 1024
SMALL_CALL_ROWS = 1024


def _cparams(n_axes, rows=None):
    vmem = VMEM_LIMIT if rows is None or rows > SMALL_CALL_ROWS else VMEM_LIMIT_SMALL
    return pltpu.CompilerParams(dimension_semantics=("arbitrary",) * n_axes, vmem_limit_bytes=vmem)


def _ln(x, g, b):
    mu = jnp.mean(x, axis=-1, keepdims=True)
    xc = x - mu
    var = jnp.mean(xc * xc, axis=-1, keepdims=True)
    return xc * lax.rsqrt(var + LN_EPS) * g + b


def _dot(a, b):
    return jnp.dot(a, b, preferred_element_type=F32)


def _dot_nt(a, b, precision=None):
    return lax.dot_general(a, b, (((1,), (1,)), ((), ())), precision=precision, preferred_element_type=F32)


def _split_bf16(v):
    hi = v.astype(BF16)
    return hi, (v.astype(F32) - hi.astype(F32)).astype(BF16)


def _mm(a, b, precise, nt=False):
    dot = _dot_nt if nt else _dot
    if not precise:
        b0 = b[0] if isinstance(b, tuple) else b
        return dot(a.astype(BF16), b0.astype(BF16))
    ah, al = _split_bf16(a)
    bh, bl = b if isinstance(b, tuple) else _split_bf16(b)
    return dot(ah, bh) + (dot(al, bh) + dot(ah, bl))


def _load_all(refs):
    return tuple(r[...] for r in refs)


def split_weight(w, precise):
    if not precise:
        return (w.astype(BF16),)
    bits = lax.bitcast_convert_type(w, jnp.uint32) & jnp.uint32(0xFFFF0000)
    hi = lax.bitcast_convert_type(bits, F32)
    return hi.astype(BF16), (w - hi).astype(BF16)


def _mm_kernel(x_ref, *refs, n_gelu, precise):
    o_ref = refs[-1]
    acc = _mm(x_ref[...], _load_all(refs[:-1]), precise)
    if n_gelu:
        j = pl.program_id(0)

        @pl.when(j < n_gelu)
        def _():
            o_ref[...] = jax.nn.gelu(acc).astype(o_ref.dtype)

        @pl.when(j >= n_gelu)
        def _():
            o_ref[...] = acc.astype(o_ref.dtype)
    else:
        o_ref[...] = acc.astype(o_ref.dtype)


def matmul(x, w, *, m, tm, tn, xoff=0, n_gelu=0, precise=False, name="mm"):
    k = x.shape[1]
    n = w[0].shape[1]
    return pl.pallas_call(
        functools.partial(_mm_kernel, n_gelu=n_gelu, precise=precise),
        grid=(n // tn, m // tm),
        in_specs=[pl.BlockSpec((tm, k), lambda j, i: (i + xoff, 0))]
        + [pl.BlockSpec((k, tn), lambda j, i: (0, j))] * len(w),
        out_specs=pl.BlockSpec((tm, tn), lambda j, i: (i, j)),
        out_shape=jax.ShapeDtypeStruct((m, n), F32),
        compiler_params=_cparams(2, m),
        name=name,
    )(x, *w)


def _mm_res_ln_kernel(*refs, n_in, precise):
    n_w = 2 if precise else 1
    a_refs = refs[:n_in]
    w_refs = refs[n_in:n_in + n_in * n_w]
    r_ref, g_ref, b_ref, o_ref = refs[n_in + n_in * n_w:]
    acc = None
    for k, a_ref in enumerate(a_refs):
        d = _mm(a_ref[...], _load_all(w_refs[k * n_w:(k + 1) * n_w]), precise)
        acc = d if acc is None else acc + d
    o_ref[...] = _ln(ALPHA * r_ref[...] + acc, g_ref[...], b_ref[...])


def matmul_res_ln(a_list, w_list, resid, g, b, *, m, tm, roff=0, precise=False, name="mm_res_ln"):
    n_in = len(a_list)
    w_flat = [part for w in w_list for part in w]
    in_specs = [pl.BlockSpec((tm, a.shape[1]), lambda i: (i, 0)) for a in a_list]
    in_specs += [pl.BlockSpec(w.shape, lambda i: (0, 0)) for w in w_flat]
    in_specs += [pl.BlockSpec((tm, D_MODEL), lambda i: (i + roff, 0)),
                 pl.BlockSpec((1, D_MODEL), lambda i: (0, 0)),
                 pl.BlockSpec((1, D_MODEL), lambda i: (0, 0))]
    return pl.pallas_call(
        functools.partial(_mm_res_ln_kernel, n_in=n_in, precise=precise),
        grid=(m // tm,),
        in_specs=in_specs,
        out_specs=pl.BlockSpec((tm, D_MODEL), lambda i: (i, 0)),
        out_shape=jax.ShapeDtypeStruct((m, D_MODEL), F32),
        compiler_params=_cparams(1, m),
        name=name,
    )(*a_list, *w_flat, resid, g, b)


def _mixer_ab_kernel(u_ref, v_ref, z_ref, zp_ref, lng_ref, lnb_ref, ws_ref, bs_ref, wg_ref, sc_ref,
                     mix_ref, vn_ref, *, pos0, has_ctx, precise):
    c = pl.program_id(1)
    vn = _ln(v_ref[...], lng_ref[...], lnb_ref[...])
    vn_ref[...] = vn
    u = u_ref[...]
    row = lax.broadcasted_iota(jnp.int32, (CHUNK, CHUNK), 0)
    col = lax.broadcasted_iota(jnp.int32, (CHUNK, CHUNK), 1)
    causal = col <= row
    gd = A_WIDTH // A_GROUPS
    for g in range(A_GROUPS):
        sl = slice(g * gd, (g + 1) * gd)
        w = jnp.where(causal, ws_ref[g], 0.0)
        mixed = _mm(w, vn[:, sl], precise) + bs_ref[:, g:g + 1]
        mix_ref[:, sl] = (u[:, sl] * mixed).astype(mix_ref.dtype)

    z = z_ref[...]
    zp = zp_ref[...]
    if not has_ctx:
        zp = jnp.where(c == 0, 0.0, zp)
    zext = jnp.concatenate([zp, z], axis=0)
    pos = pos0 + c * CHUNK + lax.broadcasted_iota(jnp.int32, (CHUNK, 1), 0)
    for gi, wdw in enumerate(POOL_WINDOWS):
        sl = slice(gi * B_GROUP_DIM, (gi + 1) * B_GROUP_DIM)
        s = zext[:, sl]
        sh = 1
        while sh < wdw:
            s = s + pltpu.roll(s, sh, axis=0)
            sh *= 2
        cnt = jnp.minimum(wdw, pos + 1).astype(F32)
        d = s[16:, :] / cnt - z[:, sl]
        bo = _mm(d, wg_ref[gi], precise) * sc_ref[:, sl]
        mix_ref[:, A_WIDTH + gi * B_GROUP_DIM:A_WIDTH + (gi + 1) * B_GROUP_DIM] = bo.astype(mix_ref.dtype)


def mixer_ab(h, zprev, zprev_map, *, nb, n_chunks, pos0, has_ctx, prm, name, precise=False):
    m = nb * n_chunks * CHUNK
    row = lambda b, c: b * n_chunks + c
    const2 = lambda b, c: (0, 0)
    return pl.pallas_call(
        functools.partial(_mixer_ab_kernel, pos0=pos0, has_ctx=has_ctx, precise=precise),
        grid=(nb, n_chunks),
        in_specs=[pl.BlockSpec((CHUNK, A_WIDTH), lambda b, c: (row(b, c), 0)),
                  pl.BlockSpec((CHUNK, A_WIDTH), lambda b, c: (row(b, c), 1)),
                  pl.BlockSpec((CHUNK, B_WIDTH), lambda b, c: (row(b, c), 2)),
                  pl.BlockSpec((16, B_WIDTH), zprev_map),
                  pl.BlockSpec((1, A_WIDTH), const2),
                  pl.BlockSpec((1, A_WIDTH), const2),
                  pl.BlockSpec((A_GROUPS, CHUNK, CHUNK), lambda b, c: (0, 0, 0)),
                  pl.BlockSpec((CHUNK, A_GROUPS), const2),
                  pl.BlockSpec((len(POOL_WINDOWS), B_GROUP_DIM, B_GROUP_DIM), lambda b, c: (0, 0, 0)),
                  pl.BlockSpec((1, B_WIDTH), const2)],
        out_specs=[pl.BlockSpec((CHUNK, A_WIDTH + B_WIDTH), lambda b, c: (row(b, c), 0)),
                   pl.BlockSpec((CHUNK, A_WIDTH), lambda b, c: (row(b, c), 0))],
        out_shape=[jax.ShapeDtypeStruct((m, A_WIDTH + B_WIDTH), F32 if precise else BF16),
                   jax.ShapeDtypeStruct((m, A_WIDTH), F32)],
        compiler_params=_cparams(2, m),
        name=name,
    )(h, h, h, zprev, prm["ln_g"], prm["ln_b"], prm["ws"], prm["bs_t"], prm["wg"], prm["scale"])


def _rope_kernel(q_ref, k_ref, c_ref, sa_ref, sb_ref, qo_ref, ko_ref):
    c = c_ref[...]
    sa = sa_ref[...]
    sb = sb_ref[...]
    half = ROT_DIM // 2
    for h in range(C_HEADS):
        sl = slice(h * HEAD_DIM, (h + 1) * HEAD_DIM)
        for src, dst in ((q_ref, qo_ref), (k_ref, ko_ref)):
            x = src[:, sl]
            dst[:, sl] = (x * c + pltpu.roll(x, half, axis=1) * sa
                          + pltpu.roll(x, HEAD_DIM - half, axis=1) * sb)


def rope_qk(h, tabs, *, m, tm, xoff, tab_blocks, name):
    tmap = lambda i: (i % tab_blocks, 0)
    return pl.pallas_call(
        _rope_kernel,
        grid=(m // tm,),
        in_specs=[pl.BlockSpec((tm, C_WIDTH), lambda i: (i + xoff, 0)),
                  pl.BlockSpec((tm, C_WIDTH), lambda i: (i + xoff, 1)),
                  pl.BlockSpec((tm, HEAD_DIM), tmap),
                  pl.BlockSpec((tm, HEAD_DIM), tmap),
                  pl.BlockSpec((tm, HEAD_DIM), tmap)],
        out_specs=[pl.BlockSpec((tm, C_WIDTH), lambda i: (i, 0)),
                   pl.BlockSpec((tm, C_WIDTH), lambda i: (i, 0))],
        out_shape=[jax.ShapeDtypeStruct((m, C_WIDTH), F32),
                   jax.ShapeDtypeStruct((m, C_WIDTH), F32)],
        compiler_params=_cparams(1, m),
        name=name,
    )(h, h, *tabs)


def _moba_kernel(q_ref, qall_ref, k_ref, v_ref, o_ref, selt_ref, *, n_blocks):
    i = pl.program_id(2)
    blk_rows = MOBA_BLOCK
    seq = n_blocks * blk_rows
    scale = HEAD_DIM ** -0.5

    @pl.when(i == 0)
    def _():
        kmean = jnp.concatenate(
            [jnp.mean(k_ref[j * blk_rows:(j + 1) * blk_rows, :], axis=0, keepdims=True) for j in range(n_blocks)],
            axis=0)
        bst = _mm(kmean, qall_ref[...], True, nt=True)
        blk = lax.broadcasted_iota(jnp.int32, (n_blocks, seq), 0)
        own = lax.broadcasted_iota(jnp.int32, (n_blocks, seq), 1) >> int(math.log2(blk_rows))
        work = jnp.where(blk < own, bst, -jnp.inf)
        sel = jnp.zeros((n_blocks, seq), F32)
        for _ in range(MOBA_TOPK):
            mx = jnp.max(work, axis=0, keepdims=True)
            first = jnp.min(jnp.where(work == mx, blk, n_blocks), axis=0, keepdims=True)
            hit = (blk == first) & (mx > -jnp.inf)
            sel = jnp.where(hit, 1.0, sel)
            work = jnp.where(hit, -jnp.inf, work)
        selt_ref[...] = jnp.zeros(selt_ref.shape, F32)
        for c in range(n_blocks):
            selt_ref[c, 0:n_blocks, :] = sel[:, c * blk_rows:(c + 1) * blk_rows]

    qb = q_ref[...].astype(BF16)
    sel = jnp.transpose(selt_ref[i])

    r_i = lax.broadcasted_iota(jnp.int32, (blk_rows, blk_rows), 0)
    c_i = lax.broadcasted_iota(jnp.int32, (blk_rows, blk_rows), 1)
    causal = (c_i <= r_i).astype(F32)

    for c in range(n_blocks):
        @pl.when(i == c)
        def _():
            nk = (c + 1) * blk_rows
            s = _dot_nt(qb, k_ref[0:nk, :].astype(BF16)) * scale
            allowed = jnp.concatenate(
                [jnp.broadcast_to(sel[:, j:j + 1], (blk_rows, blk_rows)) for j in range(c)] + [causal], axis=1)
            s = jnp.where(allowed > 0.0, s, -jnp.inf)
            m = jnp.max(s, axis=-1, keepdims=True)
            p = jnp.exp(s - m)
            l = jnp.sum(p, axis=-1, keepdims=True)
            o = _dot(p.astype(BF16), v_ref[0:nk, :].astype(BF16))
            o_ref[...] = (o / l).astype(o_ref.dtype)


def moba_prompt(q_rot, k_rot, h, *, nb, seq, name="moba_prompt"):
    n_blocks = seq // MOBA_BLOCK
    v_col0 = 2 * C_WIDTH // HEAD_DIM
    return pl.pallas_call(
        functools.partial(_moba_kernel, n_blocks=n_blocks),
        grid=(nb, C_HEADS, n_blocks),
        in_specs=[pl.BlockSpec((MOBA_BLOCK, HEAD_DIM), lambda b, hh, i: (b * n_blocks + i, hh)),
                  pl.BlockSpec((seq, HEAD_DIM), lambda b, hh, i: (b, hh)),
                  pl.BlockSpec((seq, HEAD_DIM), lambda b, hh, i: (b, hh)),
                  pl.BlockSpec((seq, HEAD_DIM), lambda b, hh, i: (b, v_col0 + hh))],
        out_specs=pl.BlockSpec((MOBA_BLOCK, HEAD_DIM), lambda b, hh, i: (b * n_blocks + i, hh)),
        out_shape=jax.ShapeDtypeStruct((nb * seq, C_WIDTH), BF16),
        scratch_shapes=[pltpu.VMEM((n_blocks, 128, MOBA_BLOCK), F32)],
        compiler_params=_cparams(3),
        name=name,
    )(q_rot, q_rot, k_rot, h)


CONV_HALO = 32
SUBLANES = 8


def _conv_kernel(ga_ref, gg_ref, pa_ref, pg_ref, w_ref, bdw_ref, lng_ref, lnb_ref,
                 o_ref, tail_ref, ext_ref, sh_ref, y_ref, *, tm, rt, tail, prev_is_state):
    t = pl.program_id(1)
    hcur = ga_ref[...] * jax.nn.sigmoid(gg_ref[...])
    if prev_is_state:
        hprev = pa_ref[...]
    else:
        hprev = pa_ref[...] * jax.nn.sigmoid(pg_ref[...])
        hprev = jnp.where(t == 0, 0.0, hprev)
    ext_ref[0:CONV_HALO, :] = hprev
    ext_ref[CONV_HALO:CONV_HALO + tm, :] = hcur
    tail_ref[...] = hcur[tm - tail:, :]
    off = CONV_HALO - D_BUF
    sh_rows = sh_ref.shape[1]
    for s in range(1, SUBLANES):
        sh_ref[s - 1] = ext_ref[s:s + sh_rows, :]
    for cc in range(D_WIDTH // 128):
        cs = slice(cc * 128, (cc + 1) * 128)
        for rc in range(tm // rt):
            r0 = rc * rt
            acc = jnp.zeros((rt, 128), F32)
            for j in range(CONV_WIDTH):
                s, a = (off + j) % SUBLANES, (off + j) // SUBLANES
                row0 = r0 + a * SUBLANES
                src = ext_ref[row0:row0 + rt, cs] if s == 0 else sh_ref[s - 1, row0:row0 + rt, cs]
                acc = acc + w_ref[j:j + 1, cs] * src
            y_ref[r0:r0 + rt, cs] = acc + bdw_ref[:, cs]
    y = _ln(y_ref[...], lng_ref[...], lnb_ref[...])
    o_ref[...] = (y * jax.nn.sigmoid(y)).astype(o_ref.dtype)


def conv_module(h, col_a, prev_a, prev_g, prev_map_a, prev_map_g, *, nb, n_tiles, tm, rt, tail, xoff,
                prev_is_state, prm, name, out_dtype=BF16):
    m = nb * n_tiles * tm
    row = lambda b, t: b * n_tiles + t + xoff
    const2 = lambda b, t: (0, 0)
    return pl.pallas_call(
        functools.partial(_conv_kernel, tm=tm, rt=rt, tail=tail, prev_is_state=prev_is_state),
        grid=(nb, n_tiles),
        in_specs=[pl.BlockSpec((tm, D_WIDTH), lambda b, t: (row(b, t), col_a)),
                  pl.BlockSpec((tm, D_WIDTH), lambda b, t: (row(b, t), col_a + 1)),
                  pl.BlockSpec((CONV_HALO, D_WIDTH), prev_map_a),
                  pl.BlockSpec((CONV_HALO, D_WIDTH), prev_map_g),
                  pl.BlockSpec((CONV_WIDTH, D_WIDTH), const2),
                  pl.BlockSpec((1, D_WIDTH), const2),
                  pl.BlockSpec((1, D_WIDTH), const2),
                  pl.BlockSpec((1, D_WIDTH), const2)],
        out_specs=[pl.BlockSpec((tm, D_WIDTH), lambda b, t: (b * n_tiles + t, 0)),
                   pl.BlockSpec((tail, D_WIDTH), lambda b, t: (b, 0))],
        out_shape=[jax.ShapeDtypeStruct((m, D_WIDTH), out_dtype),
                   jax.ShapeDtypeStruct((nb * tail, D_WIDTH), F32)],
        scratch_shapes=[pltpu.VMEM((CONV_HALO + tm, D_WIDTH), F32),
                        pltpu.VMEM((SUBLANES - 1, CONV_HALO + tm - SUBLANES, D_WIDTH), F32),
                        pltpu.VMEM((tm, D_WIDTH), F32)],
        compiler_params=_cparams(2, m),
        name=name,
    )(h, h, prev_a, prev_g, prm["conv_w"], prm["conv_b"], prm["ln_g"], prm["ln_b"])


PAGED_BLOCKS_PER_STEP = 4
PAGED_PAGES_PER_STEP = PAGED_BLOCKS_PER_STEP * PAGES_PER_BLOCK


def _paged_partial_kernel(pt_ref, q_ref, *refs):
    del pt_ref
    pages = refs[:2 * PAGED_PAGES_PER_STEP]
    m_ref, l_ref, o_ref, ks_ref = refs[2 * PAGED_PAGES_PER_STEP:]
    rows = PAGE_SIZE * C_HEADS
    scale = HEAD_DIM ** -0.5
    k_refs = pages[:PAGED_PAGES_PER_STEP]
    v_refs = pages[PAGED_PAGES_PER_STEP:]
    qb = q_ref[...].astype(BF16)
    c_i = lax.broadcasted_iota(jnp.int32, (N_COMBO, rows), 0)
    l_i = lax.broadcasted_iota(jnp.int32, (N_COMBO, rows), 1)
    same_head = (l_i & (C_HEADS - 1)) == (c_i & (C_HEADS - 1))
    for blk in range(PAGED_BLOCKS_PER_STEP):
        ks = None
        s = []
        for pg in range(PAGES_PER_BLOCK):
            k = k_refs[blk * PAGES_PER_BLOCK + pg][...]
            ksum = jnp.sum(k, axis=0)
            ks = ksum if ks is None else ks + ksum
            sp = _dot_nt(qb, k.reshape(rows, HEAD_DIM).astype(BF16)) * scale
            s.append(jnp.where(same_head, sp, -jnp.inf))
        ks_ref[blk] = ks
        m = jnp.max(jnp.maximum(s[0], s[1]), axis=-1, keepdims=True)
        l = None
        o = None
        for pg in range(PAGES_PER_BLOCK):
            p = jnp.exp(s[pg] - m)
            v = v_refs[blk * PAGES_PER_BLOCK + pg][...].reshape(rows, HEAD_DIM).astype(BF16)
            lp = jnp.sum(p, axis=-1, keepdims=True)
            op = _dot(p.astype(BF16), v)
            l = lp if l is None else l + lp
            o = op if o is None else o + op
        m_ref[blk] = jnp.broadcast_to(m, (N_COMBO, HEAD_DIM))
        l_ref[blk] = jnp.broadcast_to(l, (N_COMBO, HEAD_DIM))
        o_ref[blk] = o


def paged_partials(page_table, q_combo, cache_k, cache_v, layer):
    page_block = (None, None, PAGE_SIZE, C_HEADS, HEAD_DIM)
    nbs = PAGED_BLOCKS_PER_STEP

    def page_map(which):
        return lambda b, n, pt: (layer, pt[b, PAGED_PAGES_PER_STEP * n + which], 0, 0, 0)

    page_specs = [pl.BlockSpec(page_block, page_map(w)) for w in range(PAGED_PAGES_PER_STEP)]
    part_shape = jax.ShapeDtypeStruct((DEC_BATCH, N_PAST_BLOCKS, N_COMBO, HEAD_DIM), F32)
    part_spec = pl.BlockSpec((None, nbs, N_COMBO, HEAD_DIM), lambda b, n, pt: (b, n, 0, 0))
    return pl.pallas_call(
        _paged_partial_kernel,
        grid_spec=pltpu.PrefetchScalarGridSpec(
            num_scalar_prefetch=1,
            grid=(DEC_BATCH, N_PAST_BLOCKS // nbs),
            in_specs=[pl.BlockSpec((None, N_COMBO, HEAD_DIM), lambda b, n, pt: (b, 0, 0))] + page_specs + page_specs,
            out_specs=[part_spec, part_spec, part_spec,
                       pl.BlockSpec((None, nbs, C_HEADS, HEAD_DIM), lambda b, n, pt: (b, n, 0, 0))]),
        out_shape=[part_shape, part_shape, part_shape,
                   jax.ShapeDtypeStruct((DEC_BATCH, N_PAST_BLOCKS, C_HEADS, HEAD_DIM), F32)],
        compiler_params=_cparams(2, N_SAMPLE),
        name="paged_partials",
    )(page_table, q_combo, *([cache_k] * PAGED_PAGES_PER_STEP), *([cache_v] * PAGED_PAGES_PER_STEP))


def _paged_merge_kernel(q_ref, kn_ref, vn_ref, m_ref, l_ref, o_ref, ks_ref, out_ref):
    nb = N_PAST_BLOCKS
    scale = HEAD_DIM ** -0.5
    q = q_ref[...]
    kmean = ks_ref[...] * (1.0 / MOBA_BLOCK)
    kmean = jnp.concatenate([kmean] * DEC_SEQ, axis=1)
    bs = jnp.sum(kmean * q[None], axis=-1, keepdims=True)
    work = jnp.broadcast_to(bs, (nb, N_COMBO, HEAD_DIM))
    n_i = lax.broadcasted_iota(jnp.int32, (nb, N_COMBO, HEAD_DIM), 0)
    sel = n_i < 0
    for _ in range(MOBA_TOPK):
        mx = jnp.max(work, axis=0, keepdims=True)
        first = jnp.min(jnp.where(work == mx, n_i, nb), axis=0, keepdims=True)
        hit = n_i == first
        sel = sel | hit
        work = jnp.where(hit, -jnp.inf, work)

    s = _dot_nt(q.astype(BF16), kn_ref[...].astype(BF16)) * scale
    c_i = lax.broadcasted_iota(jnp.int32, s.shape, 0)
    l_i = lax.broadcasted_iota(jnp.int32, s.shape, 1)
    ok = ((l_i & (C_HEADS - 1)) == (c_i & (C_HEADS - 1))) & ((l_i >> 3) <= (c_i >> 3))
    s = jnp.where(ok, s, -jnp.inf)
    m_own = jnp.max(s, axis=-1, keepdims=True)
    p = jnp.exp(s - m_own)
    l_own = jnp.sum(p, axis=-1, keepdims=True)
    o_own = _dot(p.astype(BF16), vn_ref[...].astype(BF16))

    mp = m_ref[...]
    m_all = jnp.maximum(jnp.max(jnp.where(sel, mp, -jnp.inf), axis=0), m_own)
    w = jnp.where(sel, jnp.exp(mp - m_all[None]), 0.0)
    w_own = jnp.exp(m_own - m_all)
    den = jnp.sum(w * l_ref[...], axis=0) + w_own * l_own
    num = jnp.sum(w * o_ref[...], axis=0) + w_own * o_own
    out_ref[...] = num / den


def paged_merge(q_combo, k_new, v_new, m_p, l_p, o_p, ksum):
    combo = pl.BlockSpec((None, N_COMBO, HEAD_DIM), lambda b: (b, 0, 0))
    part = pl.BlockSpec((None, N_PAST_BLOCKS, N_COMBO, HEAD_DIM), lambda b: (b, 0, 0, 0))
    return pl.pallas_call(
        _paged_merge_kernel,
        grid=(DEC_BATCH,),
        in_specs=[combo, combo, combo, part, part, part,
                  pl.BlockSpec((None, N_PAST_BLOCKS, C_HEADS, HEAD_DIM), lambda b: (b, 0, 0, 0))],
        out_specs=combo,
        out_shape=jax.ShapeDtypeStruct((DEC_BATCH, N_COMBO, HEAD_DIM), F32),
        compiler_params=_cparams(1, N_SAMPLE),
        name="paged_merge",
    )(q_combo, k_new, v_new, m_p, l_p, o_p, ksum)


def _memattn_kernel(x_ref, mk_ref, mv_ref, g_ref, b_ref, *refs, nb_tile, rows_per_b, precise):
    n_w = 2 if precise else 1
    wq = _load_all(refs[:n_w])
    wo = _load_all(refs[n_w:2 * n_w])
    o_ref = refs[2 * n_w]
    scale = MEM_HEAD_DIM ** -0.5
    x = x_ref[...]
    tm = x.shape[0]
    q = _mm(x, wq, precise)
    row_b = lax.broadcasted_iota(jnp.int32, (tm, 1), 0) >> int(math.log2(rows_per_b))
    heads = []
    for hh in range(MEM_HEADS):
        sl = slice(hh * MEM_HEAD_DIM, (hh + 1) * MEM_HEAD_DIM)
        qh = q[:, sl]
        oh = None
        for bb in range(nb_tile):
            s = _mm(qh, mk_ref[bb, :, sl], precise, nt=True) * scale
            s = s - jnp.max(s, axis=-1, keepdims=True)
            p = jnp.exp(s)
            p = p / jnp.sum(p, axis=-1, keepdims=True)
            ob = _mm(p, mv_ref[bb, :, sl], precise)
            if nb_tile > 1:
                ob = jnp.where(row_b == bb, ob, 0.0)
            oh = ob if oh is None else oh + ob
        heads.append(oh)
    o = jnp.concatenate(heads, axis=-1)
    y = ALPHA * x + _mm(o, wo, precise)
    o_ref[...] = _ln(y, g_ref[...], b_ref[...])


def memory_attention_ln(x, wq, mk, mv, wo, g, b, *, m, tm, rows_per_b, name, precise=False, b0=0):
    if rows_per_b >= tm:
        nb_tile = 1
        tiles_per_b = rows_per_b // tm
        kv_map = lambda i: (b0 + i // tiles_per_b, 0, 0)
    else:
        nb_tile = tm // rows_per_b
        kv_map = lambda i: (i, 0, 0)
    const2 = lambda i: (0, 0)
    return pl.pallas_call(
        functools.partial(_memattn_kernel, nb_tile=nb_tile, rows_per_b=rows_per_b, precise=precise),
        grid=(m // tm,),
        in_specs=[pl.BlockSpec((tm, D_MODEL), lambda i: (i, 0)),
                  pl.BlockSpec((nb_tile, N_MEM, MEM_WIDTH), kv_map),
                  pl.BlockSpec((nb_tile, N_MEM, MEM_WIDTH), kv_map),
                  pl.BlockSpec((1, D_MODEL), const2),
                  pl.BlockSpec((1, D_MODEL), const2)]
        + [pl.BlockSpec((D_MODEL, MEM_WIDTH), const2)] * len(wq)
        + [pl.BlockSpec((MEM_WIDTH, D_MODEL), const2)] * len(wo),
        out_specs=pl.BlockSpec((tm, D_MODEL), lambda i: (i, 0)),
        out_shape=jax.ShapeDtypeStruct((m, D_MODEL), F32),
        compiler_params=_cparams(1, m),
        name=name,
    )(x, mk, mv, g, b, *wq, *wo)


def _router_kernel(x_ref, xt_ref, w_ref, b_ref, o_ref, cnt_ref, run_ref, *, tiles_per_b):
    step = pl.program_id(0)
    tm = x_ref.shape[0]

    @pl.when(step == 0)
    def _():
        run_ref[...] = jnp.zeros(run_ref.shape, F32)

    x = x_ref[...]
    if tiles_per_b:
        is_last = lax.rem(step, tiles_per_b) == tiles_per_b - 1
        x = jnp.concatenate([x[:tm - CHUNK], jnp.where(is_last, xt_ref[...], x[tm - CHUNK:])], axis=0)
    xh, xl = _split_bf16(x)
    wh, wl = _split_bf16(w_ref[...])
    logits = _dot_nt(wh, xh) + (_dot_nt(wh, xl) + _dot_nt(wl, xh)) + b_ref[...]
    logits = logits - jnp.max(logits, axis=0, keepdims=True)
    e = jnp.exp(logits)
    probs = e / jnp.sum(e, axis=0, keepdims=True)
    p = [probs[j:j + 1, :] for j in range(N_EXPERTS)]
    gbest = None
    gsel = None
    for g in range(N_EXPERT_GROUPS):
        a, b_, c, d = p[4 * g:4 * g + 4]
        hi1, lo1 = jnp.maximum(a, b_), jnp.minimum(a, b_)
        hi2, lo2 = jnp.maximum(c, d), jnp.minimum(c, d)
        gs = jnp.maximum(hi1, hi2) + jnp.maximum(jnp.minimum(hi1, hi2), jnp.maximum(lo1, lo2))
        if g == 0:
            gbest, gsel = gs, jnp.zeros(gs.shape, jnp.int32)
        else:
            better = gs > gbest
            gbest = jnp.where(better, gs, gbest)
            gsel = jnp.where(better, g, gsel)
    cand = [jnp.where(gsel == j // EXPERTS_PER_GROUP, p[j], -1.0) for j in range(N_EXPERTS)]
    v1 = cand[0]
    i1 = jnp.zeros(v1.shape, jnp.int32)
    for j in range(1, N_EXPERTS):
        better = cand[j] > v1
        v1 = jnp.where(better, cand[j], v1)
        i1 = jnp.where(better, j, i1)
    v2 = jnp.full(v1.shape, -2.0, F32)
    i2 = jnp.zeros(v1.shape, jnp.int32)
    for j in range(N_EXPERTS):
        better = (cand[j] > v2) & (i1 != j)
        v2 = jnp.where(better, cand[j], v2)
        i2 = jnp.where(better, j, i2)
    tot = v1 + v2

    e_i = lax.broadcasted_iota(jnp.int32, (N_EXPERTS, tm), 0)
    pick1 = e_i == i1
    pick2 = e_i == i2
    onehot = (pick1 | pick2).astype(F32)
    t_r = lax.broadcasted_iota(jnp.int32, (tm, tm), 0)
    t_c = lax.broadcasted_iota(jnp.int32, (tm, tm), 1)
    earlier = (t_r < t_c).astype(BF16)
    before = _dot(onehot.astype(BF16), earlier) + run_ref[:, 0:1]
    rank1 = jnp.sum(jnp.where(pick1, before, 0.0), axis=0, keepdims=True)
    rank2 = jnp.sum(jnp.where(pick2, before, 0.0), axis=0, keepdims=True)
    run = run_ref[...] + jnp.sum(onehot, axis=1, keepdims=True)
    run_ref[...] = run
    cnt_ref[...] = run

    row = lax.broadcasted_iota(jnp.int32, o_ref.shape, 0)
    vals = (i1.astype(F32), i2.astype(F32), v1 / tot, v2 / tot, rank1, rank2)
    out = jnp.zeros(o_ref.shape, F32)
    for r, v in enumerate(vals):
        out = jnp.where(row == r, v, out)
    o_ref[...] = out


def router(x, x_tail, w_t, b, *, tm, name):
    m = x.shape[0]
    tiles_per_b = 0 if x_tail is None else SEQ // tm
    if x_tail is None:
        x_tail, tail_spec = x, pl.BlockSpec((min(CHUNK, tm), D_MODEL), lambda i: (0, 0))
    else:
        tail_spec = pl.BlockSpec((CHUNK, D_MODEL), lambda i: (i // tiles_per_b, 0))
    return pl.pallas_call(
        functools.partial(_router_kernel, tiles_per_b=tiles_per_b),
        grid=(m // tm,),
        in_specs=[pl.BlockSpec((tm, D_MODEL), lambda i: (i, 0)),
                  tail_spec,
                  pl.BlockSpec((N_EXPERTS, D_MODEL), lambda i: (0, 0)),
                  pl.BlockSpec((N_EXPERTS, 1), lambda i: (0, 0))],
        out_specs=[pl.BlockSpec((8, tm), lambda i: (0, i)),
                   pl.BlockSpec((N_EXPERTS, 128), lambda i: (0, 0))],
        out_shape=[jax.ShapeDtypeStruct((8, m), F32),
                   jax.ShapeDtypeStruct((N_EXPERTS, 128), F32)],
        scratch_shapes=[pltpu.VMEM((N_EXPERTS, 128), F32)],
        compiler_params=_cparams(1, m),
        name=name,
    )(x, x_tail, w_t, b)


def _dispatch_kernel(d0_ref, d1_ref, starts_ref, ends_ref, nv_ref, x_ref, o_hbm, zero_ref, sem, *, cfg):
    i = pl.program_id(0)
    tm = cfg.tok_tile

    def fill_copy(row0):
        return pltpu.make_async_copy(zero_ref, o_hbm.at[pl.ds(row0, cfg.tile)], sem)

    @pl.when(i == 0)
    def _():
        zero_ref[...] = jnp.zeros(zero_ref.shape, zero_ref.dtype)
        for wait in (False, True):
            for e in range(N_EXPERTS):
                @pl.when(ends_ref[e] > starts_ref[e])
                def _():
                    cp = fill_copy(pl.multiple_of(ends_ref[e] - cfg.tile, cfg.tile))
                    cp.wait() if wait else cp.start()

            def tail(t, carry):
                cp = fill_copy(pl.multiple_of(t * cfg.tile, cfg.tile))
                cp.wait() if wait else cp.start()
                return carry

            lax.fori_loop(nv_ref[0], cfg.n_tiles, tail, 0)

    base = i * tm

    def body(it, carry):
        for j in range(ROW_DMA_UNROLL):
            r = it * ROW_DMA_UNROLL + j
            pltpu.make_async_copy(x_ref.at[pl.ds(r, 1)], o_hbm.at[pl.ds(d0_ref[base + r], 1)], sem).start(priority=0)
            pltpu.make_async_copy(x_ref.at[pl.ds(r, 1)], o_hbm.at[pl.ds(d1_ref[base + r], 1)], sem).start(priority=1)
        return carry

    lax.fori_loop(0, tm // ROW_DMA_UNROLL, body, 0)
    for _ in range(MOE_TOPK):
        pltpu.make_async_copy(x_ref, o_hbm.at[pl.ds(0, tm)], sem).wait()


def moe_dispatch(d0, d1, starts, ends, n_valid, x, cfg, name):
    tm = cfg.tok_tile
    return pl.pallas_call(
        functools.partial(_dispatch_kernel, cfg=cfg),
        grid_spec=pltpu.PrefetchScalarGridSpec(
            num_scalar_prefetch=5,
            grid=(cfg.n_tok // tm,),
            in_specs=[pl.BlockSpec((tm, D_MODEL), lambda i, *_: (i, 0))],
            out_specs=pl.BlockSpec(memory_space=pl.ANY),
            scratch_shapes=[pltpu.VMEM((cfg.tile, D_MODEL), F32), pltpu.SemaphoreType.DMA]),
        out_shape=jax.ShapeDtypeStruct((cfg.rows, D_MODEL), F32),
        compiler_params=_cparams(1, cfg.n_tok),
        name=name,
    )(d0, d1, starts, ends, n_valid, x)


def _expert_weights(first, w_refs, scratch_refs, precise):
    @pl.when(first)
    def _():
        for k, w_ref in enumerate(w_refs):
            if precise:
                hi, lo = _split_bf16(w_ref[...])
                scratch_refs[2 * k][...] = hi
                scratch_refs[2 * k + 1][...] = lo
            else:
                scratch_refs[k][...] = w_ref[...].astype(BF16)


def _expert_dot(x, scratch_refs, k, precise):
    if not precise:
        return _dot(x.astype(BF16), scratch_refs[k][...])
    xh, xl = _split_bf16(x)
    wh, wl = scratch_refs[2 * k][...], scratch_refs[2 * k + 1][...]
    return _dot(xh, wh) + (_dot(xl, wh) + _dot(xh, wl))


def _moe_up_kernel(te_ref, first_ref, nv_ref, x_ref, wg_ref, wu_ref, hh_ref, *scratch, precise, f_tiles):
    i = pl.program_id(0)

    @pl.when(i < nv_ref[0])
    def _():
        _expert_weights((first_ref[i] == 1) | (f_tiles > 1), (wg_ref, wu_ref), scratch, precise)
        x = x_ref[...]
        hg = _expert_dot(x, scratch, 0, precise)
        hu = _expert_dot(x, scratch, 1, precise)
        hh_ref[...] = (hg * jax.nn.sigmoid(hg) * hu).astype(hh_ref.dtype)

    @pl.when(i >= nv_ref[0])
    def _():
        hh_ref[...] = jnp.zeros(hh_ref.shape, hh_ref.dtype)


def _moe_down_kernel(te_ref, first_ref, nv_ref, hh_ref, wd_ref, y_ref, *scratch, precise):
    i = pl.program_id(0)

    @pl.when(i < nv_ref[0])
    def _():
        _expert_weights(first_ref[i] == 1, (wd_ref,), scratch, precise)
        y_ref[...] = _expert_dot(hh_ref[...], scratch, 0, precise)

    @pl.when(i >= nv_ref[0])
    def _():
        y_ref[...] = jnp.zeros(y_ref.shape, y_ref.dtype)


def _stream_expert_weights(i, te_ref, first_ref, nxt_ref, w_hbms, land_refs, bf16_refs, sem, layer):
    def copies(e):
        return [pltpu.make_async_copy(w.at[layer, e], land, sem.at[k])
                for k, (w, land) in enumerate(zip(w_hbms, land_refs))]

    @pl.when(i == 0)
    def _():
        for cp in copies(te_ref[0]):
            cp.start()

    @pl.when(first_ref[i] == 1)
    def _():
        for cp in copies(te_ref[i]):
            cp.wait()
        for land, dst in zip(land_refs, bf16_refs):
            dst[...] = land[...].astype(BF16)

        @pl.when(nxt_ref[i] >= 0)
        def _():
            for cp in copies(nxt_ref[i]):
                cp.start()


def _moe_up_stream_kernel(te_ref, first_ref, nxt_ref, nv_ref, x_ref, wg_hbm, wu_hbm, hh_ref,
                          land_g, land_u, wgb_ref, wub_ref, sem, *, layer):
    i = pl.program_id(0)

    @pl.when(i < nv_ref[0])
    def _():
        _stream_expert_weights(i, te_ref, first_ref, nxt_ref, (wg_hbm, wu_hbm), (land_g, land_u),
                               (wgb_ref, wub_ref), sem, layer)
        x = x_ref[...].astype(BF16)
        hg = _dot(x, wgb_ref[...])
        hu = _dot(x, wub_ref[...])
        hh_ref[...] = (hg * jax.nn.sigmoid(hg) * hu).astype(hh_ref.dtype)

    @pl.when(i >= nv_ref[0])
    def _():
        hh_ref[...] = jnp.zeros(hh_ref.shape, hh_ref.dtype)


def _moe_down_stream_kernel(te_ref, first_ref, nxt_ref, nv_ref, hh_ref, wd_hbm, y_ref, land_d, wdb_ref, sem,
                            *, layer):
    i = pl.program_id(0)

    @pl.when(i < nv_ref[0])
    def _():
        _stream_expert_weights(i, te_ref, first_ref, nxt_ref, (wd_hbm,), (land_d,), (wdb_ref,), sem, layer)
        y_ref[...] = _dot(hh_ref[...], wdb_ref[...])

    @pl.when(i >= nv_ref[0])
    def _():
        y_ref[...] = jnp.zeros(y_ref.shape, y_ref.dtype)


def moe_experts_streamed(tile_expert, tile_first, tile_next, n_valid, x_sorted, w_gate, w_up, w_down, layer, cfg,
                         name):
    tm = cfg.tile
    row_map = lambda i, te, fi, nx, nv: (jnp.minimum(i, nv[0] - 1), 0)
    out_map = lambda i, te, fi, nx, nv: (i, 0)
    hbm = pl.BlockSpec(memory_space=pl.ANY)
    hh = pl.pallas_call(
        functools.partial(_moe_up_stream_kernel, layer=layer),
        grid_spec=pltpu.PrefetchScalarGridSpec(
            num_scalar_prefetch=4,
            grid=(cfg.n_tiles,),
            in_specs=[pl.BlockSpec((tm, D_MODEL), row_map), hbm, hbm],
            out_specs=pl.BlockSpec((tm, D_EXPERT), out_map),
            scratch_shapes=[pltpu.VMEM((D_MODEL, D_EXPERT), F32), pltpu.VMEM((D_MODEL, D_EXPERT), F32),
                            pltpu.VMEM((D_MODEL, D_EXPERT), BF16), pltpu.VMEM((D_MODEL, D_EXPERT), BF16),
                            pltpu.SemaphoreType.DMA((2,))]),
        out_shape=jax.ShapeDtypeStruct((cfg.rows, D_EXPERT), BF16),
        compiler_params=_cparams(1, cfg.n_tok),
        name=name + "_up",
    )(tile_expert, tile_first, tile_next, n_valid, x_sorted, w_gate, w_up)
    return pl.pallas_call(
        functools.partial(_moe_down_stream_kernel, layer=layer),
        grid_spec=pltpu.PrefetchScalarGridSpec(
            num_scalar_prefetch=4,
            grid=(cfg.n_tiles,),
            in_specs=[pl.BlockSpec((tm, D_EXPERT), row_map), hbm],
            out_specs=pl.BlockSpec((tm, D_MODEL), out_map),
            scratch_shapes=[pltpu.VMEM((D_EXPERT, D_MODEL), F32), pltpu.VMEM((D_EXPERT, D_MODEL), BF16),
                            pltpu.SemaphoreType.DMA((1,))]),
        out_shape=jax.ShapeDtypeStruct((cfg.rows, D_MODEL), F32),
        compiler_params=_cparams(1, cfg.n_tok),
        name=name + "_down",
    )(tile_expert, tile_first, tile_next, n_valid, hh, w_down)


def moe_experts(tile_expert, tile_first, n_valid, x_sorted, w_gate, w_up, w_down, layer, cfg, name):
    tm = cfg.tile
    n_copies = 2 if cfg.precise else 1
    f_tiles = 2 if cfg.precise else 1
    fw = D_EXPERT // f_tiles
    row_map = lambda i, te, fi, nv: (jnp.minimum(i, nv[0] - 1), 0)
    wmap = lambda i, te, fi, nv: (layer, te[i], 0, 0)
    hh = pl.pallas_call(
        functools.partial(_moe_up_kernel, precise=cfg.precise, f_tiles=f_tiles),
        grid_spec=pltpu.PrefetchScalarGridSpec(
            num_scalar_prefetch=3,
            grid=(cfg.n_tiles, f_tiles),
            in_specs=[pl.BlockSpec((tm, D_MODEL), lambda i, f, te, fi, nv: (jnp.minimum(i, nv[0] - 1), 0)),
                      pl.BlockSpec((None, None, D_MODEL, fw), lambda i, f, te, fi, nv: (layer, te[i], 0, f)),
                      pl.BlockSpec((None, None, D_MODEL, fw), lambda i, f, te, fi, nv: (layer, te[i], 0, f))],
            out_specs=pl.BlockSpec((tm, fw), lambda i, f, te, fi, nv: (i, f)),
            scratch_shapes=[pltpu.VMEM((D_MODEL, fw), BF16)] * (2 * n_copies)),
        out_shape=jax.ShapeDtypeStruct((cfg.rows, D_EXPERT), F32 if cfg.precise else BF16),
        compiler_params=_cparams(2, cfg.n_tok),
        name=name + "_up",
    )(tile_expert, tile_first, n_valid, x_sorted, w_gate, w_up)
    return pl.pallas_call(
        functools.partial(_moe_down_kernel, precise=cfg.precise),
        grid_spec=pltpu.PrefetchScalarGridSpec(
            num_scalar_prefetch=3,
            grid=(cfg.n_tiles,),
            in_specs=[pl.BlockSpec((tm, D_EXPERT), row_map),
                      pl.BlockSpec((None, None, D_EXPERT, D_MODEL), wmap)],
            out_specs=pl.BlockSpec((tm, D_MODEL), lambda i, te, fi, nv: (i, 0)),
            scratch_shapes=[pltpu.VMEM((D_EXPERT, D_MODEL), BF16)] * n_copies),
        out_shape=jax.ShapeDtypeStruct((cfg.rows, D_MODEL), F32),
        compiler_params=_cparams(1, cfg.n_tok),
        name=name + "_down",
    )(tile_expert, tile_first, n_valid, hh, w_down)


def _combine_kernel(d0_ref, d1_ref, x_ref, w_ref, g_ref, b_ref, y_hbm, o_ref, buf_ref, sem, *, tm):
    i = pl.program_id(0)
    n_steps = pl.num_programs(0)

    def gather(tile, half, start):
        base = tile * tm

        def body(it, carry):
            for j in range(ROW_DMA_UNROLL):
                r = it * ROW_DMA_UNROLL + j
                pltpu.make_async_copy(y_hbm.at[pl.ds(d0_ref[base + r], 1)], buf_ref.at[half, 0, pl.ds(r, 1)],
                                      sem.at[half]).start(priority=0)
                pltpu.make_async_copy(y_hbm.at[pl.ds(d1_ref[base + r], 1)], buf_ref.at[half, 1, pl.ds(r, 1)],
                                      sem.at[half]).start(priority=1)
            return carry

        if start:
            lax.fori_loop(0, tm // ROW_DMA_UNROLL, body, 0)
        else:
            for k in range(MOE_TOPK):
                pltpu.make_async_copy(y_hbm.at[pl.ds(0, tm)], buf_ref.at[half, k], sem.at[half]).wait()

    @pl.when(i == 0)
    def _():
        gather(0, 0, True)

    @pl.when(i + 1 < n_steps)
    def _():
        gather(i + 1, (i + 1) % 2, True)

    half = i % 2
    gather(i, half, False)
    w = w_ref[...]
    y = ALPHA * x_ref[...] + w[:, 0:1] * buf_ref[half, 0] + w[:, 1:2] * buf_ref[half, 1]
    o_ref[...] = _ln(y, g_ref[...], b_ref[...])


def moe_combine_ln(d0, d1, x, w_tok, g, b, y_sorted, *, tm, name):
    m = x.shape[0]
    return pl.pallas_call(
        functools.partial(_combine_kernel, tm=tm),
        grid_spec=pltpu.PrefetchScalarGridSpec(
            num_scalar_prefetch=2,
            grid=(m // tm,),
            in_specs=[pl.BlockSpec((tm, D_MODEL), lambda i, a, c: (i, 0)),
                      pl.BlockSpec((tm, MOE_TOPK), lambda i, a, c: (i, 0)),
                      pl.BlockSpec((1, D_MODEL), lambda i, a, c: (0, 0)),
                      pl.BlockSpec((1, D_MODEL), lambda i, a, c: (0, 0)),
                      pl.BlockSpec(memory_space=pl.ANY)],
            out_specs=pl.BlockSpec((tm, D_MODEL), lambda i, a, c: (i, 0)),
            scratch_shapes=[pltpu.VMEM((2, MOE_TOPK, tm, D_MODEL), F32), pltpu.SemaphoreType.DMA((2,))]),
        out_shape=jax.ShapeDtypeStruct((m, D_MODEL), F32),
        compiler_params=_cparams(1, m),
        name=name,
    )(d0, d1, x, w_tok, g, b, y_sorted)


def _routing_plan(route, counts, cfg):
    counts = counts[:, 0].astype(jnp.int32)
    padded = ((counts + cfg.tile - 1) // cfg.tile) * cfg.tile
    ends = jnp.cumsum(padded)
    starts = ends - padded
    experts = jnp.arange(N_EXPERTS, dtype=jnp.int32)[:, None]

    def slot_rows(k):
        picked = route[k].astype(jnp.int32)[None, :] == experts
        return jnp.sum(jnp.where(picked, starts[:, None], 0), axis=0) + route[4 + k].astype(jnp.int32)

    tile_start = jnp.arange(cfg.n_tiles, dtype=jnp.int32) * cfg.tile
    tile_expert = jnp.minimum(jnp.sum((tile_start[:, None] >= ends[None, :]).astype(jnp.int32), axis=1),
                              N_EXPERTS - 1)
    start_of_tile_expert = jnp.sum(jnp.where(tile_expert[:, None] == experts.T, starts[None, :], 0), axis=1)
    tile_first = (tile_start == start_of_tile_expert).astype(jnp.int32)
    n_valid = (ends[-1] // cfg.tile).astype(jnp.int32).reshape(1)
    later_nonempty = (experts.T > tile_expert[:, None]) & (padded[None, :] > 0)
    tile_next = jnp.min(jnp.where(later_nonempty, experts.T, N_EXPERTS), axis=1)
    tile_next = jnp.where(tile_next == N_EXPERTS, -1, tile_next).astype(jnp.int32)
    return (slot_rows(0), slot_rows(1), route[2:4].T, starts, ends, tile_expert, tile_first, tile_next, n_valid)


def moe_ln(x, x_route, layer, p, cfg, name):
    route, cnt = router(x, x_route, p["router_w_t"], p["router_b"], tm=cfg.tok_tile, name=name + "_router")
    d0, d1, w_tok, starts, ends, tile_expert, tile_first, tile_next, n_valid = _routing_plan(route, cnt, cfg)
    x_sorted = moe_dispatch(d0, d1, starts, ends, n_valid, x, cfg, name + "_dispatch")
    weights = (p["moe_w_gate"], p["moe_w_up"], p["moe_w_down"])
    if cfg.precise:
        y_sorted = moe_experts(tile_expert, tile_first, n_valid, x_sorted, *weights, layer, cfg, name)
    else:
        y_sorted = moe_experts_streamed(tile_expert, tile_first, tile_next, n_valid, x_sorted, *weights, layer,
                                        cfg, name)
    return moe_combine_ln(d0, d1, x, w_tok, p["moe_ln_g"][layer], p["moe_ln_b"][layer], y_sorted,
                          tm=cfg.tok_tile, name=name + "_combine")


def _rope_tables(pos):
    half = ROT_DIM // 2
    inv = jnp.power(ROPE_THETA, -jnp.arange(half, dtype=F32) / half)
    ang = pos.astype(F32)[:, None] * inv[None, :]
    cos, sin = jnp.cos(ang), jnp.sin(ang)
    n = pos.shape[0]
    ones = jnp.ones((n, HEAD_DIM - ROT_DIM), F32)
    zeros_h = jnp.zeros((n, half), F32)
    zeros_r = jnp.zeros((n, HEAD_DIM - ROT_DIM), F32)
    c = jnp.concatenate([cos, cos, ones], axis=1)
    sa = jnp.concatenate([zeros_h, sin, zeros_r], axis=1)
    sb = jnp.concatenate([-sin, zeros_h, zeros_r], axis=1)
    return c, sa, sb


def kernel(x_prompt, x_sample, mem_prompt, cache_c_k, cache_c_v, page_table, state_b_buf, state_d_buf, cache_mem_k, cache_mem_v, ab_w_in, ab_a_ln_g, ab_a_ln_b, ab_a_ws, ab_a_bs, ab_b_wg, ab_b_scale, ab_w_out, cd_w_in, cd_d_conv_w, cd_d_conv_b, cd_d_ln_g, cd_d_ln_b, cd_w_out, mix_ln_g, mix_ln_b, mem_w_q, mem_w_k, mem_w_v, mem_w_o, mem_ln_g, mem_ln_b, router_w, router_b, moe_w_gate, moe_w_up, moe_w_down, moe_ln_g, moe_ln_b):
    row = lambda v: v.reshape(1, -1)
    xp = x_prompt.reshape(N_PROMPT, D_MODEL)
    xs = x_sample.reshape(N_SAMPLE, D_MODEL)
    moe_p = {"router_w_t": router_w.T, "router_b": router_b.reshape(N_EXPERTS, 1),
             "moe_w_gate": moe_w_gate, "moe_w_up": moe_w_up,
             "moe_w_down": moe_w_down, "moe_ln_g": [row(moe_ln_g[l]) for l in range(DEPTH)],
             "moe_ln_b": [row(moe_ln_b[l]) for l in range(DEPTH)]}

    w_kv = split_weight(jnp.concatenate([mem_w_k[0], mem_w_k[1], mem_w_v[0], mem_w_v[1]], axis=1), True)
    n_memrows = BATCH * N_MEM
    kv = matmul(mem_prompt.reshape(n_memrows, D_MODEL), w_kv, m=n_memrows, tm=256, tn=1024, precise=True,
                name="mem_kv")
    pmk = [kv[:, l * MEM_WIDTH:(l + 1) * MEM_WIDTH].reshape(BATCH, N_MEM, MEM_WIDTH) for l in range(DEPTH)]
    pmv = [kv[:, (DEPTH + l) * MEM_WIDTH:(DEPTH + l + 1) * MEM_WIDTH].reshape(BATCH, N_MEM, MEM_WIDTH)
           for l in range(DEPTH)]
    smk = cache_mem_k.reshape(DEPTH, DEC_BATCH, N_MEM, MEM_WIDTH)
    smv = cache_mem_v.reshape(DEPTH, DEC_BATCH, N_MEM, MEM_WIDTH)

    mem_w = [(split_weight(mem_w_q[l], True), split_weight(mem_w_o[l], True)) for l in range(DEPTH)]
    mem_ln = [(row(mem_ln_g[l]), row(mem_ln_b[l])) for l in range(DEPTH)]

    def memattn_prompt(x, layer):
        wq, wo = split_weight(mem_w_q[layer], False), split_weight(mem_w_o[layer], False)
        return memory_attention_ln(x, wq, pmk[layer], pmv[layer], wo, *mem_ln[layer], m=N_PROMPT, tm=512,
                                   rows_per_b=SEQ, name="memattn_prompt")

    def memattn_sample(x, layer):
        wq, wo = mem_w[layer]
        return memory_attention_ln(x, wq, smk[layer], smv[layer], wo, *mem_ln[layer], m=N_SAMPLE, tm=N_SAMPLE,
                                   rows_per_b=DEC_SEQ, precise=True, name="memattn_sample")

    w_in0 = split_weight(ab_w_in[0], True)
    w_out0 = split_weight(ab_w_out[0], True)
    ab_prm = {"ln_g": row(ab_a_ln_g[0]), "ln_b": row(ab_a_ln_b[0]), "ws": ab_a_ws[0], "bs_t": ab_a_bs[0].T,
              "wg": ab_b_wg[0].astype(BF16), "scale": row(ab_b_scale[0])}
    ab_prm_precise = dict(ab_prm, wg=ab_b_wg[0])
    g, b = row(mix_ln_g[0]), row(mix_ln_b[0])
    n_chunks = SEQ // CHUNK
    n_tail = BATCH * CHUNK

    h0p = matmul(xp, split_weight(ab_w_in[0], False), m=N_PROMPT, tm=1024, tn=1024, n_gelu=2, name="ab_in_prompt")
    mix_p, _ = mixer_ab(h0p, h0p, lambda b, c: (jnp.maximum(b * (SEQ // 16) + c * (CHUNK // 16) - 1, 0), 2),
                        nb=BATCH, n_chunks=n_chunks, pos0=0, has_ctx=False, prm=ab_prm, name="mixer_ab_prompt")
    x1p = matmul_res_ln([mix_p], [split_weight(ab_w_out[0], False)], xp, g, b, m=N_PROMPT, tm=512,
                        name="ab_out_prompt")
    x1p = memattn_prompt(x1p, 0)

    xt = x_prompt[:, SEQ - CHUNK:].reshape(n_tail, D_MODEL)
    h0t = matmul(xt, w_in0, m=n_tail, tm=CHUNK, tn=1024, n_gelu=2, precise=True, name="ab_in_tail")
    mix_t, _ = mixer_ab(h0t, h0p, lambda b, c: ((b + 1) * (SEQ // 16) - CHUNK // 16 - 1, 2), nb=BATCH, n_chunks=1,
                        pos0=SEQ - CHUNK, has_ctx=True, prm=ab_prm_precise, precise=True, name="mixer_ab_tail")
    x1t = matmul_res_ln([mix_t], [w_out0], xt, g, b, m=n_tail, tm=CHUNK, precise=True, name="ab_out_tail")
    wq, wo = mem_w[0]
    x1t = memory_attention_ln(x1t, wq, pmk[0], pmv[0], wo, *mem_ln[0], m=n_tail, tm=CHUNK, rows_per_b=CHUNK,
                              precise=True, name="memattn_tail")

    h0s = matmul(xs, w_in0, m=N_SAMPLE, tm=N_SAMPLE, tn=1024, n_gelu=2, precise=True, name="ab_in_sample")
    h0s_pad = jnp.pad(h0s.reshape(DEC_BATCH, DEC_SEQ, -1), ((0, 0), (0, CHUNK - DEC_SEQ), (0, 0)))
    h0s_pad = h0s_pad.reshape(DEC_BATCH * CHUNK, -1)
    zctx = jnp.pad(state_b_buf[0], ((0, 0), (16 - B_BUF, 0), (0, 0))).reshape(DEC_BATCH * 16, B_WIDTH)
    mix_s, vn_s = mixer_ab(h0s_pad, zctx, lambda b, c: (b, 0), nb=DEC_BATCH, n_chunks=1, pos0=PAST_LEN,
                           has_ctx=True, prm=ab_prm_precise, precise=True, name="mixer_ab_sample")
    mix_s = mix_s.reshape(DEC_BATCH, CHUNK, -1)[:, :DEC_SEQ].reshape(N_SAMPLE, -1)
    x1s = matmul_res_ln([mix_s], [w_out0], xs, g, b, m=N_SAMPLE, tm=N_SAMPLE, precise=True, name="ab_out_sample")
    x1s = memattn_sample(x1s, 0)

    x2p = moe_ln(x1p, x1t, 0, moe_p, MOE_PROMPT, "moe_prompt")
    x2s = moe_ln(x1s, None, 0, moe_p, MOE_SAMPLE, "moe_sample")

    w_in1 = split_weight(cd_w_in[0], True)
    h1p = matmul(x2p, split_weight(cd_w_in[0], False), m=N_PROMPT, tm=1024, tn=1024, name="cd_in_prompt")
    h1s = matmul(x2s, w_in1, m=N_SAMPLE, tm=N_SAMPLE, tn=1024, precise=True, name="cd_in_sample")
    tabs_p = _rope_tables(jnp.arange(SEQ, dtype=jnp.int32))
    tabs_s = _rope_tables(PAST_LEN + (jnp.arange(N_SAMPLE, dtype=jnp.int32) % DEC_SEQ))
    qr_p, kr_p = rope_qk(h1p, tabs_p, m=N_PROMPT, tm=256, xoff=0, tab_blocks=SEQ // 256, name="rope_prompt")
    qr_s, kr_s = rope_qk(h1s, tabs_s, m=N_SAMPLE, tm=N_SAMPLE, xoff=0, tab_blocks=1, name="rope_sample")
    c_p = moba_prompt(qr_p, kr_p, h1p, nb=BATCH, seq=SEQ)
    v_s = h1s[:, 2 * C_WIDTH:3 * C_WIDTH]
    q_combo = qr_s.reshape(DEC_BATCH, N_COMBO, HEAD_DIM)
    m_p, l_p, o_p, ksum = paged_partials(page_table, q_combo, cache_c_k, cache_c_v, 0)
    c_s = paged_merge(q_combo, kr_s.reshape(DEC_BATCH, N_COMBO, HEAD_DIM),
                      v_s.reshape(DEC_BATCH, N_COMBO, HEAD_DIM), m_p, l_p, o_p, ksum)
    c_s = c_s.reshape(N_SAMPLE, C_WIDTH)

    cd_prm = {"conv_w": cd_d_conv_w[0], "conv_b": row(cd_d_conv_b[0]), "ln_g": row(cd_d_ln_g[0]),
              "ln_b": row(cd_d_ln_b[0])}
    col_a = 3 * C_WIDTH // D_WIDTH
    tiles_b = SEQ // 256
    halo_per_tile = 256 // CONV_HALO

    def prev_map(col):
        return lambda b, t: (jnp.maximum((b * tiles_b + t) * halo_per_tile - 1, 0), col)

    d_p, tail_p = conv_module(h1p, col_a, h1p, h1p, prev_map(col_a), prev_map(col_a + 1), nb=BATCH,
                              n_tiles=tiles_b, tm=256, rt=128, tail=CONV_HALO, xoff=0, prev_is_state=False,
                              prm=cd_prm, name="conv_prompt")
    gl_s = jnp.pad(h1s[:, 3 * C_WIDTH:].reshape(DEC_BATCH, DEC_SEQ, 2 * D_WIDTH), ((0, 0), (0, 8 - DEC_SEQ), (0, 0)))
    gl_s = gl_s.reshape(DEC_BATCH * 8, 2 * D_WIDTH)
    dctx = jnp.pad(state_d_buf[0], ((0, 0), (CONV_HALO - D_BUF, 0), (0, 0))).reshape(DEC_BATCH * CONV_HALO, D_WIDTH)
    d_s, tail_s = conv_module(gl_s, 0, dctx, dctx, lambda b, t: (b, 0), lambda b, t: (b, 0), nb=DEC_BATCH,
                              n_tiles=1, tm=8, rt=8, tail=8, xoff=0, prev_is_state=True, prm=cd_prm,
                              name="conv_sample", out_dtype=F32)
    d_s = d_s.reshape(DEC_BATCH, 8, D_WIDTH)[:, :DEC_SEQ].reshape(N_SAMPLE, D_WIDTH)
    w_out1c = split_weight(cd_w_out[0][:C_WIDTH], True)
    w_out1d = split_weight(cd_w_out[0][C_WIDTH:], True)
    g, b = row(mix_ln_g[1]), row(mix_ln_b[1])
    w_out1 = cd_w_out[0].astype(BF16)
    x3p = matmul_res_ln([c_p, d_p], [(w_out1[:C_WIDTH],), (w_out1[C_WIDTH:],)], x2p, g, b, m=N_PROMPT, tm=512,
                        name="cd_out_prompt")
    x3s = matmul_res_ln([c_s, d_s], [w_out1c, w_out1d], x2s, g, b, m=N_SAMPLE, tm=N_SAMPLE, precise=True,
                        name="cd_out_sample")
    y_p = moe_ln(memattn_prompt(x3p, 1), None, 1, moe_p, MOE_PROMPT, "moe_prompt")
    y_s = moe_ln(memattn_sample(x3s, 1), None, 1, moe_p, MOE_SAMPLE, "moe_sample")

    kv_shape_p = (1, BATCH, SEQ, C_HEADS, HEAD_DIM)
    kv_shape_s = (1, DEC_BATCH, DEC_SEQ, C_HEADS, HEAD_DIM)
    z_p = h0p[:, 2 * A_WIDTH:].reshape(BATCH, SEQ, B_WIDTH)
    z_s = h0s[:, 2 * A_WIDTH:].reshape(DEC_BATCH, DEC_SEQ, B_WIDTH)
    h_s = tail_s.reshape(DEC_BATCH, 8, D_WIDTH)[:, :DEC_SEQ]
    mem_shape = (BATCH, N_MEM, MEM_HEADS, MEM_HEAD_DIM)
    return (y_p.reshape(BATCH, SEQ, D_MODEL),
            y_s.reshape(DEC_BATCH, DEC_SEQ, D_MODEL),
            kr_p.reshape(kv_shape_p),
            h1p[:, 2 * C_WIDTH:3 * C_WIDTH].reshape(kv_shape_p),
            kr_s.reshape(kv_shape_s),
            v_s.reshape(kv_shape_s),
            z_p[:, SEQ - B_BUF:][None],
            jnp.concatenate([state_b_buf[0], z_s], axis=1)[:, DEC_SEQ:][None],
            vn_s.reshape(DEC_BATCH, CHUNK, A_WIDTH)[:, :DEC_SEQ][None],
            tail_p.reshape(BATCH, CONV_HALO, D_WIDTH)[:, CONV_HALO - D_BUF:][None],
            jnp.concatenate([state_d_buf[0], h_s], axis=1)[:, DEC_SEQ:][None],
            jnp.stack([m_.reshape(mem_shape) for m_ in pmk]),
            jnp.stack([m_.reshape(mem_shape) for m_ in pmv]))
```

```python
import functools
import math
from typing import NamedTuple

import jax
import jax.numpy as jnp
from jax import lax
from jax.experimental import pallas as pl
from jax.experimental.pallas import tpu as pltpu

F32 = jnp.float32
BF16 = jnp.bfloat16
HIGHEST = lax.Precision.HIGHEST

D_MODEL = 2048
BATCH = 4
SEQ = 2048
DEPTH = 2
DEC_BATCH = 8
DEC_SEQ = 4
PAST_LEN = 16384
PAGE_SIZE = 128
A_WIDTH = 1024
CHUNK = 128
A_GROUPS = 8
B_WIDTH = 1024
POOL_WINDOWS = (2, 4, 8, 16)
B_GROUP_DIM = B_WIDTH // len(POOL_WINDOWS)
B_BUF = 15
C_HEADS = 8
HEAD_DIM = 128
C_WIDTH = 1024
ROT_DIM = 32
ROPE_THETA = 500000.0
MOBA_BLOCK = 256
MOBA_TOPK = 3
D_WIDTH = 1024
CONV_WIDTH = 31
D_BUF = 30
N_MEM = 256
MEM_HEADS = 4
MEM_HEAD_DIM = 128
MEM_WIDTH = 512
N_EXPERTS = 16
N_EXPERT_GROUPS = 4
EXPERTS_PER_GROUP = 4
MOE_TOPK = 2
D_EXPERT = 1024
ALPHA = (2 * DEPTH) ** 0.25
LN_EPS = 1e-5

N_PROMPT = BATCH * SEQ
N_SAMPLE = DEC_BATCH * DEC_SEQ
N_TOK = N_PROMPT + N_SAMPLE
N_PAST_BLOCKS = PAST_LEN // MOBA_BLOCK
PAGES_PER_BLOCK = MOBA_BLOCK // PAGE_SIZE
N_COMBO = DEC_SEQ * C_HEADS


class MoeCfg(NamedTuple):
    n_tok: int
    tok_tile: int
    tile: int
    rows: int
    n_tiles: int
    precise: bool


def _moe_cfg(n_tok, tile, precise):
    rows = ((n_tok * MOE_TOPK + N_EXPERTS * (tile - 1)) // tile + 1) * tile
    return MoeCfg(n_tok, tile, tile, rows, rows // tile, precise)


MOE_PROMPT = _moe_cfg(N_PROMPT, 256, False)
MOE_SAMPLE = _moe_cfg(N_SAMPLE, N_SAMPLE, True)

ROW_DMA_UNROLL = 8
VMEM_LIMIT = 56 * 1024 * 1024
VMEM_LIMIT_SMALL = 40 * 1024 * 1024
SMALL_CALL_ROWS = 1024


def _cparams(n_axes, rows=None):
    vmem = VMEM_LIMIT if rows is None or rows > SMALL_CALL_ROWS else VMEM_LIMIT_SMALL
    return pltpu.CompilerParams(dimension_semantics=("arbitrary",) * n_axes, vmem_limit_bytes=vmem)


def _ln(x, g, b):
    mu = jnp.mean(x, axis=-1, keepdims=True)
    xc = x - mu
    var = jnp.mean(xc * xc, axis=-1, keepdims=True)
    return xc * lax.rsqrt(var + LN_EPS) * g + b


def _dot(a, b):
    return jnp.dot(a, b, preferred_element_type=F32)


def _dot_nt(a, b, precision=None):
    return lax.dot_general(a, b, (((1,), (1,)), ((), ())), precision=precision, preferred_element_type=F32)


def _split_bf16(v):
    hi = v.astype(BF16)
    return hi, (v.astype(F32) - hi.astype(F32)).astype(BF16)


def _mm(a, b, precise, nt=False):
    dot = _dot_nt if nt else _dot
    if not precise:
        b0 = b[0] if isinstance(b, tuple) else b
        return dot(a.astype(BF16), b0.astype(BF16))
    ah, al = _split_bf16(a)
    bh, bl = b if isinstance(b, tuple) else _split_bf16(b)
    return dot(ah, bh) + (dot(al, bh) + dot(ah, bl))


def _load_all(refs):
    return tuple(r[...] for r in refs)


def split_weight(w, precise):
    if not precise:
        return (w.astype(BF16),)
    bits = lax.bitcast_convert_type(w, jnp.uint32) & jnp.uint32(0xFFFF0000)
    hi = lax.bitcast_convert_type(bits, F32)
    return hi.astype(BF16), (w - hi).astype(BF16)


def _mm_kernel(x_ref, *refs, n_gelu, precise):
    o_ref = refs[-1]
    acc = _mm(x_ref[...], _load_all(refs[:-1]), precise)
    if n_gelu:
        j = pl.program_id(0)

        @pl.when(j < n_gelu)
        def _():
            o_ref[...] = jax.nn.gelu(acc).astype(o_ref.dtype)

        @pl.when(j >= n_gelu)
        def _():
            o_ref[...] = acc.astype(o_ref.dtype)
    else:
        o_ref[...] = acc.astype(o_ref.dtype)


def matmul(x, w, *, m, tm, tn, xoff=0, n_gelu=0, precise=False, name="mm"):
    k = x.shape[1]
    n = w[0].shape[1]
    return pl.pallas_call(
        functools.partial(_mm_kernel, n_gelu=n_gelu, precise=precise),
        grid=(n // tn, m // tm),
        in_specs=[pl.BlockSpec((tm, k), lambda j, i: (i + xoff, 0))]
        + [pl.BlockSpec((k, tn), lambda j, i: (0, j))] * len(w),
        out_specs=pl.BlockSpec((tm, tn), lambda j, i: (i, j)),
        out_shape=jax.ShapeDtypeStruct((m, n), F32),
        compiler_params=_cparams(2, m),
        name=name,
    )(x, *w)


def _mm_res_ln_kernel(*refs, n_in, precise):
    n_w = 2 if precise else 1
    a_refs = refs[:n_in]
    w_refs = refs[n_in:n_in + n_in * n_w]
    r_ref, g_ref, b_ref, o_ref = refs[n_in + n_in * n_w:]
    acc = None
    for k, a_ref in enumerate(a_refs):
        d = _mm(a_ref[...], _load_all(w_refs[k * n_w:(k + 1) * n_w]), precise)
        acc = d if acc is None else acc + d
    o_ref[...] = _ln(ALPHA * r_ref[...] + acc, g_ref[...], b_ref[...])


def matmul_res_ln(a_list, w_list, resid, g, b, *, m, tm, roff=0, precise=False, name="mm_res_ln"):
    n_in = len(a_list)
    w_flat = [part for w in w_list for part in w]
    in_specs = [pl.BlockSpec((tm, a.shape[1]), lambda i: (i, 0)) for a in a_list]
    in_specs += [pl.BlockSpec(w.shape, lambda i: (0, 0)) for w in w_flat]
    in_specs += [pl.BlockSpec((tm, D_MODEL), lambda i: (i + roff, 0)),
                 pl.BlockSpec((1, D_MODEL), lambda i: (0, 0)),
                 pl.BlockSpec((1, D_MODEL), lambda i: (0, 0))]
    return pl.pallas_call(
        functools.partial(_mm_res_ln_kernel, n_in=n_in, precise=precise),
        grid=(m // tm,),
        in_specs=in_specs,
        out_specs=pl.BlockSpec((tm, D_MODEL), lambda i: (i, 0)),
        out_shape=jax.ShapeDtypeStruct((m, D_MODEL), F32),
        compiler_params=_cparams(1, m),
        name=name,
    )(*a_list, *w_flat, resid, g, b)


def _mixer_ab_kernel(u_ref, v_ref, z_ref, zp_ref, lng_ref, lnb_ref, ws_ref, bs_ref, wg_ref, sc_ref,
                     mix_ref, vn_ref, *, pos0, has_ctx, precise):
    c = pl.program_id(1)
    vn = _ln(v_ref[...], lng_ref[...], lnb_ref[...])
    vn_ref[...] = vn
    u = u_ref[...]
    row = lax.broadcasted_iota(jnp.int32, (CHUNK, CHUNK), 0)
    col = lax.broadcasted_iota(jnp.int32, (CHUNK, CHUNK), 1)
    causal = col <= row
    gd = A_WIDTH // A_GROUPS
    for g in range(A_GROUPS):
        sl = slice(g * gd, (g + 1) * gd)
        w = jnp.where(causal, ws_ref[g], 0.0)
        mixed = _mm(w, vn[:, sl], precise) + bs_ref[:, g:g + 1]
        mix_ref[:, sl] = (u[:, sl] * mixed).astype(mix_ref.dtype)

    z = z_ref[...]
    zp = zp_ref[...]
    if not has_ctx:
        zp = jnp.where(c == 0, 0.0, zp)
    zext = jnp.concatenate([zp, z], axis=0)
    pos = pos0 + c * CHUNK + lax.broadcasted_iota(jnp.int32, (CHUNK, 1), 0)
    for gi, wdw in enumerate(POOL_WINDOWS):
        sl = slice(gi * B_GROUP_DIM, (gi + 1) * B_GROUP_DIM)
        s = zext[:, sl]
        sh = 1
        while sh < wdw:
            s = s + pltpu.roll(s, sh, axis=0)
            sh *= 2
        cnt = jnp.minimum(wdw, pos + 1).astype(F32)
        d = s[16:, :] / cnt - z[:, sl]
        bo = _mm(d, wg_ref[gi], precise) * sc_ref[:, sl]
        mix_ref[:, A_WIDTH + gi * B_GROUP_DIM:A_WIDTH + (gi + 1) * B_GROUP_DIM] = bo.astype(mix_ref.dtype)


def mixer_ab(h, zprev, zprev_map, *, nb, n_chunks, pos0, has_ctx, prm, name, precise=False):
    m = nb * n_chunks * CHUNK
    row = lambda b, c: b * n_chunks + c
    const2 = lambda b, c: (0, 0)
    return pl.pallas_call(
        functools.partial(_mixer_ab_kernel, pos0=pos0, has_ctx=has_ctx, precise=precise),
        grid=(nb, n_chunks),
        in_specs=[pl.BlockSpec((CHUNK, A_WIDTH), lambda b, c: (row(b, c), 0)),
                  pl.BlockSpec((CHUNK, A_WIDTH), lambda b, c: (row(b, c), 1)),
                  pl.BlockSpec((CHUNK, B_WIDTH), lambda b, c: (row(b, c), 2)),
                  pl.BlockSpec((16, B_WIDTH), zprev_map),
                  pl.BlockSpec((1, A_WIDTH), const2),
                  pl.BlockSpec((1, A_WIDTH), const2),
                  pl.BlockSpec((A_GROUPS, CHUNK, CHUNK), lambda b, c: (0, 0, 0)),
                  pl.BlockSpec((CHUNK, A_GROUPS), const2),
                  pl.BlockSpec((len(POOL_WINDOWS), B_GROUP_DIM, B_GROUP_DIM), lambda b, c: (0, 0, 0)),
                  pl.BlockSpec((1, B_WIDTH), const2)],
        out_specs=[pl.BlockSpec((CHUNK, A_WIDTH + B_WIDTH), lambda b, c: (row(b, c), 0)),
                   pl.BlockSpec((CHUNK, A_WIDTH), lambda b, c: (row(b, c), 0))],
        out_shape=[jax.ShapeDtypeStruct((m, A_WIDTH + B_WIDTH), F32 if precise else BF16),
                   jax.ShapeDtypeStruct((m, A_WIDTH), F32)],
        compiler_params=_cparams(2, m),
        name=name,
    )(h, h, h, zprev, prm["ln_g"], prm["ln_b"], prm["ws"], prm["bs_t"], prm["wg"], prm["scale"])


def _rope_kernel(q_ref, k_ref, c_ref, sa_ref, sb_ref, qo_ref, ko_ref):
    c = c_ref[...]
    sa = sa_ref[...]
    sb = sb_ref[...]
    half = ROT_DIM // 2
    for h in range(C_HEADS):
        sl = slice(h * HEAD_DIM, (h + 1) * HEAD_DIM)
        for src, dst in ((q_ref, qo_ref), (k_ref, ko_ref)):
            x = src[:, sl]
            dst[:, sl] = (x * c + pltpu.roll(x, half, axis=1) * sa
                          + pltpu.roll(x, HEAD_DIM - half, axis=1) * sb)


def rope_qk(h, tabs, *, m, tm, xoff, tab_blocks, name):
    tmap = lambda i: (i % tab_blocks, 0)
    return pl.pallas_call(
        _rope_kernel,
        grid=(m // tm,),
        in_specs=[pl.BlockSpec((tm, C_WIDTH), lambda i: (i + xoff, 0)),
                  pl.BlockSpec((tm, C_WIDTH), lambda i: (i + xoff, 1)),
                  pl.BlockSpec((tm, HEAD_DIM), tmap),
                  pl.BlockSpec((tm, HEAD_DIM), tmap),
                  pl.BlockSpec((tm, HEAD_DIM), tmap)],
        out_specs=[pl.BlockSpec((tm, C_WIDTH), lambda i: (i, 0)),
                   pl.BlockSpec((tm, C_WIDTH), lambda i: (i, 0))],
        out_shape=[jax.ShapeDtypeStruct((m, C_WIDTH), F32),
                   jax.ShapeDtypeStruct((m, C_WIDTH), F32)],
        compiler_params=_cparams(1, m),
        name=name,
    )(h, h, *tabs)


def _moba_kernel(q_ref, qall_ref, k_ref, v_ref, o_ref, selt_ref, *, n_blocks):
    i = pl.program_id(2)
    blk_rows = MOBA_BLOCK
    seq = n_blocks * blk_rows
    scale = HEAD_DIM ** -0.5

    @pl.when(i == 0)
    def _():
        kmean = jnp.concatenate(
            [jnp.mean(k_ref[j * blk_rows:(j + 1) * blk_rows, :], axis=0, keepdims=True) for j in range(n_blocks)],
            axis=0)
        bst = _mm(kmean, qall_ref[...], True, nt=True)
        blk = lax.broadcasted_iota(jnp.int32, (n_blocks, seq), 0)
        own = lax.broadcasted_iota(jnp.int32, (n_blocks, seq), 1) >> int(math.log2(blk_rows))
        work = jnp.where(blk < own, bst, -jnp.inf)
        sel = jnp.zeros((n_blocks, seq), F32)
        for _ in range(MOBA_TOPK):
            mx = jnp.max(work, axis=0, keepdims=True)
            first = jnp.min(jnp.where(work == mx, blk, n_blocks), axis=0, keepdims=True)
            hit = (blk == first) & (mx > -jnp.inf)
            sel = jnp.where(hit, 1.0, sel)
            work = jnp.where(hit, -jnp.inf, work)
        selt_ref[...] = jnp.zeros(selt_ref.shape, F32)
        for c in range(n_blocks):
            selt_ref[c, 0:n_blocks, :] = sel[:, c * blk_rows:(c + 1) * blk_rows]

    qb = q_ref[...].astype(BF16)
    sel = jnp.transpose(selt_ref[i])

    r_i = lax.broadcasted_iota(jnp.int32, (blk_rows, blk_rows), 0)
    c_i = lax.broadcasted_iota(jnp.int32, (blk_rows, blk_rows), 1)
    causal = (c_i <= r_i).astype(F32)

    for c in range(n_blocks):
        @pl.when(i == c)
        def _():
            nk = (c + 1) * blk_rows
            s = _dot_nt(qb, k_ref[0:nk, :].astype(BF16)) * scale
            allowed = jnp.concatenate(
                [jnp.broadcast_to(sel[:, j:j + 1], (blk_rows, blk_rows)) for j in range(c)] + [causal], axis=1)
            s = jnp.where(allowed > 0.0, s, -jnp.inf)
            m = jnp.max(s, axis=-1, keepdims=True)
            p = jnp.exp(s - m)
            l = jnp.sum(p, axis=-1, keepdims=True)
            o = _dot(p.astype(BF16), v_ref[0:nk, :].astype(BF16))
            o_ref[...] = (o / l).astype(o_ref.dtype)


def moba_prompt(q_rot, k_rot, h, *, nb, seq, name="moba_prompt"):
    n_blocks = seq // MOBA_BLOCK
    v_col0 = 2 * C_WIDTH // HEAD_DIM
    return pl.pallas_call(
        functools.partial(_moba_kernel, n_blocks=n_blocks),
        grid=(nb, C_HEADS, n_blocks),
        in_specs=[pl.BlockSpec((MOBA_BLOCK, HEAD_DIM), lambda b, hh, i: (b * n_blocks + i, hh)),
                  pl.BlockSpec((seq, HEAD_DIM), lambda b, hh, i: (b, hh)),
                  pl.BlockSpec((seq, HEAD_DIM), lambda b, hh, i: (b, hh)),
                  pl.BlockSpec((seq, HEAD_DIM), lambda b, hh, i: (b, v_col0 + hh))],
        out_specs=pl.BlockSpec((MOBA_BLOCK, HEAD_DIM), lambda b, hh, i: (b * n_blocks + i, hh)),
        out_shape=jax.ShapeDtypeStruct((nb * seq, C_WIDTH), BF16),
        scratch_shapes=[pltpu.VMEM((n_blocks, 128, MOBA_BLOCK), F32)],
        compiler_params=_cparams(3),
        name=name,
    )(q_rot, q_rot, k_rot, h)


CONV_HALO = 32
SUBLANES = 8


def _conv_kernel(ga_ref, gg_ref, pa_ref, pg_ref, w_ref, bdw_ref, lng_ref, lnb_ref,
                 o_ref, tail_ref, ext_ref, sh_ref, y_ref, *, tm, rt, tail, prev_is_state):
    t = pl.program_id(1)
    hcur = ga_ref[...] * jax.nn.sigmoid(gg_ref[...])
    if prev_is_state:
        hprev = pa_ref[...]
    else:
        hprev = pa_ref[...] * jax.nn.sigmoid(pg_ref[...])
        hprev = jnp.where(t == 0, 0.0, hprev)
    ext_ref[0:CONV_HALO, :] = hprev
    ext_ref[CONV_HALO:CONV_HALO + tm, :] = hcur
    tail_ref[...] = hcur[tm - tail:, :]
    off = CONV_HALO - D_BUF
    sh_rows = sh_ref.shape[1]
    for s in range(1, SUBLANES):
        sh_ref[s - 1] = ext_ref[s:s + sh_rows, :]
    for cc in range(D_WIDTH // 128):
        cs = slice(cc * 128, (cc + 1) * 128)
        for rc in range(tm // rt):
            r0 = rc * rt
            acc = jnp.zeros((rt, 128), F32)
            for j in range(CONV_WIDTH):
                s, a = (off + j) % SUBLANES, (off + j) // SUBLANES
                row0 = r0 + a * SUBLANES
                src = ext_ref[row0:row0 + rt, cs] if s == 0 else sh_ref[s - 1, row0:row0 + rt, cs]
                acc = acc + w_ref[j:j + 1, cs] * src
            y_ref[r0:r0 + rt, cs] = acc + bdw_ref[:, cs]
    y = _ln(y_ref[...], lng_ref[...], lnb_ref[...])
    o_ref[...] = (y * jax.nn.sigmoid(y)).astype(o_ref.dtype)


def conv_module(h, col_a, prev_a, prev_g, prev_map_a, prev_map_g, *, nb, n_tiles, tm, rt, tail, xoff,
                prev_is_state, prm, name, out_dtype=BF16):
    m = nb * n_tiles * tm
    row = lambda b, t: b * n_tiles + t + xoff
    const2 = lambda b, t: (0, 0)
    return pl.pallas_call(
        functools.partial(_conv_kernel, tm=tm, rt=rt, tail=tail, prev_is_state=prev_is_state),
        grid=(nb, n_tiles),
        in_specs=[pl.BlockSpec((tm, D_WIDTH), lambda b, t: (row(b, t), col_a)),
                  pl.BlockSpec((tm, D_WIDTH), lambda b, t: (row(b, t), col_a + 1)),
                  pl.BlockSpec((CONV_HALO, D_WIDTH), prev_map_a),
                  pl.BlockSpec((CONV_HALO, D_WIDTH), prev_map_g),
                  pl.BlockSpec((CONV_WIDTH, D_WIDTH), const2),
                  pl.BlockSpec((1, D_WIDTH), const2),
                  pl.BlockSpec((1, D_WIDTH), const2),
                  pl.BlockSpec((1, D_WIDTH), const2)],
        out_specs=[pl.BlockSpec((tm, D_WIDTH), lambda b, t: (b * n_tiles + t, 0)),
                   pl.BlockSpec((tail, D_WIDTH), lambda b, t: (b, 0))],
        out_shape=[jax.ShapeDtypeStruct((m, D_WIDTH), out_dtype),
                   jax.ShapeDtypeStruct((nb * tail, D_WIDTH), F32)],
        scratch_shapes=[pltpu.VMEM((CONV_HALO + tm, D_WIDTH), F32),
                        pltpu.VMEM((SUBLANES - 1, CONV_HALO + tm - SUBLANES, D_WIDTH), F32),
                        pltpu.VMEM((tm, D_WIDTH), F32)],
        compiler_params=_cparams(2, m),
        name=name,
    )(h, h, prev_a, prev_g, prm["conv_w"], prm["conv_b"], prm["ln_g"], prm["ln_b"])


PAGED_BLOCKS_PER_STEP = 4
PAGED_PAGES_PER_STEP = PAGED_BLOCKS_PER_STEP * PAGES_PER_BLOCK


def _paged_partial_kernel(pt_ref, q_ref, *refs):
    del pt_ref
    pages = refs[:2 * PAGED_PAGES_PER_STEP]
    m_ref, l_ref, o_ref, ks_ref, kb_ref, vb_ref = refs[2 * PAGED_PAGES_PER_STEP:]
    rows = PAGE_SIZE * C_HEADS
    blk_keys = PAGES_PER_BLOCK * rows
    scale = HEAD_DIM ** -0.5
    k_refs = pages[:PAGED_PAGES_PER_STEP]
    v_refs = pages[PAGED_PAGES_PER_STEP:]
    for pg in range(PAGED_PAGES_PER_STEP):
        k = k_refs[pg][...]
        ksum = jnp.sum(k, axis=0)
        if pg % PAGES_PER_BLOCK == 0:
            ks = ksum
        else:
            ks = ks + ksum
        if pg % PAGES_PER_BLOCK == PAGES_PER_BLOCK - 1:
            ks_ref[pg // PAGES_PER_BLOCK] = ks
        kb_ref[pg * rows:(pg + 1) * rows, :] = k.reshape(rows, HEAD_DIM).astype(BF16)
        vb_ref[pg * rows:(pg + 1) * rows, :] = v_refs[pg][...].reshape(rows, HEAD_DIM).astype(BF16)
    s = _dot_nt(q_ref[...].astype(BF16), kb_ref[...]) * scale
    c_i = lax.broadcasted_iota(jnp.int32, s.shape, 0)
    l_i = lax.broadcasted_iota(jnp.int32, s.shape, 1)
    s = jnp.where((l_i & (C_HEADS - 1)) == (c_i & (C_HEADS - 1)), s, -jnp.inf)
    for blk in range(PAGED_BLOCKS_PER_STEP):
        cols = slice(blk * blk_keys, (blk + 1) * blk_keys)
        sb = s[:, cols]
        m = jnp.max(sb, axis=-1, keepdims=True)
        p = jnp.exp(sb - m)
        m_ref[blk] = jnp.broadcast_to(m, (N_COMBO, HEAD_DIM))
        l_ref[blk] = jnp.broadcast_to(jnp.sum(p, axis=-1, keepdims=True), (N_COMBO, HEAD_DIM))
        o_ref[blk] = _dot(p.astype(BF16), vb_ref[cols, :])


def paged_partials(page_table, q_combo, cache_k, cache_v, layer):
    page_block = (None, None, PAGE_SIZE, C_HEADS, HEAD_DIM)
    nbs = PAGED_BLOCKS_PER_STEP

    def page_map(which):
        return lambda b, n, pt: (layer, pt[b, PAGED_PAGES_PER_STEP * n + which], 0, 0, 0)

    page_specs = [pl.BlockSpec(page_block, page_map(w)) for w in range(PAGED_PAGES_PER_STEP)]
    part_shape = jax.ShapeDtypeStruct((DEC_BATCH, N_PAST_BLOCKS, N_COMBO, HEAD_DIM), F32)
    part_spec = pl.BlockSpec((None, nbs, N_COMBO, HEAD_DIM), lambda b, n, pt: (b, n, 0, 0))
    return pl.pallas_call(
        _paged_partial_kernel,
        grid_spec=pltpu.PrefetchScalarGridSpec(
            num_scalar_prefetch=1,
            grid=(DEC_BATCH, N_PAST_BLOCKS // nbs),
            in_specs=[pl.BlockSpec((None, N_COMBO, HEAD_DIM), lambda b, n, pt: (b, 0, 0))] + page_specs + page_specs,
            out_specs=[part_spec, part_spec, part_spec,
                       pl.BlockSpec((None, nbs, C_HEADS, HEAD_DIM), lambda b, n, pt: (b, n, 0, 0))],
            scratch_shapes=[pltpu.VMEM((PAGED_PAGES_PER_STEP * PAGE_SIZE * C_HEADS, HEAD_DIM), BF16)] * 2),
        out_shape=[part_shape, part_shape, part_shape,
                   jax.ShapeDtypeStruct((DEC_BATCH, N_PAST_BLOCKS, C_HEADS, HEAD_DIM), F32)],
        compiler_params=_cparams(2, N_SAMPLE),
        name="paged_partials",
    )(page_table, q_combo, *([cache_k] * PAGED_PAGES_PER_STEP), *([cache_v] * PAGED_PAGES_PER_STEP))


def _paged_merge_kernel(q_ref, kn_ref, vn_ref, m_ref, l_ref, o_ref, ks_ref, out_ref):
    nb = N_PAST_BLOCKS
    scale = HEAD_DIM ** -0.5
    q = q_ref[...]
    kmean = ks_ref[...] * (1.0 / MOBA_BLOCK)
    kmean = jnp.concatenate([kmean] * DEC_SEQ, axis=1)
    bs = jnp.sum(kmean * q[None], axis=-1, keepdims=True)
    work = jnp.broadcast_to(bs, (nb, N_COMBO, HEAD_DIM))
    n_i = lax.broadcasted_iota(jnp.int32, (nb, N_COMBO, HEAD_DIM), 0)
    sel = n_i < 0
    for _ in range(MOBA_TOPK):
        mx = jnp.max(work, axis=0, keepdims=True)
        first = jnp.min(jnp.where(work == mx, n_i, nb), axis=0, keepdims=True)
        hit = n_i == first
        sel = sel | hit
        work = jnp.where(hit, -jnp.inf, work)

    s = _dot_nt(q.astype(BF16), kn_ref[...].astype(BF16)) * scale
    c_i = lax.broadcasted_iota(jnp.int32, s.shape, 0)
    l_i = lax.broadcasted_iota(jnp.int32, s.shape, 1)
    ok = ((l_i & (C_HEADS - 1)) == (c_i & (C_HEADS - 1))) & ((l_i >> 3) <= (c_i >> 3))
    s = jnp.where(ok, s, -jnp.inf)
    m_own = jnp.max(s, axis=-1, keepdims=True)
    p = jnp.exp(s - m_own)
    l_own = jnp.sum(p, axis=-1, keepdims=True)
    o_own = _dot(p.astype(BF16), vn_ref[...].astype(BF16))

    mp = m_ref[...]
    m_all = jnp.maximum(jnp.max(jnp.where(sel, mp, -jnp.inf), axis=0), m_own)
    w = jnp.where(sel, jnp.exp(mp - m_all[None]), 0.0)
    w_own = jnp.exp(m_own - m_all)
    den = jnp.sum(w * l_ref[...], axis=0) + w_own * l_own
    num = jnp.sum(w * o_ref[...], axis=0) + w_own * o_own
    out_ref[...] = num / den


def paged_merge(q_combo, k_new, v_new, m_p, l_p, o_p, ksum):
    combo = pl.BlockSpec((None, N_COMBO, HEAD_DIM), lambda b: (b, 0, 0))
    part = pl.BlockSpec((None, N_PAST_BLOCKS, N_COMBO, HEAD_DIM), lambda b: (b, 0, 0, 0))
    return pl.pallas_call(
        _paged_merge_kernel,
        grid=(DEC_BATCH,),
        in_specs=[combo, combo, combo, part, part, part,
                  pl.BlockSpec((None, N_PAST_BLOCKS, C_HEADS, HEAD_DIM), lambda b: (b, 0, 0, 0))],
        out_specs=combo,
        out_shape=jax.ShapeDtypeStruct((DEC_BATCH, N_COMBO, HEAD_DIM), F32),
        compiler_params=_cparams(1, N_SAMPLE),
        name="paged_merge",
    )(q_combo, k_new, v_new, m_p, l_p, o_p, ksum)


def _memattn_kernel(x_ref, mk_ref, mv_ref, g_ref, b_ref, *refs, nb_tile, rows_per_b, precise):
    n_w = 2 if precise else 1
    wq = _load_all(refs[:n_w])
    wo = _load_all(refs[n_w:2 * n_w])
    o_ref = refs[2 * n_w]
    scale = MEM_HEAD_DIM ** -0.5
    x = x_ref[...]
    tm = x.shape[0]
    q = _mm(x, wq, precise)
    row_b = lax.broadcasted_iota(jnp.int32, (tm, 1), 0) >> int(math.log2(rows_per_b))
    heads = []
    for hh in range(MEM_HEADS):
        sl = slice(hh * MEM_HEAD_DIM, (hh + 1) * MEM_HEAD_DIM)
        qh = q[:, sl]
        oh = None
        for bb in range(nb_tile):
            s = _mm(qh, mk_ref[bb, :, sl], precise, nt=True) * scale
            s = s - jnp.max(s, axis=-1, keepdims=True)
            p = jnp.exp(s)
            p = p / jnp.sum(p, axis=-1, keepdims=True)
            ob = _mm(p, mv_ref[bb, :, sl], precise)
            if nb_tile > 1:
                ob = jnp.where(row_b == bb, ob, 0.0)
            oh = ob if oh is None else oh + ob
        heads.append(oh)
    o = jnp.concatenate(heads, axis=-1)
    y = ALPHA * x + _mm(o, wo, precise)
    o_ref[...] = _ln(y, g_ref[...], b_ref[...])


def memory_attention_ln(x, wq, mk, mv, wo, g, b, *, m, tm, rows_per_b, name, precise=False, b0=0):
    if rows_per_b >= tm:
        nb_tile = 1
        tiles_per_b = rows_per_b // tm
        kv_map = lambda i: (b0 + i // tiles_per_b, 0, 0)
    else:
        nb_tile = tm // rows_per_b
        kv_map = lambda i: (i, 0, 0)
    const2 = lambda i: (0, 0)
    return pl.pallas_call(
        functools.partial(_memattn_kernel, nb_tile=nb_tile, rows_per_b=rows_per_b, precise=precise),
        grid=(m // tm,),
        in_specs=[pl.BlockSpec((tm, D_MODEL), lambda i: (i, 0)),
                  pl.BlockSpec((nb_tile, N_MEM, MEM_WIDTH), kv_map),
                  pl.BlockSpec((nb_tile, N_MEM, MEM_WIDTH), kv_map),
                  pl.BlockSpec((1, D_MODEL), const2),
                  pl.BlockSpec((1, D_MODEL), const2)]
        + [pl.BlockSpec((D_MODEL, MEM_WIDTH), const2)] * len(wq)
        + [pl.BlockSpec((MEM_WIDTH, D_MODEL), const2)] * len(wo),
        out_specs=pl.BlockSpec((tm, D_MODEL), lambda i: (i, 0)),
        out_shape=jax.ShapeDtypeStruct((m, D_MODEL), F32),
        compiler_params=_cparams(1, m),
        name=name,
    )(x, mk, mv, g, b, *wq, *wo)


def _router_kernel(x_ref, xt_ref, w_ref, b_ref, o_ref, cnt_ref, run_ref, *, tiles_per_b):
    step = pl.program_id(0)
    tm = x_ref.shape[0]

    @pl.when(step == 0)
    def _():
        run_ref[...] = jnp.zeros(run_ref.shape, F32)

    x = x_ref[...]
    if tiles_per_b:
        is_last = lax.rem(step, tiles_per_b) == tiles_per_b - 1
        x = jnp.concatenate([x[:tm - CHUNK], jnp.where(is_last, xt_ref[...], x[tm - CHUNK:])], axis=0)
    xh, xl = _split_bf16(x)
    wh, wl = _split_bf16(w_ref[...])
    logits = _dot_nt(wh, xh) + (_dot_nt(wh, xl) + _dot_nt(wl, xh)) + b_ref[...]
    logits = logits - jnp.max(logits, axis=0, keepdims=True)
    e = jnp.exp(logits)
    probs = e / jnp.sum(e, axis=0, keepdims=True)
    p = [probs[j:j + 1, :] for j in range(N_EXPERTS)]
    gbest = None
    gsel = None
    for g in range(N_EXPERT_GROUPS):
        a, b_, c, d = p[4 * g:4 * g + 4]
        hi1, lo1 = jnp.maximum(a, b_), jnp.minimum(a, b_)
        hi2, lo2 = jnp.maximum(c, d), jnp.minimum(c, d)
        gs = jnp.maximum(hi1, hi2) + jnp.maximum(jnp.minimum(hi1, hi2), jnp.maximum(lo1, lo2))
        if g == 0:
            gbest, gsel = gs, jnp.zeros(gs.shape, jnp.int32)
        else:
            better = gs > gbest
            gbest = jnp.where(better, gs, gbest)
            gsel = jnp.where(better, g, gsel)
    cand = [jnp.where(gsel == j // EXPERTS_PER_GROUP, p[j], -1.0) for j in range(N_EXPERTS)]
    v1 = cand[0]
    i1 = jnp.zeros(v1.shape, jnp.int32)
    for j in range(1, N_EXPERTS):
        better = cand[j] > v1
        v1 = jnp.where(better, cand[j], v1)
        i1 = jnp.where(better, j, i1)
    v2 = jnp.full(v1.shape, -2.0, F32)
    i2 = jnp.zeros(v1.shape, jnp.int32)
    for j in range(N_EXPERTS):
        better = (cand[j] > v2) & (i1 != j)
        v2 = jnp.where(better, cand[j], v2)
        i2 = jnp.where(better, j, i2)
    tot = v1 + v2

    e_i = lax.broadcasted_iota(jnp.int32, (N_EXPERTS, tm), 0)
    pick1 = e_i == i1
    pick2 = e_i == i2
    onehot = (pick1 | pick2).astype(F32)
    t_r = lax.broadcasted_iota(jnp.int32, (tm, tm), 0)
    t_c = lax.broadcasted_iota(jnp.int32, (tm, tm), 1)
    earlier = (t_r < t_c).astype(BF16)
    before = _dot(onehot.astype(BF16), earlier) + run_ref[:, 0:1]
    rank1 = jnp.sum(jnp.where(pick1, before, 0.0), axis=0, keepdims=True)
    rank2 = jnp.sum(jnp.where(pick2, before, 0.0), axis=0, keepdims=True)
    run = run_ref[...] + jnp.sum(onehot, axis=1, keepdims=True)
    run_ref[...] = run
    cnt_ref[...] = run

    row = lax.broadcasted_iota(jnp.int32, o_ref.shape, 0)
    vals = (i1.astype(F32), i2.astype(F32), v1 / tot, v2 / tot, rank1, rank2)
    out = jnp.zeros(o_ref.shape, F32)
    for r, v in enumerate(vals):
        out = jnp.where(row == r, v, out)
    o_ref[...] = out


def router(x, x_tail, w_t, b, *, tm, name):
    m = x.shape[0]
    tiles_per_b = 0 if x_tail is None else SEQ // tm
    if x_tail is None:
        x_tail, tail_spec = x, pl.BlockSpec((min(CHUNK, tm), D_MODEL), lambda i: (0, 0))
    else:
        tail_spec = pl.BlockSpec((CHUNK, D_MODEL), lambda i: (i // tiles_per_b, 0))
    return pl.pallas_call(
        functools.partial(_router_kernel, tiles_per_b=tiles_per_b),
        grid=(m // tm,),
        in_specs=[pl.BlockSpec((tm, D_MODEL), lambda i: (i, 0)),
                  tail_spec,
                  pl.BlockSpec((N_EXPERTS, D_MODEL), lambda i: (0, 0)),
                  pl.BlockSpec((N_EXPERTS, 1), lambda i: (0, 0))],
        out_specs=[pl.BlockSpec((8, tm), lambda i: (0, i)),
                   pl.BlockSpec((N_EXPERTS, 128), lambda i: (0, 0))],
        out_shape=[jax.ShapeDtypeStruct((8, m), F32),
                   jax.ShapeDtypeStruct((N_EXPERTS, 128), F32)],
        scratch_shapes=[pltpu.VMEM((N_EXPERTS, 128), F32)],
        compiler_params=_cparams(1, m),
        name=name,
    )(x, x_tail, w_t, b)


def _dispatch_kernel(d0_ref, d1_ref, starts_ref, ends_ref, nv_ref, x_ref, o_hbm, zero_ref, sem, *, cfg):
    i = pl.program_id(0)
    tm = cfg.tok_tile

    def fill_copy(row0):
        return pltpu.make_async_copy(zero_ref, o_hbm.at[pl.ds(row0, cfg.tile)], sem)

    @pl.when(i == 0)
    def _():
        zero_ref[...] = jnp.zeros(zero_ref.shape, zero_ref.dtype)
        for wait in (False, True):
            for e in range(N_EXPERTS):
                @pl.when(ends_ref[e] > starts_ref[e])
                def _():
                    cp = fill_copy(pl.multiple_of(ends_ref[e] - cfg.tile, cfg.tile))
                    cp.wait() if wait else cp.start()

            def tail(t, carry):
                cp = fill_copy(pl.multiple_of(t * cfg.tile, cfg.tile))
                cp.wait() if wait else cp.start()
                return carry

            lax.fori_loop(nv_ref[0], cfg.n_tiles, tail, 0)

    base = i * tm

    def body(it, carry):
        for j in range(ROW_DMA_UNROLL):
            r = it * ROW_DMA_UNROLL + j
            pltpu.make_async_copy(x_ref.at[pl.ds(r, 1)], o_hbm.at[pl.ds(d0_ref[base + r], 1)], sem).start(priority=0)
            pltpu.make_async_copy(x_ref.at[pl.ds(r, 1)], o_hbm.at[pl.ds(d1_ref[base + r], 1)], sem).start(priority=1)
        return carry

    lax.fori_loop(0, tm // ROW_DMA_UNROLL, body, 0)
    for _ in range(MOE_TOPK):
        pltpu.make_async_copy(x_ref, o_hbm.at[pl.ds(0, tm)], sem).wait()


def moe_dispatch(d0, d1, starts, ends, n_valid, x, cfg, name):
    tm = cfg.tok_tile
    return pl.pallas_call(
        functools.partial(_dispatch_kernel, cfg=cfg),
        grid_spec=pltpu.PrefetchScalarGridSpec(
            num_scalar_prefetch=5,
            grid=(cfg.n_tok // tm,),
            in_specs=[pl.BlockSpec((tm, D_MODEL), lambda i, *_: (i, 0))],
            out_specs=pl.BlockSpec(memory_space=pl.ANY),
            scratch_shapes=[pltpu.VMEM((cfg.tile, D_MODEL), F32), pltpu.SemaphoreType.DMA]),
        out_shape=jax.ShapeDtypeStruct((cfg.rows, D_MODEL), F32),
        compiler_params=_cparams(1, cfg.n_tok),
        name=name,
    )(d0, d1, starts, ends, n_valid, x)


def _expert_weights(first, w_refs, scratch_refs, precise):
    @pl.when(first)
    def _():
        for k, w_ref in enumerate(w_refs):
            if precise:
                hi, lo = _split_bf16(w_ref[...])
                scratch_refs[2 * k][...] = hi
                scratch_refs[2 * k + 1][...] = lo
            else:
                scratch_refs[k][...] = w_ref[...].astype(BF16)


def _expert_dot(x, scratch_refs, k, precise):
    if not precise:
        return _dot(x.astype(BF16), scratch_refs[k][...])
    xh, xl = _split_bf16(x)
    wh, wl = scratch_refs[2 * k][...], scratch_refs[2 * k + 1][...]
    return _dot(xh, wh) + (_dot(xl, wh) + _dot(xh, wl))


def _moe_up_kernel(te_ref, first_ref, nv_ref, x_ref, wg_ref, wu_ref, hh_ref, *scratch, precise, f_tiles):
    i = pl.program_id(0)

    @pl.when(i < nv_ref[0])
    def _():
        _expert_weights((first_ref[i] == 1) | (f_tiles > 1), (wg_ref, wu_ref), scratch, precise)
        x = x_ref[...]
        hg = _expert_dot(x, scratch, 0, precise)
        hu = _expert_dot(x, scratch, 1, precise)
        hh_ref[...] = (hg * jax.nn.sigmoid(hg) * hu).astype(hh_ref.dtype)

    @pl.when(i >= nv_ref[0])
    def _():
        hh_ref[...] = jnp.zeros(hh_ref.shape, hh_ref.dtype)


def _moe_down_kernel(te_ref, first_ref, nv_ref, hh_ref, wd_ref, y_ref, *scratch, precise):
    i = pl.program_id(0)

    @pl.when(i < nv_ref[0])
    def _():
        _expert_weights(first_ref[i] == 1, (wd_ref,), scratch, precise)
        y_ref[...] = _expert_dot(hh_ref[...], scratch, 0, precise)

    @pl.when(i >= nv_ref[0])
    def _():
        y_ref[...] = jnp.zeros(y_ref.shape, y_ref.dtype)


def _stream_expert_weights(i, te_ref, first_ref, nxt_ref, w_hbms, land_refs, bf16_refs, sem, layer):
    def copies(e):
        return [pltpu.make_async_copy(w.at[layer, e], land, sem.at[k])
                for k, (w, land) in enumerate(zip(w_hbms, land_refs))]

    @pl.when(i == 0)
    def _():
        for cp in copies(te_ref[0]):
            cp.start()

    @pl.when(first_ref[i] == 1)
    def _():
        for cp in copies(te_ref[i]):
            cp.wait()
        for land, dst in zip(land_refs, bf16_refs):
            dst[...] = land[...].astype(BF16)

        @pl.when(nxt_ref[i] >= 0)
        def _():
            for cp in copies(nxt_ref[i]):
                cp.start()


def _moe_up_stream_kernel(te_ref, first_ref, nxt_ref, nv_ref, x_ref, wg_hbm, wu_hbm, hh_ref,
                          land_g, land_u, wgb_ref, wub_ref, sem, *, layer):
    i = pl.program_id(0)

    @pl.when(i < nv_ref[0])
    def _():
        _stream_expert_weights(i, te_ref, first_ref, nxt_ref, (wg_hbm, wu_hbm), (land_g, land_u),
                               (wgb_ref, wub_ref), sem, layer)
        x = x_ref[...].astype(BF16)
        hg = _dot(x, wgb_ref[...])
        hu = _dot(x, wub_ref[...])
        hh_ref[...] = (hg * jax.nn.sigmoid(hg) * hu).astype(hh_ref.dtype)

    @pl.when(i >= nv_ref[0])
    def _():
        hh_ref[...] = jnp.zeros(hh_ref.shape, hh_ref.dtype)


def _moe_down_stream_kernel(te_ref, first_ref, nxt_ref, nv_ref, hh_ref, wd_hbm, y_ref, land_d, wdb_ref, sem,
                            *, layer):
    i = pl.program_id(0)

    @pl.when(i < nv_ref[0])
    def _():
        _stream_expert_weights(i, te_ref, first_ref, nxt_ref, (wd_hbm,), (land_d,), (wdb_ref,), sem, layer)
        y_ref[...] = _dot(hh_ref[...], wdb_ref[...])

    @pl.when(i >= nv_ref[0])
    def _():
        y_ref[...] = jnp.zeros(y_ref.shape, y_ref.dtype)


def moe_experts_streamed(tile_expert, tile_first, tile_next, n_valid, x_sorted, w_gate, w_up, w_down, layer, cfg,
                         name):
    tm = cfg.tile
    row_map = lambda i, te, fi, nx, nv: (jnp.minimum(i, nv[0] - 1), 0)
    out_map = lambda i, te, fi, nx, nv: (i, 0)
    hbm = pl.BlockSpec(memory_space=pl.ANY)
    hh = pl.pallas_call(
        functools.partial(_moe_up_stream_kernel, layer=layer),
        grid_spec=pltpu.PrefetchScalarGridSpec(
            num_scalar_prefetch=4,
            grid=(cfg.n_tiles,),
            in_specs=[pl.BlockSpec((tm, D_MODEL), row_map), hbm, hbm],
            out_specs=pl.BlockSpec((tm, D_EXPERT), out_map),
            scratch_shapes=[pltpu.VMEM((D_MODEL, D_EXPERT), F32), pltpu.VMEM((D_MODEL, D_EXPERT), F32),
                            pltpu.VMEM((D_MODEL, D_EXPERT), BF16), pltpu.VMEM((D_MODEL, D_EXPERT), BF16),
                            pltpu.SemaphoreType.DMA((2,))]),
        out_shape=jax.ShapeDtypeStruct((cfg.rows, D_EXPERT), BF16),
        compiler_params=_cparams(1, cfg.n_tok),
        name=name + "_up",
    )(tile_expert, tile_first, tile_next, n_valid, x_sorted, w_gate, w_up)
    return pl.pallas_call(
        functools.partial(_moe_down_stream_kernel, layer=layer),
        grid_spec=pltpu.PrefetchScalarGridSpec(
            num_scalar_prefetch=4,
            grid=(cfg.n_tiles,),
            in_specs=[pl.BlockSpec((tm, D_EXPERT), row_map), hbm],
            out_specs=pl.BlockSpec((tm, D_MODEL), out_map),
            scratch_shapes=[pltpu.VMEM((D_EXPERT, D_MODEL), F32), pltpu.VMEM((D_EXPERT, D_MODEL), BF16),
                            pltpu.SemaphoreType.DMA((1,))]),
        out_shape=jax.ShapeDtypeStruct((cfg.rows, D_MODEL), F32),
        compiler_params=_cparams(1, cfg.n_tok),
        name=name + "_down",
    )(tile_expert, tile_first, tile_next, n_valid, hh, w_down)


def moe_experts(tile_expert, tile_first, n_valid, x_sorted, w_gate, w_up, w_down, layer, cfg, name):
    tm = cfg.tile
    n_copies = 2 if cfg.precise else 1
    f_tiles = 2 if cfg.precise else 1
    fw = D_EXPERT // f_tiles
    row_map = lambda i, te, fi, nv: (jnp.minimum(i, nv[0] - 1), 0)
    wmap = lambda i, te, fi, nv: (layer, te[i], 0, 0)
    hh = pl.pallas_call(
        functools.partial(_moe_up_kernel, precise=cfg.precise, f_tiles=f_tiles),
        grid_spec=pltpu.PrefetchScalarGridSpec(
            num_scalar_prefetch=3,
            grid=(cfg.n_tiles, f_tiles),
            in_specs=[pl.BlockSpec((tm, D_MODEL), lambda i, f, te, fi, nv: (jnp.minimum(i, nv[0] - 1), 0)),
                      pl.BlockSpec((None, None, D_MODEL, fw), lambda i, f, te, fi, nv: (layer, te[i], 0, f)),
                      pl.BlockSpec((None, None, D_MODEL, fw), lambda i, f, te, fi, nv: (layer, te[i], 0, f))],
            out_specs=pl.BlockSpec((tm, fw), lambda i, f, te, fi, nv: (i, f)),
            scratch_shapes=[pltpu.VMEM((D_MODEL, fw), BF16)] * (2 * n_copies)),
        out_shape=jax.ShapeDtypeStruct((cfg.rows, D_EXPERT), F32 if cfg.precise else BF16),
        compiler_params=_cparams(2, cfg.n_tok),
        name=name + "_up",
    )(tile_expert, tile_first, n_valid, x_sorted, w_gate, w_up)
    return pl.pallas_call(
        functools.partial(_moe_down_kernel, precise=cfg.precise),
        grid_spec=pltpu.PrefetchScalarGridSpec(
            num_scalar_prefetch=3,
            grid=(cfg.n_tiles,),
            in_specs=[pl.BlockSpec((tm, D_EXPERT), row_map),
                      pl.BlockSpec((None, None, D_EXPERT, D_MODEL), wmap)],
            out_specs=pl.BlockSpec((tm, D_MODEL), lambda i, te, fi, nv: (i, 0)),
            scratch_shapes=[pltpu.VMEM((D_EXPERT, D_MODEL), BF16)] * n_copies),
        out_shape=jax.ShapeDtypeStruct((cfg.rows, D_MODEL), F32),
        compiler_params=_cparams(1, cfg.n_tok),
        name=name + "_down",
    )(tile_expert, tile_first, n_valid, hh, w_down)


def _combine_kernel(d0_ref, d1_ref, x_ref, w_ref, g_ref, b_ref, y_hbm, o_ref, buf_ref, sem, *, tm):
    i = pl.program_id(0)
    n_steps = pl.num_programs(0)

    def gather(tile, half, start):
        base = tile * tm

        def body(it, carry):
            for j in range(ROW_DMA_UNROLL):
                r = it * ROW_DMA_UNROLL + j
                pltpu.make_async_copy(y_hbm.at[pl.ds(d0_ref[base + r], 1)], buf_ref.at[half, 0, pl.ds(r, 1)],
                                      sem.at[half]).start(priority=0)
                pltpu.make_async_copy(y_hbm.at[pl.ds(d1_ref[base + r], 1)], buf_ref.at[half, 1, pl.ds(r, 1)],
                                      sem.at[half]).start(priority=1)
            return carry

        if start:
            lax.fori_loop(0, tm // ROW_DMA_UNROLL, body, 0)
        else:
            for k in range(MOE_TOPK):
                pltpu.make_async_copy(y_hbm.at[pl.ds(0, tm)], buf_ref.at[half, k], sem.at[half]).wait()

    @pl.when(i == 0)
    def _():
        gather(0, 0, True)

    @pl.when(i + 1 < n_steps)
    def _():
        gather(i + 1, (i + 1) % 2, True)

    half = i % 2
    gather(i, half, False)
    w = w_ref[...]
    y = ALPHA * x_ref[...] + w[:, 0:1] * buf_ref[half, 0] + w[:, 1:2] * buf_ref[half, 1]
    o_ref[...] = _ln(y, g_ref[...], b_ref[...])


def moe_combine_ln(d0, d1, x, w_tok, g, b, y_sorted, *, tm, name):
    m = x.shape[0]
    return pl.pallas_call(
        functools.partial(_combine_kernel, tm=tm),
        grid_spec=pltpu.PrefetchScalarGridSpec(
            num_scalar_prefetch=2,
            grid=(m // tm,),
            in_specs=[pl.BlockSpec((tm, D_MODEL), lambda i, a, c: (i, 0)),
                      pl.BlockSpec((tm, MOE_TOPK), lambda i, a, c: (i, 0)),
                      pl.BlockSpec((1, D_MODEL), lambda i, a, c: (0, 0)),
                      pl.BlockSpec((1, D_MODEL), lambda i, a, c: (0, 0)),
                      pl.BlockSpec(memory_space=pl.ANY)],
            out_specs=pl.BlockSpec((tm, D_MODEL), lambda i, a, c: (i, 0)),
            scratch_shapes=[pltpu.VMEM((2, MOE_TOPK, tm, D_MODEL), F32), pltpu.SemaphoreType.DMA((2,))]),
        out_shape=jax.ShapeDtypeStruct((m, D_MODEL), F32),
        compiler_params=_cparams(1, m),
        name=name,
    )(d0, d1, x, w_tok, g, b, y_sorted)


def _routing_plan(route, counts, cfg):
    counts = counts[:, 0].astype(jnp.int32)
    padded = ((counts + cfg.tile - 1) // cfg.tile) * cfg.tile
    ends = jnp.cumsum(padded)
    starts = ends - padded
    experts = jnp.arange(N_EXPERTS, dtype=jnp.int32)[:, None]

    def slot_rows(k):
        picked = route[k].astype(jnp.int32)[None, :] == experts
        return jnp.sum(jnp.where(picked, starts[:, None], 0), axis=0) + route[4 + k].astype(jnp.int32)

    tile_start = jnp.arange(cfg.n_tiles, dtype=jnp.int32) * cfg.tile
    tile_expert = jnp.minimum(jnp.sum((tile_start[:, None] >= ends[None, :]).astype(jnp.int32), axis=1),
                              N_EXPERTS - 1)
    start_of_tile_expert = jnp.sum(jnp.where(tile_expert[:, None] == experts.T, starts[None, :], 0), axis=1)
    tile_first = (tile_start == start_of_tile_expert).astype(jnp.int32)
    n_valid = (ends[-1] // cfg.tile).astype(jnp.int32).reshape(1)
    later_nonempty = (experts.T > tile_expert[:, None]) & (padded[None, :] > 0)
    tile_next = jnp.min(jnp.where(later_nonempty, experts.T, N_EXPERTS), axis=1)
    tile_next = jnp.where(tile_next == N_EXPERTS, -1, tile_next).astype(jnp.int32)
    return (slot_rows(0), slot_rows(1), route[2:4].T, starts, ends, tile_expert, tile_first, tile_next, n_valid)


def moe_ln(x, x_route, layer, p, cfg, name):
    route, cnt = router(x, x_route, p["router_w_t"], p["router_b"], tm=cfg.tok_tile, name=name + "_router")
    d0, d1, w_tok, starts, ends, tile_expert, tile_first, tile_next, n_valid = _routing_plan(route, cnt, cfg)
    x_sorted = moe_dispatch(d0, d1, starts, ends, n_valid, x, cfg, name + "_dispatch")
    weights = (p["moe_w_gate"], p["moe_w_up"], p["moe_w_down"])
    if cfg.precise:
        y_sorted = moe_experts(tile_expert, tile_first, n_valid, x_sorted, *weights, layer, cfg, name)
    else:
        y_sorted = moe_experts_streamed(tile_expert, tile_first, tile_next, n_valid, x_sorted, *weights, layer,
                                        cfg, name)
    return moe_combine_ln(d0, d1, x, w_tok, p["moe_ln_g"][layer], p["moe_ln_b"][layer], y_sorted,
                          tm=cfg.tok_tile, name=name + "_combine")


def _rope_tables(pos):
    half = ROT_DIM // 2
    inv = jnp.power(ROPE_THETA, -jnp.arange(half, dtype=F32) / half)
    ang = pos.astype(F32)[:, None] * inv[None, :]
    cos, sin = jnp.cos(ang), jnp.sin(ang)
    n = pos.shape[0]
    ones = jnp.ones((n, HEAD_DIM - ROT_DIM), F32)
    zeros_h = jnp.zeros((n, half), F32)
    zeros_r = jnp.zeros((n, HEAD_DIM - ROT_DIM), F32)
    c = jnp.concatenate([cos, cos, ones], axis=1)
    sa = jnp.concatenate([zeros_h, sin, zeros_r], axis=1)
    sb = jnp.concatenate([-sin, zeros_h, zeros_r], axis=1)
    return c, sa, sb


def kernel(x_prompt, x_sample, mem_prompt, cache_c_k, cache_c_v, page_table, state_b_buf, state_d_buf, cache_mem_k, cache_mem_v, ab_w_in, ab_a_ln_g, ab_a_ln_b, ab_a_ws, ab_a_bs, ab_b_wg, ab_b_scale, ab_w_out, cd_w_in, cd_d_conv_w, cd_d_conv_b, cd_d_ln_g, cd_d_ln_b, cd_w_out, mix_ln_g, mix_ln_b, mem_w_q, mem_w_k, mem_w_v, mem_w_o, mem_ln_g, mem_ln_b, router_w, router_b, moe_w_gate, moe_w_up, moe_w_down, moe_ln_g, moe_ln_b):
    row = lambda v: v.reshape(1, -1)
    xp = x_prompt.reshape(N_PROMPT, D_MODEL)
    xs = x_sample.reshape(N_SAMPLE, D_MODEL)
    moe_p = {"router_w_t": router_w.T, "router_b": router_b.reshape(N_EXPERTS, 1),
             "moe_w_gate": moe_w_gate, "moe_w_up": moe_w_up,
             "moe_w_down": moe_w_down, "moe_ln_g": [row(moe_ln_g[l]) for l in range(DEPTH)],
             "moe_ln_b": [row(moe_ln_b[l]) for l in range(DEPTH)]}

    w_kv = split_weight(jnp.concatenate([mem_w_k[0], mem_w_k[1], mem_w_v[0], mem_w_v[1]], axis=1), True)
    n_memrows = BATCH * N_MEM
    kv = matmul(mem_prompt.reshape(n_memrows, D_MODEL), w_kv, m=n_memrows, tm=256, tn=1024, precise=True,
                name="mem_kv")
    pmk = [kv[:, l * MEM_WIDTH:(l + 1) * MEM_WIDTH].reshape(BATCH, N_MEM, MEM_WIDTH) for l in range(DEPTH)]
    pmv = [kv[:, (DEPTH + l) * MEM_WIDTH:(DEPTH + l + 1) * MEM_WIDTH].reshape(BATCH, N_MEM, MEM_WIDTH)
           for l in range(DEPTH)]
    smk = cache_mem_k.reshape(DEPTH, DEC_BATCH, N_MEM, MEM_WIDTH)
    smv = cache_mem_v.reshape(DEPTH, DEC_BATCH, N_MEM, MEM_WIDTH)

    mem_w = [(split_weight(mem_w_q[l], True), split_weight(mem_w_o[l], True)) for l in range(DEPTH)]
    mem_ln = [(row(mem_ln_g[l]), row(mem_ln_b[l])) for l in range(DEPTH)]

    def memattn_prompt(x, layer):
        wq, wo = split_weight(mem_w_q[layer], False), split_weight(mem_w_o[layer], False)
        return memory_attention_ln(x, wq, pmk[layer], pmv[layer], wo, *mem_ln[layer], m=N_PROMPT, tm=512,
                                   rows_per_b=SEQ, name="memattn_prompt")

    def memattn_sample(x, layer):
        wq, wo = mem_w[layer]
        return memory_attention_ln(x, wq, smk[layer], smv[layer], wo, *mem_ln[layer], m=N_SAMPLE, tm=N_SAMPLE,
                                   rows_per_b=DEC_SEQ, precise=True, name="memattn_sample")

    w_in0 = split_weight(ab_w_in[0], True)
    w_out0 = split_weight(ab_w_out[0], True)
    ab_prm = {"ln_g": row(ab_a_ln_g[0]), "ln_b": row(ab_a_ln_b[0]), "ws": ab_a_ws[0], "bs_t": ab_a_bs[0].T,
              "wg": ab_b_wg[0].astype(BF16), "scale": row(ab_b_scale[0])}
    ab_prm_precise = dict(ab_prm, wg=ab_b_wg[0])
    g, b = row(mix_ln_g[0]), row(mix_ln_b[0])
    n_chunks = SEQ // CHUNK
    n_tail = BATCH * CHUNK

    h0p = matmul(xp, split_weight(ab_w_in[0], False), m=N_PROMPT, tm=1024, tn=1024, n_gelu=2, name="ab_in_prompt")
    mix_p, _ = mixer_ab(h0p, h0p, lambda b, c: (jnp.maximum(b * (SEQ // 16) + c * (CHUNK // 16) - 1, 0), 2),
                        nb=BATCH, n_chunks=n_chunks, pos0=0, has_ctx=False, prm=ab_prm, name="mixer_ab_prompt")
    x1p = matmul_res_ln([mix_p], [split_weight(ab_w_out[0], False)], xp, g, b, m=N_PROMPT, tm=512,
                        name="ab_out_prompt")
    x1p = memattn_prompt(x1p, 0)

    xt = x_prompt[:, SEQ - CHUNK:].reshape(n_tail, D_MODEL)
    h0t = matmul(xt, w_in0, m=n_tail, tm=CHUNK, tn=1024, n_gelu=2, precise=True, name="ab_in_tail")
    mix_t, _ = mixer_ab(h0t, h0p, lambda b, c: ((b + 1) * (SEQ // 16) - CHUNK // 16 - 1, 2), nb=BATCH, n_chunks=1,
                        pos0=SEQ - CHUNK, has_ctx=True, prm=ab_prm_precise, precise=True, name="mixer_ab_tail")
    x1t = matmul_res_ln([mix_t], [w_out0], xt, g, b, m=n_tail, tm=CHUNK, precise=True, name="ab_out_tail")
    wq, wo = mem_w[0]
    x1t = memory_attention_ln(x1t, wq, pmk[0], pmv[0], wo, *mem_ln[0], m=n_tail, tm=CHUNK, rows_per_b=CHUNK,
                              precise=True, name="memattn_tail")

    h0s = matmul(xs, w_in0, m=N_SAMPLE, tm=N_SAMPLE, tn=1024, n_gelu=2, precise=True, name="ab_in_sample")
    h0s_pad = jnp.pad(h0s.reshape(DEC_BATCH, DEC_SEQ, -1), ((0, 0), (0, CHUNK - DEC_SEQ), (0, 0)))
    h0s_pad = h0s_pad.reshape(DEC_BATCH * CHUNK, -1)
    zctx = jnp.pad(state_b_buf[0], ((0, 0), (16 - B_BUF, 0), (0, 0))).reshape(DEC_BATCH * 16, B_WIDTH)
    mix_s, vn_s = mixer_ab(h0s_pad, zctx, lambda b, c: (b, 0), nb=DEC_BATCH, n_chunks=1, pos0=PAST_LEN,
                           has_ctx=True, prm=ab_prm_precise, precise=True, name="mixer_ab_sample")
    mix_s = mix_s.reshape(DEC_BATCH, CHUNK, -1)[:, :DEC_SEQ].reshape(N_SAMPLE, -1)
    x1s = matmul_res_ln([mix_s], [w_out0], xs, g, b, m=N_SAMPLE, tm=N_SAMPLE, precise=True, name="ab_out_sample")
    x1s = memattn_sample(x1s, 0)

    x2p = moe_ln(x1p, x1t, 0, moe_p, MOE_PROMPT, "moe_prompt")
    x2s = moe_ln(x1s, None, 0, moe_p, MOE_SAMPLE, "moe_sample")

    w_in1 = split_weight(cd_w_in[0], True)
    h1p = matmul(x2p, split_weight(cd_w_in[0], False), m=N_PROMPT, tm=1024, tn=1024, name="cd_in_prompt")
    h1s = matmul(x2s, w_in1, m=N_SAMPLE, tm=N_SAMPLE, tn=1024, precise=True, name="cd_in_sample")
    tabs_p = _rope_tables(jnp.arange(SEQ, dtype=jnp.int32))
    tabs_s = _rope_tables(PAST_LEN + (jnp.arange(N_SAMPLE, dtype=jnp.int32) % DEC_SEQ))
    qr_p, kr_p = rope_qk(h1p, tabs_p, m=N_PROMPT, tm=256, xoff=0, tab_blocks=SEQ // 256, name="rope_prompt")
    qr_s, kr_s = rope_qk(h1s, tabs_s, m=N_SAMPLE, tm=N_SAMPLE, xoff=0, tab_blocks=1, name="rope_sample")
    c_p = moba_prompt(qr_p, kr_p, h1p, nb=BATCH, seq=SEQ)
    v_s = h1s[:, 2 * C_WIDTH:3 * C_WIDTH]
    q_combo = qr_s.reshape(DEC_BATCH, N_COMBO, HEAD_DIM)
    m_p, l_p, o_p, ksum = paged_partials(page_table, q_combo, cache_c_k, cache_c_v, 0)
    c_s = paged_merge(q_combo, kr_s.reshape(DEC_BATCH, N_COMBO, HEAD_DIM),
                      v_s.reshape(DEC_BATCH, N_COMBO, HEAD_DIM), m_p, l_p, o_p, ksum)
    c_s = c_s.reshape(N_SAMPLE, C_WIDTH)

    cd_prm = {"conv_w": cd_d_conv_w[0], "conv_b": row(cd_d_conv_b[0]), "ln_g": row(cd_d_ln_g[0]),
              "ln_b": row(cd_d_ln_b[0])}
    col_a = 3 * C_WIDTH // D_WIDTH
    tiles_b = SEQ // 256
    halo_per_tile = 256 // CONV_HALO

    def prev_map(col):
        return lambda b, t: (jnp.maximum((b * tiles_b + t) * halo_per_tile - 1, 0), col)

    d_p, tail_p = conv_module(h1p, col_a, h1p, h1p, prev_map(col_a), prev_map(col_a + 1), nb=BATCH,
                              n_tiles=tiles_b, tm=256, rt=128, tail=CONV_HALO, xoff=0, prev_is_state=False,
                              prm=cd_prm, name="conv_prompt")
    gl_s = jnp.pad(h1s[:, 3 * C_WIDTH:].reshape(DEC_BATCH, DEC_SEQ, 2 * D_WIDTH), ((0, 0), (0, 8 - DEC_SEQ), (0, 0)))
    gl_s = gl_s.reshape(DEC_BATCH * 8, 2 * D_WIDTH)
    dctx = jnp.pad(state_d_buf[0], ((0, 0), (CONV_HALO - D_BUF, 0), (0, 0))).reshape(DEC_BATCH * CONV_HALO, D_WIDTH)
    d_s, tail_s = conv_module(gl_s, 0, dctx, dctx, lambda b, t: (b, 0), lambda b, t: (b, 0), nb=DEC_BATCH,
                              n_tiles=1, tm=8, rt=8, tail=8, xoff=0, prev_is_state=True, prm=cd_prm,
                              name="conv_sample", out_dtype=F32)
    d_s = d_s.reshape(DEC_BATCH, 8, D_WIDTH)[:, :DEC_SEQ].reshape(N_SAMPLE, D_WIDTH)
    w_out1c = split_weight(cd_w_out[0][:C_WIDTH], True)
    w_out1d = split_weight(cd_w_out[0][C_WIDTH:], True)
    g, b = row(mix_ln_g[1]), row(mix_ln_b[1])
    w_out1 = cd_w_out[0].astype(BF16)
    x3p = matmul_res_ln([c_p, d_p], [(w_out1[:C_WIDTH],), (w_out1[C_WIDTH:],)], x2p, g, b, m=N_PROMPT, tm=512,
                        name="cd_out_prompt")
    x3s = matmul_res_ln([c_s, d_s], [w_out1c, w_out1d], x2s, g, b, m=N_SAMPLE, tm=N_SAMPLE, precise=True,
                        name="cd_out_sample")
    y_p = moe_ln(memattn_prompt(x3p, 1), None, 1, moe_p, MOE_PROMPT, "moe_prompt")
    y_s = moe_ln(memattn_sample(x3s, 1), None, 1, moe_p, MOE_SAMPLE, "moe_sample")

    kv_shape_p = (1, BATCH, SEQ, C_HEADS, HEAD_DIM)
    kv_shape_s = (1, DEC_BATCH, DEC_SEQ, C_HEADS, HEAD_DIM)
    z_p = h0p[:, 2 * A_WIDTH:].reshape(BATCH, SEQ, B_WIDTH)
    z_s = h0s[:, 2 * A_WIDTH:].reshape(DEC_BATCH, DEC_SEQ, B_WIDTH)
    h_s = tail_s.reshape(DEC_BATCH, 8, D_WIDTH)[:, :DEC_SEQ]
    mem_shape = (BATCH, N_MEM, MEM_HEADS, MEM_HEAD_DIM)
    return (y_p.reshape(BATCH, SEQ, D_MODEL),
            y_s.reshape(DEC_BATCH, DEC_SEQ, D_MODEL),
            kr_p.reshape(kv_shape_p),
            h1p[:, 2 * C_WIDTH:3 * C_WIDTH].reshape(kv_shape_p),
            kr_s.reshape(kv_shape_s),
            v_s.reshape(kv_shape_s),
            z_p[:, SEQ - B_BUF:][None],
            jnp.concatenate([state_b_buf[0], z_s], axis=1)[:, DEC_SEQ:][None],
            vn_s.reshape(DEC_BATCH, CHUNK, A_WIDTH)[:, :DEC_SEQ][None],
            tail_p.reshape(BATCH, CONV_HALO, D_WIDTH)[:, CONV_HALO - D_BUF:][None],
            jnp.concatenate([state_d_buf[0], h_s], axis=1)[:, DEC_SEQ:][None],
            jnp.stack([m_.reshape(mem_shape) for m_ in pmk]),
            jnp.stack([m_.reshape(mem_shape) for m_ in pmv]))
```

```python
import functools
import math
from typing import NamedTuple

import jax
import jax.numpy as jnp
from jax import lax
from jax.experimental import pallas as pl
from jax.experimental.pallas import tpu as pltpu

F32 = jnp.float32
BF16 = jnp.bfloat16
HIGHEST = lax.Precision.HIGHEST

D_MODEL = 2048
BATCH = 4
SEQ = 2048
DEPTH = 2
DEC_BATCH = 8
DEC_SEQ = 4
PAST_LEN = 16384
PAGE_SIZE = 128
A_WIDTH = 1024
CHUNK = 128
A_GROUPS = 8
B_WIDTH = 1024
POOL_WINDOWS = (2, 4, 8, 16)
B_GROUP_DIM = B_WIDTH // len(POOL_WINDOWS)
B_BUF = 15
C_HEADS = 8
HEAD_DIM = 128
C_WIDTH = 1024
ROT_DIM = 32
ROPE_THETA = 500000.0
MOBA_BLOCK = 256
MOBA_TOPK = 3
D_WIDTH = 1024
CONV_WIDTH = 31
D_BUF = 30
N_MEM = 256
MEM_HEADS = 4
MEM_HEAD_DIM = 128
MEM_WIDTH = 512
N_EXPERTS = 16
N_EXPERT_GROUPS = 4
EXPERTS_PER_GROUP = 4
MOE_TOPK = 2
D_EXPERT = 1024
ALPHA = (2 * DEPTH) ** 0.25
LN_EPS = 1e-5

N_PROMPT = BATCH * SEQ
N_SAMPLE = DEC_BATCH * DEC_SEQ
N_TOK = N_PROMPT + N_SAMPLE
N_PAST_BLOCKS = PAST_LEN // MOBA_BLOCK
PAGES_PER_BLOCK = MOBA_BLOCK // PAGE_SIZE
N_COMBO = DEC_SEQ * C_HEADS


class MoeCfg(NamedTuple):
    n_tok: int
    tok_tile: int
    tile: int
    rows: int
    n_tiles: int
    precise: bool


def _moe_cfg(n_tok, tile, precise):
    rows = ((n_tok * MOE_TOPK + N_EXPERTS * (tile - 1)) // tile + 1) * tile
    return MoeCfg(n_tok, tile, tile, rows, rows // tile, precise)


MOE_PROMPT = _moe_cfg(N_PROMPT, 256, False)
MOE_SAMPLE = _moe_cfg(N_SAMPLE, N_SAMPLE, True)

ROW_DMA_UNROLL = 8
VMEM_LIMIT = 56 * 1024 * 1024
VMEM_LIMIT_SMALL = 40 * 1024 * 1024
SMALL_CALL_ROWS = 1024


def _cparams(n_axes, rows=None):
    vmem = VMEM_LIMIT if rows is None or rows > SMALL_CALL_ROWS else VMEM_LIMIT_SMALL
    return pltpu.CompilerParams(dimension_semantics=("arbitrary",) * n_axes, vmem_limit_bytes=vmem)


def _ln(x, g, b):
    mu = jnp.mean(x, axis=-1, keepdims=True)
    xc = x - mu
    var = jnp.mean(xc * xc, axis=-1, keepdims=True)
    return xc * lax.rsqrt(var + LN_EPS) * g + b


def _dot(a, b):
    return jnp.dot(a, b, preferred_element_type=F32)


def _dot_nt(a, b, precision=None):
    return lax.dot_general(a, b, (((1,), (1,)), ((), ())), precision=precision, preferred_element_type=F32)


def _split_bf16(v):
    hi = v.astype(BF16)
    return hi, (v.astype(F32) - hi.astype(F32)).astype(BF16)


def _mm(a, b, precise, nt=False):
    dot = _dot_nt if nt else _dot
    if not precise:
        b0 = b[0] if isinstance(b, tuple) else b
        return dot(a.astype(BF16), b0.astype(BF16))
    ah, al = _split_bf16(a)
    bh, bl = b if isinstance(b, tuple) else _split_bf16(b)
    return dot(ah, bh) + (dot(al, bh) + dot(ah, bl))


def _load_all(refs):
    return tuple(r[...] for r in refs)


def split_weight(w, precise):
    if not precise:
        return (w.astype(BF16),)
    bits = lax.bitcast_convert_type(w, jnp.uint32) & jnp.uint32(0xFFFF0000)
    hi = lax.bitcast_convert_type(bits, F32)
    return hi.astype(BF16), (w - hi).astype(BF16)


def _mm_kernel(x_ref, *refs, n_gelu, precise):
    o_ref = refs[-1]
    acc = _mm(x_ref[...], _load_all(refs[:-1]), precise)
    if n_gelu:
        j = pl.program_id(0)

        @pl.when(j < n_gelu)
        def _():
            o_ref[...] = jax.nn.gelu(acc).astype(o_ref.dtype)

        @pl.when(j >= n_gelu)
        def _():
            o_ref[...] = acc.astype(o_ref.dtype)
    else:
        o_ref[...] = acc.astype(o_ref.dtype)


def matmul(x, w, *, m, tm, tn, xoff=0, n_gelu=0, precise=False, name="mm"):
    k = x.shape[1]
    n = w[0].shape[1]
    return pl.pallas_call(
        functools.partial(_mm_kernel, n_gelu=n_gelu, precise=precise),
        grid=(n // tn, m // tm),
        in_specs=[pl.BlockSpec((tm, k), lambda j, i: (i + xoff, 0))]
        + [pl.BlockSpec((k, tn), lambda j, i: (0, j))] * len(w),
        out_specs=pl.BlockSpec((tm, tn), lambda j, i: (i, j)),
        out_shape=jax.ShapeDtypeStruct((m, n), F32),
        compiler_params=_cparams(2, m),
        name=name,
    )(x, *w)


def _rope(x, c, sa, sb):
    half = ROT_DIM // 2
    return x * c + pltpu.roll(x, half, axis=1) * sa + pltpu.roll(x, HEAD_DIM - half, axis=1) * sb


def _mm_rope_kernel(x_ref, w_ref, c_ref, sa_ref, sb_ref, o_ref, *, n_rope):
    j = pl.program_id(0)
    acc = _mm(x_ref[...], w_ref[...], False)

    @pl.when(j < n_rope)
    def _():
        c, sa, sb = c_ref[...], sa_ref[...], sb_ref[...]
        for h in range(acc.shape[1] // HEAD_DIM):
            sl = slice(h * HEAD_DIM, (h + 1) * HEAD_DIM)
            o_ref[:, sl] = _rope(acc[:, sl], c, sa, sb)

    @pl.when(j >= n_rope)
    def _():
        o_ref[...] = acc


def matmul_rope(x, w, tabs, *, m, tm, tn, n_rope, tab_blocks, name):
    k = x.shape[1]
    n = w.shape[1]
    tmap = lambda j, i: (i % tab_blocks, 0)
    return pl.pallas_call(
        functools.partial(_mm_rope_kernel, n_rope=n_rope),
        grid=(n // tn, m // tm),
        in_specs=[pl.BlockSpec((tm, k), lambda j, i: (i, 0)),
                  pl.BlockSpec((k, tn), lambda j, i: (0, j))] + [pl.BlockSpec((tm, HEAD_DIM), tmap)] * 3,
        out_specs=pl.BlockSpec((tm, tn), lambda j, i: (i, j)),
        out_shape=jax.ShapeDtypeStruct((m, n), F32),
        compiler_params=_cparams(2, m),
        name=name,
    )(x, w, *tabs)


def _mm_res_ln_kernel(*refs, n_in, precise):
    n_w = 2 if precise else 1
    a_refs = refs[:n_in]
    w_refs = refs[n_in:n_in + n_in * n_w]
    r_ref, g_ref, b_ref, o_ref = refs[n_in + n_in * n_w:]
    acc = None
    for k, a_ref in enumerate(a_refs):
        d = _mm(a_ref[...], _load_all(w_refs[k * n_w:(k + 1) * n_w]), precise)
        acc = d if acc is None else acc + d
    o_ref[...] = _ln(ALPHA * r_ref[...] + acc, g_ref[...], b_ref[...])


def matmul_res_ln(a_list, w_list, resid, g, b, *, m, tm, roff=0, precise=False, name="mm_res_ln"):
    n_in = len(a_list)
    w_flat = [part for w in w_list for part in w]
    in_specs = [pl.BlockSpec((tm, a.shape[1]), lambda i: (i, 0)) for a in a_list]
    in_specs += [pl.BlockSpec(w.shape, lambda i: (0, 0)) for w in w_flat]
    in_specs += [pl.BlockSpec((tm, D_MODEL), lambda i: (i + roff, 0)),
                 pl.BlockSpec((1, D_MODEL), lambda i: (0, 0)),
                 pl.BlockSpec((1, D_MODEL), lambda i: (0, 0))]
    return pl.pallas_call(
        functools.partial(_mm_res_ln_kernel, n_in=n_in, precise=precise),
        grid=(m // tm,),
        in_specs=in_specs,
        out_specs=pl.BlockSpec((tm, D_MODEL), lambda i: (i, 0)),
        out_shape=jax.ShapeDtypeStruct((m, D_MODEL), F32),
        compiler_params=_cparams(1, m),
        name=name,
    )(*a_list, *w_flat, resid, g, b)


def _mixer_ab_kernel(u_ref, v_ref, z_ref, zp_ref, lng_ref, lnb_ref, ws_ref, bs_ref, wg_ref, sc_ref,
                     mix_ref, vn_ref, *, pos0, has_ctx, precise):
    c = pl.program_id(1)
    vn = _ln(v_ref[...], lng_ref[...], lnb_ref[...])
    vn_ref[...] = vn
    u = u_ref[...]
    row = lax.broadcasted_iota(jnp.int32, (CHUNK, CHUNK), 0)
    col = lax.broadcasted_iota(jnp.int32, (CHUNK, CHUNK), 1)
    causal = col <= row
    gd = A_WIDTH // A_GROUPS
    for g in range(A_GROUPS):
        sl = slice(g * gd, (g + 1) * gd)
        w = jnp.where(causal, ws_ref[g], 0.0)
        mixed = _mm(w, vn[:, sl], precise) + bs_ref[:, g:g + 1]
        mix_ref[:, sl] = (u[:, sl] * mixed).astype(mix_ref.dtype)

    z = z_ref[...]
    zp = zp_ref[...]
    if not has_ctx:
        zp = jnp.where(c == 0, 0.0, zp)
    zext = jnp.concatenate([zp, z], axis=0)
    pos = pos0 + c * CHUNK + lax.broadcasted_iota(jnp.int32, (CHUNK, 1), 0)
    for gi, wdw in enumerate(POOL_WINDOWS):
        sl = slice(gi * B_GROUP_DIM, (gi + 1) * B_GROUP_DIM)
        s = zext[:, sl]
        sh = 1
        while sh < wdw:
            s = s + pltpu.roll(s, sh, axis=0)
            sh *= 2
        cnt = jnp.minimum(wdw, pos + 1).astype(F32)
        d = s[16:, :] / cnt - z[:, sl]
        bo = _mm(d, wg_ref[gi], precise) * sc_ref[:, sl]
        mix_ref[:, A_WIDTH + gi * B_GROUP_DIM:A_WIDTH + (gi + 1) * B_GROUP_DIM] = bo.astype(mix_ref.dtype)


def mixer_ab(h, zprev, zprev_map, *, nb, n_chunks, pos0, has_ctx, prm, name, precise=False):
    m = nb * n_chunks * CHUNK
    row = lambda b, c: b * n_chunks + c
    const2 = lambda b, c: (0, 0)
    return pl.pallas_call(
        functools.partial(_mixer_ab_kernel, pos0=pos0, has_ctx=has_ctx, precise=precise),
        grid=(nb, n_chunks),
        in_specs=[pl.BlockSpec((CHUNK, A_WIDTH), lambda b, c: (row(b, c), 0)),
                  pl.BlockSpec((CHUNK, A_WIDTH), lambda b, c: (row(b, c), 1)),
                  pl.BlockSpec((CHUNK, B_WIDTH), lambda b, c: (row(b, c), 2)),
                  pl.BlockSpec((16, B_WIDTH), zprev_map),
                  pl.BlockSpec((1, A_WIDTH), const2),
                  pl.BlockSpec((1, A_WIDTH), const2),
                  pl.BlockSpec((A_GROUPS, CHUNK, CHUNK), lambda b, c: (0, 0, 0)),
                  pl.BlockSpec((CHUNK, A_GROUPS), const2),
                  pl.BlockSpec((len(POOL_WINDOWS), B_GROUP_DIM, B_GROUP_DIM), lambda b, c: (0, 0, 0)),
                  pl.BlockSpec((1, B_WIDTH), const2)],
        out_specs=[pl.BlockSpec((CHUNK, A_WIDTH + B_WIDTH), lambda b, c: (row(b, c), 0)),
                   pl.BlockSpec((CHUNK, A_WIDTH), lambda b, c: (row(b, c), 0))],
        out_shape=[jax.ShapeDtypeStruct((m, A_WIDTH + B_WIDTH), F32 if precise else BF16),
                   jax.ShapeDtypeStruct((m, A_WIDTH), F32)],
        compiler_params=_cparams(2, m),
        name=name,
    )(h, h, h, zprev, prm["ln_g"], prm["ln_b"], prm["ws"], prm["bs_t"], prm["wg"], prm["scale"])


def _rope_kernel(q_ref, k_ref, c_ref, sa_ref, sb_ref, qo_ref, ko_ref):
    c = c_ref[...]
    sa = sa_ref[...]
    sb = sb_ref[...]
    for h in range(C_HEADS):
        sl = slice(h * HEAD_DIM, (h + 1) * HEAD_DIM)
        for src, dst in ((q_ref, qo_ref), (k_ref, ko_ref)):
            dst[:, sl] = _rope(src[:, sl], c, sa, sb)


def rope_qk(h, tabs, *, m, tm, xoff, tab_blocks, name):
    tmap = lambda i: (i % tab_blocks, 0)
    return pl.pallas_call(
        _rope_kernel,
        grid=(m // tm,),
        in_specs=[pl.BlockSpec((tm, C_WIDTH), lambda i: (i + xoff, 0)),
                  pl.BlockSpec((tm, C_WIDTH), lambda i: (i + xoff, 1)),
                  pl.BlockSpec((tm, HEAD_DIM), tmap),
                  pl.BlockSpec((tm, HEAD_DIM), tmap),
                  pl.BlockSpec((tm, HEAD_DIM), tmap)],
        out_specs=[pl.BlockSpec((tm, C_WIDTH), lambda i: (i, 0)),
                   pl.BlockSpec((tm, C_WIDTH), lambda i: (i, 0))],
        out_shape=[jax.ShapeDtypeStruct((m, C_WIDTH), F32),
                   jax.ShapeDtypeStruct((m, C_WIDTH), F32)],
        compiler_params=_cparams(1, m),
        name=name,
    )(h, h, *tabs)


def _moba_kernel(q_ref, qall_ref, k_ref, v_ref, o_ref, selt_ref, *, n_blocks):
    i = pl.program_id(2)
    blk_rows = MOBA_BLOCK
    seq = n_blocks * blk_rows
    scale = HEAD_DIM ** -0.5

    @pl.when(i == 0)
    def _():
        kmean = jnp.concatenate(
            [jnp.mean(k_ref[j * blk_rows:(j + 1) * blk_rows, :], axis=0, keepdims=True) for j in range(n_blocks)],
            axis=0)
        bst = _mm(kmean, qall_ref[...], True, nt=True)
        blk = lax.broadcasted_iota(jnp.int32, (n_blocks, seq), 0)
        own = lax.broadcasted_iota(jnp.int32, (n_blocks, seq), 1) >> int(math.log2(blk_rows))
        work = jnp.where(blk < own, bst, -jnp.inf)
        sel = jnp.zeros((n_blocks, seq), F32)
        for _ in range(MOBA_TOPK):
            mx = jnp.max(work, axis=0, keepdims=True)
            first = jnp.min(jnp.where(work == mx, blk, n_blocks), axis=0, keepdims=True)
            hit = (blk == first) & (mx > -jnp.inf)
            sel = jnp.where(hit, 1.0, sel)
            work = jnp.where(hit, -jnp.inf, work)
        selt_ref[...] = jnp.zeros(selt_ref.shape, F32)
        for c in range(n_blocks):
            selt_ref[c, 0:n_blocks, :] = sel[:, c * blk_rows:(c + 1) * blk_rows]

    qb = q_ref[...].astype(BF16)
    sel = jnp.transpose(selt_ref[i])

    r_i = lax.broadcasted_iota(jnp.int32, (blk_rows, blk_rows), 0)
    c_i = lax.broadcasted_iota(jnp.int32, (blk_rows, blk_rows), 1)
    causal = (c_i <= r_i).astype(F32)

    for c in range(n_blocks):
        @pl.when(i == c)
        def _():
            nk = (c + 1) * blk_rows
            s = _dot_nt(qb, k_ref[0:nk, :].astype(BF16)) * scale
            allowed = jnp.concatenate(
                [jnp.broadcast_to(sel[:, j:j + 1], (blk_rows, blk_rows)) for j in range(c)] + [causal], axis=1)
            s = jnp.where(allowed > 0.0, s, -jnp.inf)
            m = jnp.max(s, axis=-1, keepdims=True)
            p = jnp.exp(s - m)
            l = jnp.sum(p, axis=-1, keepdims=True)
            o = _dot(p.astype(BF16), v_ref[0:nk, :].astype(BF16))
            o_ref[...] = (o / l).astype(o_ref.dtype)


def moba_prompt(h, *, nb, seq, name="moba_prompt"):
    n_blocks = seq // MOBA_BLOCK
    k_col0 = C_WIDTH // HEAD_DIM
    v_col0 = 2 * C_WIDTH // HEAD_DIM
    return pl.pallas_call(
        functools.partial(_moba_kernel, n_blocks=n_blocks),
        grid=(nb, C_HEADS, n_blocks),
        in_specs=[pl.BlockSpec((MOBA_BLOCK, HEAD_DIM), lambda b, hh, i: (b * n_blocks + i, hh)),
                  pl.BlockSpec((seq, HEAD_DIM), lambda b, hh, i: (b, hh)),
                  pl.BlockSpec((seq, HEAD_DIM), lambda b, hh, i: (b, k_col0 + hh)),
                  pl.BlockSpec((seq, HEAD_DIM), lambda b, hh, i: (b, v_col0 + hh))],
        out_specs=pl.BlockSpec((MOBA_BLOCK, HEAD_DIM), lambda b, hh, i: (b * n_blocks + i, hh)),
        out_shape=jax.ShapeDtypeStruct((nb * seq, C_WIDTH), BF16),
        scratch_shapes=[pltpu.VMEM((n_blocks, 128, MOBA_BLOCK), F32)],
        compiler_params=_cparams(3),
        name=name,
    )(h, h, h, h)


CONV_HALO = 32
SUBLANES = 8


def _conv_kernel(ga_ref, gg_ref, pa_ref, pg_ref, w_ref, bdw_ref, lng_ref, lnb_ref,
                 o_ref, tail_ref, ext_ref, sh_ref, y_ref, *, tm, rt, tail, prev_is_state):
    t = pl.program_id(1)
    hcur = ga_ref[...] * jax.nn.sigmoid(gg_ref[...])
    if prev_is_state:
        hprev = pa_ref[...]
    else:
        hprev = pa_ref[...] * jax.nn.sigmoid(pg_ref[...])
        hprev = jnp.where(t == 0, 0.0, hprev)
    ext_ref[0:CONV_HALO, :] = hprev
    ext_ref[CONV_HALO:CONV_HALO + tm, :] = hcur
    tail_ref[...] = hcur[tm - tail:, :]
    off = CONV_HALO - D_BUF
    sh_rows = sh_ref.shape[1]
    for s in range(1, SUBLANES):
        sh_ref[s - 1] = ext_ref[s:s + sh_rows, :]
    for cc in range(D_WIDTH // 128):
        cs = slice(cc * 128, (cc + 1) * 128)
        for rc in range(tm // rt):
            r0 = rc * rt
            acc = jnp.zeros((rt, 128), F32)
            for j in range(CONV_WIDTH):
                s, a = (off + j) % SUBLANES, (off + j) // SUBLANES
                row0 = r0 + a * SUBLANES
                src = ext_ref[row0:row0 + rt, cs] if s == 0 else sh_ref[s - 1, row0:row0 + rt, cs]
                acc = acc + w_ref[j:j + 1, cs] * src
            y_ref[r0:r0 + rt, cs] = acc + bdw_ref[:, cs]
    y = _ln(y_ref[...], lng_ref[...], lnb_ref[...])
    o_ref[...] = (y * jax.nn.sigmoid(y)).astype(o_ref.dtype)


def conv_module(h, col_a, prev_a, prev_g, prev_map_a, prev_map_g, *, nb, n_tiles, tm, rt, tail, xoff,
                prev_is_state, prm, name, out_dtype=BF16):
    m = nb * n_tiles * tm
    row = lambda b, t: b * n_tiles + t + xoff
    const2 = lambda b, t: (0, 0)
    return pl.pallas_call(
        functools.partial(_conv_kernel, tm=tm, rt=rt, tail=tail, prev_is_state=prev_is_state),
        grid=(nb, n_tiles),
        in_specs=[pl.BlockSpec((tm, D_WIDTH), lambda b, t: (row(b, t), col_a)),
                  pl.BlockSpec((tm, D_WIDTH), lambda b, t: (row(b, t), col_a + 1)),
                  pl.BlockSpec((CONV_HALO, D_WIDTH), prev_map_a),
                  pl.BlockSpec((CONV_HALO, D_WIDTH), prev_map_g),
                  pl.BlockSpec((CONV_WIDTH, D_WIDTH), const2),
                  pl.BlockSpec((1, D_WIDTH), const2),
                  pl.BlockSpec((1, D_WIDTH), const2),
                  pl.BlockSpec((1, D_WIDTH), const2)],
        out_specs=[pl.BlockSpec((tm, D_WIDTH), lambda b, t: (b * n_tiles + t, 0)),
                   pl.BlockSpec((tail, D_WIDTH), lambda b, t: (b, 0))],
        out_shape=[jax.ShapeDtypeStruct((m, D_WIDTH), out_dtype),
                   jax.ShapeDtypeStruct((nb * tail, D_WIDTH), F32)],
        scratch_shapes=[pltpu.VMEM((CONV_HALO + tm, D_WIDTH), F32),
                        pltpu.VMEM((SUBLANES - 1, CONV_HALO + tm - SUBLANES, D_WIDTH), F32),
                        pltpu.VMEM((tm, D_WIDTH), F32)],
        compiler_params=_cparams(2, m),
        name=name,
    )(h, h, prev_a, prev_g, prm["conv_w"], prm["conv_b"], prm["ln_g"], prm["ln_b"])


PAGED_BLOCKS_PER_STEP = 4
PAGED_PAGES_PER_STEP = PAGED_BLOCKS_PER_STEP * PAGES_PER_BLOCK


def _paged_partial_kernel(pt_ref, q_ref, *refs):
    del pt_ref
    pages = refs[:2 * PAGED_PAGES_PER_STEP]
    m_ref, l_ref, o_ref, ks_ref, kb_ref, vb_ref = refs[2 * PAGED_PAGES_PER_STEP:]
    rows = PAGE_SIZE * C_HEADS
    blk_keys = PAGES_PER_BLOCK * rows
    scale = HEAD_DIM ** -0.5
    k_refs = pages[:PAGED_PAGES_PER_STEP]
    v_refs = pages[PAGED_PAGES_PER_STEP:]
    for pg in range(PAGED_PAGES_PER_STEP):
        k = k_refs[pg][...]
        ksum = jnp.sum(k, axis=0)
        if pg % PAGES_PER_BLOCK == 0:
            ks = ksum
        else:
            ks = ks + ksum
        if pg % PAGES_PER_BLOCK == PAGES_PER_BLOCK - 1:
            ks_ref[pg // PAGES_PER_BLOCK] = ks
        kb_ref[pg * rows:(pg + 1) * rows, :] = k.reshape(rows, HEAD_DIM).astype(BF16)
        vb_ref[pg * rows:(pg + 1) * rows, :] = v_refs[pg][...].reshape(rows, HEAD_DIM).astype(BF16)
    s = _dot_nt(q_ref[...].astype(BF16), kb_ref[...]) * scale
    c_i = lax.broadcasted_iota(jnp.int32, s.shape, 0)
    l_i = lax.broadcasted_iota(jnp.int32, s.shape, 1)
    s = jnp.where((l_i & (C_HEADS - 1)) == (c_i & (C_HEADS - 1)), s, -jnp.inf)
    for blk in range(PAGED_BLOCKS_PER_STEP):
        cols = slice(blk * blk_keys, (blk + 1) * blk_keys)
        sb = s[:, cols]
        m = jnp.max(sb, axis=-1, keepdims=True)
        p = jnp.exp(sb - m)
        m_ref[blk] = jnp.broadcast_to(m, (N_COMBO, HEAD_DIM))
        l_ref[blk] = jnp.broadcast_to(jnp.sum(p, axis=-1, keepdims=True), (N_COMBO, HEAD_DIM))
        o_ref[blk] = _dot(p.astype(BF16), vb_ref[cols, :])


def paged_partials(page_table, q_combo, cache_k, cache_v, layer):
    page_block = (None, None, PAGE_SIZE, C_HEADS, HEAD_DIM)
    nbs = PAGED_BLOCKS_PER_STEP

    def page_map(which):
        return lambda b, n, pt: (layer, pt[b, PAGED_PAGES_PER_STEP * n + which], 0, 0, 0)

    page_specs = [pl.BlockSpec(page_block, page_map(w)) for w in range(PAGED_PAGES_PER_STEP)]
    part_shape = jax.ShapeDtypeStruct((DEC_BATCH, N_PAST_BLOCKS, N_COMBO, HEAD_DIM), F32)
    part_spec = pl.BlockSpec((None, nbs, N_COMBO, HEAD_DIM), lambda b, n, pt: (b, n, 0, 0))
    return pl.pallas_call(
        _paged_partial_kernel,
        grid_spec=pltpu.PrefetchScalarGridSpec(
            num_scalar_prefetch=1,
            grid=(DEC_BATCH, N_PAST_BLOCKS // nbs),
            in_specs=[pl.BlockSpec((None, N_COMBO, HEAD_DIM), lambda b, n, pt: (b, 0, 0))] + page_specs + page_specs,
            out_specs=[part_spec, part_spec, part_spec,
                       pl.BlockSpec((None, nbs, C_HEADS, HEAD_DIM), lambda b, n, pt: (b, n, 0, 0))],
            scratch_shapes=[pltpu.VMEM((PAGED_PAGES_PER_STEP * PAGE_SIZE * C_HEADS, HEAD_DIM), BF16)] * 2),
        out_shape=[part_shape, part_shape, part_shape,
                   jax.ShapeDtypeStruct((DEC_BATCH, N_PAST_BLOCKS, C_HEADS, HEAD_DIM), F32)],
        compiler_params=_cparams(2, N_SAMPLE),
        name="paged_partials",
    )(page_table, q_combo, *([cache_k] * PAGED_PAGES_PER_STEP), *([cache_v] * PAGED_PAGES_PER_STEP))


def _paged_merge_kernel(q_ref, kn_ref, vn_ref, m_ref, l_ref, o_ref, ks_ref, out_ref):
    nb = N_PAST_BLOCKS
    scale = HEAD_DIM ** -0.5
    q = q_ref[...]
    kmean = ks_ref[...] * (1.0 / MOBA_BLOCK)
    kmean = jnp.concatenate([kmean] * DEC_SEQ, axis=1)
    bs = jnp.sum(kmean * q[None], axis=-1, keepdims=True)
    work = jnp.broadcast_to(bs, (nb, N_COMBO, HEAD_DIM))
    n_i = lax.broadcasted_iota(jnp.int32, (nb, N_COMBO, HEAD_DIM), 0)
    sel = n_i < 0
    for _ in range(MOBA_TOPK):
        mx = jnp.max(work, axis=0, keepdims=True)
        first = jnp.min(jnp.where(work == mx, n_i, nb), axis=0, keepdims=True)
        hit = n_i == first
        sel = sel | hit
        work = jnp.where(hit, -jnp.inf, work)

    s = _dot_nt(q.astype(BF16), kn_ref[...].astype(BF16)) * scale
    c_i = lax.broadcasted_iota(jnp.int32, s.shape, 0)
    l_i = lax.broadcasted_iota(jnp.int32, s.shape, 1)
    ok = ((l_i & (C_HEADS - 1)) == (c_i & (C_HEADS - 1))) & ((l_i >> 3) <= (c_i >> 3))
    s = jnp.where(ok, s, -jnp.inf)
    m_own = jnp.max(s, axis=-1, keepdims=True)
    p = jnp.exp(s - m_own)
    l_own = jnp.sum(p, axis=-1, keepdims=True)
    o_own = _dot(p.astype(BF16), vn_ref[...].astype(BF16))

    mp = m_ref[...]
    m_all = jnp.maximum(jnp.max(jnp.where(sel, mp, -jnp.inf), axis=0), m_own)
    w = jnp.where(sel, jnp.exp(mp - m_all[None]), 0.0)
    w_own = jnp.exp(m_own - m_all)
    den = jnp.sum(w * l_ref[...], axis=0) + w_own * l_own
    num = jnp.sum(w * o_ref[...], axis=0) + w_own * o_own
    out_ref[...] = num / den


def paged_merge(q_combo, k_new, v_new, m_p, l_p, o_p, ksum):
    combo = pl.BlockSpec((None, N_COMBO, HEAD_DIM), lambda b: (b, 0, 0))
    part = pl.BlockSpec((None, N_PAST_BLOCKS, N_COMBO, HEAD_DIM), lambda b: (b, 0, 0, 0))
    return pl.pallas_call(
        _paged_merge_kernel,
        grid=(DEC_BATCH,),
        in_specs=[combo, combo, combo, part, part, part,
                  pl.BlockSpec((None, N_PAST_BLOCKS, C_HEADS, HEAD_DIM), lambda b: (b, 0, 0, 0))],
        out_specs=combo,
        out_shape=jax.ShapeDtypeStruct((DEC_BATCH, N_COMBO, HEAD_DIM), F32),
        compiler_params=_cparams(1, N_SAMPLE),
        name="paged_merge",
    )(q_combo, k_new, v_new, m_p, l_p, o_p, ksum)


def _memattn_kernel(x_ref, mk_ref, mv_ref, g_ref, b_ref, *refs, nb_tile, rows_per_b, precise):
    n_w = 2 if precise else 1
    wq = _load_all(refs[:n_w])
    wo = _load_all(refs[n_w:2 * n_w])
    o_ref = refs[2 * n_w]
    scale = MEM_HEAD_DIM ** -0.5
    x = x_ref[...]
    tm = x.shape[0]
    q = _mm(x, wq, precise)
    row_b = lax.broadcasted_iota(jnp.int32, (tm, 1), 0) >> int(math.log2(rows_per_b))
    heads = []
    for hh in range(MEM_HEADS):
        sl = slice(hh * MEM_HEAD_DIM, (hh + 1) * MEM_HEAD_DIM)
        qh = q[:, sl]
        oh = None
        for bb in range(nb_tile):
            s = _mm(qh, mk_ref[bb, :, sl], precise, nt=True) * scale
            s = s - jnp.max(s, axis=-1, keepdims=True)
            p = jnp.exp(s)
            p = p / jnp.sum(p, axis=-1, keepdims=True)
            ob = _mm(p, mv_ref[bb, :, sl], precise)
            if nb_tile > 1:
                ob = jnp.where(row_b == bb, ob, 0.0)
            oh = ob if oh is None else oh + ob
        heads.append(oh)
    o = jnp.concatenate(heads, axis=-1)
    y = ALPHA * x + _mm(o, wo, precise)
    o_ref[...] = _ln(y, g_ref[...], b_ref[...])


def memory_attention_ln(x, wq, mk, mv, wo, g, b, *, m, tm, rows_per_b, name, precise=False, b0=0):
    if rows_per_b >= tm:
        nb_tile = 1
        tiles_per_b = rows_per_b // tm
        kv_map = lambda i: (b0 + i // tiles_per_b, 0, 0)
    else:
        nb_tile = tm // rows_per_b
        kv_map = lambda i: (i, 0, 0)
    const2 = lambda i: (0, 0)
    return pl.pallas_call(
        functools.partial(_memattn_kernel, nb_tile=nb_tile, rows_per_b=rows_per_b, precise=precise),
        grid=(m // tm,),
        in_specs=[pl.BlockSpec((tm, D_MODEL), lambda i: (i, 0)),
                  pl.BlockSpec((nb_tile, N_MEM, MEM_WIDTH), kv_map),
                  pl.BlockSpec((nb_tile, N_MEM, MEM_WIDTH), kv_map),
                  pl.BlockSpec((1, D_MODEL), const2),
                  pl.BlockSpec((1, D_MODEL), const2)]
        + [pl.BlockSpec((D_MODEL, MEM_WIDTH), const2)] * len(wq)
        + [pl.BlockSpec((MEM_WIDTH, D_MODEL), const2)] * len(wo),
        out_specs=pl.BlockSpec((tm, D_MODEL), lambda i: (i, 0)),
        out_shape=jax.ShapeDtypeStruct((m, D_MODEL), F32),
        compiler_params=_cparams(1, m),
        name=name,
    )(x, mk, mv, g, b, *wq, *wo)


def _router_kernel(x_ref, xt_ref, w_ref, b_ref, o_ref, cnt_ref, run_ref, *, tiles_per_b):
    step = pl.program_id(0)
    tm = x_ref.shape[0]

    @pl.when(step == 0)
    def _():
        run_ref[...] = jnp.zeros(run_ref.shape, F32)

    x = x_ref[...]
    if tiles_per_b:
        is_last = lax.rem(step, tiles_per_b) == tiles_per_b - 1
        x = jnp.concatenate([x[:tm - CHUNK], jnp.where(is_last, xt_ref[...], x[tm - CHUNK:])], axis=0)
    xh, xl = _split_bf16(x)
    wh, wl = _split_bf16(w_ref[...])
    logits = _dot_nt(wh, xh) + (_dot_nt(wh, xl) + _dot_nt(wl, xh)) + b_ref[...]
    logits = logits - jnp.max(logits, axis=0, keepdims=True)
    e = jnp.exp(logits)
    probs = e / jnp.sum(e, axis=0, keepdims=True)
    p = [probs[j:j + 1, :] for j in range(N_EXPERTS)]
    gbest = None
    gsel = None
    for g in range(N_EXPERT_GROUPS):
        a, b_, c, d = p[4 * g:4 * g + 4]
        hi1, lo1 = jnp.maximum(a, b_), jnp.minimum(a, b_)
        hi2, lo2 = jnp.maximum(c, d), jnp.minimum(c, d)
        gs = jnp.maximum(hi1, hi2) + jnp.maximum(jnp.minimum(hi1, hi2), jnp.maximum(lo1, lo2))
        if g == 0:
            gbest, gsel = gs, jnp.zeros(gs.shape, jnp.int32)
        else:
            better = gs > gbest
            gbest = jnp.where(better, gs, gbest)
            gsel = jnp.where(better, g, gsel)
    cand = [jnp.where(gsel == j // EXPERTS_PER_GROUP, p[j], -1.0) for j in range(N_EXPERTS)]
    v1 = cand[0]
    i1 = jnp.zeros(v1.shape, jnp.int32)
    for j in range(1, N_EXPERTS):
        better = cand[j] > v1
        v1 = jnp.where(better, cand[j], v1)
        i1 = jnp.where(better, j, i1)
    v2 = jnp.full(v1.shape, -2.0, F32)
    i2 = jnp.zeros(v1.shape, jnp.int32)
    for j in range(N_EXPERTS):
        better = (cand[j] > v2) & (i1 != j)
        v2 = jnp.where(better, cand[j], v2)
        i2 = jnp.where(better, j, i2)
    tot = v1 + v2

    e_i = lax.broadcasted_iota(jnp.int32, (N_EXPERTS, tm), 0)
    pick1 = e_i == i1
    pick2 = e_i == i2
    onehot = (pick1 | pick2).astype(F32)
    t_r = lax.broadcasted_iota(jnp.int32, (tm, tm), 0)
    t_c = lax.broadcasted_iota(jnp.int32, (tm, tm), 1)
    earlier = (t_r < t_c).astype(BF16)
    before = _dot(onehot.astype(BF16), earlier) + run_ref[:, 0:1]
    rank1 = jnp.sum(jnp.where(pick1, before, 0.0), axis=0, keepdims=True)
    rank2 = jnp.sum(jnp.where(pick2, before, 0.0), axis=0, keepdims=True)
    run = run_ref[...] + jnp.sum(onehot, axis=1, keepdims=True)
    run_ref[...] = run
    cnt_ref[...] = run

    row = lax.broadcasted_iota(jnp.int32, o_ref.shape, 0)
    vals = (i1.astype(F32), i2.astype(F32), v1 / tot, v2 / tot, rank1, rank2)
    out = jnp.zeros(o_ref.shape, F32)
    for r, v in enumerate(vals):
        out = jnp.where(row == r, v, out)
    o_ref[...] = out


def router(x, x_tail, w_t, b, *, tm, name):
    m = x.shape[0]
    tiles_per_b = 0 if x_tail is None else SEQ // tm
    if x_tail is None:
        x_tail, tail_spec = x, pl.BlockSpec((min(CHUNK, tm), D_MODEL), lambda i: (0, 0))
    else:
        tail_spec = pl.BlockSpec((CHUNK, D_MODEL), lambda i: (i // tiles_per_b, 0))
    return pl.pallas_call(
        functools.partial(_router_kernel, tiles_per_b=tiles_per_b),
        grid=(m // tm,),
        in_specs=[pl.BlockSpec((tm, D_MODEL), lambda i: (i, 0)),
                  tail_spec,
                  pl.BlockSpec((N_EXPERTS, D_MODEL), lambda i: (0, 0)),
                  pl.BlockSpec((N_EXPERTS, 1), lambda i: (0, 0))],
        out_specs=[pl.BlockSpec((8, tm), lambda i: (0, i)),
                   pl.BlockSpec((N_EXPERTS, 128), lambda i: (0, 0))],
        out_shape=[jax.ShapeDtypeStruct((8, m), F32),
                   jax.ShapeDtypeStruct((N_EXPERTS, 128), F32)],
        scratch_shapes=[pltpu.VMEM((N_EXPERTS, 128), F32)],
        compiler_params=_cparams(1, m),
        name=name,
    )(x, x_tail, w_t, b)


def _dispatch_kernel(d0_ref, d1_ref, starts_ref, ends_ref, nv_ref, x_ref, o_hbm, zero_ref, sem, *, cfg):
    i = pl.program_id(0)
    tm = cfg.tok_tile

    def fill_copy(row0):
        return pltpu.make_async_copy(zero_ref, o_hbm.at[pl.ds(row0, cfg.tile)], sem)

    @pl.when(i == 0)
    def _():
        zero_ref[...] = jnp.zeros(zero_ref.shape, zero_ref.dtype)
        for wait in (False, True):
            for e in range(N_EXPERTS):
                @pl.when(ends_ref[e] > starts_ref[e])
                def _():
                    cp = fill_copy(pl.multiple_of(ends_ref[e] - cfg.tile, cfg.tile))
                    cp.wait() if wait else cp.start()

            def tail(t, carry):
                cp = fill_copy(pl.multiple_of(t * cfg.tile, cfg.tile))
                cp.wait() if wait else cp.start()
                return carry

            lax.fori_loop(nv_ref[0], cfg.n_tiles, tail, 0)

    base = i * tm

    def body(it, carry):
        for j in range(ROW_DMA_UNROLL):
            r = it * ROW_DMA_UNROLL + j
            pltpu.make_async_copy(x_ref.at[pl.ds(r, 1)], o_hbm.at[pl.ds(d0_ref[base + r], 1)], sem).start(priority=0)
            pltpu.make_async_copy(x_ref.at[pl.ds(r, 1)], o_hbm.at[pl.ds(d1_ref[base + r], 1)], sem).start(priority=1)
        return carry

    lax.fori_loop(0, tm // ROW_DMA_UNROLL, body, 0)
    for _ in range(MOE_TOPK):
        pltpu.make_async_copy(x_ref, o_hbm.at[pl.ds(0, tm)], sem).wait()


def moe_dispatch(d0, d1, starts, ends, n_valid, x, cfg, name):
    tm = cfg.tok_tile
    return pl.pallas_call(
        functools.partial(_dispatch_kernel, cfg=cfg),
        grid_spec=pltpu.PrefetchScalarGridSpec(
            num_scalar_prefetch=5,
            grid=(cfg.n_tok // tm,),
            in_specs=[pl.BlockSpec((tm, D_MODEL), lambda i, *_: (i, 0))],
            out_specs=pl.BlockSpec(memory_space=pl.ANY),
            scratch_shapes=[pltpu.VMEM((cfg.tile, D_MODEL), F32), pltpu.SemaphoreType.DMA]),
        out_shape=jax.ShapeDtypeStruct((cfg.rows, D_MODEL), F32),
        compiler_params=_cparams(1, cfg.n_tok),
        name=name,
    )(d0, d1, starts, ends, n_valid, x)


def _expert_weights(first, w_refs, scratch_refs, precise):
    @pl.when(first)
    def _():
        for k, w_ref in enumerate(w_refs):
            if precise:
                hi, lo = _split_bf16(w_ref[...])
                scratch_refs[2 * k][...] = hi
                scratch_refs[2 * k + 1][...] = lo
            else:
                scratch_refs[k][...] = w_ref[...].astype(BF16)


def _expert_dot(x, scratch_refs, k, precise):
    if not precise:
        return _dot(x.astype(BF16), scratch_refs[k][...])
    xh, xl = _split_bf16(x)
    wh, wl = scratch_refs[2 * k][...], scratch_refs[2 * k + 1][...]
    return _dot(xh, wh) + (_dot(xl, wh) + _dot(xh, wl))


def _moe_up_kernel(te_ref, first_ref, nv_ref, x_ref, wg_ref, wu_ref, hh_ref, *scratch, precise, f_tiles):
    i = pl.program_id(0)

    @pl.when(i < nv_ref[0])
    def _():
        _expert_weights((first_ref[i] == 1) | (f_tiles > 1), (wg_ref, wu_ref), scratch, precise)
        x = x_ref[...]
        hg = _expert_dot(x, scratch, 0, precise)
        hu = _expert_dot(x, scratch, 1, precise)
        hh_ref[...] = (hg * jax.nn.sigmoid(hg) * hu).astype(hh_ref.dtype)

    @pl.when(i >= nv_ref[0])
    def _():
        hh_ref[...] = jnp.zeros(hh_ref.shape, hh_ref.dtype)


def _moe_down_kernel(te_ref, first_ref, nv_ref, hh_ref, wd_ref, y_ref, *scratch, precise):
    i = pl.program_id(0)

    @pl.when(i < nv_ref[0])
    def _():
        _expert_weights(first_ref[i] == 1, (wd_ref,), scratch, precise)
        y_ref[...] = _expert_dot(hh_ref[...], scratch, 0, precise)

    @pl.when(i >= nv_ref[0])
    def _():
        y_ref[...] = jnp.zeros(y_ref.shape, y_ref.dtype)


def _stream_expert_weights(i, te_ref, first_ref, nxt_ref, w_hbms, land_refs, bf16_refs, sem, layer):
    def copies(e):
        return [pltpu.make_async_copy(w.at[layer, e], land, sem.at[k])
                for k, (w, land) in enumerate(zip(w_hbms, land_refs))]

    @pl.when(i == 0)
    def _():
        for cp in copies(te_ref[0]):
            cp.start()

    @pl.when(first_ref[i] == 1)
    def _():
        for cp in copies(te_ref[i]):
            cp.wait()
        for land, dst in zip(land_refs, bf16_refs):
            dst[...] = land[...].astype(BF16)

        @pl.when(nxt_ref[i] >= 0)
        def _():
            for cp in copies(nxt_ref[i]):
                cp.start()


def _moe_up_stream_kernel(te_ref, first_ref, nxt_ref, nv_ref, x_ref, wg_hbm, wu_hbm, hh_ref,
                          land_g, land_u, wgb_ref, wub_ref, sem, *, layer):
    i = pl.program_id(0)

    @pl.when(i < nv_ref[0])
    def _():
        _stream_expert_weights(i, te_ref, first_ref, nxt_ref, (wg_hbm, wu_hbm), (land_g, land_u),
                               (wgb_ref, wub_ref), sem, layer)
        x = x_ref[...].astype(BF16)
        hg = _dot(x, wgb_ref[...])
        hu = _dot(x, wub_ref[...])
        hh_ref[...] = (hg * jax.nn.sigmoid(hg) * hu).astype(hh_ref.dtype)

    @pl.when(i >= nv_ref[0])
    def _():
        hh_ref[...] = jnp.zeros(hh_ref.shape, hh_ref.dtype)


def _moe_down_stream_kernel(te_ref, first_ref, nxt_ref, nv_ref, hh_ref, wd_hbm, y_ref, land_d, wdb_ref, sem,
                            *, layer):
    i = pl.program_id(0)

    @pl.when(i < nv_ref[0])
    def _():
        _stream_expert_weights(i, te_ref, first_ref, nxt_ref, (wd_hbm,), (land_d,), (wdb_ref,), sem, layer)
        y_ref[...] = _dot(hh_ref[...], wdb_ref[...])

    @pl.when(i >= nv_ref[0])
    def _():
        y_ref[...] = jnp.zeros(y_ref.shape, y_ref.dtype)


def moe_experts_streamed(tile_expert, tile_first, tile_next, n_valid, x_sorted, w_gate, w_up, w_down, layer, cfg,
                         name):
    tm = cfg.tile
    row_map = lambda i, te, fi, nx, nv: (jnp.minimum(i, nv[0] - 1), 0)
    out_map = lambda i, te, fi, nx, nv: (i, 0)
    hbm = pl.BlockSpec(memory_space=pl.ANY)
    hh = pl.pallas_call(
        functools.partial(_moe_up_stream_kernel, layer=layer),
        grid_spec=pltpu.PrefetchScalarGridSpec(
            num_scalar_prefetch=4,
            grid=(cfg.n_tiles,),
            in_specs=[pl.BlockSpec((tm, D_MODEL), row_map), hbm, hbm],
            out_specs=pl.BlockSpec((tm, D_EXPERT), out_map),
            scratch_shapes=[pltpu.VMEM((D_MODEL, D_EXPERT), F32), pltpu.VMEM((D_MODEL, D_EXPERT), F32),
                            pltpu.VMEM((D_MODEL, D_EXPERT), BF16), pltpu.VMEM((D_MODEL, D_EXPERT), BF16),
                            pltpu.SemaphoreType.DMA((2,))]),
        out_shape=jax.ShapeDtypeStruct((cfg.rows, D_EXPERT), BF16),
        compiler_params=_cparams(1, cfg.n_tok),
        name=name + "_up",
    )(tile_expert, tile_first, tile_next, n_valid, x_sorted, w_gate, w_up)
    return pl.pallas_call(
        functools.partial(_moe_down_stream_kernel, layer=layer),
        grid_spec=pltpu.PrefetchScalarGridSpec(
            num_scalar_prefetch=4,
            grid=(cfg.n_tiles,),
            in_specs=[pl.BlockSpec((tm, D_EXPERT), row_map), hbm],
            out_specs=pl.BlockSpec((tm, D_MODEL), out_map),
            scratch_shapes=[pltpu.VMEM((D_EXPERT, D_MODEL), F32), pltpu.VMEM((D_EXPERT, D_MODEL), BF16),
                            pltpu.SemaphoreType.DMA((1,))]),
        out_shape=jax.ShapeDtypeStruct((cfg.rows, D_MODEL), F32),
        compiler_params=_cparams(1, cfg.n_tok),
        name=name + "_down",
    )(tile_expert, tile_first, tile_next, n_valid, hh, w_down)


def moe_experts(tile_expert, tile_first, n_valid, x_sorted, w_gate, w_up, w_down, layer, cfg, name):
    tm = cfg.tile
    n_copies = 2 if cfg.precise else 1
    f_tiles = 2 if cfg.precise else 1
    fw = D_EXPERT // f_tiles
    row_map = lambda i, te, fi, nv: (jnp.minimum(i, nv[0] - 1), 0)
    wmap = lambda i, te, fi, nv: (layer, te[i], 0, 0)
    hh = pl.pallas_call(
        functools.partial(_moe_up_kernel, precise=cfg.precise, f_tiles=f_tiles),
        grid_spec=pltpu.PrefetchScalarGridSpec(
            num_scalar_prefetch=3,
            grid=(cfg.n_tiles, f_tiles),
            in_specs=[pl.BlockSpec((tm, D_MODEL), lambda i, f, te, fi, nv: (jnp.minimum(i, nv[0] - 1), 0)),
                      pl.BlockSpec((None, None, D_MODEL, fw), lambda i, f, te, fi, nv: (layer, te[i], 0, f)),
                      pl.BlockSpec((None, None, D_MODEL, fw), lambda i, f, te, fi, nv: (layer, te[i], 0, f))],
            out_specs=pl.BlockSpec((tm, fw), lambda i, f, te, fi, nv: (i, f)),
            scratch_shapes=[pltpu.VMEM((D_MODEL, fw), BF16)] * (2 * n_copies)),
        out_shape=jax.ShapeDtypeStruct((cfg.rows, D_EXPERT), F32 if cfg.precise else BF16),
        compiler_params=_cparams(2, cfg.n_tok),
        name=name + "_up",
    )(tile_expert, tile_first, n_valid, x_sorted, w_gate, w_up)
    return pl.pallas_call(
        functools.partial(_moe_down_kernel, precise=cfg.precise),
        grid_spec=pltpu.PrefetchScalarGridSpec(
            num_scalar_prefetch=3,
            grid=(cfg.n_tiles,),
            in_specs=[pl.BlockSpec((tm, D_EXPERT), row_map),
                      pl.BlockSpec((None, None, D_EXPERT, D_MODEL), wmap)],
            out_specs=pl.BlockSpec((tm, D_MODEL), lambda i, te, fi, nv: (i, 0)),
            scratch_shapes=[pltpu.VMEM((D_EXPERT, D_MODEL), BF16)] * n_copies),
        out_shape=jax.ShapeDtypeStruct((cfg.rows, D_MODEL), F32),
        compiler_params=_cparams(1, cfg.n_tok),
        name=name + "_down",
    )(tile_expert, tile_first, n_valid, hh, w_down)


def _combine_kernel(d0_ref, d1_ref, x_ref, w_ref, g_ref, b_ref, y_hbm, o_ref, buf_ref, sem, *, tm):
    i = pl.program_id(0)
    n_steps = pl.num_programs(0)

    def gather(tile, half, start):
        base = tile * tm

        def body(it, carry):
            for j in range(ROW_DMA_UNROLL):
                r = it * ROW_DMA_UNROLL + j
                pltpu.make_async_copy(y_hbm.at[pl.ds(d0_ref[base + r], 1)], buf_ref.at[half, 0, pl.ds(r, 1)],
                                      sem.at[half]).start(priority=0)
                pltpu.make_async_copy(y_hbm.at[pl.ds(d1_ref[base + r], 1)], buf_ref.at[half, 1, pl.ds(r, 1)],
                                      sem.at[half]).start(priority=1)
            return carry

        if start:
            lax.fori_loop(0, tm // ROW_DMA_UNROLL, body, 0)
        else:
            for k in range(MOE_TOPK):
                pltpu.make_async_copy(y_hbm.at[pl.ds(0, tm)], buf_ref.at[half, k], sem.at[half]).wait()

    @pl.when(i == 0)
    def _():
        gather(0, 0, True)

    @pl.when(i + 1 < n_steps)
    def _():
        gather(i + 1, (i + 1) % 2, True)

    half = i % 2
    gather(i, half, False)
    w = w_ref[...]
    y = ALPHA * x_ref[...] + w[:, 0:1] * buf_ref[half, 0] + w[:, 1:2] * buf_ref[half, 1]
    o_ref[...] = _ln(y, g_ref[...], b_ref[...])


def moe_combine_ln(d0, d1, x, w_tok, g, b, y_sorted, *, tm, name):
    m = x.shape[0]
    return pl.pallas_call(
        functools.partial(_combine_kernel, tm=tm),
        grid_spec=pltpu.PrefetchScalarGridSpec(
            num_scalar_prefetch=2,
            grid=(m // tm,),
            in_specs=[pl.BlockSpec((tm, D_MODEL), lambda i, a, c: (i, 0)),
                      pl.BlockSpec((tm, MOE_TOPK), lambda i, a, c: (i, 0)),
                      pl.BlockSpec((1, D_MODEL), lambda i, a, c: (0, 0)),
                      pl.BlockSpec((1, D_MODEL), lambda i, a, c: (0, 0)),
                      pl.BlockSpec(memory_space=pl.ANY)],
            out_specs=pl.BlockSpec((tm, D_MODEL), lambda i, a, c: (i, 0)),
            scratch_shapes=[pltpu.VMEM((2, MOE_TOPK, tm, D_MODEL), F32), pltpu.SemaphoreType.DMA((2,))]),
        out_shape=jax.ShapeDtypeStruct((m, D_MODEL), F32),
        compiler_params=_cparams(1, m),
        name=name,
    )(d0, d1, x, w_tok, g, b, y_sorted)


def _routing_plan(route, counts, cfg):
    counts = counts[:, 0].astype(jnp.int32)
    padded = ((counts + cfg.tile - 1) // cfg.tile) * cfg.tile
    ends = jnp.cumsum(padded)
    starts = ends - padded
    experts = jnp.arange(N_EXPERTS, dtype=jnp.int32)[:, None]

    def slot_rows(k):
        picked = route[k].astype(jnp.int32)[None, :] == experts
        return jnp.sum(jnp.where(picked, starts[:, None], 0), axis=0) + route[4 + k].astype(jnp.int32)

    tile_start = jnp.arange(cfg.n_tiles, dtype=jnp.int32) * cfg.tile
    tile_expert = jnp.minimum(jnp.sum((tile_start[:, None] >= ends[None, :]).astype(jnp.int32), axis=1),
                              N_EXPERTS - 1)
    start_of_tile_expert = jnp.sum(jnp.where(tile_expert[:, None] == experts.T, starts[None, :], 0), axis=1)
    tile_first = (tile_start == start_of_tile_expert).astype(jnp.int32)
    n_valid = (ends[-1] // cfg.tile).astype(jnp.int32).reshape(1)
    later_nonempty = (experts.T > tile_expert[:, None]) & (padded[None, :] > 0)
    tile_next = jnp.min(jnp.where(later_nonempty, experts.T, N_EXPERTS), axis=1)
    tile_next = jnp.where(tile_next == N_EXPERTS, -1, tile_next).astype(jnp.int32)
    return (slot_rows(0), slot_rows(1), route[2:4].T, starts, ends, tile_expert, tile_first, tile_next, n_valid)


def moe_ln(x, x_route, layer, p, cfg, name):
    route, cnt = router(x, x_route, p["router_w_t"], p["router_b"], tm=cfg.tok_tile, name=name + "_router")
    d0, d1, w_tok, starts, ends, tile_expert, tile_first, tile_next, n_valid = _routing_plan(route, cnt, cfg)
    x_sorted = moe_dispatch(d0, d1, starts, ends, n_valid, x, cfg, name + "_dispatch")
    weights = (p["moe_w_gate"], p["moe_w_up"], p["moe_w_down"])
    if cfg.precise:
        y_sorted = moe_experts(tile_expert, tile_first, n_valid, x_sorted, *weights, layer, cfg, name)
    else:
        y_sorted = moe_experts_streamed(tile_expert, tile_first, tile_next, n_valid, x_sorted, *weights, layer,
                                        cfg, name)
    return moe_combine_ln(d0, d1, x, w_tok, p["moe_ln_g"][layer], p["moe_ln_b"][layer], y_sorted,
                          tm=cfg.tok_tile, name=name + "_combine")


def _rope_tables(pos):
    half = ROT_DIM // 2
    inv = jnp.power(ROPE_THETA, -jnp.arange(half, dtype=F32) / half)
    ang = pos.astype(F32)[:, None] * inv[None, :]
    cos, sin = jnp.cos(ang), jnp.sin(ang)
    n = pos.shape[0]
    ones = jnp.ones((n, HEAD_DIM - ROT_DIM), F32)
    zeros_h = jnp.zeros((n, half), F32)
    zeros_r = jnp.zeros((n, HEAD_DIM - ROT_DIM), F32)
    c = jnp.concatenate([cos, cos, ones], axis=1)
    sa = jnp.concatenate([zeros_h, sin, zeros_r], axis=1)
    sb = jnp.concatenate([-sin, zeros_h, zeros_r], axis=1)
    return c, sa, sb


def kernel(x_prompt, x_sample, mem_prompt, cache_c_k, cache_c_v, page_table, state_b_buf, state_d_buf, cache_mem_k, cache_mem_v, ab_w_in, ab_a_ln_g, ab_a_ln_b, ab_a_ws, ab_a_bs, ab_b_wg, ab_b_scale, ab_w_out, cd_w_in, cd_d_conv_w, cd_d_conv_b, cd_d_ln_g, cd_d_ln_b, cd_w_out, mix_ln_g, mix_ln_b, mem_w_q, mem_w_k, mem_w_v, mem_w_o, mem_ln_g, mem_ln_b, router_w, router_b, moe_w_gate, moe_w_up, moe_w_down, moe_ln_g, moe_ln_b):
    row = lambda v: v.reshape(1, -1)
    xp = x_prompt.reshape(N_PROMPT, D_MODEL)
    xs = x_sample.reshape(N_SAMPLE, D_MODEL)
    moe_p = {"router_w_t": router_w.T, "router_b": router_b.reshape(N_EXPERTS, 1),
             "moe_w_gate": moe_w_gate, "moe_w_up": moe_w_up,
             "moe_w_down": moe_w_down, "moe_ln_g": [row(moe_ln_g[l]) for l in range(DEPTH)],
             "moe_ln_b": [row(moe_ln_b[l]) for l in range(DEPTH)]}

    w_kv = split_weight(jnp.concatenate([mem_w_k[0], mem_w_k[1], mem_w_v[0], mem_w_v[1]], axis=1), True)
    n_memrows = BATCH * N_MEM
    kv = matmul(mem_prompt.reshape(n_memrows, D_MODEL), w_kv, m=n_memrows, tm=256, tn=1024, precise=True,
                name="mem_kv")
    pmk = [kv[:, l * MEM_WIDTH:(l + 1) * MEM_WIDTH].reshape(BATCH, N_MEM, MEM_WIDTH) for l in range(DEPTH)]
    pmv = [kv[:, (DEPTH + l) * MEM_WIDTH:(DEPTH + l + 1) * MEM_WIDTH].reshape(BATCH, N_MEM, MEM_WIDTH)
           for l in range(DEPTH)]
    smk = cache_mem_k.reshape(DEPTH, DEC_BATCH, N_MEM, MEM_WIDTH)
    smv = cache_mem_v.reshape(DEPTH, DEC_BATCH, N_MEM, MEM_WIDTH)

    mem_w = [(split_weight(mem_w_q[l], True), split_weight(mem_w_o[l], True)) for l in range(DEPTH)]
    mem_ln = [(row(mem_ln_g[l]), row(mem_ln_b[l])) for l in range(DEPTH)]

    def memattn_prompt(x, layer):
        wq, wo = split_weight(mem_w_q[layer], False), split_weight(mem_w_o[layer], False)
        return memory_attention_ln(x, wq, pmk[layer], pmv[layer], wo, *mem_ln[layer], m=N_PROMPT, tm=512,
                                   rows_per_b=SEQ, name="memattn_prompt")

    def memattn_sample(x, layer):
        wq, wo = mem_w[layer]
        return memory_attention_ln(x, wq, smk[layer], smv[layer], wo, *mem_ln[layer], m=N_SAMPLE, tm=N_SAMPLE,
                                   rows_per_b=DEC_SEQ, precise=True, name="memattn_sample")

    w_in0 = split_weight(ab_w_in[0], True)
    w_out0 = split_weight(ab_w_out[0], True)
    ab_prm = {"ln_g": row(ab_a_ln_g[0]), "ln_b": row(ab_a_ln_b[0]), "ws": ab_a_ws[0], "bs_t": ab_a_bs[0].T,
              "wg": ab_b_wg[0].astype(BF16), "scale": row(ab_b_scale[0])}
    ab_prm_precise = dict(ab_prm, wg=ab_b_wg[0])
    g, b = row(mix_ln_g[0]), row(mix_ln_b[0])
    n_chunks = SEQ // CHUNK
    n_tail = BATCH * CHUNK

    h0p = matmul(xp, split_weight(ab_w_in[0], False), m=N_PROMPT, tm=1024, tn=1024, n_gelu=2, name="ab_in_prompt")
    mix_p, _ = mixer_ab(h0p, h0p, lambda b, c: (jnp.maximum(b * (SEQ // 16) + c * (CHUNK // 16) - 1, 0), 2),
                        nb=BATCH, n_chunks=n_chunks, pos0=0, has_ctx=False, prm=ab_prm, name="mixer_ab_prompt")
    x1p = matmul_res_ln([mix_p], [split_weight(ab_w_out[0], False)], xp, g, b, m=N_PROMPT, tm=512,
                        name="ab_out_prompt")
    x1p = memattn_prompt(x1p, 0)

    xt = x_prompt[:, SEQ - CHUNK:].reshape(n_tail, D_MODEL)
    h0t = matmul(xt, w_in0, m=n_tail, tm=CHUNK, tn=1024, n_gelu=2, precise=True, name="ab_in_tail")
    mix_t, _ = mixer_ab(h0t, h0p, lambda b, c: ((b + 1) * (SEQ // 16) - CHUNK // 16 - 1, 2), nb=BATCH, n_chunks=1,
                        pos0=SEQ - CHUNK, has_ctx=True, prm=ab_prm_precise, precise=True, name="mixer_ab_tail")
    x1t = matmul_res_ln([mix_t], [w_out0], xt, g, b, m=n_tail, tm=CHUNK, precise=True, name="ab_out_tail")
    wq, wo = mem_w[0]
    x1t = memory_attention_ln(x1t, wq, pmk[0], pmv[0], wo, *mem_ln[0], m=n_tail, tm=CHUNK, rows_per_b=CHUNK,
                              precise=True, name="memattn_tail")

    h0s = matmul(xs, w_in0, m=N_SAMPLE, tm=N_SAMPLE, tn=1024, n_gelu=2, precise=True, name="ab_in_sample")
    h0s_pad = jnp.pad(h0s.reshape(DEC_BATCH, DEC_SEQ, -1), ((0, 0), (0, CHUNK - DEC_SEQ), (0, 0)))
    h0s_pad = h0s_pad.reshape(DEC_BATCH * CHUNK, -1)
    zctx = jnp.pad(state_b_buf[0], ((0, 0), (16 - B_BUF, 0), (0, 0))).reshape(DEC_BATCH * 16, B_WIDTH)
    mix_s, vn_s = mixer_ab(h0s_pad, zctx, lambda b, c: (b, 0), nb=DEC_BATCH, n_chunks=1, pos0=PAST_LEN,
                           has_ctx=True, prm=ab_prm_precise, precise=True, name="mixer_ab_sample")
    mix_s = mix_s.reshape(DEC_BATCH, CHUNK, -1)[:, :DEC_SEQ].reshape(N_SAMPLE, -1)
    x1s = matmul_res_ln([mix_s], [w_out0], xs, g, b, m=N_SAMPLE, tm=N_SAMPLE, precise=True, name="ab_out_sample")
    x1s = memattn_sample(x1s, 0)

    x2p = moe_ln(x1p, x1t, 0, moe_p, MOE_PROMPT, "moe_prompt")
    x2s = moe_ln(x1s, None, 0, moe_p, MOE_SAMPLE, "moe_sample")

    w_in1 = split_weight(cd_w_in[0], True)
    tabs_p = _rope_tables(jnp.arange(SEQ, dtype=jnp.int32))
    tabs_s = _rope_tables(PAST_LEN + (jnp.arange(N_SAMPLE, dtype=jnp.int32) % DEC_SEQ))
    h1p = matmul_rope(x2p, cd_w_in[0].astype(BF16), tabs_p, m=N_PROMPT, tm=1024, tn=C_WIDTH, n_rope=2,
                      tab_blocks=SEQ // 1024, name="cd_in_prompt")
    h1s = matmul(x2s, w_in1, m=N_SAMPLE, tm=N_SAMPLE, tn=1024, precise=True, name="cd_in_sample")
    qr_s, kr_s = rope_qk(h1s, tabs_s, m=N_SAMPLE, tm=N_SAMPLE, xoff=0, tab_blocks=1, name="rope_sample")
    c_p = moba_prompt(h1p, nb=BATCH, seq=SEQ)
    v_s = h1s[:, 2 * C_WIDTH:3 * C_WIDTH]
    q_combo = qr_s.reshape(DEC_BATCH, N_COMBO, HEAD_DIM)
    m_p, l_p, o_p, ksum = paged_partials(page_table, q_combo, cache_c_k, cache_c_v, 0)
    c_s = paged_merge(q_combo, kr_s.reshape(DEC_BATCH, N_COMBO, HEAD_DIM),
                      v_s.reshape(DEC_BATCH, N_COMBO, HEAD_DIM), m_p, l_p, o_p, ksum)
    c_s = c_s.reshape(N_SAMPLE, C_WIDTH)

    cd_prm = {"conv_w": cd_d_conv_w[0], "conv_b": row(cd_d_conv_b[0]), "ln_g": row(cd_d_ln_g[0]),
              "ln_b": row(cd_d_ln_b[0])}
    col_a = 3 * C_WIDTH // D_WIDTH
    tiles_b = SEQ // 256
    halo_per_tile = 256 // CONV_HALO

    def prev_map(col):
        return lambda b, t: (jnp.maximum((b * tiles_b + t) * halo_per_tile - 1, 0), col)

    d_p, tail_p = conv_module(h1p, col_a, h1p, h1p, prev_map(col_a), prev_map(col_a + 1), nb=BATCH,
                              n_tiles=tiles_b, tm=256, rt=128, tail=CONV_HALO, xoff=0, prev_is_state=False,
                              prm=cd_prm, name="conv_prompt")
    gl_s = jnp.pad(h1s[:, 3 * C_WIDTH:].reshape(DEC_BATCH, DEC_SEQ, 2 * D_WIDTH), ((0, 0), (0, 8 - DEC_SEQ), (0, 0)))
    gl_s = gl_s.reshape(DEC_BATCH * 8, 2 * D_WIDTH)
    dctx = jnp.pad(state_d_buf[0], ((0, 0), (CONV_HALO - D_BUF, 0), (0, 0))).reshape(DEC_BATCH * CONV_HALO, D_WIDTH)
    d_s, tail_s = conv_module(gl_s, 0, dctx, dctx, lambda b, t: (b, 0), lambda b, t: (b, 0), nb=DEC_BATCH,
                              n_tiles=1, tm=8, rt=8, tail=8, xoff=0, prev_is_state=True, prm=cd_prm,
                              name="conv_sample", out_dtype=F32)
    d_s = d_s.reshape(DEC_BATCH, 8, D_WIDTH)[:, :DEC_SEQ].reshape(N_SAMPLE, D_WIDTH)
    w_out1c = split_weight(cd_w_out[0][:C_WIDTH], True)
    w_out1d = split_weight(cd_w_out[0][C_WIDTH:], True)
    g, b = row(mix_ln_g[1]), row(mix_ln_b[1])
    w_out1 = cd_w_out[0].astype(BF16)
    x3p = matmul_res_ln([c_p, d_p], [(w_out1[:C_WIDTH],), (w_out1[C_WIDTH:],)], x2p, g, b, m=N_PROMPT, tm=512,
                        name="cd_out_prompt")
    x3s = matmul_res_ln([c_s, d_s], [w_out1c, w_out1d], x2s, g, b, m=N_SAMPLE, tm=N_SAMPLE, precise=True,
                        name="cd_out_sample")
    y_p = moe_ln(memattn_prompt(x3p, 1), None, 1, moe_p, MOE_PROMPT, "moe_prompt")
    y_s = moe_ln(memattn_sample(x3s, 1), None, 1, moe_p, MOE_SAMPLE, "moe_sample")

    kv_shape_p = (1, BATCH, SEQ, C_HEADS, HEAD_DIM)
    kv_shape_s = (1, DEC_BATCH, DEC_SEQ, C_HEADS, HEAD_DIM)
    z_p = h0p[:, 2 * A_WIDTH:].reshape(BATCH, SEQ, B_WIDTH)
    z_s = h0s[:, 2 * A_WIDTH:].reshape(DEC_BATCH, DEC_SEQ, B_WIDTH)
    h_s = tail_s.reshape(DEC_BATCH, 8, D_WIDTH)[:, :DEC_SEQ]
    mem_shape = (BATCH, N_MEM, MEM_HEADS, MEM_HEAD_DIM)
    return (y_p.reshape(BATCH, SEQ, D_MODEL),
            y_s.reshape(DEC_BATCH, DEC_SEQ, D_MODEL),
            h1p[:, C_WIDTH:2 * C_WIDTH].reshape(kv_shape_p),
            h1p[:, 2 * C_WIDTH:3 * C_WIDTH].reshape(kv_shape_p),
            kr_s.reshape(kv_shape_s),
            v_s.reshape(kv_shape_s),
            z_p[:, SEQ - B_BUF:][None],
            jnp.concatenate([state_b_buf[0], z_s], axis=1)[:, DEC_SEQ:][None],
            vn_s.reshape(DEC_BATCH, CHUNK, A_WIDTH)[:, :DEC_SEQ][None],
            tail_p.reshape(BATCH, CONV_HALO, D_WIDTH)[:, CONV_HALO - D_BUF:][None],
            jnp.concatenate([state_d_buf[0], h_s], axis=1)[:, DEC_SEQ:][None],
            jnp.stack([m_.reshape(mem_shape) for m_ in pmk]),
            jnp.stack([m_.reshape(mem_shape) for m_ in pmv]))
```

```python
import functools
import math
from typing import NamedTuple

import jax
import jax.numpy as jnp
from jax import lax
from jax.experimental import pallas as pl
from jax.experimental.pallas import tpu as pltpu

F32 = jnp.float32
BF16 = jnp.bfloat16

D_MODEL = 2048
BATCH = 4
SEQ = 2048
DEPTH = 2
DEC_BATCH = 8
DEC_SEQ = 4
PAST_LEN = 16384
PAGE_SIZE = 128
A_WIDTH = 1024
CHUNK = 128
A_GROUPS = 8
B_WIDTH = 1024
POOL_WINDOWS = (2, 4, 8, 16)
B_GROUP_DIM = B_WIDTH // len(POOL_WINDOWS)
B_BUF = 15
C_HEADS = 8
HEAD_DIM = 128
C_WIDTH = 1024
ROT_DIM = 32
ROPE_THETA = 500000.0
MOBA_BLOCK = 256
MOBA_TOPK = 3
D_WIDTH = 1024
CONV_WIDTH = 31
D_BUF = 30
N_MEM = 256
MEM_HEADS = 4
MEM_HEAD_DIM = 128
MEM_WIDTH = 512
N_EXPERTS = 16
N_EXPERT_GROUPS = 4
EXPERTS_PER_GROUP = 4
MOE_TOPK = 2
D_EXPERT = 1024
ALPHA = (2 * DEPTH) ** 0.25
LN_EPS = 1e-5

N_PROMPT = BATCH * SEQ
N_SAMPLE = DEC_BATCH * DEC_SEQ
N_PAST_BLOCKS = PAST_LEN // MOBA_BLOCK
PAGES_PER_BLOCK = MOBA_BLOCK // PAGE_SIZE
N_COMBO = DEC_SEQ * C_HEADS


class MoeCfg(NamedTuple):
    n_tok: int
    tok_tile: int
    tile: int
    rows: int
    n_tiles: int
    precise: bool


def _moe_cfg(n_tok, tile, precise):
    rows = ((n_tok * MOE_TOPK + N_EXPERTS * (tile - 1)) // tile + 1) * tile
    return MoeCfg(n_tok, tile, tile, rows, rows // tile, precise)


MOE_PROMPT = _moe_cfg(N_PROMPT, 256, False)
MOE_SAMPLE = _moe_cfg(N_SAMPLE, N_SAMPLE, True)

ROW_DMA_UNROLL = 8
VMEM_LIMIT = 56 * 1024 * 1024
VMEM_LIMIT_SMALL = 40 * 1024 * 1024
SMALL_CALL_ROWS = 1024


def _cparams(n_axes, rows=None):
    vmem = VMEM_LIMIT if rows is None or rows > SMALL_CALL_ROWS else VMEM_LIMIT_SMALL
    return pltpu.CompilerParams(dimension_semantics=("arbitrary",) * n_axes, vmem_limit_bytes=vmem)


def _ln(x, g, b):
    mu = jnp.mean(x, axis=-1, keepdims=True)
    xc = x - mu
    var = jnp.mean(xc * xc, axis=-1, keepdims=True)
    return xc * lax.rsqrt(var + LN_EPS) * g + b


def _dot(a, b):
    return jnp.dot(a, b, preferred_element_type=F32)


def _dot_nt(a, b):
    return lax.dot_general(a, b, (((1,), (1,)), ((), ())), preferred_element_type=F32)


def _split_bf16(v):
    hi = v.astype(BF16)
    return hi, (v.astype(F32) - hi.astype(F32)).astype(BF16)


def _mm(a, b, precise, nt=False):
    dot = _dot_nt if nt else _dot
    if not precise:
        b0 = b[0] if isinstance(b, tuple) else b
        return dot(a.astype(BF16), b0.astype(BF16))
    ah, al = _split_bf16(a)
    bh, bl = b if isinstance(b, tuple) else _split_bf16(b)
    return dot(ah, bh) + (dot(al, bh) + dot(ah, bl))


def _load_all(refs):
    return tuple(r[...] for r in refs)


def split_weight(w, precise):
    if not precise:
        return (w.astype(BF16),)
    bits = lax.bitcast_convert_type(w, jnp.uint32) & jnp.uint32(0xFFFF0000)
    hi = lax.bitcast_convert_type(bits, F32)
    return hi.astype(BF16), (w - hi).astype(BF16)


def _mm_kernel(x_ref, *refs, n_gelu, precise):
    o_ref = refs[-1]
    acc = _mm(x_ref[...], _load_all(refs[:-1]), precise)
    if n_gelu:
        j = pl.program_id(0)

        @pl.when(j < n_gelu)
        def _():
            o_ref[...] = jax.nn.gelu(acc).astype(o_ref.dtype)

        @pl.when(j >= n_gelu)
        def _():
            o_ref[...] = acc.astype(o_ref.dtype)
    else:
        o_ref[...] = acc.astype(o_ref.dtype)


def matmul(x, w, *, m, tm, tn, xoff=0, n_gelu=0, precise=False, name="mm"):
    k = x.shape[1]
    n = w[0].shape[1]
    return pl.pallas_call(
        functools.partial(_mm_kernel, n_gelu=n_gelu, precise=precise),
        grid=(n // tn, m // tm),
        in_specs=[pl.BlockSpec((tm, k), lambda j, i: (i + xoff, 0))]
        + [pl.BlockSpec((k, tn), lambda j, i: (0, j))] * len(w),
        out_specs=pl.BlockSpec((tm, tn), lambda j, i: (i, j)),
        out_shape=jax.ShapeDtypeStruct((m, n), F32),
        compiler_params=_cparams(2, m),
        name=name,
    )(x, *w)


def _rope(x, c, sa, sb):
    half = ROT_DIM // 2
    return x * c + pltpu.roll(x, half, axis=1) * sa + pltpu.roll(x, HEAD_DIM - half, axis=1) * sb


def _mm_rope_kernel(x_ref, w_ref, c_ref, sa_ref, sb_ref, o_ref, *, n_rope):
    j = pl.program_id(0)
    x = x_ref[...].astype(BF16)

    @pl.when(j < n_rope)
    def _():
        c, sa, sb = c_ref[...], sa_ref[...], sb_ref[...]
        pair = 2 * HEAD_DIM
        for s0 in range(0, o_ref.shape[1], pair):
            acc = _dot(x, w_ref[:, s0:s0 + pair])
            for h0 in (0, HEAD_DIM):
                o_ref[:, s0 + h0:s0 + h0 + HEAD_DIM] = _rope(acc[:, h0:h0 + HEAD_DIM], c, sa, sb)

    @pl.when(j >= n_rope)
    def _():
        o_ref[...] = _dot(x, w_ref[...])


def matmul_rope(x, w, tabs, *, m, tm, tn, n_rope, tab_blocks, name):
    k = x.shape[1]
    n = w.shape[1]
    tmap = lambda j, i: (i % tab_blocks, 0)
    return pl.pallas_call(
        functools.partial(_mm_rope_kernel, n_rope=n_rope),
        grid=(n // tn, m // tm),
        in_specs=[pl.BlockSpec((tm, k), lambda j, i: (i, 0)),
                  pl.BlockSpec((k, tn), lambda j, i: (0, j))] + [pl.BlockSpec((tm, HEAD_DIM), tmap)] * 3,
        out_specs=pl.BlockSpec((tm, tn), lambda j, i: (i, j)),
        out_shape=jax.ShapeDtypeStruct((m, n), F32),
        compiler_params=_cparams(2, m),
        name=name,
    )(x, w, *tabs)


def _mm_res_ln_kernel(*refs, n_in, precise):
    n_w = 2 if precise else 1
    a_refs = refs[:n_in]
    w_refs = refs[n_in:n_in + n_in * n_w]
    r_ref, g_ref, b_ref, o_ref = refs[n_in + n_in * n_w:]
    acc = None
    for k, a_ref in enumerate(a_refs):
        d = _mm(a_ref[...], _load_all(w_refs[k * n_w:(k + 1) * n_w]), precise)
        acc = d if acc is None else acc + d
    o_ref[...] = _ln(ALPHA * r_ref[...] + acc, g_ref[...], b_ref[...])


def matmul_res_ln(a_list, w_list, resid, g, b, *, m, tm, roff=0, precise=False, name="mm_res_ln"):
    n_in = len(a_list)
    w_flat = [part for w in w_list for part in w]
    in_specs = [pl.BlockSpec((tm, a.shape[1]), lambda i: (i, 0)) for a in a_list]
    in_specs += [pl.BlockSpec(w.shape, lambda i: (0, 0)) for w in w_flat]
    in_specs += [pl.BlockSpec((tm, D_MODEL), lambda i: (i + roff, 0)),
                 pl.BlockSpec((1, D_MODEL), lambda i: (0, 0)),
                 pl.BlockSpec((1, D_MODEL), lambda i: (0, 0))]
    return pl.pallas_call(
        functools.partial(_mm_res_ln_kernel, n_in=n_in, precise=precise),
        grid=(m // tm,),
        in_specs=in_specs,
        out_specs=pl.BlockSpec((tm, D_MODEL), lambda i: (i, 0)),
        out_shape=jax.ShapeDtypeStruct((m, D_MODEL), F32),
        compiler_params=_cparams(1, m),
        name=name,
    )(*a_list, *w_flat, resid, g, b)


def _mixer_ab_kernel(u_ref, v_ref, z_ref, zp_ref, lng_ref, lnb_ref, ws_ref, bs_ref, wg_ref, sc_ref,
                     mix_ref, vn_ref, *, pos0, has_ctx, precise):
    c = pl.program_id(1)
    vn = _ln(v_ref[...], lng_ref[...], lnb_ref[...])
    vn_ref[...] = vn
    u = u_ref[...]
    row = lax.broadcasted_iota(jnp.int32, (CHUNK, CHUNK), 0)
    col = lax.broadcasted_iota(jnp.int32, (CHUNK, CHUNK), 1)
    causal = col <= row
    gd = A_WIDTH // A_GROUPS
    for g in range(A_GROUPS):
        sl = slice(g * gd, (g + 1) * gd)
        w = jnp.where(causal, ws_ref[g], 0.0)
        mixed = _mm(w, vn[:, sl], precise) + bs_ref[:, g:g + 1]
        mix_ref[:, sl] = (u[:, sl] * mixed).astype(mix_ref.dtype)

    z = z_ref[...]
    zp = zp_ref[...]
    if not has_ctx:
        zp = jnp.where(c == 0, 0.0, zp)
    zext = jnp.concatenate([zp, z], axis=0)
    pos = pos0 + c * CHUNK + lax.broadcasted_iota(jnp.int32, (CHUNK, 1), 0)
    for gi, wdw in enumerate(POOL_WINDOWS):
        sl = slice(gi * B_GROUP_DIM, (gi + 1) * B_GROUP_DIM)
        s = zext[:, sl]
        sh = 1
        while sh < wdw:
            s = s + pltpu.roll(s, sh, axis=0)
            sh *= 2
        cnt = jnp.minimum(wdw, pos + 1).astype(F32)
        d = s[16:, :] / cnt - z[:, sl]
        bo = _mm(d, wg_ref[gi], precise) * sc_ref[:, sl]
        mix_ref[:, A_WIDTH + gi * B_GROUP_DIM:A_WIDTH + (gi + 1) * B_GROUP_DIM] = bo.astype(mix_ref.dtype)


def mixer_ab(h, zprev, zprev_map, *, nb, n_chunks, pos0, has_ctx, prm, name, precise=False):
    m = nb * n_chunks * CHUNK
    row = lambda b, c: b * n_chunks + c
    const2 = lambda b, c: (0, 0)
    return pl.pallas_call(
        functools.partial(_mixer_ab_kernel, pos0=pos0, has_ctx=has_ctx, precise=precise),
        grid=(nb, n_chunks),
        in_specs=[pl.BlockSpec((CHUNK, A_WIDTH), lambda b, c: (row(b, c), 0)),
                  pl.BlockSpec((CHUNK, A_WIDTH), lambda b, c: (row(b, c), 1)),
                  pl.BlockSpec((CHUNK, B_WIDTH), lambda b, c: (row(b, c), 2)),
                  pl.BlockSpec((16, B_WIDTH), zprev_map),
                  pl.BlockSpec((1, A_WIDTH), const2),
                  pl.BlockSpec((1, A_WIDTH), const2),
                  pl.BlockSpec((A_GROUPS, CHUNK, CHUNK), lambda b, c: (0, 0, 0)),
                  pl.BlockSpec((CHUNK, A_GROUPS), const2),
                  pl.BlockSpec((len(POOL_WINDOWS), B_GROUP_DIM, B_GROUP_DIM), lambda b, c: (0, 0, 0)),
                  pl.BlockSpec((1, B_WIDTH), const2)],
        out_specs=[pl.BlockSpec((CHUNK, A_WIDTH + B_WIDTH), lambda b, c: (row(b, c), 0)),
                   pl.BlockSpec((CHUNK, A_WIDTH), lambda b, c: (row(b, c), 0))],
        out_shape=[jax.ShapeDtypeStruct((m, A_WIDTH + B_WIDTH), F32 if precise else BF16),
                   jax.ShapeDtypeStruct((m, A_WIDTH), F32)],
        compiler_params=_cparams(2, m),
        name=name,
    )(h, h, h, zprev, prm["ln_g"], prm["ln_b"], prm["ws"], prm["bs_t"], prm["wg"], prm["scale"])


def _rope_kernel(q_ref, k_ref, c_ref, sa_ref, sb_ref, qo_ref, ko_ref):
    c = c_ref[...]
    sa = sa_ref[...]
    sb = sb_ref[...]
    for h in range(C_HEADS):
        sl = slice(h * HEAD_DIM, (h + 1) * HEAD_DIM)
        for src, dst in ((q_ref, qo_ref), (k_ref, ko_ref)):
            dst[:, sl] = _rope(src[:, sl], c, sa, sb)


def rope_qk(h, tabs, *, m, tm, xoff, tab_blocks, name):
    tmap = lambda i: (i % tab_blocks, 0)
    return pl.pallas_call(
        _rope_kernel,
        grid=(m // tm,),
        in_specs=[pl.BlockSpec((tm, C_WIDTH), lambda i: (i + xoff, 0)),
                  pl.BlockSpec((tm, C_WIDTH), lambda i: (i + xoff, 1)),
                  pl.BlockSpec((tm, HEAD_DIM), tmap),
                  pl.BlockSpec((tm, HEAD_DIM), tmap),
                  pl.BlockSpec((tm, HEAD_DIM), tmap)],
        out_specs=[pl.BlockSpec((tm, C_WIDTH), lambda i: (i, 0)),
                   pl.BlockSpec((tm, C_WIDTH), lambda i: (i, 0))],
        out_shape=[jax.ShapeDtypeStruct((m, C_WIDTH), F32),
                   jax.ShapeDtypeStruct((m, C_WIDTH), F32)],
        compiler_params=_cparams(1, m),
        name=name,
    )(h, h, *tabs)


def _moba_kernel(q_ref, qall_ref, k_ref, v_ref, o_ref, selt_ref, *, n_blocks):
    i = pl.program_id(2)
    blk_rows = MOBA_BLOCK
    seq = n_blocks * blk_rows
    scale = HEAD_DIM ** -0.5

    @pl.when(i == 0)
    def _():
        kmean = jnp.concatenate(
            [jnp.mean(k_ref[j * blk_rows:(j + 1) * blk_rows, :], axis=0, keepdims=True) for j in range(n_blocks)],
            axis=0)
        bst = _mm(kmean, qall_ref[...], True, nt=True)
        blk = lax.broadcasted_iota(jnp.int32, (n_blocks, seq), 0)
        own = lax.broadcasted_iota(jnp.int32, (n_blocks, seq), 1) >> int(math.log2(blk_rows))
        work = jnp.where(blk < own, bst, -jnp.inf)
        sel = jnp.zeros((n_blocks, seq), F32)
        for _ in range(MOBA_TOPK):
            mx = jnp.max(work, axis=0, keepdims=True)
            first = jnp.min(jnp.where(work == mx, blk, n_blocks), axis=0, keepdims=True)
            hit = (blk == first) & (mx > -jnp.inf)
            sel = jnp.where(hit, 1.0, sel)
            work = jnp.where(hit, -jnp.inf, work)
        selt_ref[...] = jnp.zeros(selt_ref.shape, F32)
        for c in range(n_blocks):
            selt_ref[c, 0:n_blocks, :] = sel[:, c * blk_rows:(c + 1) * blk_rows]

    qb = q_ref[...].astype(BF16)
    sel = jnp.transpose(selt_ref[i])

    r_i = lax.broadcasted_iota(jnp.int32, (blk_rows, blk_rows), 0)
    c_i = lax.broadcasted_iota(jnp.int32, (blk_rows, blk_rows), 1)
    causal = (c_i <= r_i).astype(F32)

    for c in range(n_blocks):
        @pl.when(i == c)
        def _():
            nk = (c + 1) * blk_rows
            s = _dot_nt(qb, k_ref[0:nk, :].astype(BF16)) * scale
            allowed = jnp.concatenate(
                [jnp.broadcast_to(sel[:, j:j + 1], (blk_rows, blk_rows)) for j in range(c)] + [causal], axis=1)
            s = jnp.where(allowed > 0.0, s, -jnp.inf)
            m = jnp.max(s, axis=-1, keepdims=True)
            p = jnp.exp(s - m)
            l = jnp.sum(p, axis=-1, keepdims=True)
            o = _dot(p.astype(BF16), v_ref[0:nk, :].astype(BF16))
            o_ref[...] = (o / l).astype(o_ref.dtype)


def moba_prompt(h, *, nb, seq, name="moba_prompt"):
    n_blocks = seq // MOBA_BLOCK
    k_col0 = C_WIDTH // HEAD_DIM
    v_col0 = 2 * C_WIDTH // HEAD_DIM
    return pl.pallas_call(
        functools.partial(_moba_kernel, n_blocks=n_blocks),
        grid=(nb, C_HEADS, n_blocks),
        in_specs=[pl.BlockSpec((MOBA_BLOCK, HEAD_DIM), lambda b, hh, i: (b * n_blocks + i, hh)),
                  pl.BlockSpec((seq, HEAD_DIM), lambda b, hh, i: (b, hh)),
                  pl.BlockSpec((seq, HEAD_DIM), lambda b, hh, i: (b, k_col0 + hh)),
                  pl.BlockSpec((seq, HEAD_DIM), lambda b, hh, i: (b, v_col0 + hh))],
        out_specs=pl.BlockSpec((MOBA_BLOCK, HEAD_DIM), lambda b, hh, i: (b * n_blocks + i, hh)),
        out_shape=jax.ShapeDtypeStruct((nb * seq, C_WIDTH), BF16),
        scratch_shapes=[pltpu.VMEM((n_blocks, 128, MOBA_BLOCK), F32)],
        compiler_params=_cparams(3),
        name=name,
    )(h, h, h, h)


CONV_HALO = 32
SUBLANES = 8


def _conv_kernel(ga_ref, gg_ref, pa_ref, pg_ref, w_ref, bdw_ref, lng_ref, lnb_ref,
                 o_ref, tail_ref, ext_ref, sh_ref, y_ref, *, tm, rt, tail, prev_is_state):
    t = pl.program_id(1)
    hcur = ga_ref[...] * jax.nn.sigmoid(gg_ref[...])
    if prev_is_state:
        hprev = pa_ref[...]
    else:
        hprev = pa_ref[...] * jax.nn.sigmoid(pg_ref[...])
        hprev = jnp.where(t == 0, 0.0, hprev)
    ext_ref[0:CONV_HALO, :] = hprev
    ext_ref[CONV_HALO:CONV_HALO + tm, :] = hcur
    tail_ref[...] = hcur[tm - tail:, :]
    off = CONV_HALO - D_BUF
    sh_rows = sh_ref.shape[1]
    for s in range(1, SUBLANES):
        sh_ref[s - 1] = ext_ref[s:s + sh_rows, :]
    for cc in range(D_WIDTH // 128):
        cs = slice(cc * 128, (cc + 1) * 128)
        for rc in range(tm // rt):
            r0 = rc * rt
            acc = jnp.zeros((rt, 128), F32)
            for j in range(CONV_WIDTH):
                s, a = (off + j) % SUBLANES, (off + j) // SUBLANES
                row0 = r0 + a * SUBLANES
                src = ext_ref[row0:row0 + rt, cs] if s == 0 else sh_ref[s - 1, row0:row0 + rt, cs]
                acc = acc + w_ref[j:j + 1, cs] * src
            y_ref[r0:r0 + rt, cs] = acc + bdw_ref[:, cs]
    y = _ln(y_ref[...], lng_ref[...], lnb_ref[...])
    o_ref[...] = (y * jax.nn.sigmoid(y)).astype(o_ref.dtype)


def conv_module(h, col_a, prev_a, prev_g, prev_map_a, prev_map_g, *, nb, n_tiles, tm, rt, tail, xoff,
                prev_is_state, prm, name, out_dtype=BF16):
    m = nb * n_tiles * tm
    row = lambda b, t: b * n_tiles + t + xoff
    const2 = lambda b, t: (0, 0)
    return pl.pallas_call(
        functools.partial(_conv_kernel, tm=tm, rt=rt, tail=tail, prev_is_state=prev_is_state),
        grid=(nb, n_tiles),
        in_specs=[pl.BlockSpec((tm, D_WIDTH), lambda b, t: (row(b, t), col_a)),
                  pl.BlockSpec((tm, D_WIDTH), lambda b, t: (row(b, t), col_a + 1)),
                  pl.BlockSpec((CONV_HALO, D_WIDTH), prev_map_a),
                  pl.BlockSpec((CONV_HALO, D_WIDTH), prev_map_g),
                  pl.BlockSpec((CONV_WIDTH, D_WIDTH), const2),
                  pl.BlockSpec((1, D_WIDTH), const2),
                  pl.BlockSpec((1, D_WIDTH), const2),
                  pl.BlockSpec((1, D_WIDTH), const2)],
        out_specs=[pl.BlockSpec((tm, D_WIDTH), lambda b, t: (b * n_tiles + t, 0)),
                   pl.BlockSpec((tail, D_WIDTH), lambda b, t: (b, 0))],
        out_shape=[jax.ShapeDtypeStruct((m, D_WIDTH), out_dtype),
                   jax.ShapeDtypeStruct((nb * tail, D_WIDTH), F32)],
        scratch_shapes=[pltpu.VMEM((CONV_HALO + tm, D_WIDTH), F32),
                        pltpu.VMEM((SUBLANES - 1, CONV_HALO + tm - SUBLANES, D_WIDTH), F32),
                        pltpu.VMEM((tm, D_WIDTH), F32)],
        compiler_params=_cparams(2, m),
        name=name,
    )(h, h, prev_a, prev_g, prm["conv_w"], prm["conv_b"], prm["ln_g"], prm["ln_b"])


PAGED_BLOCKS_PER_STEP = 4
PAGED_PAGES_PER_STEP = PAGED_BLOCKS_PER_STEP * PAGES_PER_BLOCK


def _paged_partial_kernel(pt_ref, q_ref, *refs):
    del pt_ref
    pages = refs[:2 * PAGED_PAGES_PER_STEP]
    m_ref, l_ref, o_ref, ks_ref, kb_ref, vb_ref = refs[2 * PAGED_PAGES_PER_STEP:]
    rows = PAGE_SIZE * C_HEADS
    blk_keys = PAGES_PER_BLOCK * rows
    scale = HEAD_DIM ** -0.5
    k_refs = pages[:PAGED_PAGES_PER_STEP]
    v_refs = pages[PAGED_PAGES_PER_STEP:]
    for pg in range(PAGED_PAGES_PER_STEP):
        k = k_refs[pg][...]
        ksum = jnp.sum(k, axis=0)
        if pg % PAGES_PER_BLOCK == 0:
            ks = ksum
        else:
            ks = ks + ksum
        if pg % PAGES_PER_BLOCK == PAGES_PER_BLOCK - 1:
            ks_ref[pg // PAGES_PER_BLOCK] = ks
        kb_ref[pg * rows:(pg + 1) * rows, :] = k.reshape(rows, HEAD_DIM).astype(BF16)
        vb_ref[pg * rows:(pg + 1) * rows, :] = v_refs[pg][...].reshape(rows, HEAD_DIM).astype(BF16)
    s = _dot_nt(q_ref[...].astype(BF16), kb_ref[...]) * scale
    c_i = lax.broadcasted_iota(jnp.int32, s.shape, 0)
    l_i = lax.broadcasted_iota(jnp.int32, s.shape, 1)
    s = jnp.where((l_i & (C_HEADS - 1)) == (c_i & (C_HEADS - 1)), s, -jnp.inf)
    for blk in range(PAGED_BLOCKS_PER_STEP):
        cols = slice(blk * blk_keys, (blk + 1) * blk_keys)
        sb = s[:, cols]
        m = jnp.max(sb, axis=-1, keepdims=True)
        p = jnp.exp(sb - m)
        m_ref[blk] = jnp.broadcast_to(m, (N_COMBO, HEAD_DIM))
        l_ref[blk] = jnp.broadcast_to(jnp.sum(p, axis=-1, keepdims=True), (N_COMBO, HEAD_DIM))
        o_ref[blk] = _dot(p.astype(BF16), vb_ref[cols, :])


def paged_partials(page_table, q_combo, cache_k, cache_v, layer):
    page_block = (None, None, PAGE_SIZE, C_HEADS, HEAD_DIM)
    nbs = PAGED_BLOCKS_PER_STEP

    def page_map(which):
        return lambda b, n, pt: (layer, pt[b, PAGED_PAGES_PER_STEP * n + which], 0, 0, 0)

    page_specs = [pl.BlockSpec(page_block, page_map(w)) for w in range(PAGED_PAGES_PER_STEP)]
    part_shape = jax.ShapeDtypeStruct((DEC_BATCH, N_PAST_BLOCKS, N_COMBO, HEAD_DIM), F32)
    part_spec = pl.BlockSpec((None, nbs, N_COMBO, HEAD_DIM), lambda b, n, pt: (b, n, 0, 0))
    return pl.pallas_call(
        _paged_partial_kernel,
        grid_spec=pltpu.PrefetchScalarGridSpec(
            num_scalar_prefetch=1,
            grid=(DEC_BATCH, N_PAST_BLOCKS // nbs),
            in_specs=[pl.BlockSpec((None, N_COMBO, HEAD_DIM), lambda b, n, pt: (b, 0, 0))] + page_specs + page_specs,
            out_specs=[part_spec, part_spec, part_spec,
                       pl.BlockSpec((None, nbs, C_HEADS, HEAD_DIM), lambda b, n, pt: (b, n, 0, 0))],
            scratch_shapes=[pltpu.VMEM((PAGED_PAGES_PER_STEP * PAGE_SIZE * C_HEADS, HEAD_DIM), BF16)] * 2),
        out_shape=[part_shape, part_shape, part_shape,
                   jax.ShapeDtypeStruct((DEC_BATCH, N_PAST_BLOCKS, C_HEADS, HEAD_DIM), F32)],
        compiler_params=_cparams(2, N_SAMPLE),
        name="paged_partials",
    )(page_table, q_combo, *([cache_k] * PAGED_PAGES_PER_STEP), *([cache_v] * PAGED_PAGES_PER_STEP))


def _paged_merge_kernel(q_ref, kn_ref, vn_ref, m_ref, l_ref, o_ref, ks_ref, out_ref):
    nb = N_PAST_BLOCKS
    scale = HEAD_DIM ** -0.5
    q = q_ref[...]
    kmean = ks_ref[...] * (1.0 / MOBA_BLOCK)
    kmean = jnp.concatenate([kmean] * DEC_SEQ, axis=1)
    bs = jnp.sum(kmean * q[None], axis=-1, keepdims=True)
    work = jnp.broadcast_to(bs, (nb, N_COMBO, HEAD_DIM))
    n_i = lax.broadcasted_iota(jnp.int32, (nb, N_COMBO, HEAD_DIM), 0)
    sel = n_i < 0
    for _ in range(MOBA_TOPK):
        mx = jnp.max(work, axis=0, keepdims=True)
        first = jnp.min(jnp.where(work == mx, n_i, nb), axis=0, keepdims=True)
        hit = n_i == first
        sel = sel | hit
        work = jnp.where(hit, -jnp.inf, work)

    s = _dot_nt(q.astype(BF16), kn_ref[...].astype(BF16)) * scale
    c_i = lax.broadcasted_iota(jnp.int32, s.shape, 0)
    l_i = lax.broadcasted_iota(jnp.int32, s.shape, 1)
    ok = ((l_i & (C_HEADS - 1)) == (c_i & (C_HEADS - 1))) & ((l_i >> 3) <= (c_i >> 3))
    s = jnp.where(ok, s, -jnp.inf)
    m_own = jnp.max(s, axis=-1, keepdims=True)
    p = jnp.exp(s - m_own)
    l_own = jnp.sum(p, axis=-1, keepdims=True)
    o_own = _dot(p.astype(BF16), vn_ref[...].astype(BF16))

    mp = m_ref[...]
    m_all = jnp.maximum(jnp.max(jnp.where(sel, mp, -jnp.inf), axis=0), m_own)
    w = jnp.where(sel, jnp.exp(mp - m_all[None]), 0.0)
    w_own = jnp.exp(m_own - m_all)
    den = jnp.sum(w * l_ref[...], axis=0) + w_own * l_own
    num = jnp.sum(w * o_ref[...], axis=0) + w_own * o_own
    out_ref[...] = num / den


def paged_merge(q_combo, k_new, v_new, m_p, l_p, o_p, ksum):
    combo = pl.BlockSpec((None, N_COMBO, HEAD_DIM), lambda b: (b, 0, 0))
    part = pl.BlockSpec((None, N_PAST_BLOCKS, N_COMBO, HEAD_DIM), lambda b: (b, 0, 0, 0))
    return pl.pallas_call(
        _paged_merge_kernel,
        grid=(DEC_BATCH,),
        in_specs=[combo, combo, combo, part, part, part,
                  pl.BlockSpec((None, N_PAST_BLOCKS, C_HEADS, HEAD_DIM), lambda b: (b, 0, 0, 0))],
        out_specs=combo,
        out_shape=jax.ShapeDtypeStruct((DEC_BATCH, N_COMBO, HEAD_DIM), F32),
        compiler_params=_cparams(1, N_SAMPLE),
        name="paged_merge",
    )(q_combo, k_new, v_new, m_p, l_p, o_p, ksum)


def _memattn_kernel(x_ref, mk_ref, mv_ref, g_ref, b_ref, *refs, nb_tile, rows_per_b, precise):
    n_w = 2 if precise else 1
    wq = _load_all(refs[:n_w])
    wo = _load_all(refs[n_w:2 * n_w])
    o_ref = refs[2 * n_w]
    scale = MEM_HEAD_DIM ** -0.5
    x = x_ref[...]
    tm = x.shape[0]
    q = _mm(x, wq, precise)
    row_b = lax.broadcasted_iota(jnp.int32, (tm, 1), 0) >> int(math.log2(rows_per_b))
    heads = []
    for hh in range(MEM_HEADS):
        sl = slice(hh * MEM_HEAD_DIM, (hh + 1) * MEM_HEAD_DIM)
        qh = q[:, sl]
        oh = None
        for bb in range(nb_tile):
            s = _mm(qh, mk_ref[bb, :, sl], precise, nt=True) * scale
            s = s - jnp.max(s, axis=-1, keepdims=True)
            p = jnp.exp(s)
            p = p / jnp.sum(p, axis=-1, keepdims=True)
            ob = _mm(p, mv_ref[bb, :, sl], precise)
            if nb_tile > 1:
                ob = jnp.where(row_b == bb, ob, 0.0)
            oh = ob if oh is None else oh + ob
        heads.append(oh)
    o = jnp.concatenate(heads, axis=-1)
    y = ALPHA * x + _mm(o, wo, precise)
    o_ref[...] = _ln(y, g_ref[...], b_ref[...])


def memory_attention_ln(x, wq, mk, mv, wo, g, b, *, m, tm, rows_per_b, name, precise=False, b0=0):
    if rows_per_b >= tm:
        nb_tile = 1
        tiles_per_b = rows_per_b // tm
        kv_map = lambda i: (b0 + i // tiles_per_b, 0, 0)
    else:
        nb_tile = tm // rows_per_b
        kv_map = lambda i: (i, 0, 0)
    const2 = lambda i: (0, 0)
    return pl.pallas_call(
        functools.partial(_memattn_kernel, nb_tile=nb_tile, rows_per_b=rows_per_b, precise=precise),
        grid=(m // tm,),
        in_specs=[pl.BlockSpec((tm, D_MODEL), lambda i: (i, 0)),
                  pl.BlockSpec((nb_tile, N_MEM, MEM_WIDTH), kv_map),
                  pl.BlockSpec((nb_tile, N_MEM, MEM_WIDTH), kv_map),
                  pl.BlockSpec((1, D_MODEL), const2),
                  pl.BlockSpec((1, D_MODEL), const2)]
        + [pl.BlockSpec((D_MODEL, MEM_WIDTH), const2)] * len(wq)
        + [pl.BlockSpec((MEM_WIDTH, D_MODEL), const2)] * len(wo),
        out_specs=pl.BlockSpec((tm, D_MODEL), lambda i: (i, 0)),
        out_shape=jax.ShapeDtypeStruct((m, D_MODEL), F32),
        compiler_params=_cparams(1, m),
        name=name,
    )(x, mk, mv, g, b, *wq, *wo)


def _router_kernel(x_ref, xt_ref, w_ref, b_ref, o_ref, cnt_ref, run_ref, *, tiles_per_b):
    step = pl.program_id(0)
    tm = x_ref.shape[0]

    @pl.when(step == 0)
    def _():
        run_ref[...] = jnp.zeros(run_ref.shape, F32)

    x = x_ref[...]
    if tiles_per_b:
        is_last = lax.rem(step, tiles_per_b) == tiles_per_b - 1
        x = jnp.concatenate([x[:tm - CHUNK], jnp.where(is_last, xt_ref[...], x[tm - CHUNK:])], axis=0)
    xh, xl = _split_bf16(x)
    wh, wl = _split_bf16(w_ref[...])
    logits = _dot_nt(wh, xh) + (_dot_nt(wh, xl) + _dot_nt(wl, xh)) + b_ref[...]
    logits = logits - jnp.max(logits, axis=0, keepdims=True)
    e = jnp.exp(logits)
    probs = e / jnp.sum(e, axis=0, keepdims=True)
    p = [probs[j:j + 1, :] for j in range(N_EXPERTS)]
    gbest = None
    gsel = None
    for g in range(N_EXPERT_GROUPS):
        a, b_, c, d = p[4 * g:4 * g + 4]
        hi1, lo1 = jnp.maximum(a, b_), jnp.minimum(a, b_)
        hi2, lo2 = jnp.maximum(c, d), jnp.minimum(c, d)
        gs = jnp.maximum(hi1, hi2) + jnp.maximum(jnp.minimum(hi1, hi2), jnp.maximum(lo1, lo2))
        if g == 0:
            gbest, gsel = gs, jnp.zeros(gs.shape, jnp.int32)
        else:
            better = gs > gbest
            gbest = jnp.where(better, gs, gbest)
            gsel = jnp.where(better, g, gsel)
    cand = [jnp.where(gsel == j // EXPERTS_PER_GROUP, p[j], -1.0) for j in range(N_EXPERTS)]
    v1 = cand[0]
    i1 = jnp.zeros(v1.shape, jnp.int32)
    for j in range(1, N_EXPERTS):
        better = cand[j] > v1
        v1 = jnp.where(better, cand[j], v1)
        i1 = jnp.where(better, j, i1)
    v2 = jnp.full(v1.shape, -2.0, F32)
    i2 = jnp.zeros(v1.shape, jnp.int32)
    for j in range(N_EXPERTS):
        better = (cand[j] > v2) & (i1 != j)
        v2 = jnp.where(better, cand[j], v2)
        i2 = jnp.where(better, j, i2)
    tot = v1 + v2

    e_i = lax.broadcasted_iota(jnp.int32, (N_EXPERTS, tm), 0)
    pick1 = e_i == i1
    pick2 = e_i == i2
    onehot = (pick1 | pick2).astype(F32)
    t_r = lax.broadcasted_iota(jnp.int32, (tm, tm), 0)
    t_c = lax.broadcasted_iota(jnp.int32, (tm, tm), 1)
    earlier = (t_r < t_c).astype(BF16)
    before = _dot(onehot.astype(BF16), earlier) + run_ref[:, 0:1]
    rank1 = jnp.sum(jnp.where(pick1, before, 0.0), axis=0, keepdims=True)
    rank2 = jnp.sum(jnp.where(pick2, before, 0.0), axis=0, keepdims=True)
    run = run_ref[...] + jnp.sum(onehot, axis=1, keepdims=True)
    run_ref[...] = run
    cnt_ref[...] = run

    row = lax.broadcasted_iota(jnp.int32, o_ref.shape, 0)
    vals = (i1.astype(F32), i2.astype(F32), v1 / tot, v2 / tot, rank1, rank2)
    out = jnp.zeros(o_ref.shape, F32)
    for r, v in enumerate(vals):
        out = jnp.where(row == r, v, out)
    o_ref[...] = out


def router(x, x_tail, w_t, b, *, tm, name):
    m = x.shape[0]
    tiles_per_b = 0 if x_tail is None else SEQ // tm
    if x_tail is None:
        x_tail, tail_spec = x, pl.BlockSpec((min(CHUNK, tm), D_MODEL), lambda i: (0, 0))
    else:
        tail_spec = pl.BlockSpec((CHUNK, D_MODEL), lambda i: (i // tiles_per_b, 0))
    return pl.pallas_call(
        functools.partial(_router_kernel, tiles_per_b=tiles_per_b),
        grid=(m // tm,),
        in_specs=[pl.BlockSpec((tm, D_MODEL), lambda i: (i, 0)),
                  tail_spec,
                  pl.BlockSpec((N_EXPERTS, D_MODEL), lambda i: (0, 0)),
                  pl.BlockSpec((N_EXPERTS, 1), lambda i: (0, 0))],
        out_specs=[pl.BlockSpec((8, tm), lambda i: (0, i)),
                   pl.BlockSpec((N_EXPERTS, 128), lambda i: (0, 0))],
        out_shape=[jax.ShapeDtypeStruct((8, m), F32),
                   jax.ShapeDtypeStruct((N_EXPERTS, 128), F32)],
        scratch_shapes=[pltpu.VMEM((N_EXPERTS, 128), F32)],
        compiler_params=_cparams(1, m),
        name=name,
    )(x, x_tail, w_t, b)


def _dispatch_kernel(d0_ref, d1_ref, starts_ref, ends_ref, nv_ref, x_ref, o_hbm, zero_ref, sem, *, cfg):
    i = pl.program_id(0)
    tm = cfg.tok_tile

    def fill_copy(row0):
        return pltpu.make_async_copy(zero_ref, o_hbm.at[pl.ds(row0, cfg.tile)], sem)

    @pl.when(i == 0)
    def _():
        zero_ref[...] = jnp.zeros(zero_ref.shape, zero_ref.dtype)
        for wait in (False, True):
            for e in range(N_EXPERTS):
                @pl.when(ends_ref[e] > starts_ref[e])
                def _():
                    cp = fill_copy(pl.multiple_of(ends_ref[e] - cfg.tile, cfg.tile))
                    cp.wait() if wait else cp.start()

            def tail(t, carry):
                cp = fill_copy(pl.multiple_of(t * cfg.tile, cfg.tile))
                cp.wait() if wait else cp.start()
                return carry

            lax.fori_loop(nv_ref[0], cfg.n_tiles, tail, 0)

    base = i * tm

    def body(it, carry):
        for j in range(ROW_DMA_UNROLL):
            r = it * ROW_DMA_UNROLL + j
            pltpu.make_async_copy(x_ref.at[pl.ds(r, 1)], o_hbm.at[pl.ds(d0_ref[base + r], 1)], sem).start(priority=0)
            pltpu.make_async_copy(x_ref.at[pl.ds(r, 1)], o_hbm.at[pl.ds(d1_ref[base + r], 1)], sem).start(priority=1)
        return carry

    lax.fori_loop(0, tm // ROW_DMA_UNROLL, body, 0)
    for _ in range(MOE_TOPK):
        pltpu.make_async_copy(x_ref, o_hbm.at[pl.ds(0, tm)], sem).wait()


def moe_dispatch(d0, d1, starts, ends, n_valid, x, cfg, name):
    tm = cfg.tok_tile
    return pl.pallas_call(
        functools.partial(_dispatch_kernel, cfg=cfg),
        grid_spec=pltpu.PrefetchScalarGridSpec(
            num_scalar_prefetch=5,
            grid=(cfg.n_tok // tm,),
            in_specs=[pl.BlockSpec((tm, D_MODEL), lambda i, *_: (i, 0))],
            out_specs=pl.BlockSpec(memory_space=pl.ANY),
            scratch_shapes=[pltpu.VMEM((cfg.tile, D_MODEL), F32), pltpu.SemaphoreType.DMA]),
        out_shape=jax.ShapeDtypeStruct((cfg.rows, D_MODEL), F32),
        compiler_params=_cparams(1, cfg.n_tok),
        name=name,
    )(d0, d1, starts, ends, n_valid, x)


def _split_expert_weights(w_refs, scratch_refs):
    for k, w_ref in enumerate(w_refs):
        hi, lo = _split_bf16(w_ref[...])
        scratch_refs[2 * k][...] = hi
        scratch_refs[2 * k + 1][...] = lo


def _expert_dot3(x, scratch_refs, k):
    xh, xl = _split_bf16(x)
    wh, wl = scratch_refs[2 * k][...], scratch_refs[2 * k + 1][...]
    return _dot(xh, wh) + (_dot(xl, wh) + _dot(xh, wl))


def _moe_up_kernel(te_ref, first_ref, nv_ref, x_ref, wg_ref, wu_ref, hh_ref, *scratch):
    i = pl.program_id(0)

    @pl.when(i < nv_ref[0])
    def _():
        _split_expert_weights((wg_ref, wu_ref), scratch)
        x = x_ref[...]
        hg = _expert_dot3(x, scratch, 0)
        hu = _expert_dot3(x, scratch, 1)
        hh_ref[...] = hg * jax.nn.sigmoid(hg) * hu

    @pl.when(i >= nv_ref[0])
    def _():
        hh_ref[...] = jnp.zeros(hh_ref.shape, hh_ref.dtype)


def _moe_down_kernel(te_ref, first_ref, nv_ref, hh_ref, wd_ref, y_ref, *scratch):
    i = pl.program_id(0)

    @pl.when(i < nv_ref[0])
    def _():
        @pl.when(first_ref[i] == 1)
        def _():
            _split_expert_weights((wd_ref,), scratch)

        y_ref[...] = _expert_dot3(hh_ref[...], scratch, 0)

    @pl.when(i >= nv_ref[0])
    def _():
        y_ref[...] = jnp.zeros(y_ref.shape, y_ref.dtype)


def _stream_expert_weights(i, te_ref, first_ref, nxt_ref, w_hbms, land_refs, bf16_refs, sem, layer):
    def copies(e):
        return [pltpu.make_async_copy(w.at[layer, e], land, sem.at[k])
                for k, (w, land) in enumerate(zip(w_hbms, land_refs))]

    @pl.when(i == 0)
    def _():
        for cp in copies(te_ref[0]):
            cp.start()

    @pl.when(first_ref[i] == 1)
    def _():
        for cp in copies(te_ref[i]):
            cp.wait()
        for land, dst in zip(land_refs, bf16_refs):
            dst[...] = land[...].astype(BF16)

        @pl.when(nxt_ref[i] >= 0)
        def _():
            for cp in copies(nxt_ref[i]):
                cp.start()


def _moe_up_stream_kernel(te_ref, first_ref, nxt_ref, nv_ref, x_ref, wg_hbm, wu_hbm, hh_ref,
                          land_g, land_u, wgb_ref, wub_ref, sem, *, layer):
    i = pl.program_id(0)

    @pl.when(i < nv_ref[0])
    def _():
        _stream_expert_weights(i, te_ref, first_ref, nxt_ref, (wg_hbm, wu_hbm), (land_g, land_u),
                               (wgb_ref, wub_ref), sem, layer)
        x = x_ref[...].astype(BF16)
        hg = _dot(x, wgb_ref[...])
        hu = _dot(x, wub_ref[...])
        hh_ref[...] = (hg * jax.nn.sigmoid(hg) * hu).astype(hh_ref.dtype)

    @pl.when(i >= nv_ref[0])
    def _():
        hh_ref[...] = jnp.zeros(hh_ref.shape, hh_ref.dtype)


def _moe_down_stream_kernel(te_ref, first_ref, nxt_ref, nv_ref, hh_ref, wd_hbm, y_ref, land_d, wdb_ref, sem,
                            *, layer):
    i = pl.program_id(0)

    @pl.when(i < nv_ref[0])
    def _():
        _stream_expert_weights(i, te_ref, first_ref, nxt_ref, (wd_hbm,), (land_d,), (wdb_ref,), sem, layer)
        y_ref[...] = _dot(hh_ref[...], wdb_ref[...])

    @pl.when(i >= nv_ref[0])
    def _():
        y_ref[...] = jnp.zeros(y_ref.shape, y_ref.dtype)


def moe_experts_streamed(tile_expert, tile_first, tile_next, n_valid, x_sorted, w_gate, w_up, w_down, layer, cfg,
                         name):
    tm = cfg.tile
    row_map = lambda i, te, fi, nx, nv: (jnp.minimum(i, nv[0] - 1), 0)
    out_map = lambda i, te, fi, nx, nv: (i, 0)
    hbm = pl.BlockSpec(memory_space=pl.ANY)
    hh = pl.pallas_call(
        functools.partial(_moe_up_stream_kernel, layer=layer),
        grid_spec=pltpu.PrefetchScalarGridSpec(
            num_scalar_prefetch=4,
            grid=(cfg.n_tiles,),
            in_specs=[pl.BlockSpec((tm, D_MODEL), row_map), hbm, hbm],
            out_specs=pl.BlockSpec((tm, D_EXPERT), out_map),
            scratch_shapes=[pltpu.VMEM((D_MODEL, D_EXPERT), F32), pltpu.VMEM((D_MODEL, D_EXPERT), F32),
                            pltpu.VMEM((D_MODEL, D_EXPERT), BF16), pltpu.VMEM((D_MODEL, D_EXPERT), BF16),
                            pltpu.SemaphoreType.DMA((2,))]),
        out_shape=jax.ShapeDtypeStruct((cfg.rows, D_EXPERT), BF16),
        compiler_params=_cparams(1, cfg.n_tok),
        name=name + "_up",
    )(tile_expert, tile_first, tile_next, n_valid, x_sorted, w_gate, w_up)
    return pl.pallas_call(
        functools.partial(_moe_down_stream_kernel, layer=layer),
        grid_spec=pltpu.PrefetchScalarGridSpec(
            num_scalar_prefetch=4,
            grid=(cfg.n_tiles,),
            in_specs=[pl.BlockSpec((tm, D_EXPERT), row_map), hbm],
            out_specs=pl.BlockSpec((tm, D_MODEL), out_map),
            scratch_shapes=[pltpu.VMEM((D_EXPERT, D_MODEL), F32), pltpu.VMEM((D_EXPERT, D_MODEL), BF16),
                            pltpu.SemaphoreType.DMA((1,))]),
        out_shape=jax.ShapeDtypeStruct((cfg.rows, D_MODEL), F32),
        compiler_params=_cparams(1, cfg.n_tok),
        name=name + "_down",
    )(tile_expert, tile_first, tile_next, n_valid, hh, w_down)


def moe_experts_precise(tile_expert, tile_first, n_valid, x_sorted, w_gate, w_up, w_down, layer, cfg, name):
    tm = cfg.tile
    n_copies = 2
    f_tiles = 2
    fw = D_EXPERT // f_tiles
    row_map = lambda i, te, fi, nv: (jnp.minimum(i, nv[0] - 1), 0)
    wmap = lambda i, te, fi, nv: (layer, te[i], 0, 0)
    hh = pl.pallas_call(
        _moe_up_kernel,
        grid_spec=pltpu.PrefetchScalarGridSpec(
            num_scalar_prefetch=3,
            grid=(cfg.n_tiles, f_tiles),
            in_specs=[pl.BlockSpec((tm, D_MODEL), lambda i, f, te, fi, nv: (jnp.minimum(i, nv[0] - 1), 0)),
                      pl.BlockSpec((None, None, D_MODEL, fw), lambda i, f, te, fi, nv: (layer, te[i], 0, f)),
                      pl.BlockSpec((None, None, D_MODEL, fw), lambda i, f, te, fi, nv: (layer, te[i], 0, f))],
            out_specs=pl.BlockSpec((tm, fw), lambda i, f, te, fi, nv: (i, f)),
            scratch_shapes=[pltpu.VMEM((D_MODEL, fw), BF16)] * (2 * n_copies)),
        out_shape=jax.ShapeDtypeStruct((cfg.rows, D_EXPERT), F32),
        compiler_params=_cparams(2, cfg.n_tok),
        name=name + "_up",
    )(tile_expert, tile_first, n_valid, x_sorted, w_gate, w_up)
    return pl.pallas_call(
        _moe_down_kernel,
        grid_spec=pltpu.PrefetchScalarGridSpec(
            num_scalar_prefetch=3,
            grid=(cfg.n_tiles,),
            in_specs=[pl.BlockSpec((tm, D_EXPERT), row_map),
                      pl.BlockSpec((None, None, D_EXPERT, D_MODEL), wmap)],
            out_specs=pl.BlockSpec((tm, D_MODEL), lambda i, te, fi, nv: (i, 0)),
            scratch_shapes=[pltpu.VMEM((D_EXPERT, D_MODEL), BF16)] * n_copies),
        out_shape=jax.ShapeDtypeStruct((cfg.rows, D_MODEL), F32),
        compiler_params=_cparams(1, cfg.n_tok),
        name=name + "_down",
    )(tile_expert, tile_first, n_valid, hh, w_down)


def _combine_kernel(d0_ref, d1_ref, x_ref, w_ref, g_ref, b_ref, y_hbm, o_ref, buf_ref, sem, *, tm):
    i = pl.program_id(0)
    n_steps = pl.num_programs(0)

    def gather(tile, half, start):
        base = tile * tm

        def body(it, carry):
            for j in range(ROW_DMA_UNROLL):
                r = it * ROW_DMA_UNROLL + j
                pltpu.make_async_copy(y_hbm.at[pl.ds(d0_ref[base + r], 1)], buf_ref.at[half, 0, pl.ds(r, 1)],
                                      sem.at[half]).start(priority=0)
                pltpu.make_async_copy(y_hbm.at[pl.ds(d1_ref[base + r], 1)], buf_ref.at[half, 1, pl.ds(r, 1)],
                                      sem.at[half]).start(priority=1)
            return carry

        if start:
            lax.fori_loop(0, tm // ROW_DMA_UNROLL, body, 0)
        else:
            for k in range(MOE_TOPK):
                pltpu.make_async_copy(y_hbm.at[pl.ds(0, tm)], buf_ref.at[half, k], sem.at[half]).wait()

    @pl.when(i == 0)
    def _():
        gather(0, 0, True)

    @pl.when(i + 1 < n_steps)
    def _():
        gather(i + 1, (i + 1) % 2, True)

    half = i % 2
    gather(i, half, False)
    w = w_ref[...]
    y = ALPHA * x_ref[...] + w[:, 0:1] * buf_ref[half, 0] + w[:, 1:2] * buf_ref[half, 1]
    o_ref[...] = _ln(y, g_ref[...], b_ref[...])


def moe_combine_ln(d0, d1, x, w_tok, g, b, y_sorted, *, tm, name):
    m = x.shape[0]
    return pl.pallas_call(
        functools.partial(_combine_kernel, tm=tm),
        grid_spec=pltpu.PrefetchScalarGridSpec(
            num_scalar_prefetch=2,
            grid=(m // tm,),
            in_specs=[pl.BlockSpec((tm, D_MODEL), lambda i, a, c: (i, 0)),
                      pl.BlockSpec((tm, MOE_TOPK), lambda i, a, c: (i, 0)),
                      pl.BlockSpec((1, D_MODEL), lambda i, a, c: (0, 0)),
                      pl.BlockSpec((1, D_MODEL), lambda i, a, c: (0, 0)),
                      pl.BlockSpec(memory_space=pl.ANY)],
            out_specs=pl.BlockSpec((tm, D_MODEL), lambda i, a, c: (i, 0)),
            scratch_shapes=[pltpu.VMEM((2, MOE_TOPK, tm, D_MODEL), F32), pltpu.SemaphoreType.DMA((2,))]),
        out_shape=jax.ShapeDtypeStruct((m, D_MODEL), F32),
        compiler_params=_cparams(1, m),
        name=name,
    )(d0, d1, x, w_tok, g, b, y_sorted)


def _routing_plan(route, counts, cfg):
    counts = counts[:, 0].astype(jnp.int32)
    padded = ((counts + cfg.tile - 1) // cfg.tile) * cfg.tile
    ends = jnp.cumsum(padded)
    starts = ends - padded
    experts = jnp.arange(N_EXPERTS, dtype=jnp.int32)[:, None]

    def slot_rows(k):
        picked = route[k].astype(jnp.int32)[None, :] == experts
        return jnp.sum(jnp.where(picked, starts[:, None], 0), axis=0) + route[4 + k].astype(jnp.int32)

    tile_start = jnp.arange(cfg.n_tiles, dtype=jnp.int32) * cfg.tile
    tile_expert = jnp.minimum(jnp.sum((tile_start[:, None] >= ends[None, :]).astype(jnp.int32), axis=1),
                              N_EXPERTS - 1)
    start_of_tile_expert = jnp.sum(jnp.where(tile_expert[:, None] == experts.T, starts[None, :], 0), axis=1)
    tile_first = (tile_start == start_of_tile_expert).astype(jnp.int32)
    n_valid = (ends[-1] // cfg.tile).astype(jnp.int32).reshape(1)
    later_nonempty = (experts.T > tile_expert[:, None]) & (padded[None, :] > 0)
    tile_next = jnp.min(jnp.where(later_nonempty, experts.T, N_EXPERTS), axis=1)
    tile_next = jnp.where(tile_next == N_EXPERTS, -1, tile_next).astype(jnp.int32)
    return (slot_rows(0), slot_rows(1), route[2:4].T, starts, ends, tile_expert, tile_first, tile_next, n_valid)


def moe_ln(x, x_route, layer, p, cfg, name):
    route, cnt = router(x, x_route, p["router_w_t"], p["router_b"], tm=cfg.tok_tile, name=name + "_router")
    d0, d1, w_tok, starts, ends, tile_expert, tile_first, tile_next, n_valid = _routing_plan(route, cnt, cfg)
    x_sorted = moe_dispatch(d0, d1, starts, ends, n_valid, x, cfg, name + "_dispatch")
    weights = (p["moe_w_gate"], p["moe_w_up"], p["moe_w_down"])
    if cfg.precise:
        y_sorted = moe_experts_precise(tile_expert, tile_first, n_valid, x_sorted, *weights, layer, cfg, name)
    else:
        y_sorted = moe_experts_streamed(tile_expert, tile_first, tile_next, n_valid, x_sorted, *weights, layer,
                                        cfg, name)
    return moe_combine_ln(d0, d1, x, w_tok, p["moe_ln_g"][layer], p["moe_ln_b"][layer], y_sorted,
                          tm=cfg.tok_tile, name=name + "_combine")


def _rope_tables(pos):
    half = ROT_DIM // 2
    inv = jnp.power(ROPE_THETA, -jnp.arange(half, dtype=F32) / half)
    ang = pos.astype(F32)[:, None] * inv[None, :]
    cos, sin = jnp.cos(ang), jnp.sin(ang)
    n = pos.shape[0]
    ones = jnp.ones((n, HEAD_DIM - ROT_DIM), F32)
    zeros_h = jnp.zeros((n, half), F32)
    zeros_r = jnp.zeros((n, HEAD_DIM - ROT_DIM), F32)
    c = jnp.concatenate([cos, cos, ones], axis=1)
    sa = jnp.concatenate([zeros_h, sin, zeros_r], axis=1)
    sb = jnp.concatenate([-sin, zeros_h, zeros_r], axis=1)
    return c, sa, sb


def kernel(x_prompt, x_sample, mem_prompt, cache_c_k, cache_c_v, page_table, state_b_buf, state_d_buf, cache_mem_k, cache_mem_v, ab_w_in, ab_a_ln_g, ab_a_ln_b, ab_a_ws, ab_a_bs, ab_b_wg, ab_b_scale, ab_w_out, cd_w_in, cd_d_conv_w, cd_d_conv_b, cd_d_ln_g, cd_d_ln_b, cd_w_out, mix_ln_g, mix_ln_b, mem_w_q, mem_w_k, mem_w_v, mem_w_o, mem_ln_g, mem_ln_b, router_w, router_b, moe_w_gate, moe_w_up, moe_w_down, moe_ln_g, moe_ln_b):
    row = lambda v: v.reshape(1, -1)
    xp = x_prompt.reshape(N_PROMPT, D_MODEL)
    xs = x_sample.reshape(N_SAMPLE, D_MODEL)
    moe_p = {"router_w_t": router_w.T, "router_b": router_b.reshape(N_EXPERTS, 1),
             "moe_w_gate": moe_w_gate, "moe_w_up": moe_w_up,
             "moe_w_down": moe_w_down, "moe_ln_g": [row(moe_ln_g[l]) for l in range(DEPTH)],
             "moe_ln_b": [row(moe_ln_b[l]) for l in range(DEPTH)]}

    w_kv = split_weight(jnp.concatenate([mem_w_k[0], mem_w_k[1], mem_w_v[0], mem_w_v[1]], axis=1), True)
    n_memrows = BATCH * N_MEM
    kv = matmul(mem_prompt.reshape(n_memrows, D_MODEL), w_kv, m=n_memrows, tm=256, tn=1024, precise=True,
                name="mem_kv")
    pmk = [kv[:, l * MEM_WIDTH:(l + 1) * MEM_WIDTH].reshape(BATCH, N_MEM, MEM_WIDTH) for l in range(DEPTH)]
    pmv = [kv[:, (DEPTH + l) * MEM_WIDTH:(DEPTH + l + 1) * MEM_WIDTH].reshape(BATCH, N_MEM, MEM_WIDTH)
           for l in range(DEPTH)]
    smk = cache_mem_k.reshape(DEPTH, DEC_BATCH, N_MEM, MEM_WIDTH)
    smv = cache_mem_v.reshape(DEPTH, DEC_BATCH, N_MEM, MEM_WIDTH)

    mem_w = [(split_weight(mem_w_q[l], True), split_weight(mem_w_o[l], True)) for l in range(DEPTH)]
    mem_ln = [(row(mem_ln_g[l]), row(mem_ln_b[l])) for l in range(DEPTH)]

    def memattn_prompt(x, layer):
        wq, wo = split_weight(mem_w_q[layer], False), split_weight(mem_w_o[layer], False)
        return memory_attention_ln(x, wq, pmk[layer], pmv[layer], wo, *mem_ln[layer], m=N_PROMPT, tm=512,
                                   rows_per_b=SEQ, name="memattn_prompt")

    def memattn_sample(x, layer):
        wq, wo = mem_w[layer]
        return memory_attention_ln(x, wq, smk[layer], smv[layer], wo, *mem_ln[layer], m=N_SAMPLE, tm=N_SAMPLE,
                                   rows_per_b=DEC_SEQ, precise=True, name="memattn_sample")

    w_in0 = split_weight(ab_w_in[0], True)
    w_out0 = split_weight(ab_w_out[0], True)
    ab_prm = {"ln_g": row(ab_a_ln_g[0]), "ln_b": row(ab_a_ln_b[0]), "ws": ab_a_ws[0], "bs_t": ab_a_bs[0].T,
              "wg": ab_b_wg[0].astype(BF16), "scale": row(ab_b_scale[0])}
    ab_prm_precise = dict(ab_prm, wg=ab_b_wg[0])
    g, b = row(mix_ln_g[0]), row(mix_ln_b[0])
    n_chunks = SEQ // CHUNK
    n_tail = BATCH * CHUNK

    h0p = matmul(xp, split_weight(ab_w_in[0], False), m=N_PROMPT, tm=1024, tn=1024, n_gelu=2, name="ab_in_prompt")
    mix_p, _ = mixer_ab(h0p, h0p, lambda b, c: (jnp.maximum(b * (SEQ // 16) + c * (CHUNK // 16) - 1, 0), 2),
                        nb=BATCH, n_chunks=n_chunks, pos0=0, has_ctx=False, prm=ab_prm, name="mixer_ab_prompt")
    x1p = matmul_res_ln([mix_p], [split_weight(ab_w_out[0], False)], xp, g, b, m=N_PROMPT, tm=512,
                        name="ab_out_prompt")
    x1p = memattn_prompt(x1p, 0)

    xt = x_prompt[:, SEQ - CHUNK:].reshape(n_tail, D_MODEL)
    h0t = matmul(xt, w_in0, m=n_tail, tm=CHUNK, tn=1024, n_gelu=2, precise=True, name="ab_in_tail")
    mix_t, _ = mixer_ab(h0t, h0p, lambda b, c: ((b + 1) * (SEQ // 16) - CHUNK // 16 - 1, 2), nb=BATCH, n_chunks=1,
                        pos0=SEQ - CHUNK, has_ctx=True, prm=ab_prm_precise, precise=True, name="mixer_ab_tail")
    x1t = matmul_res_ln([mix_t], [w_out0], xt, g, b, m=n_tail, tm=CHUNK, precise=True, name="ab_out_tail")
    wq, wo = mem_w[0]
    x1t = memory_attention_ln(x1t, wq, pmk[0], pmv[0], wo, *mem_ln[0], m=n_tail, tm=CHUNK, rows_per_b=CHUNK,
                              precise=True, name="memattn_tail")

    h0s = matmul(xs, w_in0, m=N_SAMPLE, tm=N_SAMPLE, tn=1024, n_gelu=2, precise=True, name="ab_in_sample")
    h0s_pad = jnp.pad(h0s.reshape(DEC_BATCH, DEC_SEQ, -1), ((0, 0), (0, CHUNK - DEC_SEQ), (0, 0)))
    h0s_pad = h0s_pad.reshape(DEC_BATCH * CHUNK, -1)
    zctx = jnp.pad(state_b_buf[0], ((0, 0), (16 - B_BUF, 0), (0, 0))).reshape(DEC_BATCH * 16, B_WIDTH)
    mix_s, vn_s = mixer_ab(h0s_pad, zctx, lambda b, c: (b, 0), nb=DEC_BATCH, n_chunks=1, pos0=PAST_LEN,
                           has_ctx=True, prm=ab_prm_precise, precise=True, name="mixer_ab_sample")
    mix_s = mix_s.reshape(DEC_BATCH, CHUNK, -1)[:, :DEC_SEQ].reshape(N_SAMPLE, -1)
    x1s = matmul_res_ln([mix_s], [w_out0], xs, g, b, m=N_SAMPLE, tm=N_SAMPLE, precise=True, name="ab_out_sample")
    x1s = memattn_sample(x1s, 0)

    x2p = moe_ln(x1p, x1t, 0, moe_p, MOE_PROMPT, "moe_prompt")
    x2s = moe_ln(x1s, None, 0, moe_p, MOE_SAMPLE, "moe_sample")

    w_in1 = split_weight(cd_w_in[0], True)
    tabs_p = _rope_tables(jnp.arange(SEQ, dtype=jnp.int32))
    tabs_s = _rope_tables(PAST_LEN + (jnp.arange(N_SAMPLE, dtype=jnp.int32) % DEC_SEQ))
    h1p = matmul_rope(x2p, cd_w_in[0].astype(BF16), tabs_p, m=N_PROMPT, tm=1024, tn=C_WIDTH, n_rope=2,
                      tab_blocks=SEQ // 1024, name="cd_in_prompt")
    h1s = matmul(x2s, w_in1, m=N_SAMPLE, tm=N_SAMPLE, tn=1024, precise=True, name="cd_in_sample")
    qr_s, kr_s = rope_qk(h1s, tabs_s, m=N_SAMPLE, tm=N_SAMPLE, xoff=0, tab_blocks=1, name="rope_sample")
    c_p = moba_prompt(h1p, nb=BATCH, seq=SEQ)
    v_s = h1s[:, 2 * C_WIDTH:3 * C_WIDTH]
    q_combo = qr_s.reshape(DEC_BATCH, N_COMBO, HEAD_DIM)
    m_p, l_p, o_p, ksum = paged_partials(page_table, q_combo, cache_c_k, cache_c_v, 0)
    c_s = paged_merge(q_combo, kr_s.reshape(DEC_BATCH, N_COMBO, HEAD_DIM),
                      v_s.reshape(DEC_BATCH, N_COMBO, HEAD_DIM), m_p, l_p, o_p, ksum)
    c_s = c_s.reshape(N_SAMPLE, C_WIDTH)

    cd_prm = {"conv_w": cd_d_conv_w[0], "conv_b": row(cd_d_conv_b[0]), "ln_g": row(cd_d_ln_g[0]),
              "ln_b": row(cd_d_ln_b[0])}
    col_a = 3 * C_WIDTH // D_WIDTH
    tiles_b = SEQ // 256
    halo_per_tile = 256 // CONV_HALO

    def prev_map(col):
        return lambda b, t: (jnp.maximum((b * tiles_b + t) * halo_per_tile - 1, 0), col)

    d_p, tail_p = conv_module(h1p, col_a, h1p, h1p, prev_map(col_a), prev_map(col_a + 1), nb=BATCH,
                              n_tiles=tiles_b, tm=256, rt=128, tail=CONV_HALO, xoff=0, prev_is_state=False,
                              prm=cd_prm, name="conv_prompt")
    gl_s = jnp.pad(h1s[:, 3 * C_WIDTH:].reshape(DEC_BATCH, DEC_SEQ, 2 * D_WIDTH), ((0, 0), (0, 8 - DEC_SEQ), (0, 0)))
    gl_s = gl_s.reshape(DEC_BATCH * 8, 2 * D_WIDTH)
    dctx = jnp.pad(state_d_buf[0], ((0, 0), (CONV_HALO - D_BUF, 0), (0, 0))).reshape(DEC_BATCH * CONV_HALO, D_WIDTH)
    d_s, tail_s = conv_module(gl_s, 0, dctx, dctx, lambda b, t: (b, 0), lambda b, t: (b, 0), nb=DEC_BATCH,
                              n_tiles=1, tm=8, rt=8, tail=8, xoff=0, prev_is_state=True, prm=cd_prm,
                              name="conv_sample", out_dtype=F32)
    d_s = d_s.reshape(DEC_BATCH, 8, D_WIDTH)[:, :DEC_SEQ].reshape(N_SAMPLE, D_WIDTH)
    w_out1c = split_weight(cd_w_out[0][:C_WIDTH], True)
    w_out1d = split_weight(cd_w_out[0][C_WIDTH:], True)
    g, b = row(mix_ln_g[1]), row(mix_ln_b[1])
    w_out1 = cd_w_out[0].astype(BF16)
    x3p = matmul_res_ln([c_p, d_p], [(w_out1[:C_WIDTH],), (w_out1[C_WIDTH:],)], x2p, g, b, m=N_PROMPT, tm=512,
                        name="cd_out_prompt")
    x3s = matmul_res_ln([c_s, d_s], [w_out1c, w_out1d], x2s, g, b, m=N_SAMPLE, tm=N_SAMPLE, precise=True,
                        name="cd_out_sample")
    y_p = moe_ln(memattn_prompt(x3p, 1), None, 1, moe_p, MOE_PROMPT, "moe_prompt")
    y_s = moe_ln(memattn_sample(x3s, 1), None, 1, moe_p, MOE_SAMPLE, "moe_sample")

    kv_shape_p = (1, BATCH, SEQ, C_HEADS, HEAD_DIM)
    kv_shape_s = (1, DEC_BATCH, DEC_SEQ, C_HEADS, HEAD_DIM)
    z_p = h0p[:, 2 * A_WIDTH:].reshape(BATCH, SEQ, B_WIDTH)
    z_s = h0s[:, 2 * A_WIDTH:].reshape(DEC_BATCH, DEC_SEQ, B_WIDTH)
    h_s = tail_s.reshape(DEC_BATCH, 8, D_WIDTH)[:, :DEC_SEQ]
    mem_shape = (BATCH, N_MEM, MEM_HEADS, MEM_HEAD_DIM)
    return (y_p.reshape(BATCH, SEQ, D_MODEL),
            y_s.reshape(DEC_BATCH, DEC_SEQ, D_MODEL),
            h1p[:, C_WIDTH:2 * C_WIDTH].reshape(kv_shape_p),
            h1p[:, 2 * C_WIDTH:3 * C_WIDTH].reshape(kv_shape_p),
            kr_s.reshape(kv_shape_s),
            v_s.reshape(kv_shape_s),
            z_p[:, SEQ - B_BUF:][None],
            jnp.concatenate([state_b_buf[0], z_s], axis=1)[:, DEC_SEQ:][None],
            vn_s.reshape(DEC_BATCH, CHUNK, A_WIDTH)[:, :DEC_SEQ][None],
            tail_p.reshape(BATCH, CONV_HALO, D_WIDTH)[:, CONV_HALO - D_BUF:][None],
            jnp.concatenate([state_d_buf[0], h_s], axis=1)[:, DEC_SEQ:][None],
            jnp.stack([m_.reshape(mem_shape) for m_ in pmk]),
            jnp.stack([m_.reshape(mem_shape) for m_ in pmv]))
```

```python
import functools
import math
from typing import NamedTuple

import jax
import jax.numpy as jnp
from jax import lax
from jax.experimental import pallas as pl
from jax.experimental.pallas import tpu as pltpu

F32 = jnp.float32
BF16 = jnp.bfloat16

D_MODEL = 2048
BATCH = 4
SEQ = 2048
DEPTH = 2
DEC_BATCH = 8
DEC_SEQ = 4
PAST_LEN = 16384
PAGE_SIZE = 128
A_WIDTH = 1024
CHUNK = 128
A_GROUPS = 8
B_WIDTH = 1024
POOL_WINDOWS = (2, 4, 8, 16)
B_GROUP_DIM = B_WIDTH // len(POOL_WINDOWS)
B_BUF = 15
C_HEADS = 8
HEAD_DIM = 128
C_WIDTH = 1024
ROT_DIM = 32
ROPE_THETA = 500000.0
MOBA_BLOCK = 256
MOBA_TOPK = 3
D_WIDTH = 1024
CONV_WIDTH = 31
D_BUF = 30
N_MEM = 256
MEM_HEADS = 4
MEM_HEAD_DIM = 128
MEM_WIDTH = 512
N_EXPERTS = 16
N_EXPERT_GROUPS = 4
EXPERTS_PER_GROUP = 4
MOE_TOPK = 2
D_EXPERT = 1024
ALPHA = (2 * DEPTH) ** 0.25
LN_EPS = 1e-5

N_PROMPT = BATCH * SEQ
N_SAMPLE = DEC_BATCH * DEC_SEQ
N_PAST_BLOCKS = PAST_LEN // MOBA_BLOCK
PAGES_PER_BLOCK = MOBA_BLOCK // PAGE_SIZE
N_COMBO = DEC_SEQ * C_HEADS


class MoeCfg(NamedTuple):
    n_tok: int
    tok_tile: int
    tile: int
    rows: int
    n_tiles: int
    precise: bool


def _moe_cfg(n_tok, tile, precise):
    rows = ((n_tok * MOE_TOPK + N_EXPERTS * (tile - 1)) // tile + 1) * tile
    return MoeCfg(n_tok, tile, tile, rows, rows // tile, precise)


MOE_PROMPT = _moe_cfg(N_PROMPT, 256, False)
MOE_SAMPLE = _moe_cfg(N_SAMPLE, N_SAMPLE, True)

ROW_DMA_UNROLL = 8
VMEM_LIMIT = 56 * 1024 * 1024
VMEM_LIMIT_SMALL = 40 * 1024 * 1024
SMALL_CALL_ROWS = 1024


def _cparams(n_axes, rows=None):
    vmem = VMEM_LIMIT if rows is None or rows > SMALL_CALL_ROWS else VMEM_LIMIT_SMALL
    return pltpu.CompilerParams(dimension_semantics=("arbitrary",) * n_axes, vmem_limit_bytes=vmem)


def _ln(x, g, b):
    mu = jnp.mean(x, axis=-1, keepdims=True)
    xc = x - mu
    var = jnp.mean(xc * xc, axis=-1, keepdims=True)
    return xc * lax.rsqrt(var + LN_EPS) * g + b


def _dot(a, b):
    return jnp.dot(a, b, preferred_element_type=F32)


def _dot_nt(a, b):
    return lax.dot_general(a, b, (((1,), (1,)), ((), ())), preferred_element_type=F32)


def _split_bf16(v):
    hi = v.astype(BF16)
    return hi, (v.astype(F32) - hi.astype(F32)).astype(BF16)


def _mm(a, b, precise, nt=False):
    dot = _dot_nt if nt else _dot
    if not precise:
        b0 = b[0] if isinstance(b, tuple) else b
        return dot(a.astype(BF16), b0.astype(BF16))
    ah, al = _split_bf16(a)
    bh, bl = b if isinstance(b, tuple) else _split_bf16(b)
    return dot(ah, bh) + (dot(al, bh) + dot(ah, bl))


def _load_all(refs):
    return tuple(r[...] for r in refs)


def split_weight(w, precise):
    if not precise:
        return (w.astype(BF16),)
    bits = lax.bitcast_convert_type(w, jnp.uint32) & jnp.uint32(0xFFFF0000)
    hi = lax.bitcast_convert_type(bits, F32)
    return hi.astype(BF16), (w - hi).astype(BF16)


def _mm_kernel(x_ref, *refs, n_gelu, precise):
    o_ref = refs[-1]
    acc = _mm(x_ref[...], _load_all(refs[:-1]), precise)
    if n_gelu:
        j = pl.program_id(0)

        @pl.when(j < n_gelu)
        def _():
            o_ref[...] = jax.nn.gelu(acc).astype(o_ref.dtype)

        @pl.when(j >= n_gelu)
        def _():
            o_ref[...] = acc.astype(o_ref.dtype)
    else:
        o_ref[...] = acc.astype(o_ref.dtype)


def matmul(x, w, *, m, tm, tn, xoff=0, n_gelu=0, precise=False, name="mm"):
    k = x.shape[1]
    n = w[0].shape[1]
    return pl.pallas_call(
        functools.partial(_mm_kernel, n_gelu=n_gelu, precise=precise),
        grid=(n // tn, m // tm),
        in_specs=[pl.BlockSpec((tm, k), lambda j, i: (i + xoff, 0))]
        + [pl.BlockSpec((k, tn), lambda j, i: (0, j))] * len(w),
        out_specs=pl.BlockSpec((tm, tn), lambda j, i: (i, j)),
        out_shape=jax.ShapeDtypeStruct((m, n), F32),
        compiler_params=_cparams(2, m),
        name=name,
    )(x, *w)


def _rope(x, c, sa, sb):
    half = ROT_DIM // 2
    return x * c + pltpu.roll(x, half, axis=1) * sa + pltpu.roll(x, HEAD_DIM - half, axis=1) * sb


def _mm_rope_kernel(x_ref, w_ref, c_ref, sa_ref, sb_ref, o_ref, *, n_rope):
    j = pl.program_id(0)
    x = x_ref[...].astype(BF16)

    @pl.when(j < n_rope)
    def _():
        c, sa, sb = c_ref[...], sa_ref[...], sb_ref[...]
        pair = 2 * HEAD_DIM
        for s0 in range(0, o_ref.shape[1], pair):
            acc = _dot(x, w_ref[:, s0:s0 + pair])
            for h0 in (0, HEAD_DIM):
                o_ref[:, s0 + h0:s0 + h0 + HEAD_DIM] = _rope(acc[:, h0:h0 + HEAD_DIM], c, sa, sb)

    @pl.when(j >= n_rope)
    def _():
        o_ref[...] = _dot(x, w_ref[...])


def matmul_rope(x, w, tabs, *, m, tm, tn, n_rope, tab_blocks, name):
    k = x.shape[1]
    n = w.shape[1]
    tmap = lambda j, i: (i % tab_blocks, 0)
    return pl.pallas_call(
        functools.partial(_mm_rope_kernel, n_rope=n_rope),
        grid=(n // tn, m // tm),
        in_specs=[pl.BlockSpec((tm, k), lambda j, i: (i, 0)),
                  pl.BlockSpec((k, tn), lambda j, i: (0, j))] + [pl.BlockSpec((tm, HEAD_DIM), tmap)] * 3,
        out_specs=pl.BlockSpec((tm, tn), lambda j, i: (i, j)),
        out_shape=jax.ShapeDtypeStruct((m, n), F32),
        compiler_params=_cparams(2, m),
        name=name,
    )(x, w, *tabs)


def _mm_res_ln_kernel(*refs, n_in, precise):
    n_w = 2 if precise else 1
    a_refs = refs[:n_in]
    w_refs = refs[n_in:n_in + n_in * n_w]
    r_ref, g_ref, b_ref, o_ref = refs[n_in + n_in * n_w:]
    acc = None
    for k, a_ref in enumerate(a_refs):
        d = _mm(a_ref[...], _load_all(w_refs[k * n_w:(k + 1) * n_w]), precise)
        acc = d if acc is None else acc + d
    o_ref[...] = _ln(ALPHA * r_ref[...] + acc, g_ref[...], b_ref[...])


def matmul_res_ln(a_list, w_list, resid, g, b, *, m, tm, roff=0, precise=False, name="mm_res_ln"):
    n_in = len(a_list)
    w_flat = [part for w in w_list for part in w]
    in_specs = [pl.BlockSpec((tm, a.shape[1]), lambda i: (i, 0)) for a in a_list]
    in_specs += [pl.BlockSpec(w.shape, lambda i: (0, 0)) for w in w_flat]
    in_specs += [pl.BlockSpec((tm, D_MODEL), lambda i: (i + roff, 0)),
                 pl.BlockSpec((1, D_MODEL), lambda i: (0, 0)),
                 pl.BlockSpec((1, D_MODEL), lambda i: (0, 0))]
    return pl.pallas_call(
        functools.partial(_mm_res_ln_kernel, n_in=n_in, precise=precise),
        grid=(m // tm,),
        in_specs=in_specs,
        out_specs=pl.BlockSpec((tm, D_MODEL), lambda i: (i, 0)),
        out_shape=jax.ShapeDtypeStruct((m, D_MODEL), F32),
        compiler_params=_cparams(1, m),
        name=name,
    )(*a_list, *w_flat, resid, g, b)


def _mixer_ab_kernel(u_ref, v_ref, z_ref, zp_ref, lng_ref, lnb_ref, ws_ref, bs_ref, wg_ref, sc_ref,
                     mix_ref, vn_ref, *, pos0, has_ctx, precise):
    c = pl.program_id(1)
    vn = _ln(v_ref[...], lng_ref[...], lnb_ref[...])
    vn_ref[...] = vn
    u = u_ref[...]
    row = lax.broadcasted_iota(jnp.int32, (CHUNK, CHUNK), 0)
    col = lax.broadcasted_iota(jnp.int32, (CHUNK, CHUNK), 1)
    causal = col <= row
    gd = A_WIDTH // A_GROUPS
    for g in range(A_GROUPS):
        sl = slice(g * gd, (g + 1) * gd)
        w = jnp.where(causal, ws_ref[g], 0.0)
        mixed = _mm(w, vn[:, sl], precise) + bs_ref[:, g:g + 1]
        mix_ref[:, sl] = (u[:, sl] * mixed).astype(mix_ref.dtype)

    z = z_ref[...]
    zp = zp_ref[...]
    if not has_ctx:
        zp = jnp.where(c == 0, 0.0, zp)
    zext = jnp.concatenate([zp, z], axis=0)
    pos = pos0 + c * CHUNK + lax.broadcasted_iota(jnp.int32, (CHUNK, 1), 0)
    for gi, wdw in enumerate(POOL_WINDOWS):
        sl = slice(gi * B_GROUP_DIM, (gi + 1) * B_GROUP_DIM)
        s = zext[:, sl]
        sh = 1
        while sh < wdw:
            s = s + pltpu.roll(s, sh, axis=0)
            sh *= 2
        cnt = jnp.minimum(wdw, pos + 1).astype(F32)
        d = s[16:, :] / cnt - z[:, sl]
        bo = _mm(d, wg_ref[gi], precise) * sc_ref[:, sl]
        mix_ref[:, A_WIDTH + gi * B_GROUP_DIM:A_WIDTH + (gi + 1) * B_GROUP_DIM] = bo.astype(mix_ref.dtype)


def mixer_ab(h, zprev, zprev_map, *, nb, n_chunks, pos0, has_ctx, prm, name, precise=False):
    m = nb * n_chunks * CHUNK
    row = lambda b, c: b * n_chunks + c
    const2 = lambda b, c: (0, 0)
    return pl.pallas_call(
        functools.partial(_mixer_ab_kernel, pos0=pos0, has_ctx=has_ctx, precise=precise),
        grid=(nb, n_chunks),
        in_specs=[pl.BlockSpec((CHUNK, A_WIDTH), lambda b, c: (row(b, c), 0)),
                  pl.BlockSpec((CHUNK, A_WIDTH), lambda b, c: (row(b, c), 1)),
                  pl.BlockSpec((CHUNK, B_WIDTH), lambda b, c: (row(b, c), 2)),
                  pl.BlockSpec((16, B_WIDTH), zprev_map),
                  pl.BlockSpec((1, A_WIDTH), const2),
                  pl.BlockSpec((1, A_WIDTH), const2),
                  pl.BlockSpec((A_GROUPS, CHUNK, CHUNK), lambda b, c: (0, 0, 0)),
                  pl.BlockSpec((CHUNK, A_GROUPS), const2),
                  pl.BlockSpec((len(POOL_WINDOWS), B_GROUP_DIM, B_GROUP_DIM), lambda b, c: (0, 0, 0)),
                  pl.BlockSpec((1, B_WIDTH), const2)],
        out_specs=[pl.BlockSpec((CHUNK, A_WIDTH + B_WIDTH), lambda b, c: (row(b, c), 0)),
                   pl.BlockSpec((CHUNK, A_WIDTH), lambda b, c: (row(b, c), 0))],
        out_shape=[jax.ShapeDtypeStruct((m, A_WIDTH + B_WIDTH), F32 if precise else BF16),
                   jax.ShapeDtypeStruct((m, A_WIDTH), F32)],
        compiler_params=_cparams(2, m),
        name=name,
    )(h, h, h, zprev, prm["ln_g"], prm["ln_b"], prm["ws"], prm["bs_t"], prm["wg"], prm["scale"])


def _rope_kernel(q_ref, k_ref, c_ref, sa_ref, sb_ref, qo_ref, ko_ref):
    c = c_ref[...]
    sa = sa_ref[...]
    sb = sb_ref[...]
    for h in range(C_HEADS):
        sl = slice(h * HEAD_DIM, (h + 1) * HEAD_DIM)
        for src, dst in ((q_ref, qo_ref), (k_ref, ko_ref)):
            dst[:, sl] = _rope(src[:, sl], c, sa, sb)


def rope_qk(h, tabs, *, m, tm, xoff, tab_blocks, name):
    tmap = lambda i: (i % tab_blocks, 0)
    return pl.pallas_call(
        _rope_kernel,
        grid=(m // tm,),
        in_specs=[pl.BlockSpec((tm, C_WIDTH), lambda i: (i + xoff, 0)),
                  pl.BlockSpec((tm, C_WIDTH), lambda i: (i + xoff, 1)),
                  pl.BlockSpec((tm, HEAD_DIM), tmap),
                  pl.BlockSpec((tm, HEAD_DIM), tmap),
                  pl.BlockSpec((tm, HEAD_DIM), tmap)],
        out_specs=[pl.BlockSpec((tm, C_WIDTH), lambda i: (i, 0)),
                   pl.BlockSpec((tm, C_WIDTH), lambda i: (i, 0))],
        out_shape=[jax.ShapeDtypeStruct((m, C_WIDTH), F32),
                   jax.ShapeDtypeStruct((m, C_WIDTH), F32)],
        compiler_params=_cparams(1, m),
        name=name,
    )(h, h, *tabs)


def _moba_kernel(q_ref, qall_ref, k_ref, v_ref, o_ref, selt_ref, *, n_blocks):
    i = pl.program_id(2)
    blk_rows = MOBA_BLOCK
    seq = n_blocks * blk_rows
    scale = HEAD_DIM ** -0.5

    @pl.when(i == 0)
    def _():
        kmean = jnp.concatenate(
            [jnp.mean(k_ref[j * blk_rows:(j + 1) * blk_rows, :], axis=0, keepdims=True) for j in range(n_blocks)],
            axis=0)
        bst = _mm(kmean, qall_ref[...], True, nt=True)
        blk = lax.broadcasted_iota(jnp.int32, (n_blocks, seq), 0)
        own = lax.broadcasted_iota(jnp.int32, (n_blocks, seq), 1) >> int(math.log2(blk_rows))
        work = jnp.where(blk < own, bst, -jnp.inf)
        sel = jnp.zeros((n_blocks, seq), F32)
        for _ in range(MOBA_TOPK):
            mx = jnp.max(work, axis=0, keepdims=True)
            first = jnp.min(jnp.where(work == mx, blk, n_blocks), axis=0, keepdims=True)
            hit = (blk == first) & (mx > -jnp.inf)
            sel = jnp.where(hit, 1.0, sel)
            work = jnp.where(hit, -jnp.inf, work)
        selt_ref[...] = jnp.zeros(selt_ref.shape, F32)
        for c in range(n_blocks):
            selt_ref[c, 0:n_blocks, :] = sel[:, c * blk_rows:(c + 1) * blk_rows]

    qb = q_ref[...].astype(BF16)
    sel = jnp.transpose(selt_ref[i])

    r_i = lax.broadcasted_iota(jnp.int32, (blk_rows, blk_rows), 0)
    c_i = lax.broadcasted_iota(jnp.int32, (blk_rows, blk_rows), 1)
    causal = (c_i <= r_i).astype(F32)

    for c in range(n_blocks):
        @pl.when(i == c)
        def _():
            nk = (c + 1) * blk_rows
            s = _dot_nt(qb, k_ref[0:nk, :].astype(BF16)) * scale
            allowed = jnp.concatenate(
                [jnp.broadcast_to(sel[:, j:j + 1], (blk_rows, blk_rows)) for j in range(c)] + [causal], axis=1)
            s = jnp.where(allowed > 0.0, s, -jnp.inf)
            m = jnp.max(s, axis=-1, keepdims=True)
            p = jnp.exp(s - m)
            l = jnp.sum(p, axis=-1, keepdims=True)
            o = _dot(p.astype(BF16), v_ref[0:nk, :].astype(BF16))
            o_ref[...] = (o / l).astype(o_ref.dtype)


def moba_prompt(h, *, nb, seq, name="moba_prompt"):
    n_blocks = seq // MOBA_BLOCK
    k_col0 = C_WIDTH // HEAD_DIM
    v_col0 = 2 * C_WIDTH // HEAD_DIM
    return pl.pallas_call(
        functools.partial(_moba_kernel, n_blocks=n_blocks),
        grid=(nb, C_HEADS, n_blocks),
        in_specs=[pl.BlockSpec((MOBA_BLOCK, HEAD_DIM), lambda b, hh, i: (b * n_blocks + i, hh)),
                  pl.BlockSpec((seq, HEAD_DIM), lambda b, hh, i: (b, hh)),
                  pl.BlockSpec((seq, HEAD_DIM), lambda b, hh, i: (b, k_col0 + hh)),
                  pl.BlockSpec((seq, HEAD_DIM), lambda b, hh, i: (b, v_col0 + hh))],
        out_specs=pl.BlockSpec((MOBA_BLOCK, HEAD_DIM), lambda b, hh, i: (b * n_blocks + i, hh)),
        out_shape=jax.ShapeDtypeStruct((nb * seq, C_WIDTH), BF16),
        scratch_shapes=[pltpu.VMEM((n_blocks, 128, MOBA_BLOCK), F32)],
        compiler_params=_cparams(3),
        name=name,
    )(h, h, h, h)


CONV_HALO = 32
SUBLANES = 8


def _conv_kernel(ga_ref, gg_ref, pa_ref, pg_ref, w_ref, bdw_ref, lng_ref, lnb_ref,
                 o_ref, tail_ref, ext_ref, sh_ref, y_ref, *, tm, rt, tail, prev_is_state):
    t = pl.program_id(1)
    hcur = ga_ref[...] * jax.nn.sigmoid(gg_ref[...])
    if prev_is_state:
        hprev = pa_ref[...]
    else:
        hprev = pa_ref[...] * jax.nn.sigmoid(pg_ref[...])
        hprev = jnp.where(t == 0, 0.0, hprev)
    ext_ref[0:CONV_HALO, :] = hprev
    ext_ref[CONV_HALO:CONV_HALO + tm, :] = hcur
    tail_ref[...] = hcur[tm - tail:, :]
    off = CONV_HALO - D_BUF
    sh_rows = sh_ref.shape[1]
    for s in range(1, SUBLANES):
        sh_ref[s - 1] = ext_ref[s:s + sh_rows, :]
    for cc in range(D_WIDTH // 128):
        cs = slice(cc * 128, (cc + 1) * 128)
        for rc in range(tm // rt):
            r0 = rc * rt
            acc = jnp.zeros((rt, 128), F32)
            for j in range(CONV_WIDTH):
                s, a = (off + j) % SUBLANES, (off + j) // SUBLANES
                row0 = r0 + a * SUBLANES
                src = ext_ref[row0:row0 + rt, cs] if s == 0 else sh_ref[s - 1, row0:row0 + rt, cs]
                acc = acc + w_ref[j:j + 1, cs] * src
            y_ref[r0:r0 + rt, cs] = acc + bdw_ref[:, cs]
    y = _ln(y_ref[...], lng_ref[...], lnb_ref[...])
    o_ref[...] = (y * jax.nn.sigmoid(y)).astype(o_ref.dtype)


def conv_module(h, col_a, prev_a, prev_g, prev_map_a, prev_map_g, *, nb, n_tiles, tm, rt, tail, xoff,
                prev_is_state, prm, name, out_dtype=BF16):
    m = nb * n_tiles * tm
    row = lambda b, t: b * n_tiles + t + xoff
    const2 = lambda b, t: (0, 0)
    return pl.pallas_call(
        functools.partial(_conv_kernel, tm=tm, rt=rt, tail=tail, prev_is_state=prev_is_state),
        grid=(nb, n_tiles),
        in_specs=[pl.BlockSpec((tm, D_WIDTH), lambda b, t: (row(b, t), col_a)),
                  pl.BlockSpec((tm, D_WIDTH), lambda b, t: (row(b, t), col_a + 1)),
                  pl.BlockSpec((CONV_HALO, D_WIDTH), prev_map_a),
                  pl.BlockSpec((CONV_HALO, D_WIDTH), prev_map_g),
                  pl.BlockSpec((CONV_WIDTH, D_WIDTH), const2),
                  pl.BlockSpec((1, D_WIDTH), const2),
                  pl.BlockSpec((1, D_WIDTH), const2),
                  pl.BlockSpec((1, D_WIDTH), const2)],
        out_specs=[pl.BlockSpec((tm, D_WIDTH), lambda b, t: (b * n_tiles + t, 0)),
                   pl.BlockSpec((tail, D_WIDTH), lambda b, t: (b, 0))],
        out_shape=[jax.ShapeDtypeStruct((m, D_WIDTH), out_dtype),
                   jax.ShapeDtypeStruct((nb * tail, D_WIDTH), F32)],
        scratch_shapes=[pltpu.VMEM((CONV_HALO + tm, D_WIDTH), F32),
                        pltpu.VMEM((SUBLANES - 1, CONV_HALO + tm - SUBLANES, D_WIDTH), F32),
                        pltpu.VMEM((tm, D_WIDTH), F32)],
        compiler_params=_cparams(2, m),
        name=name,
    )(h, h, prev_a, prev_g, prm["conv_w"], prm["conv_b"], prm["ln_g"], prm["ln_b"])


PAGED_BLOCKS_PER_STEP = 8
PAGED_PAGES_PER_STEP = PAGED_BLOCKS_PER_STEP * PAGES_PER_BLOCK


def _paged_partial_kernel(pt_ref, q_ref, *refs):
    del pt_ref
    pages = refs[:2 * PAGED_PAGES_PER_STEP]
    m_ref, l_ref, o_ref, ks_ref, kb_ref, vb_ref = refs[2 * PAGED_PAGES_PER_STEP:]
    rows = PAGE_SIZE * C_HEADS
    blk_keys = PAGES_PER_BLOCK * rows
    scale = HEAD_DIM ** -0.5
    k_refs = pages[:PAGED_PAGES_PER_STEP]
    v_refs = pages[PAGED_PAGES_PER_STEP:]
    for pg in range(PAGED_PAGES_PER_STEP):
        k = k_refs[pg][...]
        ksum = jnp.sum(k, axis=0)
        if pg % PAGES_PER_BLOCK == 0:
            ks = ksum
        else:
            ks = ks + ksum
        if pg % PAGES_PER_BLOCK == PAGES_PER_BLOCK - 1:
            ks_ref[pg // PAGES_PER_BLOCK] = ks
        kb_ref[pg * rows:(pg + 1) * rows, :] = k.reshape(rows, HEAD_DIM).astype(BF16)
        vb_ref[pg * rows:(pg + 1) * rows, :] = v_refs[pg][...].reshape(rows, HEAD_DIM).astype(BF16)
    s = _dot_nt(q_ref[...].astype(BF16), kb_ref[...]) * scale
    c_i = lax.broadcasted_iota(jnp.int32, s.shape, 0)
    l_i = lax.broadcasted_iota(jnp.int32, s.shape, 1)
    s = jnp.where((l_i & (C_HEADS - 1)) == (c_i & (C_HEADS - 1)), s, -jnp.inf)
    for blk in range(PAGED_BLOCKS_PER_STEP):
        cols = slice(blk * blk_keys, (blk + 1) * blk_keys)
        sb = s[:, cols]
        m = jnp.max(sb, axis=-1, keepdims=True)
        p = jnp.exp(sb - m)
        m_ref[blk] = jnp.broadcast_to(m, (N_COMBO, HEAD_DIM))
        l_ref[blk] = jnp.broadcast_to(jnp.sum(p, axis=-1, keepdims=True), (N_COMBO, HEAD_DIM))
        o_ref[blk] = _dot(p.astype(BF16), vb_ref[cols, :])


def paged_partials(page_table, q_combo, cache_k, cache_v, layer):
    page_block = (None, None, PAGE_SIZE, C_HEADS, HEAD_DIM)
    nbs = PAGED_BLOCKS_PER_STEP

    def page_map(which):
        return lambda b, n, pt: (layer, pt[b, PAGED_PAGES_PER_STEP * n + which], 0, 0, 0)

    page_specs = [pl.BlockSpec(page_block, page_map(w)) for w in range(PAGED_PAGES_PER_STEP)]
    part_shape = jax.ShapeDtypeStruct((DEC_BATCH, N_PAST_BLOCKS, N_COMBO, HEAD_DIM), F32)
    part_spec = pl.BlockSpec((None, nbs, N_COMBO, HEAD_DIM), lambda b, n, pt: (b, n, 0, 0))
    return pl.pallas_call(
        _paged_partial_kernel,
        grid_spec=pltpu.PrefetchScalarGridSpec(
            num_scalar_prefetch=1,
            grid=(DEC_BATCH, N_PAST_BLOCKS // nbs),
            in_specs=[pl.BlockSpec((None, N_COMBO, HEAD_DIM), lambda b, n, pt: (b, 0, 0))] + page_specs + page_specs,
            out_specs=[part_spec, part_spec, part_spec,
                       pl.BlockSpec((None, nbs, C_HEADS, HEAD_DIM), lambda b, n, pt: (b, n, 0, 0))],
            scratch_shapes=[pltpu.VMEM((PAGED_PAGES_PER_STEP * PAGE_SIZE * C_HEADS, HEAD_DIM), BF16)] * 2),
        out_shape=[part_shape, part_shape, part_shape,
                   jax.ShapeDtypeStruct((DEC_BATCH, N_PAST_BLOCKS, C_HEADS, HEAD_DIM), F32)],
        compiler_params=_cparams(2),
        name="paged_partials",
    )(page_table, q_combo, *([cache_k] * PAGED_PAGES_PER_STEP), *([cache_v] * PAGED_PAGES_PER_STEP))


def _paged_merge_kernel(q_ref, kn_ref, vn_ref, m_ref, l_ref, o_ref, ks_ref, out_ref):
    nb = N_PAST_BLOCKS
    scale = HEAD_DIM ** -0.5
    q = q_ref[...]
    kmean = ks_ref[...] * (1.0 / MOBA_BLOCK)
    kmean = jnp.concatenate([kmean] * DEC_SEQ, axis=1)
    bs = jnp.sum(kmean * q[None], axis=-1, keepdims=True)
    work = jnp.broadcast_to(bs, (nb, N_COMBO, HEAD_DIM))
    n_i = lax.broadcasted_iota(jnp.int32, (nb, N_COMBO, HEAD_DIM), 0)
    sel = n_i < 0
    for _ in range(MOBA_TOPK):
        mx = jnp.max(work, axis=0, keepdims=True)
        first = jnp.min(jnp.where(work == mx, n_i, nb), axis=0, keepdims=True)
        hit = n_i == first
        sel = sel | hit
        work = jnp.where(hit, -jnp.inf, work)

    s = _dot_nt(q.astype(BF16), kn_ref[...].astype(BF16)) * scale
    c_i = lax.broadcasted_iota(jnp.int32, s.shape, 0)
    l_i = lax.broadcasted_iota(jnp.int32, s.shape, 1)
    ok = ((l_i & (C_HEADS - 1)) == (c_i & (C_HEADS - 1))) & ((l_i >> 3) <= (c_i >> 3))
    s = jnp.where(ok, s, -jnp.inf)
    m_own = jnp.max(s, axis=-1, keepdims=True)
    p = jnp.exp(s - m_own)
    l_own = jnp.sum(p, axis=-1, keepdims=True)
    o_own = _dot(p.astype(BF16), vn_ref[...].astype(BF16))

    mp = m_ref[...]
    m_all = jnp.maximum(jnp.max(jnp.where(sel, mp, -jnp.inf), axis=0), m_own)
    w = jnp.where(sel, jnp.exp(mp - m_all[None]), 0.0)
    w_own = jnp.exp(m_own - m_all)
    den = jnp.sum(w * l_ref[...], axis=0) + w_own * l_own
    num = jnp.sum(w * o_ref[...], axis=0) + w_own * o_own
    out_ref[...] = num / den


def paged_merge(q_combo, k_new, v_new, m_p, l_p, o_p, ksum):
    combo = pl.BlockSpec((None, N_COMBO, HEAD_DIM), lambda b: (b, 0, 0))
    part = pl.BlockSpec((None, N_PAST_BLOCKS, N_COMBO, HEAD_DIM), lambda b: (b, 0, 0, 0))
    return pl.pallas_call(
        _paged_merge_kernel,
        grid=(DEC_BATCH,),
        in_specs=[combo, combo, combo, part, part, part,
                  pl.BlockSpec((None, N_PAST_BLOCKS, C_HEADS, HEAD_DIM), lambda b: (b, 0, 0, 0))],
        out_specs=combo,
        out_shape=jax.ShapeDtypeStruct((DEC_BATCH, N_COMBO, HEAD_DIM), F32),
        compiler_params=_cparams(1, N_SAMPLE),
        name="paged_merge",
    )(q_combo, k_new, v_new, m_p, l_p, o_p, ksum)


def _memattn_kernel(x_ref, mk_ref, mv_ref, g_ref, b_ref, *refs, nb_tile, rows_per_b, precise):
    n_w = 2 if precise else 1
    wq = _load_all(refs[:n_w])
    wo = _load_all(refs[n_w:2 * n_w])
    o_ref = refs[2 * n_w]
    scale = MEM_HEAD_DIM ** -0.5
    x = x_ref[...]
    tm = x.shape[0]
    q = _mm(x, wq, precise)
    row_b = lax.broadcasted_iota(jnp.int32, (tm, 1), 0) >> int(math.log2(rows_per_b))
    heads = []
    for hh in range(MEM_HEADS):
        sl = slice(hh * MEM_HEAD_DIM, (hh + 1) * MEM_HEAD_DIM)
        qh = q[:, sl]
        oh = None
        for bb in range(nb_tile):
            s = _mm(qh, mk_ref[bb, :, sl], precise, nt=True) * scale
            s = s - jnp.max(s, axis=-1, keepdims=True)
            p = jnp.exp(s)
            p = p / jnp.sum(p, axis=-1, keepdims=True)
            ob = _mm(p, mv_ref[bb, :, sl], precise)
            if nb_tile > 1:
                ob = jnp.where(row_b == bb, ob, 0.0)
            oh = ob if oh is None else oh + ob
        heads.append(oh)
    o = jnp.concatenate(heads, axis=-1)
    y = ALPHA * x + _mm(o, wo, precise)
    o_ref[...] = _ln(y, g_ref[...], b_ref[...])


def memory_attention_ln(x, wq, mk, mv, wo, g, b, *, m, tm, rows_per_b, name, precise=False, b0=0):
    if rows_per_b >= tm:
        nb_tile = 1
        tiles_per_b = rows_per_b // tm
        kv_map = lambda i: (b0 + i // tiles_per_b, 0, 0)
    else:
        nb_tile = tm // rows_per_b
        kv_map = lambda i: (i, 0, 0)
    const2 = lambda i: (0, 0)
    return pl.pallas_call(
        functools.partial(_memattn_kernel, nb_tile=nb_tile, rows_per_b=rows_per_b, precise=precise),
        grid=(m // tm,),
        in_specs=[pl.BlockSpec((tm, D_MODEL), lambda i: (i, 0)),
                  pl.BlockSpec((nb_tile, N_MEM, MEM_WIDTH), kv_map),
                  pl.BlockSpec((nb_tile, N_MEM, MEM_WIDTH), kv_map),
                  pl.BlockSpec((1, D_MODEL), const2),
                  pl.BlockSpec((1, D_MODEL), const2)]
        + [pl.BlockSpec((D_MODEL, MEM_WIDTH), const2)] * len(wq)
        + [pl.BlockSpec((MEM_WIDTH, D_MODEL), const2)] * len(wo),
        out_specs=pl.BlockSpec((tm, D_MODEL), lambda i: (i, 0)),
        out_shape=jax.ShapeDtypeStruct((m, D_MODEL), F32),
        compiler_params=_cparams(1, m),
        name=name,
    )(x, mk, mv, g, b, *wq, *wo)


def _router_kernel(x_ref, xt_ref, w_ref, b_ref, o_ref, cnt_ref, run_ref, *, tiles_per_b):
    step = pl.program_id(0)
    tm = x_ref.shape[0]

    @pl.when(step == 0)
    def _():
        run_ref[...] = jnp.zeros(run_ref.shape, F32)

    x = x_ref[...]
    if tiles_per_b:
        is_last = lax.rem(step, tiles_per_b) == tiles_per_b - 1
        x = jnp.concatenate([x[:tm - CHUNK], jnp.where(is_last, xt_ref[...], x[tm - CHUNK:])], axis=0)
    xh, xl = _split_bf16(x)
    wh, wl = _split_bf16(w_ref[...])
    logits = _dot_nt(wh, xh) + (_dot_nt(wh, xl) + _dot_nt(wl, xh)) + b_ref[...]
    logits = logits - jnp.max(logits, axis=0, keepdims=True)
    e = jnp.exp(logits)
    probs = e / jnp.sum(e, axis=0, keepdims=True)
    p = [probs[j:j + 1, :] for j in range(N_EXPERTS)]
    gbest = None
    gsel = None
    for g in range(N_EXPERT_GROUPS):
        a, b_, c, d = p[4 * g:4 * g + 4]
        hi1, lo1 = jnp.maximum(a, b_), jnp.minimum(a, b_)
        hi2, lo2 = jnp.maximum(c, d), jnp.minimum(c, d)
        gs = jnp.maximum(hi1, hi2) + jnp.maximum(jnp.minimum(hi1, hi2), jnp.maximum(lo1, lo2))
        if g == 0:
            gbest, gsel = gs, jnp.zeros(gs.shape, jnp.int32)
        else:
            better = gs > gbest
            gbest = jnp.where(better, gs, gbest)
            gsel = jnp.where(better, g, gsel)
    cand = [jnp.where(gsel == j // EXPERTS_PER_GROUP, p[j], -1.0) for j in range(N_EXPERTS)]
    v1 = cand[0]
    i1 = jnp.zeros(v1.shape, jnp.int32)
    for j in range(1, N_EXPERTS):
        better = cand[j] > v1
        v1 = jnp.where(better, cand[j], v1)
        i1 = jnp.where(better, j, i1)
    v2 = jnp.full(v1.shape, -2.0, F32)
    i2 = jnp.zeros(v1.shape, jnp.int32)
    for j in range(N_EXPERTS):
        better = (cand[j] > v2) & (i1 != j)
        v2 = jnp.where(better, cand[j], v2)
        i2 = jnp.where(better, j, i2)
    tot = v1 + v2

    e_i = lax.broadcasted_iota(jnp.int32, (N_EXPERTS, tm), 0)
    pick1 = e_i == i1
    pick2 = e_i == i2
    onehot = (pick1 | pick2).astype(F32)
    t_r = lax.broadcasted_iota(jnp.int32, (tm, tm), 0)
    t_c = lax.broadcasted_iota(jnp.int32, (tm, tm), 1)
    earlier = (t_r < t_c).astype(BF16)
    before = _dot(onehot.astype(BF16), earlier) + run_ref[:, 0:1]
    rank1 = jnp.sum(jnp.where(pick1, before, 0.0), axis=0, keepdims=True)
    rank2 = jnp.sum(jnp.where(pick2, before, 0.0), axis=0, keepdims=True)
    run = run_ref[...] + jnp.sum(onehot, axis=1, keepdims=True)
    run_ref[...] = run
    cnt_ref[...] = run

    row = lax.broadcasted_iota(jnp.int32, o_ref.shape, 0)
    vals = (i1.astype(F32), i2.astype(F32), v1 / tot, v2 / tot, rank1, rank2)
    out = jnp.zeros(o_ref.shape, F32)
    for r, v in enumerate(vals):
        out = jnp.where(row == r, v, out)
    o_ref[...] = out


def router(x, x_tail, w_t, b, *, tm, name):
    m = x.shape[0]
    tiles_per_b = 0 if x_tail is None else SEQ // tm
    if x_tail is None:
        x_tail, tail_spec = x, pl.BlockSpec((min(CHUNK, tm), D_MODEL), lambda i: (0, 0))
    else:
        tail_spec = pl.BlockSpec((CHUNK, D_MODEL), lambda i: (i // tiles_per_b, 0))
    return pl.pallas_call(
        functools.partial(_router_kernel, tiles_per_b=tiles_per_b),
        grid=(m // tm,),
        in_specs=[pl.BlockSpec((tm, D_MODEL), lambda i: (i, 0)),
                  tail_spec,
                  pl.BlockSpec((N_EXPERTS, D_MODEL), lambda i: (0, 0)),
                  pl.BlockSpec((N_EXPERTS, 1), lambda i: (0, 0))],
        out_specs=[pl.BlockSpec((8, tm), lambda i: (0, i)),
                   pl.BlockSpec((N_EXPERTS, 128), lambda i: (0, 0))],
        out_shape=[jax.ShapeDtypeStruct((8, m), F32),
                   jax.ShapeDtypeStruct((N_EXPERTS, 128), F32)],
        scratch_shapes=[pltpu.VMEM((N_EXPERTS, 128), F32)],
        compiler_params=_cparams(1, m),
        name=name,
    )(x, x_tail, w_t, b)


def _dispatch_kernel(d0_ref, d1_ref, starts_ref, ends_ref, nv_ref, x_ref, o_hbm, zero_ref, sem, *, cfg):
    i = pl.program_id(0)
    tm = cfg.tok_tile

    def fill_copy(row0):
        return pltpu.make_async_copy(zero_ref, o_hbm.at[pl.ds(row0, cfg.tile)], sem)

    @pl.when(i == 0)
    def _():
        zero_ref[...] = jnp.zeros(zero_ref.shape, zero_ref.dtype)
        for wait in (False, True):
            for e in range(N_EXPERTS):
                @pl.when(ends_ref[e] > starts_ref[e])
                def _():
                    cp = fill_copy(pl.multiple_of(ends_ref[e] - cfg.tile, cfg.tile))
                    cp.wait() if wait else cp.start()

            def tail(t, carry):
                cp = fill_copy(pl.multiple_of(t * cfg.tile, cfg.tile))
                cp.wait() if wait else cp.start()
                return carry

            lax.fori_loop(nv_ref[0], cfg.n_tiles, tail, 0)

    base = i * tm

    def body(it, carry):
        for j in range(ROW_DMA_UNROLL):
            r = it * ROW_DMA_UNROLL + j
            pltpu.make_async_copy(x_ref.at[pl.ds(r, 1)], o_hbm.at[pl.ds(d0_ref[base + r], 1)], sem).start(priority=0)
            pltpu.make_async_copy(x_ref.at[pl.ds(r, 1)], o_hbm.at[pl.ds(d1_ref[base + r], 1)], sem).start(priority=1)
        return carry

    lax.fori_loop(0, tm // ROW_DMA_UNROLL, body, 0)
    for _ in range(MOE_TOPK):
        pltpu.make_async_copy(x_ref, o_hbm.at[pl.ds(0, tm)], sem).wait()


def moe_dispatch(d0, d1, starts, ends, n_valid, x, cfg, name):
    tm = cfg.tok_tile
    return pl.pallas_call(
        functools.partial(_dispatch_kernel, cfg=cfg),
        grid_spec=pltpu.PrefetchScalarGridSpec(
            num_scalar_prefetch=5,
            grid=(cfg.n_tok // tm,),
            in_specs=[pl.BlockSpec((tm, D_MODEL), lambda i, *_: (i, 0))],
            out_specs=pl.BlockSpec(memory_space=pl.ANY),
            scratch_shapes=[pltpu.VMEM((cfg.tile, D_MODEL), F32), pltpu.SemaphoreType.DMA]),
        out_shape=jax.ShapeDtypeStruct((cfg.rows, D_MODEL), F32),
        compiler_params=_cparams(1, cfg.n_tok),
        name=name,
    )(d0, d1, starts, ends, n_valid, x)


def _split_expert_weights(w_refs, scratch_refs):
    for k, w_ref in enumerate(w_refs):
        hi, lo = _split_bf16(w_ref[...])
        scratch_refs[2 * k][...] = hi
        scratch_refs[2 * k + 1][...] = lo


def _expert_dot3(x, scratch_refs, k):
    xh, xl = _split_bf16(x)
    wh, wl = scratch_refs[2 * k][...], scratch_refs[2 * k + 1][...]
    return _dot(xh, wh) + (_dot(xl, wh) + _dot(xh, wl))


def _moe_up_kernel(te_ref, first_ref, nv_ref, x_ref, wg_ref, wu_ref, hh_ref, *scratch):
    i = pl.program_id(0)

    @pl.when(i < nv_ref[0])
    def _():
        _split_expert_weights((wg_ref, wu_ref), scratch)
        x = x_ref[...]
        hg = _expert_dot3(x, scratch, 0)
        hu = _expert_dot3(x, scratch, 1)
        hh_ref[...] = hg * jax.nn.sigmoid(hg) * hu

    @pl.when(i >= nv_ref[0])
    def _():
        hh_ref[...] = jnp.zeros(hh_ref.shape, hh_ref.dtype)


def _moe_down_kernel(te_ref, first_ref, nv_ref, hh_ref, wd_ref, y_ref, *scratch):
    i = pl.program_id(0)

    @pl.when(i < nv_ref[0])
    def _():
        @pl.when(first_ref[i] == 1)
        def _():
            _split_expert_weights((wd_ref,), scratch)

        y_ref[...] = _expert_dot3(hh_ref[...], scratch, 0)

    @pl.when(i >= nv_ref[0])
    def _():
        y_ref[...] = jnp.zeros(y_ref.shape, y_ref.dtype)


def _stream_expert_weights(i, te_ref, first_ref, nxt_ref, w_hbms, land_refs, bf16_refs, sem, layer):
    def copies(e):
        return [pltpu.make_async_copy(w.at[layer, e], land, sem.at[k])
                for k, (w, land) in enumerate(zip(w_hbms, land_refs))]

    @pl.when(i == 0)
    def _():
        for cp in copies(te_ref[0]):
            cp.start()

    @pl.when(first_ref[i] == 1)
    def _():
        for cp in copies(te_ref[i]):
            cp.wait()
        for land, dst in zip(land_refs, bf16_refs):
            dst[...] = land[...].astype(BF16)

        @pl.when(nxt_ref[i] >= 0)
        def _():
            for cp in copies(nxt_ref[i]):
                cp.start()


def _moe_up_stream_kernel(te_ref, first_ref, nxt_ref, nv_ref, x_ref, wg_hbm, wu_hbm, hh_ref,
                          land_g, land_u, wgb_ref, wub_ref, sem, *, layer):
    i = pl.program_id(0)

    @pl.when(i < nv_ref[0])
    def _():
        _stream_expert_weights(i, te_ref, first_ref, nxt_ref, (wg_hbm, wu_hbm), (land_g, land_u),
                               (wgb_ref, wub_ref), sem, layer)
        x = x_ref[...].astype(BF16)
        hg = _dot(x, wgb_ref[...])
        hu = _dot(x, wub_ref[...])
        hh_ref[...] = (hg * jax.nn.sigmoid(hg) * hu).astype(hh_ref.dtype)

    @pl.when(i >= nv_ref[0])
    def _():
        hh_ref[...] = jnp.zeros(hh_ref.shape, hh_ref.dtype)


def _moe_down_stream_kernel(te_ref, first_ref, nxt_ref, nv_ref, hh_ref, wd_hbm, y_ref, land_d, wdb_ref, sem,
                            *, layer):
    i = pl.program_id(0)

    @pl.when(i < nv_ref[0])
    def _():
        _stream_expert_weights(i, te_ref, first_ref, nxt_ref, (wd_hbm,), (land_d,), (wdb_ref,), sem, layer)
        y_ref[...] = _dot(hh_ref[...], wdb_ref[...])

    @pl.when(i >= nv_ref[0])
    def _():
        y_ref[...] = jnp.zeros(y_ref.shape, y_ref.dtype)


def moe_experts_streamed(tile_expert, tile_first, tile_next, n_valid, x_sorted, w_gate, w_up, w_down, layer, cfg,
                         name):
    tm = cfg.tile
    row_map = lambda i, te, fi, nx, nv: (jnp.minimum(i, nv[0] - 1), 0)
    out_map = lambda i, te, fi, nx, nv: (i, 0)
    hbm = pl.BlockSpec(memory_space=pl.ANY)
    hh = pl.pallas_call(
        functools.partial(_moe_up_stream_kernel, layer=layer),
        grid_spec=pltpu.PrefetchScalarGridSpec(
            num_scalar_prefetch=4,
            grid=(cfg.n_tiles,),
            in_specs=[pl.BlockSpec((tm, D_MODEL), row_map), hbm, hbm],
            out_specs=pl.BlockSpec((tm, D_EXPERT), out_map),
            scratch_shapes=[pltpu.VMEM((D_MODEL, D_EXPERT), F32), pltpu.VMEM((D_MODEL, D_EXPERT), F32),
                            pltpu.VMEM((D_MODEL, D_EXPERT), BF16), pltpu.VMEM((D_MODEL, D_EXPERT), BF16),
                            pltpu.SemaphoreType.DMA((2,))]),
        out_shape=jax.ShapeDtypeStruct((cfg.rows, D_EXPERT), BF16),
        compiler_params=_cparams(1, cfg.n_tok),
        name=name + "_up",
    )(tile_expert, tile_first, tile_next, n_valid, x_sorted, w_gate, w_up)
    return pl.pallas_call(
        functools.partial(_moe_down_stream_kernel, layer=layer),
        grid_spec=pltpu.PrefetchScalarGridSpec(
            num_scalar_prefetch=4,
            grid=(cfg.n_tiles,),
            in_specs=[pl.BlockSpec((tm, D_EXPERT), row_map), hbm],
            out_specs=pl.BlockSpec((tm, D_MODEL), out_map),
            scratch_shapes=[pltpu.VMEM((D_EXPERT, D_MODEL), F32), pltpu.VMEM((D_EXPERT, D_MODEL), BF16),
                            pltpu.SemaphoreType.DMA((1,))]),
        out_shape=jax.ShapeDtypeStruct((cfg.rows, D_MODEL), F32),
        compiler_params=_cparams(1, cfg.n_tok),
        name=name + "_down",
    )(tile_expert, tile_first, tile_next, n_valid, hh, w_down)


def moe_experts_precise(tile_expert, tile_first, n_valid, x_sorted, w_gate, w_up, w_down, layer, cfg, name):
    tm = cfg.tile
    n_copies = 2
    f_tiles = 2
    fw = D_EXPERT // f_tiles
    row_map = lambda i, te, fi, nv: (jnp.minimum(i, nv[0] - 1), 0)
    wmap = lambda i, te, fi, nv: (layer, te[i], 0, 0)
    hh = pl.pallas_call(
        _moe_up_kernel,
        grid_spec=pltpu.PrefetchScalarGridSpec(
            num_scalar_prefetch=3,
            grid=(cfg.n_tiles, f_tiles),
            in_specs=[pl.BlockSpec((tm, D_MODEL), lambda i, f, te, fi, nv: (jnp.minimum(i, nv[0] - 1), 0)),
                      pl.BlockSpec((None, None, D_MODEL, fw), lambda i, f, te, fi, nv: (layer, te[i], 0, f)),
                      pl.BlockSpec((None, None, D_MODEL, fw), lambda i, f, te, fi, nv: (layer, te[i], 0, f))],
            out_specs=pl.BlockSpec((tm, fw), lambda i, f, te, fi, nv: (i, f)),
            scratch_shapes=[pltpu.VMEM((D_MODEL, fw), BF16)] * (2 * n_copies)),
        out_shape=jax.ShapeDtypeStruct((cfg.rows, D_EXPERT), F32),
        compiler_params=_cparams(2, cfg.n_tok),
        name=name + "_up",
    )(tile_expert, tile_first, n_valid, x_sorted, w_gate, w_up)
    return pl.pallas_call(
        _moe_down_kernel,
        grid_spec=pltpu.PrefetchScalarGridSpec(
            num_scalar_prefetch=3,
            grid=(cfg.n_tiles,),
            in_specs=[pl.BlockSpec((tm, D_EXPERT), row_map),
                      pl.BlockSpec((None, None, D_EXPERT, D_MODEL), wmap)],
            out_specs=pl.BlockSpec((tm, D_MODEL), lambda i, te, fi, nv: (i, 0)),
            scratch_shapes=[pltpu.VMEM((D_EXPERT, D_MODEL), BF16)] * n_copies),
        out_shape=jax.ShapeDtypeStruct((cfg.rows, D_MODEL), F32),
        compiler_params=_cparams(1, cfg.n_tok),
        name=name + "_down",
    )(tile_expert, tile_first, n_valid, hh, w_down)


def _combine_kernel(d0_ref, d1_ref, x_ref, w_ref, g_ref, b_ref, y_hbm, o_ref, buf_ref, sem, *, tm):
    i = pl.program_id(0)
    n_steps = pl.num_programs(0)

    def gather(tile, half, start):
        base = tile * tm

        def body(it, carry):
            for j in range(ROW_DMA_UNROLL):
                r = it * ROW_DMA_UNROLL + j
                pltpu.make_async_copy(y_hbm.at[pl.ds(d0_ref[base + r], 1)], buf_ref.at[half, 0, pl.ds(r, 1)],
                                      sem.at[half]).start(priority=0)
                pltpu.make_async_copy(y_hbm.at[pl.ds(d1_ref[base + r], 1)], buf_ref.at[half, 1, pl.ds(r, 1)],
                                      sem.at[half]).start(priority=1)
            return carry

        if start:
            lax.fori_loop(0, tm // ROW_DMA_UNROLL, body, 0)
        else:
            for k in range(MOE_TOPK):
                pltpu.make_async_copy(y_hbm.at[pl.ds(0, tm)], buf_ref.at[half, k], sem.at[half]).wait()

    @pl.when(i == 0)
    def _():
        gather(0, 0, True)

    @pl.when(i + 1 < n_steps)
    def _():
        gather(i + 1, (i + 1) % 2, True)

    half = i % 2
    gather(i, half, False)
    w = w_ref[...]
    y = ALPHA * x_ref[...] + w[:, 0:1] * buf_ref[half, 0] + w[:, 1:2] * buf_ref[half, 1]
    o_ref[...] = _ln(y, g_ref[...], b_ref[...])


def moe_combine_ln(d0, d1, x, w_tok, g, b, y_sorted, *, tm, name):
    m = x.shape[0]
    return pl.pallas_call(
        functools.partial(_combine_kernel, tm=tm),
        grid_spec=pltpu.PrefetchScalarGridSpec(
            num_scalar_prefetch=2,
            grid=(m // tm,),
            in_specs=[pl.BlockSpec((tm, D_MODEL), lambda i, a, c: (i, 0)),
                      pl.BlockSpec((tm, MOE_TOPK), lambda i, a, c: (i, 0)),
                      pl.BlockSpec((1, D_MODEL), lambda i, a, c: (0, 0)),
                      pl.BlockSpec((1, D_MODEL), lambda i, a, c: (0, 0)),
                      pl.BlockSpec(memory_space=pl.ANY)],
            out_specs=pl.BlockSpec((tm, D_MODEL), lambda i, a, c: (i, 0)),
            scratch_shapes=[pltpu.VMEM((2, MOE_TOPK, tm, D_MODEL), F32), pltpu.SemaphoreType.DMA((2,))]),
        out_shape=jax.ShapeDtypeStruct((m, D_MODEL), F32),
        compiler_params=_cparams(1, m),
        name=name,
    )(d0, d1, x, w_tok, g, b, y_sorted)


def _routing_plan(route, counts, cfg):
    counts = counts[:, 0].astype(jnp.int32)
    padded = ((counts + cfg.tile - 1) // cfg.tile) * cfg.tile
    ends = jnp.cumsum(padded)
    starts = ends - padded
    experts = jnp.arange(N_EXPERTS, dtype=jnp.int32)[:, None]

    def slot_rows(k):
        picked = route[k].astype(jnp.int32)[None, :] == experts
        return jnp.sum(jnp.where(picked, starts[:, None], 0), axis=0) + route[4 + k].astype(jnp.int32)

    tile_start = jnp.arange(cfg.n_tiles, dtype=jnp.int32) * cfg.tile
    tile_expert = jnp.minimum(jnp.sum((tile_start[:, None] >= ends[None, :]).astype(jnp.int32), axis=1),
                              N_EXPERTS - 1)
    start_of_tile_expert = jnp.sum(jnp.where(tile_expert[:, None] == experts.T, starts[None, :], 0), axis=1)
    tile_first = (tile_start == start_of_tile_expert).astype(jnp.int32)
    n_valid = (ends[-1] // cfg.tile).astype(jnp.int32).reshape(1)
    later_nonempty = (experts.T > tile_expert[:, None]) & (padded[None, :] > 0)
    tile_next = jnp.min(jnp.where(later_nonempty, experts.T, N_EXPERTS), axis=1)
    tile_next = jnp.where(tile_next == N_EXPERTS, -1, tile_next).astype(jnp.int32)
    return (slot_rows(0), slot_rows(1), route[2:4].T, starts, ends, tile_expert, tile_first, tile_next, n_valid)


def moe_ln(x, x_route, layer, p, cfg, name):
    route, cnt = router(x, x_route, p["router_w_t"], p["router_b"], tm=cfg.tok_tile, name=name + "_router")
    d0, d1, w_tok, starts, ends, tile_expert, tile_first, tile_next, n_valid = _routing_plan(route, cnt, cfg)
    x_sorted = moe_dispatch(d0, d1, starts, ends, n_valid, x, cfg, name + "_dispatch")
    weights = (p["moe_w_gate"], p["moe_w_up"], p["moe_w_down"])
    if cfg.precise:
        y_sorted = moe_experts_precise(tile_expert, tile_first, n_valid, x_sorted, *weights, layer, cfg, name)
    else:
        y_sorted = moe_experts_streamed(tile_expert, tile_first, tile_next, n_valid, x_sorted, *weights, layer,
                                        cfg, name)
    return moe_combine_ln(d0, d1, x, w_tok, p["moe_ln_g"][layer], p["moe_ln_b"][layer], y_sorted,
                          tm=cfg.tok_tile, name=name + "_combine")


def _rope_tables(pos):
    half = ROT_DIM // 2
    inv = jnp.power(ROPE_THETA, -jnp.arange(half, dtype=F32) / half)
    ang = pos.astype(F32)[:, None] * inv[None, :]
    cos, sin = jnp.cos(ang), jnp.sin(ang)
    n = pos.shape[0]
    ones = jnp.ones((n, HEAD_DIM - ROT_DIM), F32)
    zeros_h = jnp.zeros((n, half), F32)
    zeros_r = jnp.zeros((n, HEAD_DIM - ROT_DIM), F32)
    c = jnp.concatenate([cos, cos, ones], axis=1)
    sa = jnp.concatenate([zeros_h, sin, zeros_r], axis=1)
    sb = jnp.concatenate([-sin, zeros_h, zeros_r], axis=1)
    return c, sa, sb


def kernel(x_prompt, x_sample, mem_prompt, cache_c_k, cache_c_v, page_table, state_b_buf, state_d_buf, cache_mem_k, cache_mem_v, ab_w_in, ab_a_ln_g, ab_a_ln_b, ab_a_ws, ab_a_bs, ab_b_wg, ab_b_scale, ab_w_out, cd_w_in, cd_d_conv_w, cd_d_conv_b, cd_d_ln_g, cd_d_ln_b, cd_w_out, mix_ln_g, mix_ln_b, mem_w_q, mem_w_k, mem_w_v, mem_w_o, mem_ln_g, mem_ln_b, router_w, router_b, moe_w_gate, moe_w_up, moe_w_down, moe_ln_g, moe_ln_b):
    row = lambda v: v.reshape(1, -1)
    xp = x_prompt.reshape(N_PROMPT, D_MODEL)
    xs = x_sample.reshape(N_SAMPLE, D_MODEL)
    moe_p = {"router_w_t": router_w.T, "router_b": router_b.reshape(N_EXPERTS, 1),
             "moe_w_gate": moe_w_gate, "moe_w_up": moe_w_up,
             "moe_w_down": moe_w_down, "moe_ln_g": [row(moe_ln_g[l]) for l in range(DEPTH)],
             "moe_ln_b": [row(moe_ln_b[l]) for l in range(DEPTH)]}

    w_kv = split_weight(jnp.concatenate([mem_w_k[0], mem_w_k[1], mem_w_v[0], mem_w_v[1]], axis=1), True)
    n_memrows = BATCH * N_MEM
    kv = matmul(mem_prompt.reshape(n_memrows, D_MODEL), w_kv, m=n_memrows, tm=256, tn=1024, precise=True,
                name="mem_kv")
    pmk = [kv[:, l * MEM_WIDTH:(l + 1) * MEM_WIDTH].reshape(BATCH, N_MEM, MEM_WIDTH) for l in range(DEPTH)]
    pmv = [kv[:, (DEPTH + l) * MEM_WIDTH:(DEPTH + l + 1) * MEM_WIDTH].reshape(BATCH, N_MEM, MEM_WIDTH)
           for l in range(DEPTH)]
    smk = cache_mem_k.reshape(DEPTH, DEC_BATCH, N_MEM, MEM_WIDTH)
    smv = cache_mem_v.reshape(DEPTH, DEC_BATCH, N_MEM, MEM_WIDTH)

    mem_w = [(split_weight(mem_w_q[l], True), split_weight(mem_w_o[l], True)) for l in range(DEPTH)]
    mem_ln = [(row(mem_ln_g[l]), row(mem_ln_b[l])) for l in range(DEPTH)]

    def memattn_prompt(x, layer):
        wq, wo = split_weight(mem_w_q[layer], False), split_weight(mem_w_o[layer], False)
        return memory_attention_ln(x, wq, pmk[layer], pmv[layer], wo, *mem_ln[layer], m=N_PROMPT, tm=512,
                                   rows_per_b=SEQ, name="memattn_prompt")

    def memattn_sample(x, layer):
        wq, wo = mem_w[layer]
        return memory_attention_ln(x, wq, smk[layer], smv[layer], wo, *mem_ln[layer], m=N_SAMPLE, tm=N_SAMPLE,
                                   rows_per_b=DEC_SEQ, precise=True, name="memattn_sample")

    w_in0 = split_weight(ab_w_in[0], True)
    w_out0 = split_weight(ab_w_out[0], True)
    ab_prm = {"ln_g": row(ab_a_ln_g[0]), "ln_b": row(ab_a_ln_b[0]), "ws": ab_a_ws[0], "bs_t": ab_a_bs[0].T,
              "wg": ab_b_wg[0].astype(BF16), "scale": row(ab_b_scale[0])}
    ab_prm_precise = dict(ab_prm, wg=ab_b_wg[0])
    g, b = row(mix_ln_g[0]), row(mix_ln_b[0])
    n_chunks = SEQ // CHUNK
    n_tail = BATCH * CHUNK

    h0p = matmul(xp, split_weight(ab_w_in[0], False), m=N_PROMPT, tm=1024, tn=1024, n_gelu=2, name="ab_in_prompt")
    mix_p, _ = mixer_ab(h0p, h0p, lambda b, c: (jnp.maximum(b * (SEQ // 16) + c * (CHUNK // 16) - 1, 0), 2),
                        nb=BATCH, n_chunks=n_chunks, pos0=0, has_ctx=False, prm=ab_prm, name="mixer_ab_prompt")
    x1p = matmul_res_ln([mix_p], [split_weight(ab_w_out[0], False)], xp, g, b, m=N_PROMPT, tm=512,
                        name="ab_out_prompt")
    x1p = memattn_prompt(x1p, 0)

    xt = x_prompt[:, SEQ - CHUNK:].reshape(n_tail, D_MODEL)
    h0t = matmul(xt, w_in0, m=n_tail, tm=CHUNK, tn=1024, n_gelu=2, precise=True, name="ab_in_tail")
    mix_t, _ = mixer_ab(h0t, h0p, lambda b, c: ((b + 1) * (SEQ // 16) - CHUNK // 16 - 1, 2), nb=BATCH, n_chunks=1,
                        pos0=SEQ - CHUNK, has_ctx=True, prm=ab_prm_precise, precise=True, name="mixer_ab_tail")
    x1t = matmul_res_ln([mix_t], [w_out0], xt, g, b, m=n_tail, tm=CHUNK, precise=True, name="ab_out_tail")
    wq, wo = mem_w[0]
    x1t = memory_attention_ln(x1t, wq, pmk[0], pmv[0], wo, *mem_ln[0], m=n_tail, tm=CHUNK, rows_per_b=CHUNK,
                              precise=True, name="memattn_tail")

    h0s = matmul(xs, w_in0, m=N_SAMPLE, tm=N_SAMPLE, tn=1024, n_gelu=2, precise=True, name="ab_in_sample")
    h0s_pad = jnp.pad(h0s.reshape(DEC_BATCH, DEC_SEQ, -1), ((0, 0), (0, CHUNK - DEC_SEQ), (0, 0)))
    h0s_pad = h0s_pad.reshape(DEC_BATCH * CHUNK, -1)
    zctx = jnp.pad(state_b_buf[0], ((0, 0), (16 - B_BUF, 0), (0, 0))).reshape(DEC_BATCH * 16, B_WIDTH)
    mix_s, vn_s = mixer_ab(h0s_pad, zctx, lambda b, c: (b, 0), nb=DEC_BATCH, n_chunks=1, pos0=PAST_LEN,
                           has_ctx=True, prm=ab_prm_precise, precise=True, name="mixer_ab_sample")
    mix_s = mix_s.reshape(DEC_BATCH, CHUNK, -1)[:, :DEC_SEQ].reshape(N_SAMPLE, -1)
    x1s = matmul_res_ln([mix_s], [w_out0], xs, g, b, m=N_SAMPLE, tm=N_SAMPLE, precise=True, name="ab_out_sample")
    x1s = memattn_sample(x1s, 0)

    x2p = moe_ln(x1p, x1t, 0, moe_p, MOE_PROMPT, "moe_prompt")
    x2s = moe_ln(x1s, None, 0, moe_p, MOE_SAMPLE, "moe_sample")

    w_in1 = split_weight(cd_w_in[0], True)
    tabs_p = _rope_tables(jnp.arange(SEQ, dtype=jnp.int32))
    tabs_s = _rope_tables(PAST_LEN + (jnp.arange(N_SAMPLE, dtype=jnp.int32) % DEC_SEQ))
    h1p = matmul_rope(x2p, cd_w_in[0].astype(BF16), tabs_p, m=N_PROMPT, tm=1024, tn=C_WIDTH, n_rope=2,
                      tab_blocks=SEQ // 1024, name="cd_in_prompt")
    h1s = matmul(x2s, w_in1, m=N_SAMPLE, tm=N_SAMPLE, tn=1024, precise=True, name="cd_in_sample")
    qr_s, kr_s = rope_qk(h1s, tabs_s, m=N_SAMPLE, tm=N_SAMPLE, xoff=0, tab_blocks=1, name="rope_sample")
    c_p = moba_prompt(h1p, nb=BATCH, seq=SEQ)
    v_s = h1s[:, 2 * C_WIDTH:3 * C_WIDTH]
    q_combo = qr_s.reshape(DEC_BATCH, N_COMBO, HEAD_DIM)
    m_p, l_p, o_p, ksum = paged_partials(page_table, q_combo, cache_c_k, cache_c_v, 0)
    c_s = paged_merge(q_combo, kr_s.reshape(DEC_BATCH, N_COMBO, HEAD_DIM),
                      v_s.reshape(DEC_BATCH, N_COMBO, HEAD_DIM), m_p, l_p, o_p, ksum)
    c_s = c_s.reshape(N_SAMPLE, C_WIDTH)

    cd_prm = {"conv_w": cd_d_conv_w[0], "conv_b": row(cd_d_conv_b[0]), "ln_g": row(cd_d_ln_g[0]),
              "ln_b": row(cd_d_ln_b[0])}
    col_a = 3 * C_WIDTH // D_WIDTH
    tiles_b = SEQ // 256
    halo_per_tile = 256 // CONV_HALO

    def prev_map(col):
        return lambda b, t: (jnp.maximum((b * tiles_b + t) * halo_per_tile - 1, 0), col)

    d_p, tail_p = conv_module(h1p, col_a, h1p, h1p, prev_map(col_a), prev_map(col_a + 1), nb=BATCH,
                              n_tiles=tiles_b, tm=256, rt=128, tail=CONV_HALO, xoff=0, prev_is_state=False,
                              prm=cd_prm, name="conv_prompt")
    gl_s = jnp.pad(h1s[:, 3 * C_WIDTH:].reshape(DEC_BATCH, DEC_SEQ, 2 * D_WIDTH), ((0, 0), (0, 8 - DEC_SEQ), (0, 0)))
    gl_s = gl_s.reshape(DEC_BATCH * 8, 2 * D_WIDTH)
    dctx = jnp.pad(state_d_buf[0], ((0, 0), (CONV_HALO - D_BUF, 0), (0, 0))).reshape(DEC_BATCH * CONV_HALO, D_WIDTH)
    d_s, tail_s = conv_module(gl_s, 0, dctx, dctx, lambda b, t: (b, 0), lambda b, t: (b, 0), nb=DEC_BATCH,
                              n_tiles=1, tm=8, rt=8, tail=8, xoff=0, prev_is_state=True, prm=cd_prm,
                              name="conv_sample", out_dtype=F32)
    d_s = d_s.reshape(DEC_BATCH, 8, D_WIDTH)[:, :DEC_SEQ].reshape(N_SAMPLE, D_WIDTH)
    w_out1c = split_weight(cd_w_out[0][:C_WIDTH], True)
    w_out1d = split_weight(cd_w_out[0][C_WIDTH:], True)
    g, b = row(mix_ln_g[1]), row(mix_ln_b[1])
    w_out1 = cd_w_out[0].astype(BF16)
    x3p = matmul_res_ln([c_p, d_p], [(w_out1[:C_WIDTH],), (w_out1[C_WIDTH:],)], x2p, g, b, m=N_PROMPT, tm=512,
                        name="cd_out_prompt")
    x3s = matmul_res_ln([c_s, d_s], [w_out1c, w_out1d], x2s, g, b, m=N_SAMPLE, tm=N_SAMPLE, precise=True,
                        name="cd_out_sample")
    y_p = moe_ln(memattn_prompt(x3p, 1), None, 1, moe_p, MOE_PROMPT, "moe_prompt")
    y_s = moe_ln(memattn_sample(x3s, 1), None, 1, moe_p, MOE_SAMPLE, "moe_sample")

    kv_shape_p = (1, BATCH, SEQ, C_HEADS, HEAD_DIM)
    kv_shape_s = (1, DEC_BATCH, DEC_SEQ, C_HEADS, HEAD_DIM)
    z_p = h0p[:, 2 * A_WIDTH:].reshape(BATCH, SEQ, B_WIDTH)
    z_s = h0s[:, 2 * A_WIDTH:].reshape(DEC_BATCH, DEC_SEQ, B_WIDTH)
    h_s = tail_s.reshape(DEC_BATCH, 8, D_WIDTH)[:, :DEC_SEQ]
    mem_shape = (BATCH, N_MEM, MEM_HEADS, MEM_HEAD_DIM)
    return (y_p.reshape(BATCH, SEQ, D_MODEL),
            y_s.reshape(DEC_BATCH, DEC_SEQ, D_MODEL),
            h1p[:, C_WIDTH:2 * C_WIDTH].reshape(kv_shape_p),
            h1p[:, 2 * C_WIDTH:3 * C_WIDTH].reshape(kv_shape_p),
            kr_s.reshape(kv_shape_s),
            v_s.reshape(kv_shape_s),
            z_p[:, SEQ - B_BUF:][None],
            jnp.concatenate([state_b_buf[0], z_s], axis=1)[:, DEC_SEQ:][None],
            vn_s.reshape(DEC_BATCH, CHUNK, A_WIDTH)[:, :DEC_SEQ][None],
            tail_p.reshape(BATCH, CONV_HALO, D_WIDTH)[:, CONV_HALO - D_BUF:][None],
            jnp.concatenate([state_d_buf[0], h_s], axis=1)[:, DEC_SEQ:][None],
            jnp.stack([m_.reshape(mem_shape) for m_ in pmk]),
            jnp.stack([m_.reshape(mem_shape) for m_ in pmv]))
```
